```python
import jax, jax.numpy as jnp
from jax import lax
import numpy as np

D_MODEL = 1024
BATCH = 8
SEQ = 2048
DEPTH = 2
DEC_BATCH = 32
DEC_SEQ = 4
PAST_LEN = 8192
PAGE_SIZE = 128

HEAD_DIM = 64
H_FOX = 8
W_FOX = H_FOX * HEAD_DIM
Q_BLOCK = 128
FOX_SCALE = HEAD_DIM ** -0.5
H_RET = 8
RET_DK = 64
RET_DV = 128
W_RET = H_RET * RET_DV
RET_CHUNK = 128
ROPE_BASE = 10000.0
W_MIX0 = W_FOX + W_RET
EVEN_COLS = (W_FOX, W_FOX, W_FOX, H_FOX, W_FOX, H_RET * RET_DK, H_RET * RET_DK, W_RET, W_RET)
EVEN_IN = 4 * W_FOX + H_FOX + 2 * H_RET * RET_DK + 2 * W_RET
GM_WIDTH = 2 * D_MODEL
GM_GROUPS = 8
GM_GC = GM_WIDTH // GM_GROUPS
GM_CHUNK = 128
N_EVEN = (DEPTH + 1) // 2
N_ODD = DEPTH // 2
EPS = 1e-6
F32 = jnp.float32

kernel_name = 'fox_retention_gmlp_hybrid_step'


def _norm(x, g, center):
    xf = x.astype(F32)
    if center:
        xf = xf - jnp.mean(xf, axis=-1, keepdims=True)
    y = xf * lax.rsqrt(jnp.mean(xf * xf, axis=-1, keepdims=True) + EPS)
    return (y * g.astype(F32)).astype(x.dtype)


def rms_norm(x, g):
    return _norm(x, g, False)


def layer_norm(x, g):
    return _norm(x, g, True)


def _split(p, sizes):
    idx = np.cumsum(np.array(sizes))[:-1].tolist()
    return jnp.split(p, idx, axis=-1)


def rope(x, pos):
    half = x.shape[-1] // 2
    inv = ROPE_BASE ** (-jnp.arange(half, dtype=F32) / half)
    ang = pos.astype(F32)[:, None] * inv[None, :]
    cos = jnp.cos(ang)[None, :, None, :]
    sin = jnp.sin(ang)[None, :, None, :]
    xf = x.astype(F32)
    x1, x2 = xf[..., :half], xf[..., half:]
    return jnp.concatenate([x1 * cos - x2 * sin, x1 * sin + x2 * cos], axis=-1).astype(x.dtype)


def ret_log_gamma():
    return jnp.log(1.0 - 2.0 ** (-5.0 - jnp.arange(H_RET, dtype=F32)))


def even_project(x, g_norm, w_in, b_f, g_q, g_k, pos):
    B, L, _ = x.shape
    h = rms_norm(x, g_norm)
    p = jnp.einsum('bld,dc->blc', h, w_in)
    fq, fk, fv, ff, fg, rq, rk, rv, rg = _split(p, EVEN_COLS)
    fq = rms_norm(fq.reshape(B, L, H_FOX, HEAD_DIM), g_q)
    fk = rms_norm(fk.reshape(B, L, H_FOX, HEAD_DIM), g_k)
    fv = fv.reshape(B, L, H_FOX, HEAD_DIM)
    logf = jax.nn.log_sigmoid((ff + b_f).astype(F32))
    rq = rope(rq.reshape(B, L, H_RET, RET_DK), pos)
    rk = rope(rk.reshape(B, L, H_RET, RET_DK), pos) * (RET_DK ** -0.5)
    rv = rv.reshape(B, L, H_RET, RET_DV)
    return fq, fk, fv, logf, fg, rq, rk, rv, rg


def fox_prompt(q, k, v, logf):
    B, S, H, D = q.shape
    nb = S // Q_BLOCK
    c = jnp.cumsum(logf, axis=1).transpose(0, 2, 1)
    qb = jnp.moveaxis(q.reshape(B, nb, Q_BLOCK, H, D), 1, 0)
    cb = jnp.moveaxis(c.reshape(B, H, nb, Q_BLOCK), 2, 0)
    starts = jnp.arange(nb) * Q_BLOCK
    kpos = jnp.arange(S)

    def one_block(args):
        qi, ci, start = args
        s = jnp.einsum('bqhd,bkhd->bhqk', qi, k, preferred_element_type=F32) * FOX_SCALE
        s = s + ci[..., :, None] - c[:, :, None, :]
        qpos = start + jnp.arange(Q_BLOCK)
        s = jnp.where(kpos[None, :] <= qpos[:, None], s, -jnp.inf)
        p = jax.nn.softmax(s, axis=-1).astype(v.dtype)
        return jnp.einsum('bhqk,bkhd->bqhd', p, v)

    o = lax.map(one_block, (qb, cb, starts))
    return jnp.moveaxis(o, 0, 1).reshape(B, S, H, D)


def fox_sample(q, k_new, v_new, logf_new, k_past, v_past, logf_past):
    P = k_past.shape[1]
    L = q.shape[1]
    k_all = jnp.concatenate([k_past.astype(k_new.dtype), k_new], axis=1)
    v_all = jnp.concatenate([v_past.astype(v_new.dtype), v_new], axis=1)
    lf_all = jnp.concatenate([logf_past.astype(F32), logf_new], axis=1)
    c = jnp.cumsum(lf_all, axis=1).transpose(0, 2, 1)
    cq = c[:, :, P:]
    s = jnp.einsum('bqhd,bkhd->bhqk', q, k_all, preferred_element_type=F32) * FOX_SCALE
    s = s + cq[..., :, None] - c[:, :, None, :]
    kpos = jnp.arange(P + L)
    qpos = P + jnp.arange(L)
    s = jnp.where(kpos[None, :] <= qpos[:, None], s, -jnp.inf)
    p = jax.nn.softmax(s, axis=-1).astype(v_all.dtype)
    return jnp.einsum('bhqk,bkhd->bqhd', p, v_all)


def retention_chunk(S_prev, qkv):
    q, k, v = qkv
    L = q.shape[1]
    lg = ret_log_gamma()
    i = jnp.arange(L, dtype=F32)
    diff = i[:, None] - i[None, :]
    decay = jnp.where(diff[None] >= 0, jnp.exp(lg[:, None, None] * jnp.maximum(diff, 0.0)[None]), 0.0)
    qf, kf, vf = q.astype(F32), k.astype(F32), v.astype(F32)
    scores = jnp.einsum('blhd,bmhd->bhlm', qf, kf) * decay[None]
    intra = jnp.einsum('bhlm,bmhe->blhe', scores, vf)
    cross = jnp.einsum('blhd,bhde->blhe', qf, S_prev) * jnp.exp(lg[None, :] * (i[:, None] + 1.0))[None, :, :, None]
    kdec = kf * jnp.exp(lg[None, :] * (L - 1.0 - i)[:, None])[None, :, :, None]
    S_new = jnp.exp(lg * L)[None, :, None, None] * S_prev + jnp.einsum('blhd,blhe->bhde', kdec, vf)
    return S_new, intra + cross


def retention_prompt(q, k, v):
    B, S = q.shape[:2]
    nc = S // RET_CHUNK

    def chunks(t):
        return jnp.moveaxis(t.reshape((B, nc, RET_CHUNK) + t.shape[2:]), 1, 0)

    S0 = jnp.zeros((B, H_RET, RET_DK, RET_DV), F32)
    s_fin, o = lax.scan(retention_chunk, S0, (chunks(q), chunks(k), chunks(v)))
    return jnp.moveaxis(o, 0, 1).reshape(B, S, H_RET, RET_DV), s_fin


def even_output(x, o_fox, o_ret, fg, rg, g_ret, w_out):
    B, L, _ = x.shape
    a = o_fox.astype(x.dtype).reshape(B, L, W_FOX) * jax.nn.silu(fg)
    r = layer_norm(o_ret.astype(x.dtype), g_ret).reshape(B, L, W_RET) * jax.nn.silu(rg)
    m = jnp.concatenate([a, r], axis=-1)
    return x + jnp.einsum('blc,cd->bld', m, w_out)


def odd_project(x, g_norm, w_in, g_v):
    h = rms_norm(x, g_norm)
    p = jnp.einsum('bld,dc->blc', h, w_in)
    u, v, gate = jnp.split(p, 3, axis=-1)
    return jax.nn.gelu(u), layer_norm(jax.nn.gelu(v), g_v), gate


def chunk_spatial(v, w_s, b_s):
    L = v.shape[1]
    w = jnp.where(jnp.tril(jnp.ones((L, L), dtype=bool))[None], w_s[:, :L, :L], 0.0).astype(v.dtype)
    return jnp.einsum('gts,nsgc->ntgc', w, v) + b_s[:, :L].T.astype(v.dtype)[None, :, :, None]


def odd_output(x, u, sv, gate, w_out):
    return x + jnp.einsum('blc,cd->bld', u * sv * jax.nn.silu(gate), w_out)


def setup_inputs(seed: int = 0) -> dict:
    key = jax.random.key(seed)
    ks = jax.random.split(key, 24)
    n_pages = PAST_LEN // PAGE_SIZE
    n_used = DEC_BATCH * n_pages
    n_phys = n_used + max(1, n_used // 4)
    perm = jax.random.permutation(ks[0], n_phys)
    page_table = perm[:n_used].reshape(DEC_BATCH, n_pages).astype(jnp.int32)
    nrm = jax.random.normal
    return {
        'x_prompt': nrm(ks[1], (BATCH, SEQ, D_MODEL), F32),
        'x_sample': nrm(ks[2], (DEC_BATCH, DEC_SEQ, D_MODEL), F32),
        'cache_k': nrm(ks[3], (N_EVEN, n_phys, PAGE_SIZE, H_FOX, HEAD_DIM), F32),
        'cache_v': nrm(ks[4], (N_EVEN, n_phys, PAGE_SIZE, H_FOX, HEAD_DIM), F32),
        'cache_logf': jax.nn.log_sigmoid(jax.random.uniform(ks[5], (N_EVEN, n_phys, PAGE_SIZE, H_FOX), F32, 1.0, 6.0)),
        'state_ret': 0.5 * nrm(ks[6], (N_EVEN, DEC_BATCH, H_RET, RET_DK, RET_DV), F32),
        'page_table': page_table,
        'norm_even': 1.0 + 0.05 * nrm(ks[7], (N_EVEN, D_MODEL), F32),
        'w_in_even': nrm(ks[8], (N_EVEN, D_MODEL, EVEN_IN), F32) * D_MODEL ** -0.5,
        'b_forget': jax.random.uniform(ks[9], (N_EVEN, H_FOX), F32, 1.0, 6.0),
        'qnorm': 1.0 + 0.05 * nrm(ks[10], (N_EVEN, HEAD_DIM), F32),
        'knorm': 1.0 + 0.05 * nrm(ks[11], (N_EVEN, HEAD_DIM), F32),
        'ret_norm': 1.0 + 0.05 * nrm(ks[12], (N_EVEN, H_RET, RET_DV), F32),
        'w_out_even': nrm(ks[13], (N_EVEN, W_MIX0, D_MODEL), F32) * W_MIX0 ** -0.5,
        'norm_odd': 1.0 + 0.05 * nrm(ks[14], (N_ODD, D_MODEL), F32),
        'w_in_odd': nrm(ks[15], (N_ODD, D_MODEL, 3 * GM_WIDTH), F32) * D_MODEL ** -0.5,
        'vnorm_odd': 1.0 + 0.05 * nrm(ks[16], (N_ODD, GM_WIDTH), F32),
        'w_spatial': nrm(ks[17], (N_ODD, GM_GROUPS, GM_CHUNK, GM_CHUNK), F32) * GM_CHUNK ** -0.5,
        'b_spatial': 1.0 + 0.1 * nrm(ks[18], (N_ODD, GM_GROUPS, GM_CHUNK), F32),
        'w_out_odd': nrm(ks[19], (N_ODD, GM_WIDTH, D_MODEL), F32) * GM_WIDTH ** -0.5,
    }


def reference(x_prompt, x_sample, cache_k, cache_v, cache_logf, state_ret, page_table,
              norm_even, w_in_even, b_forget, qnorm, knorm, ret_norm, w_out_even,
              norm_odd, w_in_odd, vnorm_odd, w_spatial, b_spatial, w_out_odd):
    Bp, S, _ = x_prompt.shape
    Bd, L, _ = x_sample.shape
    n_pages = page_table.shape[1]
    P = n_pages * PAGE_SIZE
    pos_p = jnp.arange(S)
    pos_d = P + jnp.arange(L)
    xp, xd = x_prompt, x_sample
    k_p, v_p, lf_p, sr_p = [], [], [], []
    k_d, v_d, lf_d, sr_d, gv_d = [], [], [], [], []
    for layer in range(DEPTH):
        i = layer // 2
        if layer % 2 == 0:
            fq, fk, fv, logf, fg, rq, rk, rv, rg = even_project(
                xp, norm_even[i], w_in_even[i], b_forget[i], qnorm[i], knorm[i], pos_p)
            o_f = fox_prompt(fq, fk, fv, logf)
            o_r, s_fin = retention_prompt(rq, rk, rv)
            xp = even_output(xp, o_f, o_r, fg, rg, ret_norm[i], w_out_even[i])
            k_p.append(fk)
            v_p.append(fv)
            lf_p.append(logf.astype(x_prompt.dtype))
            sr_p.append(s_fin.astype(x_prompt.dtype))
            fq, fk, fv, logf, fg, rq, rk, rv, rg = even_project(
                xd, norm_even[i], w_in_even[i], b_forget[i], qnorm[i], knorm[i], pos_d)
            kpast = cache_k[i][page_table].reshape(Bd, P, H_FOX, HEAD_DIM)
            vpast = cache_v[i][page_table].reshape(Bd, P, H_FOX, HEAD_DIM)
            lfpast = cache_logf[i][page_table].reshape(Bd, P, H_FOX)
            o_f = fox_sample(fq, fk, fv, logf, kpast, vpast, lfpast)
            s_new, o_r = retention_chunk(state_ret[i].astype(F32), (rq, rk, rv))
            xd = even_output(xd, o_f, o_r, fg, rg, ret_norm[i], w_out_even[i])
            k_d.append(fk)
            v_d.append(fv)
            lf_d.append(logf.astype(x_sample.dtype))
            sr_d.append(s_new.astype(x_sample.dtype))
        else:
            u, v, gate = odd_project(xp, norm_odd[i], w_in_odd[i], vnorm_odd[i])
            sv = chunk_spatial(v.reshape(Bp * (S // GM_CHUNK), GM_CHUNK, GM_GROUPS, GM_GC),
                               w_spatial[i], b_spatial[i]).reshape(Bp, S, GM_WIDTH)
            xp = odd_output(xp, u, sv, gate, w_out_odd[i])
            u, v, gate = odd_project(xd, norm_odd[i], w_in_odd[i], vnorm_odd[i])
            sv = chunk_spatial(v.reshape(Bd, L, GM_GROUPS, GM_GC),
                               w_spatial[i], b_spatial[i]).reshape(Bd, L, GM_WIDTH)
            xd = odd_output(xd, u, sv, gate, w_out_odd[i])
            gv_d.append(v)
    return (xp, xd, jnp.stack(k_p), jnp.stack(v_p), jnp.stack(lf_p), jnp.stack(sr_p),
            jnp.stack(k_d), jnp.stack(v_d), jnp.stack(lf_d), jnp.stack(sr_d), jnp.stack(gv_d))
```

```python
import functools

import numpy as np
import jax
import jax.numpy as jnp
from jax import lax
from jax.experimental import pallas as pl
from jax.experimental.pallas import tpu as pltpu

D_MODEL = 1024
HEAD_DIM = 64
H_FOX = 8
W_FOX = H_FOX * HEAD_DIM
H_RET = 8
RET_DK = 64
RET_DV = 128
W_RET = H_RET * RET_DV
W_RQK = H_RET * RET_DK
ROPE_BASE = 10000.0
GM_WIDTH = 2 * D_MODEL
GM_GROUPS = 8
GM_GC = GM_WIDTH // GM_GROUPS
CHUNK = 128
PAGE = 128
EPS = 1e-6
F32 = jnp.float32
BF16 = jnp.bfloat16

LANES = 128
VMEM_LIMIT = 56 * 1024 * 1024

C_FQ, C_FK, C_FV, C_FG = 0, 512, 1024, 1536
C_RQ, C_RK, C_RV, C_RG, C_FF = 2048, 2560, 3072, 4096, 5120
EVEN_COLS_PAD = C_FF + LANES

LOG_GAMMA = [float(np.log(np.float32(1.0) - np.float32(2.0) ** np.float32(-5.0 - h))) for h in range(H_RET)]

NT_DIMS = (((1,), (1,)), ((), ()))
TN_DIMS = (((0,), (0,)), ((), ()))


def _params(sem):
    return pltpu.CompilerParams(dimension_semantics=sem, vmem_limit_bytes=VMEM_LIMIT)


def _const_spec(shape):
    nd = len(shape)
    return pl.BlockSpec(shape, lambda *_: (0,) * nd, pipeline_mode=pl.Buffered(1))


def _rms(x, g):
    ms = jnp.mean(x * x, axis=-1, keepdims=True)
    return x * lax.rsqrt(ms + EPS) * g


def _layer_norm(x, g):
    xc = x - jnp.mean(x, axis=-1, keepdims=True)
    return xc * lax.rsqrt(jnp.mean(xc * xc, axis=-1, keepdims=True) + EPS) * g


def _head_rms(x, g):
    rows = x.shape[0]
    low = lax.broadcasted_iota(jnp.int32, (rows, LANES), 1) < HEAD_DIM
    outs = []
    for c in range(x.shape[1] // LANES):
        xc = x[:, c * LANES:(c + 1) * LANES]
        sq = xc * xc
        s_lo = jnp.sum(jnp.where(low, sq, 0.0), axis=-1, keepdims=True)
        s_hi = jnp.sum(jnp.where(low, 0.0, sq), axis=-1, keepdims=True)
        ms = jnp.where(low, s_lo, s_hi) * (1.0 / HEAD_DIM)
        outs.append(xc * lax.rsqrt(ms + EPS))
    return jnp.concatenate(outs, axis=-1) * g


def _rope(x, cos, sin_signed):
    rows = x.shape[0]
    first_half = (lax.broadcasted_iota(jnp.int32, (rows, LANES), 1) % RET_DK) < RET_DK // 2
    outs = []
    for c in range(x.shape[1] // LANES):
        xc = x[:, c * LANES:(c + 1) * LANES]
        nxt = pltpu.roll(xc, LANES - RET_DK // 2, axis=1)
        prv = pltpu.roll(xc, RET_DK // 2, axis=1)
        outs.append(xc * cos + jnp.where(first_half, nxt, prv) * sin_signed)
    return jnp.concatenate(outs, axis=-1)


def _log_sigmoid(x):
    return jnp.minimum(x, 0.0) - jnp.log1p(jnp.exp(-jnp.abs(x)))


def _even_proj_kernel(x_ref, g_ref, w_ref, bf_ref, gq_ref, gk_ref, cos_ref, sin_ref,
                      q_ref, k32_ref, kb_ref, v32_ref, vb_ref, lf_ref, sfg_ref,
                      rq_ref, rk_ref, rv_ref, srg_ref):
    h = _rms(x_ref[...], g_ref[...]).astype(BF16)

    def proj(lo, hi):
        return jnp.dot(h, w_ref[:, lo:hi], preferred_element_type=F32)

    q_ref[...] = (_head_rms(proj(C_FQ, C_FK), gq_ref[...]) * (HEAD_DIM ** -0.5)).astype(BF16)
    k = _head_rms(proj(C_FK, C_FV), gk_ref[...])
    k32_ref[...] = k
    kb_ref[...] = k.astype(BF16)
    v = proj(C_FV, C_FG)
    v32_ref[...] = v
    vb_ref[...] = v.astype(BF16)
    sfg_ref[...] = jax.nn.silu(proj(C_FG, C_RQ)).astype(BF16)
    cos = cos_ref[...]
    sin = sin_ref[...]
    rq_ref[...] = _rope(proj(C_RQ, C_RK), cos, sin).astype(BF16)
    rk_ref[...] = (_rope(proj(C_RK, C_RV), cos, sin) * (RET_DK ** -0.5)).astype(BF16)
    rv_ref[...] = proj(C_RV, C_RG).astype(BF16)
    srg_ref[...] = jax.nn.silu(proj(C_RG, C_FF)).astype(BF16)
    lf = _log_sigmoid(proj(C_FF, EVEN_COLS_PAD) + bf_ref[...])
    lf_ref[...] = lf[:, :H_FOX]


def _even_proj(x2d, g_norm, w, b_f, g_q, g_k, cos, sin, tm):
    m = x2d.shape[0]
    row = lambda n: pl.BlockSpec((tm, n), lambda i: (i, 0))
    tab_blocks = cos.shape[0] // tm
    tab = pl.BlockSpec((tm, LANES), lambda i: (i % tab_blocks, 0))
    outs = [
        (W_FOX, BF16), (W_FOX, F32), (W_FOX, BF16), (W_FOX, F32), (W_FOX, BF16), (H_FOX, F32),
        (W_FOX, BF16), (W_RQK, BF16), (W_RQK, BF16), (W_RET, BF16), (W_RET, BF16),
    ]
    return pl.pallas_call(
        _even_proj_kernel,
        grid=(m // tm,),
        in_specs=[row(D_MODEL), _const_spec((1, D_MODEL)), _const_spec((D_MODEL, EVEN_COLS_PAD)),
                  _const_spec((1, LANES)), _const_spec((1, W_FOX)), _const_spec((1, W_FOX)),
                  tab, tab],
        out_specs=[row(n) for n, _ in outs],
        out_shape=[jax.ShapeDtypeStruct((m, n), dt) for n, dt in outs],
        compiler_params=_params(("parallel",)),
        name="even_proj",
    )(x2d, g_norm, w, b_f, g_q, g_k, cos, sin)


def _cumsum_kernel(x_ref, o_ref):
    x = x_ref[0]
    n = x.shape[1]
    lane = lax.broadcasted_iota(jnp.int32, x.shape, 1)
    sh = 1
    while sh < n:
        x = x + jnp.where(lane >= sh, pltpu.roll(x, sh, axis=1), 0.0)
        sh *= 2
    o_ref[0] = x


def _cumsum_lanes(x):
    b, h, s = x.shape
    spec = pl.BlockSpec((1, h, s), lambda i: (i, 0, 0))
    return pl.pallas_call(
        _cumsum_kernel, grid=(b,), in_specs=[spec], out_specs=spec,
        out_shape=jax.ShapeDtypeStruct(x.shape, F32),
        compiler_params=_params(("parallel",)), name="logf_cumsum",
    )(x)


HEADS_PER_GROUP = 4
GROUP_W = HEADS_PER_GROUP * HEAD_DIM


def _fox_prompt_kernel(q_ref, k_ref, v_ref, crow_ref, ccol_ref, sfg_ref, a_ref, *, blk):
    i = pl.program_id(2)
    q = q_ref[0]
    lane_head = lax.broadcasted_iota(jnp.int32, (blk, GROUP_W), 1) // HEAD_DIM
    ccol = ccol_ref[0, 0]
    causal = (lax.broadcasted_iota(jnp.int32, (blk, blk), 0)
              >= lax.broadcasted_iota(jnp.int32, (blk, blk), 1))
    o_all = jnp.zeros((blk, GROUP_W), F32)
    for hh in range(HEADS_PER_GROUP):
        qm = jnp.where(lane_head == hh, q, jnp.zeros_like(q))
        cq = ccol[:, hh:hh + 1]

        def step(j, carry, masked, qm=qm, cq=cq, hh=hh):
            m, l, acc = carry
            off = pl.multiple_of(j * blk, blk)
            kb = k_ref[0, pl.ds(off, blk), :]
            vb = v_ref[0, pl.ds(off, blk), :]
            s = lax.dot_general(qm, kb, NT_DIMS, preferred_element_type=F32)
            s = s + (cq - crow_ref[0, 0, pl.ds(hh, 1), pl.ds(off, blk)])
            if masked:
                s = jnp.where(causal, s, -jnp.inf)
            m_new = jnp.maximum(m, jnp.max(s, axis=-1, keepdims=True))
            alpha = jnp.exp(m - m_new)
            p = jnp.exp(s - m_new)
            l = alpha * l + jnp.sum(p, axis=-1, keepdims=True)
            acc = alpha * acc + jnp.dot(p.astype(BF16), vb, preferred_element_type=F32)
            return m_new, l, acc

        init = (jnp.full((blk, 1), -1e30, F32), jnp.zeros((blk, 1), F32), jnp.zeros((blk, GROUP_W), F32))
        carry = lax.fori_loop(0, i, functools.partial(step, masked=False), init)
        _, l, acc = step(i, carry, True)
        o_all = jnp.where(lane_head == hh, acc / l, o_all)
    a_ref[0] = (o_all * sfg_ref[0].astype(F32)).astype(BF16)


def _fox_prompt(q, kb, vb, crow, ccol, sfg, blk):
    b, s, _ = q.shape
    ng = H_FOX // HEADS_PER_GROUP
    qspec = pl.BlockSpec((1, blk, GROUP_W), lambda bi, g, i: (bi, i, g))
    kvspec = pl.BlockSpec((1, s, GROUP_W), lambda bi, g, i: (bi, 0, g))
    return pl.pallas_call(
        functools.partial(_fox_prompt_kernel, blk=blk),
        grid=(b, ng, s // blk),
        in_specs=[qspec, kvspec, kvspec,
                  pl.BlockSpec((1, 1, HEADS_PER_GROUP, s), lambda bi, g, i: (bi, g, 0, 0)),
                  pl.BlockSpec((1, 1, blk, HEADS_PER_GROUP), lambda bi, g, i: (bi, g, i, 0)),
                  qspec],
        out_specs=qspec,
        out_shape=jax.ShapeDtypeStruct((b, s, W_FOX), BF16),
        compiler_params=_params(("parallel", "parallel", "arbitrary")),
        name="fox_prompt",
    )(q, kb, vb, crow, ccol, sfg)


def _ret_head_out(o, g, srg):
    return (_layer_norm(o, g) * srg.astype(F32)).astype(BF16)


def _ret_prompt_kernel(rq_ref, rk_ref, rv_ref, srg_ref, gret_ref, r_ref, st_ref, dec_ref):
    L = CHUNK
    c = pl.program_id(1)

    @pl.when(c == 0)
    def _():
        st_ref[...] = jnp.zeros(st_ref.shape, F32)
        diff = (lax.broadcasted_iota(jnp.int32, (L, L), 0)
                - lax.broadcasted_iota(jnp.int32, (L, L), 1)).astype(F32)
        for h in range(H_RET):
            dec_ref[h] = jnp.where(diff >= 0, jnp.exp(LOG_GAMMA[h] * jnp.maximum(diff, 0.0)), 0.0)

    pos = lax.broadcasted_iota(jnp.int32, (L, 1), 0).astype(F32)
    lane_half = lax.broadcasted_iota(jnp.int32, (L, LANES), 1) // RET_DK
    for pp in range(H_RET // 2):
        qp = rq_ref[0, :, pp * LANES:(pp + 1) * LANES]
        kp = rk_ref[0, :, pp * LANES:(pp + 1) * LANES]
        s_pair = st_ref[0, pp]
        s_pair_b = s_pair.astype(BF16)
        for j in range(2):
            h = 2 * pp + j
            lg = LOG_GAMMA[h]
            qm = jnp.where(lane_half == j, qp, jnp.zeros_like(qp))
            vh = rv_ref[0, :, h * RET_DV:(h + 1) * RET_DV]
            scores = lax.dot_general(qm, kp, NT_DIMS, preferred_element_type=F32) * dec_ref[h]
            intra = jnp.dot(scores.astype(BF16), vh, preferred_element_type=F32)
            cross = jnp.dot(qm, s_pair_b, preferred_element_type=F32) * jnp.exp(lg * (pos + 1.0))
            kd = (kp.astype(F32) * jnp.exp(lg * (L - 1.0 - pos))).astype(BF16)
            upd = lax.dot_general(kd, vh, TN_DIMS, preferred_element_type=F32)
            rows = slice(j * RET_DK, (j + 1) * RET_DK)
            st_ref[0, pp, rows, :] = float(np.exp(lg * L)) * s_pair[rows] + upd[rows]
            cols = slice(h * RET_DV, (h + 1) * RET_DV)
            r_ref[0, :, cols] = _ret_head_out(intra + cross, gret_ref[:, cols], srg_ref[0, :, cols])


def _ret_prompt(rq, rk, rv, srg, g_ret):
    b, s, _ = rq.shape
    qk = pl.BlockSpec((1, CHUNK, W_RQK), lambda bi, c: (bi, c, 0))
    wide = pl.BlockSpec((1, CHUNK, W_RET), lambda bi, c: (bi, c, 0))
    st = pl.BlockSpec((1, H_RET // 2, 2 * RET_DK, RET_DV), lambda bi, c: (bi, 0, 0, 0))
    return pl.pallas_call(
        _ret_prompt_kernel,
        grid=(b, s // CHUNK),
        in_specs=[qk, qk, wide, wide, _const_spec((1, W_RET))],
        out_specs=[wide, st],
        out_shape=[jax.ShapeDtypeStruct((b, s, W_RET), BF16),
                   jax.ShapeDtypeStruct((b, H_RET // 2, 2 * RET_DK, RET_DV), F32)],
        scratch_shapes=[pltpu.VMEM((H_RET, CHUNK, CHUNK), F32)],
        compiler_params=_params(("parallel", "arbitrary")),
        name="ret_prompt",
    )(rq, rk, rv, srg, g_ret)


def _ret_sample_kernel(rq_ref, rk_ref, rv_ref, srg_ref, gret_ref, s0_ref, r_ref, s1_ref, *, n_new):
    rows = rq_ref.shape[0]
    nb = rows // n_new
    two_dk = 2 * RET_DK
    ri = lax.broadcasted_iota(jnp.int32, (rows, rows), 0)
    ci = lax.broadcasted_iota(jnp.int32, (rows, rows), 1)
    same = (ri // n_new) == (ci // n_new)
    diff = ((ri % n_new) - (ci % n_new)).astype(F32)
    pos = (lax.broadcasted_iota(jnp.int32, (rows, 1), 0) % n_new).astype(F32)
    lane_half = lax.broadcasted_iota(jnp.int32, (rows, LANES), 1) // RET_DK
    q_sel = (lax.broadcasted_iota(jnp.int32, (rows, nb * two_dk), 1) // two_dk
             == lax.broadcasted_iota(jnp.int32, (rows, nb * two_dk), 0) // n_new)
    k_sel = (lax.broadcasted_iota(jnp.int32, (nb * two_dk, rows), 0) // two_dk
             == lax.broadcasted_iota(jnp.int32, (nb * two_dk, rows), 1) // n_new)
    row_head = (lax.broadcasted_iota(jnp.int32, (nb * two_dk, 1), 0) % two_dk) // RET_DK

    qp = rq_ref[...]
    kp = rk_ref[...]
    s_old = s0_ref[...].reshape(nb * two_dk, RET_DV)
    s_old_b = s_old.astype(BF16)
    upd = []
    for j in range(2):
        h = 2 * pl.program_id(0) + j
        lg = jnp.where(h == 0, LOG_GAMMA[0], 0.0)
        for hh in range(1, H_RET):
            lg = jnp.where(h == hh, LOG_GAMMA[hh], lg)
        qm = jnp.where(lane_half == j, qp, jnp.zeros_like(qp))
        vh = rv_ref[:, j * RET_DV:(j + 1) * RET_DV]
        decay = jnp.where(same & (diff >= 0), jnp.exp(lg * jnp.maximum(diff, 0.0)), 0.0)
        scores = lax.dot_general(qm, kp, NT_DIMS, preferred_element_type=F32) * decay
        intra = jnp.dot(scores.astype(BF16), vh, preferred_element_type=F32)
        q_exp = jnp.where(q_sel, jnp.tile(qm, (1, nb)), jnp.zeros((), BF16))
        cross = jnp.dot(q_exp, s_old_b, preferred_element_type=F32) * jnp.exp(lg * (pos + 1.0))
        kd = kp.astype(F32) * jnp.exp(lg * (n_new - 1.0 - pos))
        kd_exp = jnp.where(k_sel, jnp.tile(kd.T, (nb, 1)), 0.0).astype(BF16)
        keep = jnp.exp(lg * jnp.full((1, 1), float(n_new), F32))
        upd.append((keep, jnp.dot(kd_exp, vh, preferred_element_type=F32)))
        cols = slice(j * RET_DV, (j + 1) * RET_DV)
        r_ref[:, cols] = _ret_head_out(intra + cross, gret_ref[:, cols], srg_ref[:, cols])
    s_new = jnp.where(row_head == 0, upd[0][0] * s_old + upd[0][1], upd[1][0] * s_old + upd[1][1])
    s1_ref[...] = s_new.reshape(s1_ref.shape)


def _ret_sample(rq, rk, rv, srg, g_ret, state, n_new):
    rows = rq.shape[0]
    nb = rows // n_new
    pair = pl.BlockSpec((rows, LANES), lambda p: (0, p))
    wide = pl.BlockSpec((rows, 2 * RET_DV), lambda p: (0, p))
    st = pl.BlockSpec((nb, 2, RET_DK, RET_DV), lambda p: (0, p, 0, 0))
    return pl.pallas_call(
        functools.partial(_ret_sample_kernel, n_new=n_new),
        grid=(H_RET // 2,),
        in_specs=[pair, pair, wide, wide, pl.BlockSpec((1, 2 * RET_DV), lambda p: (0, p)), st],
        out_specs=[wide, st],
        out_shape=[jax.ShapeDtypeStruct((rows, W_RET), BF16),
                   jax.ShapeDtypeStruct(state.shape, F32)],
        compiler_params=_params(("parallel",)),
        name="ret_sample",
    )(rq, rk, rv, srg, g_ret, state)


def _suffix_sum_lanes(x):
    n = x.shape[1]
    lane = lax.broadcasted_iota(jnp.int32, x.shape, 1)
    incl = x
    sh = 1
    while sh < n:
        incl = incl + jnp.where(lane < n - sh, pltpu.roll(incl, n - sh, axis=1), 0.0)
        sh *= 2
    return incl - x


def _fox_sample_kernel(pt_ref, q_ref, kn_ref, vn_ref, lfn_ref, sfg_ref, *refs, n_new, pages_per_step):
    g = pages_per_step
    k_refs, v_refs, lf_refs = refs[:g], refs[g:2 * g], refs[2 * g:3 * g]
    a_ref, qbd_ref, m_ref, l_ref, acc_ref, tail_ref, pfx_ref, kn_scr, vn_scr = refs[3 * g:]
    c = pl.program_id(1)
    nq = n_new * H_FOX
    row_tok = lax.broadcasted_iota(jnp.int32, (nq, 1), 0) // H_FOX

    def update(s, v):
        m_new = jnp.maximum(m_ref[...], jnp.max(s, axis=-1, keepdims=True))
        alpha = jnp.exp(m_ref[...] - m_new)
        p = jnp.exp(s - m_new)
        l_ref[...] = alpha * l_ref[...] + jnp.sum(p, axis=-1, keepdims=True)
        acc_ref[...] = alpha * acc_ref[...] + jnp.dot(p.astype(BF16), v, preferred_element_type=F32)
        m_ref[...] = m_new

    @pl.when(c == 0)
    def _():
        head_of_lane = lax.broadcasted_iota(jnp.int32, (H_FOX, W_FOX), 1) // HEAD_DIM
        head_of_row = lax.broadcasted_iota(jnp.int32, (H_FOX, W_FOX), 0)
        q = q_ref[0].astype(F32)
        for t in range(n_new):
            qbd_ref[t * H_FOX:(t + 1) * H_FOX, :] = jnp.where(
                head_of_lane == head_of_row, jnp.broadcast_to(q[t:t + 1, :], (H_FOX, W_FOX)), 0.0).astype(BF16)
        lfn = lfn_ref[0]
        lane = lax.broadcasted_iota(jnp.int32, (H_FOX, LANES), 1)
        pfx = jnp.zeros((H_FOX, LANES), F32)
        run = jnp.zeros((H_FOX, 1), F32)
        for t in range(n_new):
            run = run + lfn[:, t:t + 1]
            pfx = jnp.where(lane == t, run, pfx)
            pfx_ref[t * H_FOX:(t + 1) * H_FOX, :] = jnp.broadcast_to(run, (H_FOX, LANES))
        tail_ref[...] = jnp.zeros(tail_ref.shape, F32)
        m_ref[...] = jnp.full(m_ref.shape, -1e30, F32)
        l_ref[...] = jnp.zeros(l_ref.shape, F32)
        acc_ref[...] = jnp.zeros(acc_ref.shape, F32)
        kn_scr[...] = jnp.zeros(kn_scr.shape, F32)
        vn_scr[...] = jnp.zeros(vn_scr.shape, F32)
        kn_scr[0:n_new, :] = kn_ref[0]
        vn_scr[0:n_new, :] = vn_ref[0]
        kn = kn_scr[...].astype(BF16)
        vn = vn_scr[...].astype(BF16)
        s = lax.dot_general(qbd_ref[...], kn, NT_DIMS, preferred_element_type=F32)
        key = lax.broadcasted_iota(jnp.int32, (nq, PAGE), 1)
        bias = pfx_ref[...] - jnp.tile(pfx, (n_new, 1))
        update(jnp.where(key <= row_tok, s + bias, -jnp.inf), vn)

    kc = jnp.concatenate([r[0].astype(BF16) for r in k_refs], axis=0)
    vc = jnp.concatenate([r[0].astype(BF16) for r in v_refs], axis=0)
    lf = jnp.concatenate([r[0] for r in lf_refs], axis=1)
    key_bias = tail_ref[:, 0:1] + _suffix_sum_lanes(lf)
    tail_ref[...] = tail_ref[...] + jnp.sum(lf, axis=-1, keepdims=True)
    s = lax.dot_general(qbd_ref[...], kc, NT_DIMS, preferred_element_type=F32)
    update(s + pfx_ref[:, 0:1] + jnp.tile(key_bias, (n_new, 1)), vc)

    @pl.when(c == pl.num_programs(1) - 1)
    def _():
        o = acc_ref[...] / l_ref[...]
        head_of_lane = lax.broadcasted_iota(jnp.int32, (H_FOX, W_FOX), 1) // HEAD_DIM
        head_of_row = lax.broadcasted_iota(jnp.int32, (H_FOX, W_FOX), 0)
        sfg = sfg_ref[0].astype(F32)
        for t in range(n_new):
            ot = jnp.where(head_of_lane == head_of_row, o[t * H_FOX:(t + 1) * H_FOX, :], 0.0)
            a_ref[0, t:t + 1, :] = jnp.sum(ot, axis=0, keepdims=True) * sfg[t:t + 1, :]


def _fox_sample(page_table, q, k_new, v_new, lf_new_t, sfg, cache_k, cache_v, cache_lf_t, pages_per_step):
    bd, n_new, _ = q.shape
    n_pages = page_table.shape[1]
    g = pages_per_step
    n_steps = n_pages // g
    nq = n_new * H_FOX

    def tok(n):
        return pl.BlockSpec((1, n_new, n), lambda b, c, pt: (b, 0, 0))

    def page_spec(shape, slot):
        return pl.BlockSpec((1,) + shape, lambda b, c, pt: (pt[b, n_pages - (c + 1) * g + slot], 0, 0))

    in_specs = [tok(W_FOX), tok(W_FOX), tok(W_FOX),
                pl.BlockSpec((1, H_FOX, n_new), lambda b, c, pt: (b, 0, 0)), tok(W_FOX)]
    in_specs += [page_spec((PAGE, W_FOX), s) for s in range(g)]
    in_specs += [page_spec((PAGE, W_FOX), s) for s in range(g)]
    in_specs += [page_spec((H_FOX, PAGE), s) for s in range(g)]
    grid_spec = pltpu.PrefetchScalarGridSpec(
        num_scalar_prefetch=1, grid=(bd, n_steps), in_specs=in_specs, out_specs=tok(W_FOX),
        scratch_shapes=[pltpu.VMEM((nq, W_FOX), BF16), pltpu.VMEM((nq, 1), F32), pltpu.VMEM((nq, 1), F32),
                        pltpu.VMEM((nq, W_FOX), F32), pltpu.VMEM((H_FOX, LANES), F32),
                        pltpu.VMEM((nq, LANES), F32), pltpu.VMEM((PAGE, W_FOX), F32),
                        pltpu.VMEM((PAGE, W_FOX), F32)])
    return pl.pallas_call(
        functools.partial(_fox_sample_kernel, n_new=n_new, pages_per_step=g),
        grid_spec=grid_spec,
        out_shape=jax.ShapeDtypeStruct((bd, n_new, W_FOX), F32),
        compiler_params=_params(("parallel", "arbitrary")),
        name="fox_sample",
    )(page_table, q, k_new, v_new, lf_new_t, sfg, *([cache_k] * g), *([cache_v] * g), *([cache_lf_t] * g))


def _tail_kernel(a_ref, r_ref, x_ref, woe_ref, gno_ref, wio_ref, gv_ref, ws_ref, bs_ref, woo_ref,
                 *out_refs, n_new):
    rows = x_ref.shape[0]
    x1 = (x_ref[...]
          + jnp.dot(a_ref[...].astype(BF16), woe_ref[:W_FOX, :], preferred_element_type=F32)
          + jnp.dot(r_ref[...], woe_ref[W_FOX:, :], preferred_element_type=F32))
    h = _rms(x1, gno_ref[...]).astype(BF16)
    u = jax.nn.gelu(jnp.dot(h, wio_ref[:, :GM_WIDTH], preferred_element_type=F32))
    v = jax.nn.gelu(jnp.dot(h, wio_ref[:, GM_WIDTH:2 * GM_WIDTH], preferred_element_type=F32))
    gate = jnp.dot(h, wio_ref[:, 2 * GM_WIDTH:], preferred_element_type=F32)
    vn = _layer_norm(v, gv_ref[...])
    if n_new is None:
        y_ref, = out_refs
        tril = (lax.broadcasted_iota(jnp.int32, (CHUNK, CHUNK), 0)
                >= lax.broadcasted_iota(jnp.int32, (CHUNK, CHUNK), 1))
        vb = vn.astype(BF16)
        cols = []
        for g in range(GM_GROUPS):
            wm = jnp.where(tril, ws_ref[g], 0.0).astype(BF16)
            bias = bs_ref[:, g:g + 1]
            parts = [jnp.dot(wm, vb[c * CHUNK:(c + 1) * CHUNK, g * GM_GC:(g + 1) * GM_GC],
                             preferred_element_type=F32) + bias for c in range(rows // CHUNK)]
            cols.append(jnp.concatenate(parts, axis=0))
        sv = jnp.concatenate(cols, axis=1)
    else:
        y_ref, vn_ref = out_refs
        vn_ref[...] = vn
        tok = lax.broadcasted_iota(jnp.int32, (rows, 1), 0) % n_new
        cols = []
        for g in range(GM_GROUPS):
            vg = vn[:, g * GM_GC:(g + 1) * GM_GC]
            acc = jnp.zeros((rows, 1), F32)
            for t in range(n_new):
                acc = jnp.where(tok == t, bs_ref[g, t], acc)
            acc = jnp.broadcast_to(acc, (rows, GM_GC))
            for d in range(n_new):
                coef = jnp.zeros((rows, 1), F32)
                for t in range(d, n_new):
                    coef = jnp.where(tok == t, ws_ref[g, t * n_new + (t - d)], coef)
                src = vg if d == 0 else pltpu.roll(vg, d, axis=0)
                acc = acc + coef * src
            cols.append(acc)
        sv = jnp.concatenate(cols, axis=1)
    z = (u * sv * jax.nn.silu(gate)).astype(BF16)
    y_ref[...] = x1 + jnp.dot(z, woo_ref[...], preferred_element_type=F32)


def _tail(a, r, x2d, woe, gno, wio, gv, ws, bs, woo, tm, n_new):
    m = x2d.shape[0]
    row = lambda n: pl.BlockSpec((tm, n), lambda i: (i, 0))
    if n_new is None:
        mix_specs = [_const_spec(ws.shape), _const_spec(bs.shape)]
        out_specs = [row(D_MODEL)]
        out_shape = [jax.ShapeDtypeStruct((m, D_MODEL), F32)]
    else:
        smem = pl.BlockSpec(memory_space=pltpu.SMEM)
        mix_specs = [smem, smem]
        out_specs = [row(D_MODEL), row(GM_WIDTH)]
        out_shape = [jax.ShapeDtypeStruct((m, D_MODEL), F32), jax.ShapeDtypeStruct((m, GM_WIDTH), F32)]
    return pl.pallas_call(
        functools.partial(_tail_kernel, n_new=n_new),
        grid=(m // tm,),
        in_specs=[row(W_FOX), row(W_RET), row(D_MODEL), _const_spec(woe.shape), _const_spec((1, D_MODEL)),
                  _const_spec(wio.shape), _const_spec((1, GM_WIDTH))] + mix_specs + [_const_spec(woo.shape)],
        out_specs=out_specs, out_shape=out_shape,
        compiler_params=_params(("parallel",)),
        name="tail_prompt" if n_new is None else "tail_sample",
    )(a, r, x2d, woe, gno, wio, gv, ws, bs, woo)


def _rope_tables(pos):
    half = RET_DK // 2
    inv = ROPE_BASE ** (-jnp.arange(half, dtype=F32) / half)
    ang = pos.astype(F32)[:, None] * inv[None, :]
    cos, sin = jnp.cos(ang), jnp.sin(ang)
    return jnp.tile(jnp.concatenate([cos, cos], axis=-1), (1, 2)), jnp.tile(jnp.concatenate([-sin, sin], axis=-1), (1, 2))


def _row_tile(m):
    return 256 if m % 256 == 0 else 128


def kernel(x_prompt, x_sample, cache_k, cache_v, cache_logf, state_ret, page_table, norm_even, w_in_even,
           b_forget, qnorm, knorm, ret_norm, w_out_even, norm_odd, w_in_odd, vnorm_odd, w_spatial, b_spatial,
           w_out_odd):
    bp, s, _ = x_prompt.shape
    bd, n_new, _ = x_sample.shape
    n_pages = page_table.shape[1]
    past = n_pages * PAGE
    assert s % 256 == 0 and (bd * n_new) % CHUNK == 0 and w_in_even.shape[0] == 1 and w_in_odd.shape[0] == 1

    wi = w_in_even[0]
    sizes = (W_FOX, W_FOX, W_FOX, H_FOX, W_FOX, W_RQK, W_RQK, W_RET, W_RET)
    offs = np.concatenate([[0], np.cumsum(sizes)])
    fq, fk, fv, ff, fg, rq, rk, rv, rg = [wi[:, offs[i]:offs[i + 1]] for i in range(9)]
    w_even = jnp.concatenate([fq, fk, fv, fg, rq, rk, rv, rg, jnp.pad(ff, ((0, 0), (0, LANES - H_FOX)))],
                             axis=1).astype(BF16)
    b_f = jnp.pad(b_forget[0], (0, LANES - H_FOX))[None, :]
    g_q = jnp.tile(qnorm[0], H_FOX)[None, :]
    g_k = jnp.tile(knorm[0], H_FOX)[None, :]
    g_ret = ret_norm[0].reshape(1, W_RET)
    woe = w_out_even[0].astype(BF16)
    wio = w_in_odd[0].astype(BF16)
    woo = w_out_odd[0].astype(BF16)
    g_ne, g_no, g_v = norm_even[0][None, :], norm_odd[0][None, :], vnorm_odd[0][None, :]

    cos_p, sin_p = _rope_tables(jnp.arange(s))
    xp2 = x_prompt.reshape(bp * s, D_MODEL)
    tm = _row_tile(s)
    q, k32, kb, v32, vb, lf, sfg, rqp, rkp, rvp, srg = _even_proj(
        xp2, g_ne, w_even, b_f, g_q, g_k, cos_p, sin_p, tm)
    lf3 = lf.reshape(bp, s, H_FOX)
    c_row = _cumsum_lanes(jnp.transpose(lf3, (0, 2, 1)))
    ng = H_FOX // HEADS_PER_GROUP
    c_row4 = c_row.reshape(bp, ng, HEADS_PER_GROUP, s)
    c_col4 = jnp.transpose(c_row4, (0, 1, 3, 2))
    r3 = lambda t: t.reshape(bp, s, t.shape[-1])
    a_p = _fox_prompt(r3(q), r3(kb), r3(vb), c_row4, c_col4, r3(sfg), 256)
    r_p, st_p = _ret_prompt(r3(rqp), r3(rkp), r3(rvp), r3(srg), g_ret)
    y_p, = _tail(a_p.reshape(bp * s, W_FOX), r_p.reshape(bp * s, W_RET), xp2, woe, g_no, wio, g_v,
                 w_spatial[0], jnp.transpose(b_spatial[0]), woo, tm, None)

    rows = bd * n_new
    cos_d, sin_d = _rope_tables(past + jnp.arange(n_new))
    xd2 = x_sample.reshape(rows, D_MODEL)
    qd, k32d, _, v32d, _, lfd, sfgd, rqd, rkd, rvd, srgd = _even_proj(
        xd2, g_ne, w_even, b_f, g_q, g_k, jnp.tile(cos_d, (bd, 1)), jnp.tile(sin_d, (bd, 1)), rows)
    d3 = lambda t: t.reshape(bd, n_new, t.shape[-1])
    n_phys = cache_k.shape[1]
    a_d = _fox_sample(page_table, d3(qd), d3(k32d), d3(v32d), jnp.transpose(d3(lfd), (0, 2, 1)), d3(sfgd),
                      cache_k[0].reshape(n_phys, PAGE, W_FOX), cache_v[0].reshape(n_phys, PAGE, W_FOX),
                      jnp.transpose(cache_logf[0], (0, 2, 1)), 8 if n_pages % 8 == 0 else 1)
    r_d, st_d = _ret_sample(rqd, rkd, rvd, srgd, g_ret, state_ret[0], n_new)
    ws_new = w_spatial[0][:, :n_new, :n_new].reshape(GM_GROUPS, n_new * n_new)
    y_d, gv_d = _tail(a_d.reshape(rows, W_FOX), r_d, xd2, woe, g_no, wio, g_v,
                      ws_new, b_spatial[0][:, :n_new], woo, rows, n_new)

    return (y_p.reshape(bp, s, D_MODEL), y_d.reshape(bd, n_new, D_MODEL),
            k32.reshape(1, bp, s, H_FOX, HEAD_DIM), v32.reshape(1, bp, s, H_FOX, HEAD_DIM),
            lf3[None], st_p.reshape(1, bp, H_RET, RET_DK, RET_DV),
            k32d.reshape(1, bd, n_new, H_FOX, HEAD_DIM), v32d.reshape(1, bd, n_new, H_FOX, HEAD_DIM),
            lfd.reshape(1, bd, n_new, H_FOX), st_d[None], gv_d.reshape(1, bd, n_new, GM_WIDTH))
```

```python
import functools

import numpy as np
import jax
import jax.numpy as jnp
from jax import lax
from jax.experimental import pallas as pl
from jax.experimental.pallas import tpu as pltpu

D_MODEL = 1024
HEAD_DIM = 64
H_FOX = 8
W_FOX = H_FOX * HEAD_DIM
H_RET = 8
RET_DK = 64
RET_DV = 128
W_RET = H_RET * RET_DV
W_RQK = H_RET * RET_DK
ROPE_BASE = 10000.0
GM_WIDTH = 2 * D_MODEL
GM_GROUPS = 8
GM_GC = GM_WIDTH // GM_GROUPS
CHUNK = 128
PAGE = 128
EPS = 1e-6
F32 = jnp.float32
BF16 = jnp.bfloat16

LANES = 128
VMEM_LIMIT = 56 * 1024 * 1024

C_FQ, C_FK, C_FV, C_FG = 0, 512, 1024, 1536
C_RQ, C_RK, C_RV, C_RG, C_FF = 2048, 2560, 3072, 4096, 5120
EVEN_COLS_PAD = C_FF + LANES

LOG_GAMMA = [float(np.log(np.float32(1.0) - np.float32(2.0) ** np.float32(-5.0 - h))) for h in range(H_RET)]

LOG2E = float(np.log2(np.e))
NT_DIMS = (((1,), (1,)), ((), ()))
TN_DIMS = (((0,), (0,)), ((), ()))


def _params(sem):
    return pltpu.CompilerParams(dimension_semantics=sem, vmem_limit_bytes=VMEM_LIMIT)


def _const_spec(shape):
    nd = len(shape)
    return pl.BlockSpec(shape, lambda *_: (0,) * nd, pipeline_mode=pl.Buffered(1))


def _rms(x, g):
    ms = jnp.mean(x * x, axis=-1, keepdims=True)
    return x * lax.rsqrt(ms + EPS) * g


def _layer_norm(x, g):
    xc = x - jnp.mean(x, axis=-1, keepdims=True)
    return xc * lax.rsqrt(jnp.mean(xc * xc, axis=-1, keepdims=True) + EPS) * g


def _head_rms(x, g):
    rows = x.shape[0]
    low = lax.broadcasted_iota(jnp.int32, (rows, LANES), 1) < HEAD_DIM
    outs = []
    for c in range(x.shape[1] // LANES):
        xc = x[:, c * LANES:(c + 1) * LANES]
        sq = xc * xc
        s_lo = jnp.sum(jnp.where(low, sq, 0.0), axis=-1, keepdims=True)
        s_hi = jnp.sum(jnp.where(low, 0.0, sq), axis=-1, keepdims=True)
        ms = jnp.where(low, s_lo, s_hi) * (1.0 / HEAD_DIM)
        outs.append(xc * lax.rsqrt(ms + EPS))
    return jnp.concatenate(outs, axis=-1) * g


def _rope(x, cos, sin_signed):
    rows = x.shape[0]
    first_half = (lax.broadcasted_iota(jnp.int32, (rows, LANES), 1) % RET_DK) < RET_DK // 2
    outs = []
    for c in range(x.shape[1] // LANES):
        xc = x[:, c * LANES:(c + 1) * LANES]
        nxt = pltpu.roll(xc, LANES - RET_DK // 2, axis=1)
        prv = pltpu.roll(xc, RET_DK // 2, axis=1)
        outs.append(xc * cos + jnp.where(first_half, nxt, prv) * sin_signed)
    return jnp.concatenate(outs, axis=-1)


def _log_sigmoid(x):
    return jnp.minimum(x, 0.0) - jnp.log1p(jnp.exp(-jnp.abs(x)))


def _even_proj_kernel(x_ref, g_ref, w_ref, bf_ref, gq_ref, gk_ref, cos_ref, sin_ref,
                      q_ref, k32_ref, kb_ref, v32_ref, vb_ref, lf_ref, sfg_ref,
                      rq_ref, rk_ref, rv_ref, srg_ref):
    h = _rms(x_ref[...], g_ref[...]).astype(BF16)

    def proj(lo, hi):
        return jnp.dot(h, w_ref[:, lo:hi], preferred_element_type=F32)

    q_ref[...] = (_head_rms(proj(C_FQ, C_FK), gq_ref[...]) * (HEAD_DIM ** -0.5 * LOG2E)).astype(BF16)
    k = _head_rms(proj(C_FK, C_FV), gk_ref[...])
    k32_ref[...] = k
    kb_ref[...] = k.astype(BF16)
    v = proj(C_FV, C_FG)
    v32_ref[...] = v
    vb_ref[...] = v.astype(BF16)
    sfg_ref[...] = jax.nn.silu(proj(C_FG, C_RQ)).astype(BF16)
    cos = cos_ref[...]
    sin = sin_ref[...]
    rq_ref[...] = _rope(proj(C_RQ, C_RK), cos, sin).astype(BF16)
    rk_ref[...] = (_rope(proj(C_RK, C_RV), cos, sin) * (RET_DK ** -0.5)).astype(BF16)
    rv_ref[...] = proj(C_RV, C_RG).astype(BF16)
    srg_ref[...] = jax.nn.silu(proj(C_RG, C_FF)).astype(BF16)
    lf = _log_sigmoid(proj(C_FF, EVEN_COLS_PAD) + bf_ref[...])
    lf_ref[...] = lf[:, :H_FOX]


def _even_proj(x2d, g_norm, w, b_f, g_q, g_k, cos, sin, tm):
    m = x2d.shape[0]
    row = lambda n: pl.BlockSpec((tm, n), lambda i: (i, 0))
    tab_blocks = cos.shape[0] // tm
    tab = pl.BlockSpec((tm, LANES), lambda i: (i % tab_blocks, 0))
    outs = [
        (W_FOX, BF16), (W_FOX, F32), (W_FOX, BF16), (W_FOX, F32), (W_FOX, BF16), (H_FOX, F32),
        (W_FOX, BF16), (W_RQK, BF16), (W_RQK, BF16), (W_RET, BF16), (W_RET, BF16),
    ]
    return pl.pallas_call(
        _even_proj_kernel,
        grid=(m // tm,),
        in_specs=[row(D_MODEL), _const_spec((1, D_MODEL)), _const_spec((D_MODEL, EVEN_COLS_PAD)),
                  _const_spec((1, LANES)), _const_spec((1, W_FOX)), _const_spec((1, W_FOX)),
                  tab, tab],
        out_specs=[row(n) for n, _ in outs],
        out_shape=[jax.ShapeDtypeStruct((m, n), dt) for n, dt in outs],
        compiler_params=_params(("parallel",)),
        name="even_proj",
    )(x2d, g_norm, w, b_f, g_q, g_k, cos, sin)


def _cumsum_kernel(x_ref, o_ref):
    x = x_ref[0]
    n = x.shape[1]
    lane = lax.broadcasted_iota(jnp.int32, x.shape, 1)
    sh = 1
    while sh < n:
        x = x + jnp.where(lane >= sh, pltpu.roll(x, sh, axis=1), 0.0)
        sh *= 2
    o_ref[0] = x


def _cumsum_lanes(x):
    b, h, s = x.shape
    spec = pl.BlockSpec((1, h, s), lambda i: (i, 0, 0))
    return pl.pallas_call(
        _cumsum_kernel, grid=(b,), in_specs=[spec], out_specs=spec,
        out_shape=jax.ShapeDtypeStruct(x.shape, F32),
        compiler_params=_params(("parallel",)), name="logf_cumsum",
    )(x)


HEADS_PER_GROUP = 4
GROUP_W = HEADS_PER_GROUP * HEAD_DIM


def _fox_prompt_kernel(q_ref, k_ref, v_ref, crow_ref, sfg_ref, a_ref, qm_ref, vp_ref, m_ref, acc_ref, *, blk):
    i = pl.program_id(2)
    s_len = k_ref.shape[1]

    @pl.when(i == 0)
    def _():
        lane = lax.broadcasted_iota(jnp.int32, (s_len, LANES), 1)
        for hh in range(HEADS_PER_GROUP):
            pair = v_ref[0, :, (hh // 2) * LANES:(hh // 2 + 1) * LANES].astype(F32)
            if hh % 2:
                pair = pltpu.roll(pair, HEAD_DIM, axis=1)
            vp_ref[hh] = jnp.where(lane < HEAD_DIM, pair, jnp.where(lane == HEAD_DIM, 1.0, 0.0)).astype(BF16)

    q = q_ref[0]
    lane_head = lax.broadcasted_iota(jnp.int32, (blk, GROUP_W), 1) // HEAD_DIM
    for hh in range(HEADS_PER_GROUP):
        qm_ref[hh] = jnp.where(lane_head == hh, q, jnp.zeros_like(q))
    m_ref[...] = jnp.full(m_ref.shape, -1e30, F32)
    acc_ref[...] = jnp.zeros(acc_ref.shape, F32)
    q_off = pl.multiple_of(i * blk, blk)
    c_start = crow_ref[0, 0, :, pl.ds(q_off, LANES)][:, 0:1]
    causal = (lax.broadcasted_iota(jnp.int32, (blk, blk), 0)
              >= lax.broadcasted_iota(jnp.int32, (blk, blk), 1))

    def step(j, masked):
        off = pl.multiple_of(j * blk, blk)
        kb = k_ref[0, pl.ds(off, blk), :]
        bias = (c_start - crow_ref[0, 0, :, pl.ds(off, blk)]) * LOG2E
        scores = [lax.dot_general(qm_ref[hh], kb, NT_DIMS, preferred_element_type=F32)
                  for hh in range(HEADS_PER_GROUP)]
        for hh in range(HEADS_PER_GROUP):
            s = scores[hh] + bias[hh:hh + 1, :]
            if masked:
                s = jnp.where(causal, s, -jnp.inf)
            m_old = m_ref[hh]
            m_new = jnp.maximum(m_old, jnp.max(s, axis=-1, keepdims=True))
            p = jnp.exp2(s - jnp.concatenate([m_new] * (blk // LANES), axis=1))
            pv = jnp.dot(p.astype(BF16), vp_ref[hh, pl.ds(off, blk), :], preferred_element_type=F32)
            acc_ref[hh] = jnp.exp2(m_old - m_new) * acc_ref[hh] + pv
            m_ref[hh] = m_new

    def body(j, carry):
        step(j, False)
        return carry

    lax.fori_loop(0, i, body, 0)
    step(i, True)
    low = lax.broadcasted_iota(jnp.int32, (blk, LANES), 1) < HEAD_DIM
    pairs = []
    for pp in range(HEADS_PER_GROUP // 2):
        o = [acc_ref[2 * pp + j] / acc_ref[2 * pp + j][:, HEAD_DIM:HEAD_DIM + 1] for j in range(2)]
        pairs.append(jnp.where(low, o[0], pltpu.roll(o[1], HEAD_DIM, axis=1)))
    a_ref[0] = (jnp.concatenate(pairs, axis=1) * sfg_ref[0].astype(F32)).astype(BF16)


def _fox_prompt(q, kb, vb, crow, sfg, blk):
    b, s, _ = q.shape
    ng = H_FOX // HEADS_PER_GROUP
    qspec = pl.BlockSpec((1, blk, GROUP_W), lambda bi, g, i: (bi, i, g))
    kvspec = pl.BlockSpec((1, s, GROUP_W), lambda bi, g, i: (bi, 0, g))
    return pl.pallas_call(
        functools.partial(_fox_prompt_kernel, blk=blk),
        grid=(b, ng, s // blk),
        in_specs=[qspec, kvspec, kvspec,
                  pl.BlockSpec((1, 1, HEADS_PER_GROUP, s), lambda bi, g, i: (bi, g, 0, 0)),
                  qspec],
        out_specs=qspec,
        out_shape=jax.ShapeDtypeStruct((b, s, W_FOX), BF16),
        scratch_shapes=[pltpu.VMEM((HEADS_PER_GROUP, blk, GROUP_W), BF16),
                        pltpu.VMEM((HEADS_PER_GROUP, s, LANES), BF16),
                        pltpu.VMEM((HEADS_PER_GROUP, blk, LANES), F32),
                        pltpu.VMEM((HEADS_PER_GROUP, blk, LANES), F32)],
        compiler_params=_params(("parallel", "parallel", "arbitrary")),
        name="fox_prompt",
    )(q, kb, vb, crow, sfg)


def _ret_head_out(o, g, srg):
    return (_layer_norm(o, g) * srg.astype(F32)).astype(BF16)


def _ret_prompt_kernel(rq_ref, rk_ref, rv_ref, srg_ref, gret_ref, r_ref, st_ref, dec_ref):
    L = CHUNK
    c = pl.program_id(1)

    @pl.when(c == 0)
    def _():
        st_ref[...] = jnp.zeros(st_ref.shape, F32)
        diff = (lax.broadcasted_iota(jnp.int32, (L, L), 0)
                - lax.broadcasted_iota(jnp.int32, (L, L), 1)).astype(F32)
        for h in range(H_RET):
            dec_ref[h] = jnp.where(diff >= 0, jnp.exp(LOG_GAMMA[h] * jnp.maximum(diff, 0.0)), 0.0)

    pos = lax.broadcasted_iota(jnp.int32, (L, 1), 0).astype(F32)
    lane_half = lax.broadcasted_iota(jnp.int32, (L, LANES), 1) // RET_DK
    for pp in range(H_RET // 2):
        qp = rq_ref[0, :, pp * LANES:(pp + 1) * LANES]
        kp = rk_ref[0, :, pp * LANES:(pp + 1) * LANES]
        s_pair = st_ref[0, pp]
        s_pair_b = s_pair.astype(BF16)
        for j in range(2):
            h = 2 * pp + j
            lg = LOG_GAMMA[h]
            qm = jnp.where(lane_half == j, qp, jnp.zeros_like(qp))
            vh = rv_ref[0, :, h * RET_DV:(h + 1) * RET_DV]
            scores = lax.dot_general(qm, kp, NT_DIMS, preferred_element_type=F32) * dec_ref[h]
            intra = jnp.dot(scores.astype(BF16), vh, preferred_element_type=F32)
            cross = jnp.dot(qm, s_pair_b, preferred_element_type=F32) * jnp.exp(lg * (pos + 1.0))
            kd = (kp.astype(F32) * jnp.exp(lg * (L - 1.0 - pos))).astype(BF16)
            upd = lax.dot_general(kd, vh, TN_DIMS, preferred_element_type=F32)
            rows = slice(j * RET_DK, (j + 1) * RET_DK)
            st_ref[0, pp, rows, :] = float(np.exp(lg * L)) * s_pair[rows] + upd[rows]
            cols = slice(h * RET_DV, (h + 1) * RET_DV)
            r_ref[0, :, cols] = _ret_head_out(intra + cross, gret_ref[:, cols], srg_ref[0, :, cols])


def _ret_prompt(rq, rk, rv, srg, g_ret):
    b, s, _ = rq.shape
    qk = pl.BlockSpec((1, CHUNK, W_RQK), lambda bi, c: (bi, c, 0))
    wide = pl.BlockSpec((1, CHUNK, W_RET), lambda bi, c: (bi, c, 0))
    st = pl.BlockSpec((1, H_RET // 2, 2 * RET_DK, RET_DV), lambda bi, c: (bi, 0, 0, 0))
    return pl.pallas_call(
        _ret_prompt_kernel,
        grid=(b, s // CHUNK),
        in_specs=[qk, qk, wide, wide, _const_spec((1, W_RET))],
        out_specs=[wide, st],
        out_shape=[jax.ShapeDtypeStruct((b, s, W_RET), BF16),
                   jax.ShapeDtypeStruct((b, H_RET // 2, 2 * RET_DK, RET_DV), F32)],
        scratch_shapes=[pltpu.VMEM((H_RET, CHUNK, CHUNK), F32)],
        compiler_params=_params(("parallel", "arbitrary")),
        name="ret_prompt",
    )(rq, rk, rv, srg, g_ret)


def _ret_sample_kernel(rq_ref, rk_ref, rv_ref, srg_ref, gret_ref, s0_ref, r_ref, s1_ref, *, n_new):
    rows = rq_ref.shape[0]
    nb = rows // n_new
    two_dk = 2 * RET_DK
    ri = lax.broadcasted_iota(jnp.int32, (rows, rows), 0)
    ci = lax.broadcasted_iota(jnp.int32, (rows, rows), 1)
    same = (ri // n_new) == (ci // n_new)
    diff = ((ri % n_new) - (ci % n_new)).astype(F32)
    pos = (lax.broadcasted_iota(jnp.int32, (rows, 1), 0) % n_new).astype(F32)
    lane_half = lax.broadcasted_iota(jnp.int32, (rows, LANES), 1) // RET_DK
    q_sel = (lax.broadcasted_iota(jnp.int32, (rows, nb * two_dk), 1) // two_dk
             == lax.broadcasted_iota(jnp.int32, (rows, nb * two_dk), 0) // n_new)
    k_sel = (lax.broadcasted_iota(jnp.int32, (nb * two_dk, rows), 0) // two_dk
             == lax.broadcasted_iota(jnp.int32, (nb * two_dk, rows), 1) // n_new)
    row_head = (lax.broadcasted_iota(jnp.int32, (nb * two_dk, 1), 0) % two_dk) // RET_DK

    qp = rq_ref[...]
    kp = rk_ref[...]
    s_old = s0_ref[...].reshape(nb * two_dk, RET_DV)
    s_old_b = s_old.astype(BF16)
    upd = []
    for j in range(2):
        h = 2 * pl.program_id(0) + j
        lg = jnp.where(h == 0, LOG_GAMMA[0], 0.0)
        for hh in range(1, H_RET):
            lg = jnp.where(h == hh, LOG_GAMMA[hh], lg)
        qm = jnp.where(lane_half == j, qp, jnp.zeros_like(qp))
        vh = rv_ref[:, j * RET_DV:(j + 1) * RET_DV]
        decay = jnp.where(same & (diff >= 0), jnp.exp(lg * jnp.maximum(diff, 0.0)), 0.0)
        scores = lax.dot_general(qm, kp, NT_DIMS, preferred_element_type=F32) * decay
        intra = jnp.dot(scores.astype(BF16), vh, preferred_element_type=F32)
        q_exp = jnp.where(q_sel, jnp.tile(qm, (1, nb)), jnp.zeros((), BF16))
        cross = jnp.dot(q_exp, s_old_b, preferred_element_type=F32) * jnp.exp(lg * (pos + 1.0))
        kd = kp.astype(F32) * jnp.exp(lg * (n_new - 1.0 - pos))
        kd_exp = jnp.where(k_sel, jnp.tile(kd.T, (nb, 1)), 0.0).astype(BF16)
        keep = jnp.exp(lg * jnp.full((1, 1), float(n_new), F32))
        upd.append((keep, jnp.dot(kd_exp, vh, preferred_element_type=F32)))
        cols = slice(j * RET_DV, (j + 1) * RET_DV)
        r_ref[:, cols] = _ret_head_out(intra + cross, gret_ref[:, cols], srg_ref[:, cols])
    s_new = jnp.where(row_head == 0, upd[0][0] * s_old + upd[0][1], upd[1][0] * s_old + upd[1][1])
    s1_ref[...] = s_new.reshape(s1_ref.shape)


def _ret_sample(rq, rk, rv, srg, g_ret, state, n_new):
    rows = rq.shape[0]
    nb = rows // n_new
    pair = pl.BlockSpec((rows, LANES), lambda p: (0, p))
    wide = pl.BlockSpec((rows, 2 * RET_DV), lambda p: (0, p))
    st = pl.BlockSpec((nb, 2, RET_DK, RET_DV), lambda p: (0, p, 0, 0))
    return pl.pallas_call(
        functools.partial(_ret_sample_kernel, n_new=n_new),
        grid=(H_RET // 2,),
        in_specs=[pair, pair, wide, wide, pl.BlockSpec((1, 2 * RET_DV), lambda p: (0, p)), st],
        out_specs=[wide, st],
        out_shape=[jax.ShapeDtypeStruct((rows, W_RET), BF16),
                   jax.ShapeDtypeStruct(state.shape, F32)],
        compiler_params=_params(("parallel",)),
        name="ret_sample",
    )(rq, rk, rv, srg, g_ret, state)


PAGE_COLS = PAGE * H_FOX


def _fox_sample_kernel(pt_ref, q_ref, kn_ref, vn_ref, lfr_ref, sfg_ref, *refs, n_new, pages_per_step):
    g = pages_per_step
    k_refs, v_refs, lf_refs = refs[:g], refs[g:2 * g], refs[2 * g:3 * g]
    a_ref, m_ref, l_ref, acc_ref, tail_ref, kn_scr, vn_scr = refs[3 * g:]
    c = pl.program_id(1)
    nq = n_new * H_FOX
    q = q_ref[0]
    row = lax.broadcasted_iota(jnp.int32, (nq, 1), 0)

    def partial_softmax(s, v2):
        m = jnp.max(s, axis=-1, keepdims=True)
        p = jnp.exp2(s - m)
        return m, jnp.sum(p, axis=-1, keepdims=True), jnp.dot(p.astype(BF16), v2, preferred_element_type=F32)

    def merge(parts):
        m_new = m_ref[...]
        for m, _, _ in parts:
            m_new = jnp.maximum(m_new, m)
        alpha = jnp.exp2(m_ref[...] - m_new)
        l_new = alpha * l_ref[...]
        acc = alpha * acc_ref[...]
        for m, l, o in parts:
            w = jnp.exp2(m - m_new)
            l_new = l_new + w * l
            acc = acc + w * o
        m_ref[...] = m_new
        l_ref[...] = l_new
        acc_ref[...] = acc

    @pl.when(c == 0)
    def _():
        tail_ref[...] = jnp.zeros(tail_ref.shape, F32)
        m_ref[...] = jnp.full(m_ref.shape, -1e30, F32)
        l_ref[...] = jnp.zeros(l_ref.shape, F32)
        acc_ref[...] = jnp.zeros(acc_ref.shape, F32)
        kn_scr[...] = jnp.zeros(kn_scr.shape, F32)
        vn_scr[...] = jnp.zeros(vn_scr.shape, F32)
        kn_scr[0:nq, :] = kn_ref[0]
        vn_scr[0:nq, :] = vn_ref[0]
        lane = lax.broadcasted_iota(jnp.int32, (1, LANES), 1)
        pfx_row = lfr_ref[0]
        sh = H_FOX
        while sh < nq:
            pfx_row = pfx_row + jnp.where(lane >= sh, pltpu.roll(pfx_row, sh, axis=1), 0.0)
            sh *= 2
        col = lax.broadcasted_iota(jnp.int32, (nq, LANES), 1)
        valid = (col % H_FOX == row % H_FOX) & (col // H_FOX <= row // H_FOX)
        s = lax.dot_general(q, kn_scr[...].astype(BF16), NT_DIMS, preferred_element_type=F32)
        s = jnp.where(valid, s - pfx_row * LOG2E, -jnp.inf)
        merge([partial_softmax(s, vn_scr[...].astype(BF16))])

    lf = jnp.concatenate([r[0] for r in lf_refs], axis=0)
    lane = lax.broadcasted_iota(jnp.int32, lf.shape, 1)
    incl = lf
    total = lf
    sh = H_FOX
    while sh < PAGE_COLS:
        incl = incl + jnp.where(lane < PAGE_COLS - sh, pltpu.roll(incl, PAGE_COLS - sh, axis=1), 0.0)
        total = total + pltpu.roll(total, sh, axis=1)
        sh *= 2
    later = incl - lf
    col = lax.broadcasted_iota(jnp.int32, (nq, PAGE_COLS), 1)
    valid = col % H_FOX == row % H_FOX
    run = tail_ref[...]
    scores = []
    for p in reversed(range(g)):
        bias = (run + later[p:p + 1, :]) * LOG2E
        run = run + total[p:p + 1, :]
        k2 = k_refs[p][0, 0].reshape(PAGE_COLS, HEAD_DIM).astype(BF16)
        s = lax.dot_general(q, k2, NT_DIMS, preferred_element_type=F32)
        scores.append((p, jnp.where(valid, s + bias, -jnp.inf)))
    tail_ref[...] = run
    merge([partial_softmax(s, v_refs[p][0, 0].reshape(PAGE_COLS, HEAD_DIM).astype(BF16)) for p, s in scores])

    @pl.when(c == pl.num_programs(1) - 1)
    def _():
        a_ref[0] = acc_ref[...] / l_ref[...] * sfg_ref[0].astype(F32)


def _fox_sample(page_table, q, k_new, v_new, lf_row, sfg, cache_k, cache_v, cache_lf, pages_per_step):
    bd, nq, _ = q.shape
    n_pages = page_table.shape[1]
    g = pages_per_step
    n_steps = n_pages // g

    def tok(shape):
        return pl.BlockSpec((1,) + shape, lambda b, c, pt: (b, 0, 0))

    def page_idx(b, c, pt, slot):
        return pt[b, n_pages - (c + 1) * g + slot]

    def kv_spec(slot):
        return pl.BlockSpec((1, 1, PAGE, H_FOX, HEAD_DIM),
                            lambda b, c, pt: (0, page_idx(b, c, pt, slot), 0, 0, 0))

    def lf_spec(slot):
        return pl.BlockSpec((1, 1, PAGE_COLS), lambda b, c, pt: (page_idx(b, c, pt, slot), 0, 0))

    in_specs = [tok((nq, HEAD_DIM)), tok((nq, HEAD_DIM)), tok((nq, HEAD_DIM)), tok((1, LANES)),
                tok((nq, HEAD_DIM))]
    in_specs += [kv_spec(s) for s in range(g)] + [kv_spec(s) for s in range(g)] + [lf_spec(s) for s in range(g)]
    grid_spec = pltpu.PrefetchScalarGridSpec(
        num_scalar_prefetch=1, grid=(bd, n_steps), in_specs=in_specs, out_specs=tok((nq, HEAD_DIM)),
        scratch_shapes=[pltpu.VMEM((nq, 1), F32), pltpu.VMEM((nq, 1), F32), pltpu.VMEM((nq, HEAD_DIM), F32),
                        pltpu.VMEM((1, PAGE_COLS), F32), pltpu.VMEM((LANES, HEAD_DIM), F32),
                        pltpu.VMEM((LANES, HEAD_DIM), F32)])
    return pl.pallas_call(
        functools.partial(_fox_sample_kernel, n_new=nq // H_FOX, pages_per_step=g),
        grid_spec=grid_spec,
        out_shape=jax.ShapeDtypeStruct((bd, nq, HEAD_DIM), F32),
        compiler_params=_params(("parallel", "arbitrary")),
        name="fox_sample",
    )(page_table, q, k_new, v_new, lf_row, sfg, *([cache_k] * g), *([cache_v] * g), *([cache_lf] * g))


def _tail_kernel(a_ref, r_ref, x_ref, woe_ref, gno_ref, wio_ref, gv_ref, ws_ref, bs_ref, woo_ref,
                 *out_refs, n_new):
    rows = x_ref.shape[0]
    x1 = (x_ref[...]
          + jnp.dot(a_ref[...].astype(BF16), woe_ref[:W_FOX, :], preferred_element_type=F32)
          + jnp.dot(r_ref[...], woe_ref[W_FOX:, :], preferred_element_type=F32))
    h = _rms(x1, gno_ref[...]).astype(BF16)
    u = jax.nn.gelu(jnp.dot(h, wio_ref[:, :GM_WIDTH], preferred_element_type=F32))
    v = jax.nn.gelu(jnp.dot(h, wio_ref[:, GM_WIDTH:2 * GM_WIDTH], preferred_element_type=F32))
    gate = jnp.dot(h, wio_ref[:, 2 * GM_WIDTH:], preferred_element_type=F32)
    vn = _layer_norm(v, gv_ref[...])
    if n_new is None:
        y_ref, = out_refs
        tril = (lax.broadcasted_iota(jnp.int32, (CHUNK, CHUNK), 0)
                >= lax.broadcasted_iota(jnp.int32, (CHUNK, CHUNK), 1))
        vb = vn.astype(BF16)
        cols = []
        for g in range(GM_GROUPS):
            wm = jnp.where(tril, ws_ref[g], 0.0).astype(BF16)
            bias = bs_ref[:, g:g + 1]
            parts = [jnp.dot(wm, vb[c * CHUNK:(c + 1) * CHUNK, g * GM_GC:(g + 1) * GM_GC],
                             preferred_element_type=F32) + bias for c in range(rows // CHUNK)]
            cols.append(jnp.concatenate(parts, axis=0))
        sv = jnp.concatenate(cols, axis=1)
    else:
        y_ref, vn_ref = out_refs
        vn_ref[...] = vn
        tok = lax.broadcasted_iota(jnp.int32, (rows, 1), 0) % n_new
        cols = []
        for g in range(GM_GROUPS):
            vg = vn[:, g * GM_GC:(g + 1) * GM_GC]
            acc = jnp.zeros((rows, 1), F32)
            for t in range(n_new):
                acc = jnp.where(tok == t, bs_ref[g, t], acc)
            acc = jnp.broadcast_to(acc, (rows, GM_GC))
            for d in range(n_new):
                coef = jnp.zeros((rows, 1), F32)
                for t in range(d, n_new):
                    coef = jnp.where(tok == t, ws_ref[g, t * n_new + (t - d)], coef)
                src = vg if d == 0 else pltpu.roll(vg, d, axis=0)
                acc = acc + coef * src
            cols.append(acc)
        sv = jnp.concatenate(cols, axis=1)
    z = (u * sv * jax.nn.silu(gate)).astype(BF16)
    y_ref[...] = x1 + jnp.dot(z, woo_ref[...], preferred_element_type=F32)


def _tail(a, r, x2d, woe, gno, wio, gv, ws, bs, woo, tm, n_new):
    m = x2d.shape[0]
    row = lambda n: pl.BlockSpec((tm, n), lambda i: (i, 0))
    if n_new is None:
        mix_specs = [_const_spec(ws.shape), _const_spec(bs.shape)]
        out_specs = [row(D_MODEL)]
        out_shape = [jax.ShapeDtypeStruct((m, D_MODEL), F32)]
    else:
        smem = pl.BlockSpec(memory_space=pltpu.SMEM)
        mix_specs = [smem, smem]
        out_specs = [row(D_MODEL), row(GM_WIDTH)]
        out_shape = [jax.ShapeDtypeStruct((m, D_MODEL), F32), jax.ShapeDtypeStruct((m, GM_WIDTH), F32)]
    return pl.pallas_call(
        functools.partial(_tail_kernel, n_new=n_new),
        grid=(m // tm,),
        in_specs=[row(W_FOX), row(W_RET), row(D_MODEL), _const_spec(woe.shape), _const_spec((1, D_MODEL)),
                  _const_spec(wio.shape), _const_spec((1, GM_WIDTH))] + mix_specs + [_const_spec(woo.shape)],
        out_specs=out_specs, out_shape=out_shape,
        compiler_params=_params(("parallel",)),
        name="tail_prompt" if n_new is None else "tail_sample",
    )(a, r, x2d, woe, gno, wio, gv, ws, bs, woo)


def _rope_tables(pos):
    half = RET_DK // 2
    inv = ROPE_BASE ** (-jnp.arange(half, dtype=F32) / half)
    ang = pos.astype(F32)[:, None] * inv[None, :]
    cos, sin = jnp.cos(ang), jnp.sin(ang)
    return jnp.tile(jnp.concatenate([cos, cos], axis=-1), (1, 2)), jnp.tile(jnp.concatenate([-sin, sin], axis=-1), (1, 2))


def _row_tile(m):
    return 256 if m % 256 == 0 else 128


def kernel(x_prompt, x_sample, cache_k, cache_v, cache_logf, state_ret, page_table, norm_even, w_in_even,
           b_forget, qnorm, knorm, ret_norm, w_out_even, norm_odd, w_in_odd, vnorm_odd, w_spatial, b_spatial,
           w_out_odd):
    bp, s, _ = x_prompt.shape
    bd, n_new, _ = x_sample.shape
    n_pages = page_table.shape[1]
    past = n_pages * PAGE
    assert s % 256 == 0 and (bd * n_new) % CHUNK == 0 and w_in_even.shape[0] == 1 and w_in_odd.shape[0] == 1

    wi = w_in_even[0]
    sizes = (W_FOX, W_FOX, W_FOX, H_FOX, W_FOX, W_RQK, W_RQK, W_RET, W_RET)
    offs = np.concatenate([[0], np.cumsum(sizes)])
    fq, fk, fv, ff, fg, rq, rk, rv, rg = [wi[:, offs[i]:offs[i + 1]] for i in range(9)]
    w_even = jnp.concatenate([fq, fk, fv, fg, rq, rk, rv, rg, jnp.pad(ff, ((0, 0), (0, LANES - H_FOX)))],
                             axis=1).astype(BF16)
    b_f = jnp.pad(b_forget[0], (0, LANES - H_FOX))[None, :]
    g_q = jnp.tile(qnorm[0], H_FOX)[None, :]
    g_k = jnp.tile(knorm[0], H_FOX)[None, :]
    g_ret = ret_norm[0].reshape(1, W_RET)
    woe = w_out_even[0].astype(BF16)
    wio = w_in_odd[0].astype(BF16)
    woo = w_out_odd[0].astype(BF16)
    g_ne, g_no, g_v = norm_even[0][None, :], norm_odd[0][None, :], vnorm_odd[0][None, :]

    cos_p, sin_p = _rope_tables(jnp.arange(s))
    xp2 = x_prompt.reshape(bp * s, D_MODEL)
    tm = _row_tile(s)
    q, k32, kb, v32, vb, lf, sfg, rqp, rkp, rvp, srg = _even_proj(
        xp2, g_ne, w_even, b_f, g_q, g_k, cos_p, sin_p, tm)
    lf3 = lf.reshape(bp, s, H_FOX)
    c_row = _cumsum_lanes(jnp.transpose(lf3, (0, 2, 1)))
    ng = H_FOX // HEADS_PER_GROUP
    c_row4 = c_row.reshape(bp, ng, HEADS_PER_GROUP, s)
    r3 = lambda t: t.reshape(bp, s, t.shape[-1])
    a_p = _fox_prompt(r3(q), r3(kb), r3(vb), c_row4, r3(sfg), 256)
    r_p, st_p = _ret_prompt(r3(rqp), r3(rkp), r3(rvp), r3(srg), g_ret)
    y_p, = _tail(a_p.reshape(bp * s, W_FOX), r_p.reshape(bp * s, W_RET), xp2, woe, g_no, wio, g_v,
                 w_spatial[0], jnp.transpose(b_spatial[0]), woo, tm, None)

    rows = bd * n_new
    cos_d, sin_d = _rope_tables(past + jnp.arange(n_new))
    xd2 = x_sample.reshape(rows, D_MODEL)
    qd, k32d, _, v32d, _, lfd, sfgd, rqd, rkd, rvd, srgd = _even_proj(
        xd2, g_ne, w_even, b_f, g_q, g_k, jnp.tile(cos_d, (bd, 1)), jnp.tile(sin_d, (bd, 1)), rows)
    nq = n_new * H_FOX
    per_head = lambda t: t.reshape(bd, nq, HEAD_DIM)
    n_phys = cache_k.shape[1]
    lf_row = jnp.pad(lfd.reshape(bd, 1, nq), ((0, 0), (0, 0), (0, LANES - nq)))
    a_d = _fox_sample(page_table, per_head(qd), per_head(k32d), per_head(v32d), lf_row,
                      per_head(sfgd), cache_k, cache_v, cache_logf[0].reshape(n_phys, 1, PAGE_COLS),
                      8 if n_pages % 8 == 0 else 1)
    r_d, st_d = _ret_sample(rqd, rkd, rvd, srgd, g_ret, state_ret[0], n_new)
    ws_new = w_spatial[0][:, :n_new, :n_new].reshape(GM_GROUPS, n_new * n_new)
    y_d, gv_d = _tail(a_d.reshape(rows, W_FOX), r_d, xd2, woe, g_no, wio, g_v,
                      ws_new, b_spatial[0][:, :n_new], woo, rows, n_new)

    return (y_p.reshape(bp, s, D_MODEL), y_d.reshape(bd, n_new, D_MODEL),
            k32.reshape(1, bp, s, H_FOX, HEAD_DIM), v32.reshape(1, bp, s, H_FOX, HEAD_DIM),
            lf3[None], st_p.reshape(1, bp, H_RET, RET_DK, RET_DV),
            k32d.reshape(1, bd, n_new, H_FOX, HEAD_DIM), v32d.reshape(1, bd, n_new, H_FOX, HEAD_DIM),
            lfd.reshape(1, bd, n_new, H_FOX), st_d[None], gv_d.reshape(1, bd, n_new, GM_WIDTH))
```

```python
import functools

import numpy as np
import jax
import jax.numpy as jnp
from jax import lax
from jax.experimental import pallas as pl
from jax.experimental.pallas import tpu as pltpu

D_MODEL = 1024
HEAD_DIM = 64
H_FOX = 8
W_FOX = H_FOX * HEAD_DIM
H_RET = 8
RET_DK = 64
RET_DV = 128
W_RET = H_RET * RET_DV
W_RQK = H_RET * RET_DK
ROPE_BASE = 10000.0
GM_WIDTH = 2 * D_MODEL
GM_GROUPS = 8
GM_GC = GM_WIDTH // GM_GROUPS
CHUNK = 128
PAGE = 128
EPS = 1e-6
F32 = jnp.float32
BF16 = jnp.bfloat16

LANES = 128
VMEM_LIMIT = 56 * 1024 * 1024

C_FQ, C_FG, C_RQ, C_RK, C_RV, C_RG, EVEN_COLS = 0, 512, 1024, 1536, 2048, 3072, 4096
R_FK, R_FV, R_FF, EVEN_ROWS_T = 0, 512, 1024, 1040

LOG_GAMMA = [float(np.log(np.float32(1.0) - np.float32(2.0) ** np.float32(-5.0 - h))) for h in range(H_RET)]

LOG2E = float(np.log2(np.e))
NT_DIMS = (((1,), (1,)), ((), ()))
TN_DIMS = (((0,), (0,)), ((), ()))


def _params(sem):
    return pltpu.CompilerParams(dimension_semantics=sem, vmem_limit_bytes=VMEM_LIMIT)


def _const_spec(shape):
    nd = len(shape)
    return pl.BlockSpec(shape, lambda *_: (0,) * nd, pipeline_mode=pl.Buffered(1))


def _rms(x, g):
    ms = jnp.mean(x * x, axis=-1, keepdims=True)
    return x * lax.rsqrt(ms + EPS) * g


def _layer_norm(x, g):
    xc = x - jnp.mean(x, axis=-1, keepdims=True)
    return xc * lax.rsqrt(jnp.mean(xc * xc, axis=-1, keepdims=True) + EPS) * g


def _head_rms(x, g):
    rows = x.shape[0]
    low = lax.broadcasted_iota(jnp.int32, (rows, LANES), 1) < HEAD_DIM
    outs = []
    for c in range(x.shape[1] // LANES):
        xc = x[:, c * LANES:(c + 1) * LANES]
        sq = xc * xc
        s_lo = jnp.sum(jnp.where(low, sq, 0.0), axis=-1, keepdims=True)
        s_hi = jnp.sum(jnp.where(low, 0.0, sq), axis=-1, keepdims=True)
        ms = jnp.where(low, s_lo, s_hi) * (1.0 / HEAD_DIM)
        outs.append(xc * lax.rsqrt(ms + EPS))
    return jnp.concatenate(outs, axis=-1) * g


def _rope(x, cos, sin_signed):
    rows = x.shape[0]
    first_half = (lax.broadcasted_iota(jnp.int32, (rows, LANES), 1) % RET_DK) < RET_DK // 2
    outs = []
    for c in range(x.shape[1] // LANES):
        xc = x[:, c * LANES:(c + 1) * LANES]
        nxt = pltpu.roll(xc, LANES - RET_DK // 2, axis=1)
        prv = pltpu.roll(xc, RET_DK // 2, axis=1)
        outs.append(xc * cos + jnp.where(first_half, nxt, prv) * sin_signed)
    return jnp.concatenate(outs, axis=-1)


def _log_sigmoid(x):
    return jnp.minimum(x, 0.0) - jnp.log1p(jnp.exp(-jnp.abs(x)))


def _even_proj_kernel(x_ref, g_ref, w_ref, wt_ref, bf_ref, gq_ref, gk_ref, cos_ref, sin_ref,
                      q_ref, sfg_ref, rq_ref, rk_ref, rv_ref, srg_ref,
                      kt32_ref, ktb_ref, vt32_ref, vtb_ref, lft_ref):
    h = _rms(x_ref[...], g_ref[...]).astype(BF16)
    tm = h.shape[0]

    def proj(lo, hi):
        return jnp.dot(h, w_ref[:, lo:hi], preferred_element_type=F32)

    q_ref[...] = (_head_rms(proj(C_FQ, C_FG), gq_ref[...]) * (HEAD_DIM ** -0.5 * LOG2E)).astype(BF16)
    sfg_ref[...] = jax.nn.silu(proj(C_FG, C_RQ)).astype(BF16)
    cos = cos_ref[...]
    sin = sin_ref[...]
    rq_ref[...] = _rope(proj(C_RQ, C_RK), cos, sin).astype(BF16)
    rk_ref[...] = (_rope(proj(C_RK, C_RV), cos, sin) * (RET_DK ** -0.5)).astype(BF16)
    rv_ref[...] = proj(C_RV, C_RG).astype(BF16)
    srg_ref[...] = jax.nn.silu(proj(C_RG, EVEN_COLS)).astype(BF16)

    t = lax.dot_general(wt_ref[...], h, NT_DIMS, preferred_element_type=F32)
    kt = t[R_FK:R_FV].reshape(H_FOX, HEAD_DIM, tm)
    kt = kt * lax.rsqrt(jnp.mean(kt * kt, axis=1, keepdims=True) + EPS) * gk_ref[...][None]
    kt = kt.reshape(W_FOX, tm)
    kt32_ref[0] = kt
    ktb_ref[0] = kt.astype(BF16)
    vt = t[R_FV:R_FF]
    vt32_ref[0] = vt
    vtb_ref[0] = vt.astype(BF16)
    lft_ref[0] = _log_sigmoid(t[R_FF:R_FF + H_FOX] + bf_ref[...])


def _even_proj(x2d, n_seq, g_norm, w, wt, b_f, g_q, g_k, cos, sin, tm):
    m = x2d.shape[0]
    s = m // n_seq
    per_seq = s // tm
    row = lambda n: pl.BlockSpec((tm, n), lambda i: (i, 0))
    tab = pl.BlockSpec((tm, LANES), lambda i: (i % per_seq, 0))
    col = lambda n: pl.BlockSpec((1, n, tm), lambda i: (i // per_seq, 0, i % per_seq))
    outs = [(W_FOX, BF16), (W_FOX, BF16), (W_RQK, BF16), (W_RQK, BF16), (W_RET, BF16), (W_RET, BF16)]
    outs_t = [(W_FOX, F32), (W_FOX, BF16), (W_FOX, F32), (W_FOX, BF16), (H_FOX, F32)]
    return pl.pallas_call(
        _even_proj_kernel,
        grid=(m // tm,),
        in_specs=[row(D_MODEL), _const_spec((1, D_MODEL)), _const_spec((D_MODEL, EVEN_COLS)),
                  _const_spec((EVEN_ROWS_T, D_MODEL)), _const_spec((H_FOX, 1)), _const_spec((1, W_FOX)),
                  _const_spec((HEAD_DIM, 1)), tab, tab],
        out_specs=[row(n) for n, _ in outs] + [col(n) for n, _ in outs_t],
        out_shape=([jax.ShapeDtypeStruct((m, n), dt) for n, dt in outs]
                   + [jax.ShapeDtypeStruct((n_seq, n, s), dt) for n, dt in outs_t]),
        compiler_params=_params(("parallel",)),
        name="even_proj",
    )(x2d, g_norm, w, wt, b_f, g_q, g_k, cos, sin)


def _cumsum_kernel(x_ref, o_ref):
    x = x_ref[0]
    n = x.shape[1]
    lane = lax.broadcasted_iota(jnp.int32, x.shape, 1)
    sh = 1
    while sh < n:
        x = x + jnp.where(lane >= sh, pltpu.roll(x, sh, axis=1), 0.0)
        sh *= 2
    o_ref[0] = x


def _cumsum_lanes(x):
    b, h, s = x.shape
    spec = pl.BlockSpec((1, h, s), lambda i: (i, 0, 0))
    return pl.pallas_call(
        _cumsum_kernel, grid=(b,), in_specs=[spec], out_specs=spec,
        out_shape=jax.ShapeDtypeStruct(x.shape, F32),
        compiler_params=_params(("parallel",)), name="logf_cumsum",
    )(x)


HEADS_PER_GROUP = 4
GROUP_W = HEADS_PER_GROUP * HEAD_DIM


def _fox_prompt_kernel(q_ref, k_ref, v_ref, crow_ref, sfg_ref, a_ref, qm_ref, vp_ref, m_ref, acc_ref, *, blk):
    i = pl.program_id(2)
    s_len = k_ref.shape[2]

    @pl.when(i == 0)
    def _():
        ones_row = jnp.where(lax.broadcasted_iota(jnp.int32, (HEAD_DIM, s_len), 0) == 0, 1.0, 0.0).astype(BF16)
        for hh in range(HEADS_PER_GROUP):
            vp_ref[hh, :HEAD_DIM, :] = v_ref[0, hh * HEAD_DIM:(hh + 1) * HEAD_DIM, :]
            vp_ref[hh, HEAD_DIM:, :] = ones_row

    q = q_ref[0]
    lane_head = lax.broadcasted_iota(jnp.int32, (blk, GROUP_W), 1) // HEAD_DIM
    for hh in range(HEADS_PER_GROUP):
        qm_ref[hh] = jnp.where(lane_head == hh, q, jnp.zeros_like(q))
    m_ref[...] = jnp.full(m_ref.shape, -1e30, F32)
    acc_ref[...] = jnp.zeros(acc_ref.shape, F32)
    q_off = pl.multiple_of(i * blk, blk)
    c_start = crow_ref[0, 0, :, pl.ds(q_off, LANES)][:, 0:1]
    causal = (lax.broadcasted_iota(jnp.int32, (blk, blk), 0)
              >= lax.broadcasted_iota(jnp.int32, (blk, blk), 1))

    def step(j, masked):
        off = pl.multiple_of(j * blk, blk)
        kb = k_ref[0, :, pl.ds(off, blk)]
        bias = (c_start - crow_ref[0, 0, :, pl.ds(off, blk)]) * LOG2E
        scores = [jnp.dot(qm_ref[hh], kb, preferred_element_type=F32) for hh in range(HEADS_PER_GROUP)]
        for hh in range(HEADS_PER_GROUP):
            s = scores[hh] + bias[hh:hh + 1, :]
            if masked:
                s = jnp.where(causal, s, -jnp.inf)
            m_old = m_ref[hh]
            m_new = jnp.maximum(m_old, jnp.max(s, axis=-1, keepdims=True))
            p = jnp.exp2(s - jnp.concatenate([m_new] * (blk // LANES), axis=1))
            pv = lax.dot_general(p.astype(BF16), vp_ref[hh, :, pl.ds(off, blk)], NT_DIMS,
                                 preferred_element_type=F32)
            acc_ref[hh] = jnp.exp2(m_old - m_new) * acc_ref[hh] + pv
            m_ref[hh] = m_new

    def body(j, carry):
        step(j, False)
        return carry

    lax.fori_loop(0, i, body, 0)
    step(i, True)
    low = lax.broadcasted_iota(jnp.int32, (blk, LANES), 1) < HEAD_DIM
    pairs = []
    for pp in range(HEADS_PER_GROUP // 2):
        o = [acc_ref[2 * pp + j] / acc_ref[2 * pp + j][:, HEAD_DIM:HEAD_DIM + 1] for j in range(2)]
        pairs.append(jnp.where(low, o[0], pltpu.roll(o[1], HEAD_DIM, axis=1)))
    a_ref[0] = (jnp.concatenate(pairs, axis=1) * sfg_ref[0].astype(F32)).astype(BF16)


def _fox_prompt(q, kb, vb, crow, sfg, blk):
    b, s, _ = q.shape
    ng = H_FOX // HEADS_PER_GROUP
    qspec = pl.BlockSpec((1, blk, GROUP_W), lambda bi, g, i: (bi, i, g))
    kvspec = pl.BlockSpec((1, GROUP_W, s), lambda bi, g, i: (bi, g, 0))
    return pl.pallas_call(
        functools.partial(_fox_prompt_kernel, blk=blk),
        grid=(b, ng, s // blk),
        in_specs=[qspec, kvspec, kvspec,
                  pl.BlockSpec((1, 1, HEADS_PER_GROUP, s), lambda bi, g, i: (bi, g, 0, 0)),
                  qspec],
        out_specs=qspec,
        out_shape=jax.ShapeDtypeStruct((b, s, W_FOX), BF16),
        scratch_shapes=[pltpu.VMEM((HEADS_PER_GROUP, blk, GROUP_W), BF16),
                        pltpu.VMEM((HEADS_PER_GROUP, LANES, s), BF16),
                        pltpu.VMEM((HEADS_PER_GROUP, blk, LANES), F32),
                        pltpu.VMEM((HEADS_PER_GROUP, blk, LANES), F32)],
        compiler_params=_params(("parallel", "parallel", "arbitrary")),
        name="fox_prompt",
    )(q, kb, vb, crow, sfg)


def _ret_head_out(o, g, srg):
    return (_layer_norm(o, g) * srg.astype(F32)).astype(BF16)


def _ret_prompt_kernel(rq_ref, rk_ref, rv_ref, srg_ref, gret_ref, r_ref, st_ref, dec_ref):
    L = CHUNK
    c = pl.program_id(1)

    @pl.when(c == 0)
    def _():
        st_ref[...] = jnp.zeros(st_ref.shape, F32)
        diff = (lax.broadcasted_iota(jnp.int32, (L, L), 0)
                - lax.broadcasted_iota(jnp.int32, (L, L), 1)).astype(F32)
        for h in range(H_RET):
            dec_ref[h] = jnp.where(diff >= 0, jnp.exp(LOG_GAMMA[h] * jnp.maximum(diff, 0.0)), 0.0)

    pos = lax.broadcasted_iota(jnp.int32, (L, 1), 0).astype(F32)
    lane_half = lax.broadcasted_iota(jnp.int32, (L, LANES), 1) // RET_DK
    for pp in range(H_RET // 2):
        qp = rq_ref[0, :, pp * LANES:(pp + 1) * LANES]
        kp = rk_ref[0, :, pp * LANES:(pp + 1) * LANES]
        s_pair = st_ref[0, pp]
        s_pair_b = s_pair.astype(BF16)
        for j in range(2):
            h = 2 * pp + j
            lg = LOG_GAMMA[h]
            qm = jnp.where(lane_half == j, qp, jnp.zeros_like(qp))
            vh = rv_ref[0, :, h * RET_DV:(h + 1) * RET_DV]
            scores = lax.dot_general(qm, kp, NT_DIMS, preferred_element_type=F32) * dec_ref[h]
            intra = jnp.dot(scores.astype(BF16), vh, preferred_element_type=F32)
            cross = jnp.dot(qm, s_pair_b, preferred_element_type=F32) * jnp.exp(lg * (pos + 1.0))
            kd = (kp.astype(F32) * jnp.exp(lg * (L - 1.0 - pos))).astype(BF16)
            upd = lax.dot_general(kd, vh, TN_DIMS, preferred_element_type=F32)
            rows = slice(j * RET_DK, (j + 1) * RET_DK)
            st_ref[0, pp, rows, :] = float(np.exp(lg * L)) * s_pair[rows] + upd[rows]
            cols = slice(h * RET_DV, (h + 1) * RET_DV)
            r_ref[0, :, cols] = _ret_head_out(intra + cross, gret_ref[:, cols], srg_ref[0, :, cols])


def _ret_prompt(rq, rk, rv, srg, g_ret):
    b, s, _ = rq.shape
    qk = pl.BlockSpec((1, CHUNK, W_RQK), lambda bi, c: (bi, c, 0))
    wide = pl.BlockSpec((1, CHUNK, W_RET), lambda bi, c: (bi, c, 0))
    st = pl.BlockSpec((1, H_RET // 2, 2 * RET_DK, RET_DV), lambda bi, c: (bi, 0, 0, 0))
    return pl.pallas_call(
        _ret_prompt_kernel,
        grid=(b, s // CHUNK),
        in_specs=[qk, qk, wide, wide, _const_spec((1, W_RET))],
        out_specs=[wide, st],
        out_shape=[jax.ShapeDtypeStruct((b, s, W_RET), BF16),
                   jax.ShapeDtypeStruct((b, H_RET // 2, 2 * RET_DK, RET_DV), F32)],
        scratch_shapes=[pltpu.VMEM((H_RET, CHUNK, CHUNK), F32)],
        compiler_params=_params(("parallel", "arbitrary")),
        name="ret_prompt",
    )(rq, rk, rv, srg, g_ret)


def _ret_sample_kernel(rq_ref, rk_ref, rv_ref, srg_ref, gret_ref, s0_ref, r_ref, s1_ref, *, n_new):
    rows = rq_ref.shape[0]
    nb = rows // n_new
    two_dk = 2 * RET_DK
    ri = lax.broadcasted_iota(jnp.int32, (rows, rows), 0)
    ci = lax.broadcasted_iota(jnp.int32, (rows, rows), 1)
    same = (ri // n_new) == (ci // n_new)
    diff = ((ri % n_new) - (ci % n_new)).astype(F32)
    pos = (lax.broadcasted_iota(jnp.int32, (rows, 1), 0) % n_new).astype(F32)
    lane_half = lax.broadcasted_iota(jnp.int32, (rows, LANES), 1) // RET_DK
    q_sel = (lax.broadcasted_iota(jnp.int32, (rows, nb * two_dk), 1) // two_dk
             == lax.broadcasted_iota(jnp.int32, (rows, nb * two_dk), 0) // n_new)
    k_sel = (lax.broadcasted_iota(jnp.int32, (nb * two_dk, rows), 0) // two_dk
             == lax.broadcasted_iota(jnp.int32, (nb * two_dk, rows), 1) // n_new)
    row_head = (lax.broadcasted_iota(jnp.int32, (nb * two_dk, 1), 0) % two_dk) // RET_DK

    qp = rq_ref[...]
    kp = rk_ref[...]
    s_old = s0_ref[...].reshape(nb * two_dk, RET_DV)
    s_old_b = s_old.astype(BF16)
    upd = []
    for j in range(2):
        h = 2 * pl.program_id(0) + j
        lg = jnp.where(h == 0, LOG_GAMMA[0], 0.0)
        for hh in range(1, H_RET):
            lg = jnp.where(h == hh, LOG_GAMMA[hh], lg)
        qm = jnp.where(lane_half == j, qp, jnp.zeros_like(qp))
        vh = rv_ref[:, j * RET_DV:(j + 1) * RET_DV]
        decay = jnp.where(same & (diff >= 0), jnp.exp(lg * jnp.maximum(diff, 0.0)), 0.0)
        scores = lax.dot_general(qm, kp, NT_DIMS, preferred_element_type=F32) * decay
        intra = jnp.dot(scores.astype(BF16), vh, preferred_element_type=F32)
        q_exp = jnp.where(q_sel, jnp.tile(qm, (1, nb)), jnp.zeros((), BF16))
        cross = jnp.dot(q_exp, s_old_b, preferred_element_type=F32) * jnp.exp(lg * (pos + 1.0))
        kd = kp.astype(F32) * jnp.exp(lg * (n_new - 1.0 - pos))
        kd_exp = jnp.where(k_sel, jnp.tile(kd.T, (nb, 1)), 0.0).astype(BF16)
        keep = jnp.exp(lg * jnp.full((1, 1), float(n_new), F32))
        upd.append((keep, jnp.dot(kd_exp, vh, preferred_element_type=F32)))
        cols = slice(j * RET_DV, (j + 1) * RET_DV)
        r_ref[:, cols] = _ret_head_out(intra + cross, gret_ref[:, cols], srg_ref[:, cols])
    s_new = jnp.where(row_head == 0, upd[0][0] * s_old + upd[0][1], upd[1][0] * s_old + upd[1][1])
    s1_ref[...] = s_new.reshape(s1_ref.shape)


def _ret_sample(rq, rk, rv, srg, g_ret, state, n_new):
    rows = rq.shape[0]
    nb = rows // n_new
    pair = pl.BlockSpec((rows, LANES), lambda p: (0, p))
    wide = pl.BlockSpec((rows, 2 * RET_DV), lambda p: (0, p))
    st = pl.BlockSpec((nb, 2, RET_DK, RET_DV), lambda p: (0, p, 0, 0))
    return pl.pallas_call(
        functools.partial(_ret_sample_kernel, n_new=n_new),
        grid=(H_RET // 2,),
        in_specs=[pair, pair, wide, wide, pl.BlockSpec((1, 2 * RET_DV), lambda p: (0, p)), st],
        out_specs=[wide, st],
        out_shape=[jax.ShapeDtypeStruct((rows, W_RET), BF16),
                   jax.ShapeDtypeStruct(state.shape, F32)],
        compiler_params=_params(("parallel",)),
        name="ret_sample",
    )(rq, rk, rv, srg, g_ret, state)


def _suffix_sum_lanes(x):
    n = x.shape[1]
    lane = lax.broadcasted_iota(jnp.int32, x.shape, 1)
    incl = x
    sh = 1
    while sh < n:
        incl = incl + jnp.where(lane < n - sh, pltpu.roll(incl, n - sh, axis=1), 0.0)
        sh *= 2
    return incl - x


def _fox_sample_kernel(pt_ref, q_ref, kn_ref, vn_ref, lfn_ref, sfg_ref, *refs, n_new, pages_per_step):
    g = pages_per_step
    k_refs, v_refs, lf_refs = refs[:g], refs[g:2 * g], refs[2 * g:3 * g]
    a_ref, qbd_ref, m_ref, l_ref, acc_ref, tail_ref = refs[3 * g:]
    b = pl.program_id(0)
    c = pl.program_id(1)
    nq = n_new * H_FOX
    row_tok = lax.broadcasted_iota(jnp.int32, (nq, 1), 0) // H_FOX
    head_of_row = lax.broadcasted_iota(jnp.int32, (H_FOX, W_FOX), 0)
    head_of_lane = lax.broadcasted_iota(jnp.int32, (H_FOX, W_FOX), 1) // HEAD_DIM

    def update(s, vt):
        m_new = jnp.maximum(m_ref[...], jnp.max(s, axis=-1, keepdims=True))
        alpha = jnp.exp2(m_ref[...] - m_new)
        p = jnp.exp2(s - m_new)
        l_ref[...] = alpha * l_ref[...] + jnp.sum(p, axis=-1, keepdims=True)
        acc_ref[...] = alpha * acc_ref[...] + lax.dot_general(p.astype(BF16), vt, NT_DIMS,
                                                              preferred_element_type=F32)
        m_ref[...] = m_new

    @pl.when(c == 0)
    def _():
        q = q_ref[0].astype(F32)
        for t in range(n_new):
            qbd_ref[t * H_FOX:(t + 1) * H_FOX, :] = jnp.where(
                head_of_lane == head_of_row, jnp.broadcast_to(q[t:t + 1, :], (H_FOX, W_FOX)), 0.0).astype(BF16)
        tail_ref[...] = jnp.zeros(tail_ref.shape, F32)
        m_ref[...] = jnp.full(m_ref.shape, -1e30, F32)
        l_ref[...] = jnp.zeros(l_ref.shape, F32)
        acc_ref[...] = jnp.zeros(acc_ref.shape, F32)
        lane = lax.broadcasted_iota(jnp.int32, (H_FOX, LANES), 1)
        pfx = lfn_ref[...]
        sh = 1
        while sh < n_new:
            pfx = pfx + jnp.where(lane % n_new >= sh, pltpu.roll(pfx, sh, axis=1), 0.0)
            sh *= 2
        col = lax.broadcasted_iota(jnp.int32, (nq, LANES), 1)
        valid = (col // n_new == b) & (col % n_new <= row_tok)
        s = jnp.dot(qbd_ref[...], kn_ref[...], preferred_element_type=F32)
        update(jnp.where(valid, s - jnp.tile(pfx, (n_new, 1)) * LOG2E, -jnp.inf), vn_ref[...])

    kc = jnp.concatenate([r[0].astype(BF16) for r in k_refs], axis=1)
    vc = jnp.concatenate([r[0].astype(BF16) for r in v_refs], axis=1)
    lf = jnp.concatenate([r[0] for r in lf_refs], axis=1)
    key_bias = (tail_ref[:, 0:1] + _suffix_sum_lanes(lf)) * LOG2E
    tail_ref[...] = tail_ref[...] + jnp.sum(lf, axis=-1, keepdims=True)
    s = jnp.dot(qbd_ref[...], kc, preferred_element_type=F32)
    update(s + jnp.tile(key_bias, (n_new, 1)), vc)

    @pl.when(c == pl.num_programs(1) - 1)
    def _():
        o = acc_ref[...] / l_ref[...]
        sfg = sfg_ref[0].astype(F32)
        for t in range(n_new):
            ot = jnp.where(head_of_lane == head_of_row, o[t * H_FOX:(t + 1) * H_FOX, :], 0.0)
            a_ref[0, t:t + 1, :] = jnp.sum(ot, axis=0, keepdims=True) * sfg[t:t + 1, :]


def _fox_sample(page_table, q, kt_new, vt_new, lft_new, sfg, cache_kt, cache_vt, cache_lft, pages_per_step):
    bd, n_new, _ = q.shape
    n_pages = page_table.shape[1]
    g = pages_per_step
    n_steps = n_pages // g
    nq = n_new * H_FOX

    def tok(n):
        return pl.BlockSpec((1, n_new, n), lambda b, c, pt: (b, 0, 0))

    def whole(arr):
        return pl.BlockSpec(arr.shape, lambda b, c, pt: (0, 0))

    def page_spec(rows, slot):
        return pl.BlockSpec((1, rows, PAGE), lambda b, c, pt: (pt[b, n_pages - (c + 1) * g + slot], 0, 0))

    in_specs = [tok(W_FOX), whole(kt_new), whole(vt_new), whole(lft_new), tok(W_FOX)]
    in_specs += [page_spec(W_FOX, s) for s in range(g)]
    in_specs += [page_spec(W_FOX, s) for s in range(g)]
    in_specs += [page_spec(H_FOX, s) for s in range(g)]
    grid_spec = pltpu.PrefetchScalarGridSpec(
        num_scalar_prefetch=1, grid=(bd, n_steps), in_specs=in_specs, out_specs=tok(W_FOX),
        scratch_shapes=[pltpu.VMEM((nq, W_FOX), BF16), pltpu.VMEM((nq, 1), F32), pltpu.VMEM((nq, 1), F32),
                        pltpu.VMEM((nq, W_FOX), F32), pltpu.VMEM((H_FOX, LANES), F32)])
    return pl.pallas_call(
        functools.partial(_fox_sample_kernel, n_new=n_new, pages_per_step=g),
        grid_spec=grid_spec,
        out_shape=jax.ShapeDtypeStruct((bd, n_new, W_FOX), F32),
        compiler_params=_params(("parallel", "arbitrary")),
        name="fox_sample",
    )(page_table, q, kt_new, vt_new, lft_new, sfg, *([cache_kt] * g), *([cache_vt] * g), *([cache_lft] * g))


def _tail_kernel(a_ref, r_ref, x_ref, woe_ref, gno_ref, wio_ref, gv_ref, ws_ref, bs_ref, woo_ref,
                 *out_refs, n_new):
    rows = x_ref.shape[0]
    x1 = (x_ref[...]
          + jnp.dot(a_ref[...].astype(BF16), woe_ref[:W_FOX, :], preferred_element_type=F32)
          + jnp.dot(r_ref[...], woe_ref[W_FOX:, :], preferred_element_type=F32))
    h = _rms(x1, gno_ref[...]).astype(BF16)
    u = jax.nn.gelu(jnp.dot(h, wio_ref[:, :GM_WIDTH], preferred_element_type=F32))
    v = jax.nn.gelu(jnp.dot(h, wio_ref[:, GM_WIDTH:2 * GM_WIDTH], preferred_element_type=F32))
    gate = jnp.dot(h, wio_ref[:, 2 * GM_WIDTH:], preferred_element_type=F32)
    vn = _layer_norm(v, gv_ref[...])
    if n_new is None:
        y_ref, = out_refs
        tril = (lax.broadcasted_iota(jnp.int32, (CHUNK, CHUNK), 0)
                >= lax.broadcasted_iota(jnp.int32, (CHUNK, CHUNK), 1))
        vb = vn.astype(BF16)
        cols = []
        for g in range(GM_GROUPS):
            wm = jnp.where(tril, ws_ref[g], 0.0).astype(BF16)
            bias = bs_ref[:, g:g + 1]
            parts = [jnp.dot(wm, vb[c * CHUNK:(c + 1) * CHUNK, g * GM_GC:(g + 1) * GM_GC],
                             preferred_element_type=F32) + bias for c in range(rows // CHUNK)]
            cols.append(jnp.concatenate(parts, axis=0))
        sv = jnp.concatenate(cols, axis=1)
    else:
        y_ref, vn_ref = out_refs
        vn_ref[...] = vn
        tok = lax.broadcasted_iota(jnp.int32, (rows, 1), 0) % n_new
        cols = []
        for g in range(GM_GROUPS):
            vg = vn[:, g * GM_GC:(g + 1) * GM_GC]
            acc = jnp.zeros((rows, 1), F32)
            for t in range(n_new):
                acc = jnp.where(tok == t, bs_ref[g, t], acc)
            acc = jnp.broadcast_to(acc, (rows, GM_GC))
            for d in range(n_new):
                coef = jnp.zeros((rows, 1), F32)
                for t in range(d, n_new):
                    coef = jnp.where(tok == t, ws_ref[g, t * n_new + (t - d)], coef)
                src = vg if d == 0 else pltpu.roll(vg, d, axis=0)
                acc = acc + coef * src
            cols.append(acc)
        sv = jnp.concatenate(cols, axis=1)
    z = (u * sv * jax.nn.silu(gate)).astype(BF16)
    y_ref[...] = x1 + jnp.dot(z, woo_ref[...], preferred_element_type=F32)


def _tail(a, r, x2d, woe, gno, wio, gv, ws, bs, woo, tm, n_new):
    m = x2d.shape[0]
    row = lambda n: pl.BlockSpec((tm, n), lambda i: (i, 0))
    if n_new is None:
        mix_specs = [_const_spec(ws.shape), _const_spec(bs.shape)]
        out_specs = [row(D_MODEL)]
        out_shape = [jax.ShapeDtypeStruct((m, D_MODEL), F32)]
    else:
        smem = pl.BlockSpec(memory_space=pltpu.SMEM)
        mix_specs = [smem, smem]
        out_specs = [row(D_MODEL), row(GM_WIDTH)]
        out_shape = [jax.ShapeDtypeStruct((m, D_MODEL), F32), jax.ShapeDtypeStruct((m, GM_WIDTH), F32)]
    return pl.pallas_call(
        functools.partial(_tail_kernel, n_new=n_new),
        grid=(m // tm,),
        in_specs=[row(W_FOX), row(W_RET), row(D_MODEL), _const_spec(woe.shape), _const_spec((1, D_MODEL)),
                  _const_spec(wio.shape), _const_spec((1, GM_WIDTH))] + mix_specs + [_const_spec(woo.shape)],
        out_specs=out_specs, out_shape=out_shape,
        compiler_params=_params(("parallel",)),
        name="tail_prompt" if n_new is None else "tail_sample",
    )(a, r, x2d, woe, gno, wio, gv, ws, bs, woo)


def _rope_tables(pos):
    half = RET_DK // 2
    inv = ROPE_BASE ** (-jnp.arange(half, dtype=F32) / half)
    ang = pos.astype(F32)[:, None] * inv[None, :]
    cos, sin = jnp.cos(ang), jnp.sin(ang)
    return jnp.tile(jnp.concatenate([cos, cos], axis=-1), (1, 2)), jnp.tile(jnp.concatenate([-sin, sin], axis=-1), (1, 2))


def _row_tile(m):
    return 256 if m % 256 == 0 else 128


def kernel(x_prompt, x_sample, cache_k, cache_v, cache_logf, state_ret, page_table, norm_even, w_in_even,
           b_forget, qnorm, knorm, ret_norm, w_out_even, norm_odd, w_in_odd, vnorm_odd, w_spatial, b_spatial,
           w_out_odd):
    bp, s, _ = x_prompt.shape
    bd, n_new, _ = x_sample.shape
    n_pages = page_table.shape[1]
    past = n_pages * PAGE
    assert s % 256 == 0 and (bd * n_new) % CHUNK == 0 and w_in_even.shape[0] == 1 and w_in_odd.shape[0] == 1

    wi = w_in_even[0]
    sizes = (W_FOX, W_FOX, W_FOX, H_FOX, W_FOX, W_RQK, W_RQK, W_RET, W_RET)
    offs = np.concatenate([[0], np.cumsum(sizes)])
    fq, fk, fv, ff, fg, rq, rk, rv, rg = [wi[:, offs[i]:offs[i + 1]] for i in range(9)]
    w_even = jnp.concatenate([fq, fg, rq, rk, rv, rg], axis=1).astype(BF16)
    w_even_t = jnp.concatenate([fk.T, fv.T, ff.T, jnp.zeros((EVEN_ROWS_T - R_FF - H_FOX, D_MODEL), F32)],
                               axis=0).astype(BF16)
    b_f = b_forget[0][:, None]
    g_q = jnp.tile(qnorm[0], H_FOX)[None, :]
    g_k = knorm[0][:, None]
    g_ret = ret_norm[0].reshape(1, W_RET)
    woe = w_out_even[0].astype(BF16)
    wio = w_in_odd[0].astype(BF16)
    woo = w_out_odd[0].astype(BF16)
    g_ne, g_no, g_v = norm_even[0][None, :], norm_odd[0][None, :], vnorm_odd[0][None, :]

    cos_p, sin_p = _rope_tables(jnp.arange(s))
    xp2 = x_prompt.reshape(bp * s, D_MODEL)
    tm = _row_tile(s)
    q, sfg, rqp, rkp, rvp, srg, kt32, ktb, vt32, vtb, lft = _even_proj(
        xp2, bp, g_ne, w_even, w_even_t, b_f, g_q, g_k, cos_p, sin_p, tm)
    ng = H_FOX // HEADS_PER_GROUP
    c_row4 = _cumsum_lanes(lft).reshape(bp, ng, HEADS_PER_GROUP, s)
    r3 = lambda t: t.reshape(bp, s, t.shape[-1])
    a_p = _fox_prompt(r3(q), ktb, vtb, c_row4, r3(sfg), 256)
    r_p, st_p = _ret_prompt(r3(rqp), r3(rkp), r3(rvp), r3(srg), g_ret)
    y_p, = _tail(a_p.reshape(bp * s, W_FOX), r_p.reshape(bp * s, W_RET), xp2, woe, g_no, wio, g_v,
                 w_spatial[0], jnp.transpose(b_spatial[0]), woo, tm, None)

    rows = bd * n_new
    cos_d, sin_d = _rope_tables(past + jnp.arange(n_new))
    xd2 = x_sample.reshape(rows, D_MODEL)
    qd, sfgd, rqd, rkd, rvd, srgd, kt32d, ktbd, vt32d, vtbd, lftd = _even_proj(
        xd2, 1, g_ne, w_even, w_even_t, b_f, g_q, g_k, jnp.tile(cos_d, (bd, 1)), jnp.tile(sin_d, (bd, 1)), rows)
    d3 = lambda t: t.reshape(bd, n_new, t.shape[-1])
    n_phys = cache_k.shape[1]
    pages_t = lambda cache: jnp.transpose(cache[0], (0, 2, 3, 1)).reshape(n_phys, W_FOX, PAGE)
    a_d = _fox_sample(page_table, d3(qd), ktbd[0], vtbd[0], lftd[0], d3(sfgd), pages_t(cache_k), pages_t(cache_v),
                      jnp.transpose(cache_logf[0], (0, 2, 1)), 8 if n_pages % 8 == 0 else 1)
    r_d, st_d = _ret_sample(rqd, rkd, rvd, srgd, g_ret, state_ret[0], n_new)
    ws_new = w_spatial[0][:, :n_new, :n_new].reshape(GM_GROUPS, n_new * n_new)
    y_d, gv_d = _tail(a_d.reshape(rows, W_FOX), r_d, xd2, woe, g_no, wio, g_v,
                      ws_new, b_spatial[0][:, :n_new], woo, rows, n_new)

    heads_last = lambda t, n, length: jnp.transpose(t.reshape(n, H_FOX, HEAD_DIM, length), (0, 3, 1, 2))[None]
    tokens_first = lambda t, n: jnp.transpose(t[0], (1, 0)).reshape(1, bd, n_new, *n)
    return (y_p.reshape(bp, s, D_MODEL), y_d.reshape(bd, n_new, D_MODEL),
            heads_last(kt32, bp, s), heads_last(vt32, bp, s),
            jnp.transpose(lft, (0, 2, 1))[None], st_p.reshape(1, bp, H_RET, RET_DK, RET_DV),
            tokens_first(kt32d, (H_FOX, HEAD_DIM)), tokens_first(vt32d, (H_FOX, HEAD_DIM)),
            tokens_first(lftd, (H_FOX,)), st_d[None], gv_d.reshape(1, bd, n_new, GM_WIDTH))
```

```python
import functools

import numpy as np
import jax
import jax.numpy as jnp
from jax import lax
from jax.experimental import pallas as pl
from jax.experimental.pallas import tpu as pltpu

D_MODEL = 1024
HEAD_DIM = 64
H_FOX = 8
W_FOX = H_FOX * HEAD_DIM
H_RET = 8
RET_DK = 64
RET_DV = 128
W_RET = H_RET * RET_DV
W_RQK = H_RET * RET_DK
ROPE_BASE = 10000.0
GM_WIDTH = 2 * D_MODEL
GM_GROUPS = 8
GM_GC = GM_WIDTH // GM_GROUPS
CHUNK = 128
PAGE = 128
EPS = 1e-6
F32 = jnp.float32
BF16 = jnp.bfloat16

LANES = 128
VMEM_LIMIT = 56 * 1024 * 1024

C_FG, C_RQ, C_RK, C_RV, C_RG, EVEN_COLS = 0, 512, 1024, 1536, 2560, 3584
R_FQ, R_FK, R_FV, R_FF, EVEN_ROWS_T = 0, 512, 1024, 1536, 1552

LOG_GAMMA = [float(np.log(np.float32(1.0) - np.float32(2.0) ** np.float32(-5.0 - h))) for h in range(H_RET)]

LOG2E = float(np.log2(np.e))
NT_DIMS = (((1,), (1,)), ((), ()))
TN_DIMS = (((0,), (0,)), ((), ()))


def _params(sem):
    return pltpu.CompilerParams(dimension_semantics=sem, vmem_limit_bytes=VMEM_LIMIT)


def _const_spec(shape):
    nd = len(shape)
    return pl.BlockSpec(shape, lambda *_: (0,) * nd, pipeline_mode=pl.Buffered(1))


def _rms(x, g):
    ms = jnp.mean(x * x, axis=-1, keepdims=True)
    return x * lax.rsqrt(ms + EPS) * g


def _layer_norm(x, g):
    xc = x - jnp.mean(x, axis=-1, keepdims=True)
    return xc * lax.rsqrt(jnp.mean(xc * xc, axis=-1, keepdims=True) + EPS) * g


def _rope(x, cos, sin_signed):
    rows = x.shape[0]
    first_half = (lax.broadcasted_iota(jnp.int32, (rows, LANES), 1) % RET_DK) < RET_DK // 2
    outs = []
    for c in range(x.shape[1] // LANES):
        xc = x[:, c * LANES:(c + 1) * LANES]
        nxt = pltpu.roll(xc, LANES - RET_DK // 2, axis=1)
        prv = pltpu.roll(xc, RET_DK // 2, axis=1)
        outs.append(xc * cos + jnp.where(first_half, nxt, prv) * sin_signed)
    return jnp.concatenate(outs, axis=-1)


def _log_sigmoid(x):
    return jnp.minimum(x, 0.0) - jnp.log1p(jnp.exp(-jnp.abs(x)))


def _head_rms_t(x, g):
    tokens = x.shape[1]
    x = x.reshape(H_FOX, HEAD_DIM, tokens)
    x = x * lax.rsqrt(jnp.mean(x * x, axis=1, keepdims=True) + EPS) * g[None]
    return x.reshape(W_FOX, tokens)


def _even_proj_kernel(x_ref, g_ref, w_ref, wt_ref, bf_ref, gq_ref, gk_ref, cos_ref, sin_ref,
                      sfg_ref, rq_ref, rk_ref, rv_ref, srg_ref, kn_ref,
                      qt_ref, kt32_ref, ktb_ref, vt32_ref, vtb_ref, lft_ref):
    h = _rms(x_ref[...], g_ref[...]).astype(BF16)

    def proj(lo, hi):
        return jnp.dot(h, w_ref[:, lo:hi], preferred_element_type=F32)

    sfg_ref[...] = jax.nn.silu(proj(C_FG, C_RQ)).astype(BF16)
    cos = cos_ref[...]
    sin = sin_ref[...]
    rq_ref[...] = _rope(proj(C_RQ, C_RK), cos, sin).astype(BF16)
    rk_ref[...] = (_rope(proj(C_RK, C_RV), cos, sin) * (RET_DK ** -0.5)).astype(BF16)
    rv_ref[...] = proj(C_RV, C_RG).astype(BF16)
    srg_ref[...] = jax.nn.silu(proj(C_RG, EVEN_COLS)).astype(BF16)

    t = lax.dot_general(wt_ref[...], h, NT_DIMS, preferred_element_type=F32)
    qt_ref[0] = (_head_rms_t(t[R_FQ:R_FK], gq_ref[...]) * (HEAD_DIM ** -0.5 * LOG2E)).astype(BF16)
    kt = _head_rms_t(t[R_FK:R_FV], gk_ref[...])
    kt32_ref[0] = kt
    ktb_ref[0] = kt.astype(BF16)
    kn_ref[...] = kt.T.astype(BF16)
    vt = t[R_FV:R_FF]
    vt32_ref[0] = vt
    vtb_ref[0] = vt.astype(BF16)
    lft_ref[0] = _log_sigmoid(t[R_FF:R_FF + H_FOX] + bf_ref[...])


def _even_proj(x2d, n_seq, g_norm, w, wt, b_f, g_q, g_k, cos, sin, tm):
    m = x2d.shape[0]
    s = m // n_seq
    per_seq = s // tm
    row = lambda n: pl.BlockSpec((tm, n), lambda i: (i, 0))
    tab = pl.BlockSpec((tm, LANES), lambda i: (i % per_seq, 0))
    col = lambda n: pl.BlockSpec((1, n, tm), lambda i: (i // per_seq, 0, i % per_seq))
    outs = [(W_FOX, BF16), (W_RQK, BF16), (W_RQK, BF16), (W_RET, BF16), (W_RET, BF16), (W_FOX, BF16)]
    outs_t = [(W_FOX, BF16), (W_FOX, F32), (W_FOX, BF16), (W_FOX, F32), (W_FOX, BF16), (H_FOX, F32)]
    return pl.pallas_call(
        _even_proj_kernel,
        grid=(m // tm,),
        in_specs=[row(D_MODEL), _const_spec((1, D_MODEL)), _const_spec((D_MODEL, EVEN_COLS)),
                  _const_spec((EVEN_ROWS_T, D_MODEL)), _const_spec((H_FOX, 1)), _const_spec((HEAD_DIM, 1)),
                  _const_spec((HEAD_DIM, 1)), tab, tab],
        out_specs=[row(n) for n, _ in outs] + [col(n) for n, _ in outs_t],
        out_shape=([jax.ShapeDtypeStruct((m, n), dt) for n, dt in outs]
                   + [jax.ShapeDtypeStruct((n_seq, n, s), dt) for n, dt in outs_t]),
        compiler_params=_params(("parallel",)),
        name="even_proj",
    )(x2d, g_norm, w, wt, b_f, g_q, g_k, cos, sin)


def _cumsum_kernel(x_ref, o_ref):
    x = x_ref[0]
    n = x.shape[1]
    lane = lax.broadcasted_iota(jnp.int32, x.shape, 1)
    sh = 1
    while sh < n:
        x = x + jnp.where(lane >= sh, pltpu.roll(x, sh, axis=1), 0.0)
        sh *= 2
    o_ref[0] = x


def _cumsum_lanes(x):
    b, h, s = x.shape
    spec = pl.BlockSpec((1, h, s), lambda i: (i, 0, 0))
    return pl.pallas_call(
        _cumsum_kernel, grid=(b,), in_specs=[spec], out_specs=spec,
        out_shape=jax.ShapeDtypeStruct(x.shape, F32),
        compiler_params=_params(("parallel",)), name="logf_cumsum",
    )(x)


HEADS_PER_GROUP = 4
GROUP_W = HEADS_PER_GROUP * HEAD_DIM


def _fox_prompt_kernel(qt_ref, k_ref, v_ref, ccol_ref, sfg_ref, a_ref,
                       qm_ref, vp_ref, cb_ref, m_ref, acc_ref, s_ref, p_ref, *, blk):
    i = pl.program_id(2)
    s_len = k_ref.shape[1]

    @pl.when(i == 0)
    def _():
        ones_row = jnp.where(lax.broadcasted_iota(jnp.int32, (HEAD_DIM, s_len), 0) == 0, 1.0, 0.0).astype(BF16)
        ccol = ccol_ref[0, 0] * (-LOG2E)
        for hh in range(HEADS_PER_GROUP):
            vp_ref[hh, :HEAD_DIM, :] = v_ref[0, hh * HEAD_DIM:(hh + 1) * HEAD_DIM, :]
            vp_ref[hh, HEAD_DIM:, :] = ones_row
            cb_ref[hh] = jnp.broadcast_to(ccol[:, hh:hh + 1], (s_len, LANES))

    qt = qt_ref[0]
    row_head = lax.broadcasted_iota(jnp.int32, (GROUP_W, blk), 0) // HEAD_DIM
    for hh in range(HEADS_PER_GROUP):
        qm_ref[hh] = jnp.where(row_head == hh, qt, jnp.zeros_like(qt))
    m_ref[...] = jnp.full(m_ref.shape, -1e30, F32)
    acc_ref[...] = jnp.zeros(acc_ref.shape, F32)
    strips = [(hh, slice(c * LANES, (c + 1) * LANES), c) for hh in range(HEADS_PER_GROUP) for c in range(blk // LANES)]

    def raw_scores(j, slot):
        off = pl.multiple_of(j * blk, blk)
        kb = k_ref[0, pl.ds(off, blk), :]
        for hh in range(HEADS_PER_GROUP):
            bias = cb_ref[hh, pl.ds(off, blk), :]
            s_ref[slot, hh] = (jnp.dot(kb, qm_ref[hh], preferred_element_type=F32)
                               + jnp.concatenate([bias] * (blk // LANES), axis=1))

    def consume(j, slot, masked):
        off = pl.multiple_of(j * blk, blk)

        def scores(hh, cols, c):
            s = s_ref[slot, hh, :, cols]
            if masked:
                causal = (lax.broadcasted_iota(jnp.int32, (blk, LANES), 0)
                          <= lax.broadcasted_iota(jnp.int32, (blk, LANES), 1) + c * LANES)
                s = jnp.where(causal, s, -jnp.inf)
            return s

        for hh, cols, c in strips:
            m_old = m_ref[hh, :, cols]
            m_new = jnp.maximum(m_old, jnp.max(scores(hh, cols, c), axis=0, keepdims=True))
            acc_ref[hh, :, cols] = jnp.exp2(m_old - m_new) * acc_ref[hh, :, cols]
            m_ref[hh, :, cols] = m_new
        for hh, cols, c in strips:
            p_ref[hh, :, cols] = jnp.exp2(scores(hh, cols, c) - m_ref[hh, :, cols]).astype(BF16)
        for hh in range(HEADS_PER_GROUP):
            acc_ref[hh] += jnp.dot(vp_ref[hh, :, pl.ds(off, blk)], p_ref[hh],
                                   preferred_element_type=F32)

    raw_scores(0, 0)

    def pair(t, carry):
        j = 2 * t
        raw_scores(j + 1, 1)
        consume(j, 0, False)
        raw_scores(j + 2, 0)
        consume(j + 1, 1, False)
        return carry

    lax.fori_loop(0, i // 2, pair, 0)

    @pl.when(i % 2 == 1)
    def _():
        raw_scores(i, 1)
        consume(i - 1, 0, False)
        consume(i, 1, True)

    @pl.when(i % 2 == 0)
    def _():
        consume(i, 0, True)

    ot = jnp.concatenate([acc_ref[hh, :HEAD_DIM, :] / acc_ref[hh, HEAD_DIM:HEAD_DIM + 1, :]
                          for hh in range(HEADS_PER_GROUP)], axis=0)
    a_ref[0] = (ot.T * sfg_ref[0].astype(F32)).astype(BF16)


def _fox_prompt(qt, kn, vtb, ccol, sfg, blk):
    b, s, _ = kn.shape
    ng = H_FOX // HEADS_PER_GROUP
    rows = pl.BlockSpec((1, blk, GROUP_W), lambda bi, g, i: (bi, i, g))
    return pl.pallas_call(
        functools.partial(_fox_prompt_kernel, blk=blk),
        grid=(b, ng, s // blk),
        in_specs=[pl.BlockSpec((1, GROUP_W, blk), lambda bi, g, i: (bi, g, i)),
                  pl.BlockSpec((1, s, GROUP_W), lambda bi, g, i: (bi, 0, g)),
                  pl.BlockSpec((1, GROUP_W, s), lambda bi, g, i: (bi, g, 0)),
                  pl.BlockSpec((1, 1, s, HEADS_PER_GROUP), lambda bi, g, i: (bi, g, 0, 0)),
                  rows],
        out_specs=rows,
        out_shape=jax.ShapeDtypeStruct((b, s, W_FOX), BF16),
        scratch_shapes=[pltpu.VMEM((HEADS_PER_GROUP, GROUP_W, blk), BF16),
                        pltpu.VMEM((HEADS_PER_GROUP, LANES, s), BF16),
                        pltpu.VMEM((HEADS_PER_GROUP, s, LANES), F32),
                        pltpu.VMEM((HEADS_PER_GROUP, 1, blk), F32),
                        pltpu.VMEM((HEADS_PER_GROUP, LANES, blk), F32),
                        pltpu.VMEM((2, HEADS_PER_GROUP, blk, blk), F32),
                        pltpu.VMEM((HEADS_PER_GROUP, blk, blk), BF16)],
        compiler_params=_params(("parallel", "parallel", "arbitrary")),
        name="fox_prompt",
    )(qt, kn, vtb, ccol, sfg)


def _ret_head_out(o, g, srg):
    return (_layer_norm(o, g) * srg.astype(F32)).astype(BF16)


def _ret_prompt_kernel(rq_ref, rk_ref, rv_ref, srg_ref, gret_ref, r_ref, st_ref, dec_ref, kdec_ref, *, chunks):
    L = CHUNK
    c = pl.program_id(1)

    @pl.when(c == 0)
    def _():
        st_ref[...] = jnp.zeros(st_ref.shape, F32)
        diff = (lax.broadcasted_iota(jnp.int32, (L, L), 0)
                - lax.broadcasted_iota(jnp.int32, (L, L), 1)).astype(F32)
        for h in range(H_RET):
            dec_ref[h] = jnp.where(diff >= 0, jnp.exp(LOG_GAMMA[h] * jnp.maximum(diff, 0.0)), 0.0)
        left = (L - 1.0) - lax.broadcasted_iota(jnp.int32, (L, LANES), 0).astype(F32)
        first = lax.broadcasted_iota(jnp.int32, (L, LANES), 1) < RET_DK
        for pp in range(H_RET // 2):
            kdec_ref[pp] = jnp.exp(jnp.where(first, LOG_GAMMA[2 * pp], LOG_GAMMA[2 * pp + 1]) * left)

    pos = lax.broadcasted_iota(jnp.int32, (L, 1), 0).astype(F32)
    lane_half = lax.broadcasted_iota(jnp.int32, (L, LANES), 1) // RET_DK
    for ci in range(chunks):
        rows_c = slice(ci * L, (ci + 1) * L)
        first_stage = []
        for pp in range(H_RET // 2):
            qp = rq_ref[0, rows_c, pp * LANES:(pp + 1) * LANES]
            kp = rk_ref[0, rows_c, pp * LANES:(pp + 1) * LANES]
            s_pair = st_ref[0, pp]
            s_pair_b = s_pair.astype(BF16)
            kd = (kp.astype(F32) * kdec_ref[pp]).astype(BF16)
            for j in range(2):
                h = 2 * pp + j
                qm = jnp.where(lane_half == j, qp, jnp.zeros_like(qp))
                vh = rv_ref[0, rows_c, h * RET_DV:(h + 1) * RET_DV]
                scores = lax.dot_general(qm, kp, NT_DIMS, preferred_element_type=F32)
                cross = jnp.dot(qm, s_pair_b, preferred_element_type=F32)
                upd = lax.dot_general(kd, vh, TN_DIMS, preferred_element_type=F32)
                first_stage.append((scores, cross, upd, vh, s_pair))
        outs = []
        for h, (scores, cross, upd, vh, s_pair) in enumerate(first_stage):
            lg = LOG_GAMMA[h]
            intra = jnp.dot((scores * dec_ref[h]).astype(BF16), vh, preferred_element_type=F32)
            outs.append(intra + cross * jnp.exp(lg * (pos + 1.0)))
            rows = slice((h % 2) * RET_DK, (h % 2 + 1) * RET_DK)
            st_ref[0, h // 2, rows, :] = float(np.exp(lg * L)) * s_pair[rows] + upd[rows]
        for h, o in enumerate(outs):
            cols = slice(h * RET_DV, (h + 1) * RET_DV)
            r_ref[0, rows_c, cols] = _ret_head_out(o, gret_ref[:, cols], srg_ref[0, rows_c, cols])


def _ret_prompt(rq, rk, rv, srg, g_ret):
    b, s, _ = rq.shape
    chunks = 2 if s % (2 * CHUNK) == 0 else 1
    rows = chunks * CHUNK
    qk = pl.BlockSpec((1, rows, W_RQK), lambda bi, c: (bi, c, 0))
    wide = pl.BlockSpec((1, rows, W_RET), lambda bi, c: (bi, c, 0))
    st = pl.BlockSpec((1, H_RET // 2, 2 * RET_DK, RET_DV), lambda bi, c: (bi, 0, 0, 0))
    return pl.pallas_call(
        functools.partial(_ret_prompt_kernel, chunks=chunks),
        grid=(b, s // rows),
        in_specs=[qk, qk, wide, wide, _const_spec((1, W_RET))],
        out_specs=[wide, st],
        out_shape=[jax.ShapeDtypeStruct((b, s, W_RET), BF16),
                   jax.ShapeDtypeStruct((b, H_RET // 2, 2 * RET_DK, RET_DV), F32)],
        scratch_shapes=[pltpu.VMEM((H_RET, CHUNK, CHUNK), F32), pltpu.VMEM((H_RET // 2, CHUNK, LANES), F32)],
        compiler_params=_params(("parallel", "arbitrary")),
        name="ret_prompt",
    )(rq, rk, rv, srg, g_ret)


def _ret_sample_kernel(rq_ref, rk_ref, rv_ref, srg_ref, gret_ref, s0_ref, r_ref, s1_ref, *, n_new):
    rows = rq_ref.shape[0]
    nb = rows // n_new
    two_dk = 2 * RET_DK
    ri = lax.broadcasted_iota(jnp.int32, (rows, rows), 0)
    ci = lax.broadcasted_iota(jnp.int32, (rows, rows), 1)
    same = (ri // n_new) == (ci // n_new)
    diff = ((ri % n_new) - (ci % n_new)).astype(F32)
    pos = (lax.broadcasted_iota(jnp.int32, (rows, 1), 0) % n_new).astype(F32)
    lane_half = lax.broadcasted_iota(jnp.int32, (rows, LANES), 1) // RET_DK
    q_sel = (lax.broadcasted_iota(jnp.int32, (rows, nb * two_dk), 1) // two_dk
             == lax.broadcasted_iota(jnp.int32, (rows, nb * two_dk), 0) // n_new)
    k_sel = (lax.broadcasted_iota(jnp.int32, (nb * two_dk, rows), 0) // two_dk
             == lax.broadcasted_iota(jnp.int32, (nb * two_dk, rows), 1) // n_new)
    row_head = (lax.broadcasted_iota(jnp.int32, (nb * two_dk, 1), 0) % two_dk) // RET_DK

    qp = rq_ref[...]
    kp = rk_ref[...]
    s_old = s0_ref[...].reshape(nb * two_dk, RET_DV)
    s_old_b = s_old.astype(BF16)
    upd = []
    for j in range(2):
        h = 2 * pl.program_id(0) + j
        lg = jnp.where(h == 0, LOG_GAMMA[0], 0.0)
        for hh in range(1, H_RET):
            lg = jnp.where(h == hh, LOG_GAMMA[hh], lg)
        qm = jnp.where(lane_half == j, qp, jnp.zeros_like(qp))
        vh = rv_ref[:, j * RET_DV:(j + 1) * RET_DV]
        decay = jnp.where(same & (diff >= 0), jnp.exp(lg * jnp.maximum(diff, 0.0)), 0.0)
        scores = lax.dot_general(qm, kp, NT_DIMS, preferred_element_type=F32) * decay
        intra = jnp.dot(scores.astype(BF16), vh, preferred_element_type=F32)
        q_exp = jnp.where(q_sel, jnp.tile(qm, (1, nb)), jnp.zeros((), BF16))
        cross = jnp.dot(q_exp, s_old_b, preferred_element_type=F32) * jnp.exp(lg * (pos + 1.0))
        kd = kp.astype(F32) * jnp.exp(lg * (n_new - 1.0 - pos))
        kd_exp = jnp.where(k_sel, jnp.tile(kd.T, (nb, 1)), 0.0).astype(BF16)
        keep = jnp.exp(lg * jnp.full((1, 1), float(n_new), F32))
        upd.append((keep, jnp.dot(kd_exp, vh, preferred_element_type=F32)))
        cols = slice(j * RET_DV, (j + 1) * RET_DV)
        r_ref[:, cols] = _ret_head_out(intra + cross, gret_ref[:, cols], srg_ref[:, cols])
    s_new = jnp.where(row_head == 0, upd[0][0] * s_old + upd[0][1], upd[1][0] * s_old + upd[1][1])
    s1_ref[...] = s_new.reshape(s1_ref.shape)


def _ret_sample(rq, rk, rv, srg, g_ret, state, n_new):
    rows = rq.shape[0]
    nb = rows // n_new
    pair = pl.BlockSpec((rows, LANES), lambda p: (0, p))
    wide = pl.BlockSpec((rows, 2 * RET_DV), lambda p: (0, p))
    st = pl.BlockSpec((nb, 2, RET_DK, RET_DV), lambda p: (0, p, 0, 0))
    return pl.pallas_call(
        functools.partial(_ret_sample_kernel, n_new=n_new),
        grid=(H_RET // 2,),
        in_specs=[pair, pair, wide, wide, pl.BlockSpec((1, 2 * RET_DV), lambda p: (0, p)), st],
        out_specs=[wide, st],
        out_shape=[jax.ShapeDtypeStruct((rows, W_RET), BF16),
                   jax.ShapeDtypeStruct(state.shape, F32)],
        compiler_params=_params(("parallel",)),
        name="ret_sample",
    )(rq, rk, rv, srg, g_ret, state)


def _suffix_sum_lanes(x):
    n = x.shape[1]
    lane = lax.broadcasted_iota(jnp.int32, x.shape, 1)
    incl = x
    sh = 1
    while sh < n:
        incl = incl + jnp.where(lane < n - sh, pltpu.roll(incl, n - sh, axis=1), 0.0)
        sh *= 2
    return incl - x


def _fox_sample_kernel(pt_ref, q_ref, kn_ref, vn_ref, lfn_ref, sfg_ref, *refs, n_new, pages_per_step):
    g = pages_per_step
    k_refs, v_refs, lf_refs = refs[:g], refs[g:2 * g], refs[2 * g:3 * g]
    a_ref, qbd_ref, m_ref, l_ref, acc_ref, tail_ref = refs[3 * g:]
    b = pl.program_id(0)
    c = pl.program_id(1)
    nq = n_new * H_FOX
    row_tok = lax.broadcasted_iota(jnp.int32, (nq, 1), 0) // H_FOX
    head_of_row = lax.broadcasted_iota(jnp.int32, (H_FOX, W_FOX), 0)
    head_of_lane = lax.broadcasted_iota(jnp.int32, (H_FOX, W_FOX), 1) // HEAD_DIM

    def update(s, vt):
        m_new = jnp.maximum(m_ref[...], jnp.max(s, axis=-1, keepdims=True))
        alpha = jnp.exp2(m_ref[...] - m_new)
        p = jnp.exp2(s - m_new)
        l_ref[...] = alpha * l_ref[...] + jnp.sum(p, axis=-1, keepdims=True)
        acc_ref[...] = alpha * acc_ref[...] + lax.dot_general(p.astype(BF16), vt, NT_DIMS,
                                                              preferred_element_type=F32)
        m_ref[...] = m_new

    @pl.when(c == 0)
    def _():
        q = q_ref[0].astype(F32)
        for t in range(n_new):
            qbd_ref[t * H_FOX:(t + 1) * H_FOX, :] = jnp.where(
                head_of_lane == head_of_row, jnp.broadcast_to(q[t:t + 1, :], (H_FOX, W_FOX)), 0.0).astype(BF16)
        tail_ref[...] = jnp.zeros(tail_ref.shape, F32)
        m_ref[...] = jnp.full(m_ref.shape, -1e30, F32)
        l_ref[...] = jnp.zeros(l_ref.shape, F32)
        acc_ref[...] = jnp.zeros(acc_ref.shape, F32)
        lane = lax.broadcasted_iota(jnp.int32, (H_FOX, LANES), 1)
        pfx = lfn_ref[...]
        sh = 1
        while sh < n_new:
            pfx = pfx + jnp.where(lane % n_new >= sh, pltpu.roll(pfx, sh, axis=1), 0.0)
            sh *= 2
        col = lax.broadcasted_iota(jnp.int32, (nq, LANES), 1)
        valid = (col // n_new == b) & (col % n_new <= row_tok)
        s = jnp.dot(qbd_ref[...], kn_ref[...], preferred_element_type=F32)
        update(jnp.where(valid, s - jnp.tile(pfx, (n_new, 1)) * LOG2E, -jnp.inf), vn_ref[...])

    kc = jnp.concatenate([r[0].astype(BF16) for r in k_refs], axis=1)
    vc = jnp.concatenate([r[0].astype(BF16) for r in v_refs], axis=1)
    lf = jnp.concatenate([r[0] for r in lf_refs], axis=1)
    key_bias = (tail_ref[:, 0:1] + _suffix_sum_lanes(lf)) * LOG2E
    tail_ref[...] = tail_ref[...] + jnp.sum(lf, axis=-1, keepdims=True)
    s = jnp.dot(qbd_ref[...], kc, preferred_element_type=F32)
    update(s + jnp.tile(key_bias, (n_new, 1)), vc)

    @pl.when(c == pl.num_programs(1) - 1)
    def _():
        o = acc_ref[...] / l_ref[...]
        sfg = sfg_ref[0].astype(F32)
        for t in range(n_new):
            ot = jnp.where(head_of_lane == head_of_row, o[t * H_FOX:(t + 1) * H_FOX, :], 0.0)
            a_ref[0, t:t + 1, :] = jnp.sum(ot, axis=0, keepdims=True) * sfg[t:t + 1, :]


def _fox_sample(page_table, q, kt_new, vt_new, lft_new, sfg, cache_kt, cache_vt, cache_lft, pages_per_step):
    bd, n_new, _ = q.shape
    n_pages = page_table.shape[1]
    g = pages_per_step
    n_steps = n_pages // g
    nq = n_new * H_FOX

    def tok(n):
        return pl.BlockSpec((1, n_new, n), lambda b, c, pt: (b, 0, 0))

    def whole(arr):
        return pl.BlockSpec(arr.shape, lambda b, c, pt: (0, 0))

    def page_spec(rows, slot):
        return pl.BlockSpec((1, rows, PAGE), lambda b, c, pt: (pt[b, n_pages - (c + 1) * g + slot], 0, 0))

    in_specs = [tok(W_FOX), whole(kt_new), whole(vt_new), whole(lft_new), tok(W_FOX)]
    in_specs += [page_spec(W_FOX, s) for s in range(g)]
    in_specs += [page_spec(W_FOX, s) for s in range(g)]
    in_specs += [page_spec(H_FOX, s) for s in range(g)]
    grid_spec = pltpu.PrefetchScalarGridSpec(
        num_scalar_prefetch=1, grid=(bd, n_steps), in_specs=in_specs, out_specs=tok(W_FOX),
        scratch_shapes=[pltpu.VMEM((nq, W_FOX), BF16), pltpu.VMEM((nq, 1), F32), pltpu.VMEM((nq, 1), F32),
                        pltpu.VMEM((nq, W_FOX), F32), pltpu.VMEM((H_FOX, LANES), F32)])
    return pl.pallas_call(
        functools.partial(_fox_sample_kernel, n_new=n_new, pages_per_step=g),
        grid_spec=grid_spec,
        out_shape=jax.ShapeDtypeStruct((bd, n_new, W_FOX), F32),
        compiler_params=_params(("parallel", "arbitrary")),
        name="fox_sample",
    )(page_table, q, kt_new, vt_new, lft_new, sfg, *([cache_kt] * g), *([cache_vt] * g), *([cache_lft] * g))


def _tail_kernel(a_ref, r_ref, x_ref, woe_ref, gno_ref, wio_ref, gv_ref, ws_ref, bs_ref, woo_ref,
                 *out_refs, n_new):
    rows = x_ref.shape[0]
    x1 = (x_ref[...]
          + jnp.dot(a_ref[...].astype(BF16), woe_ref[:W_FOX, :], preferred_element_type=F32)
          + jnp.dot(r_ref[...], woe_ref[W_FOX:, :], preferred_element_type=F32))
    h = _rms(x1, gno_ref[...]).astype(BF16)
    u = jax.nn.gelu(jnp.dot(h, wio_ref[:, :GM_WIDTH], preferred_element_type=F32))
    v = jax.nn.gelu(jnp.dot(h, wio_ref[:, GM_WIDTH:2 * GM_WIDTH], preferred_element_type=F32))
    gate = jnp.dot(h, wio_ref[:, 2 * GM_WIDTH:], preferred_element_type=F32)
    vn = _layer_norm(v, gv_ref[...])
    if n_new is None:
        y_ref, = out_refs
        tril = (lax.broadcasted_iota(jnp.int32, (CHUNK, CHUNK), 0)
                >= lax.broadcasted_iota(jnp.int32, (CHUNK, CHUNK), 1))
        vb = vn.astype(BF16)
        cols = []
        for g in range(GM_GROUPS):
            wm = jnp.where(tril, ws_ref[g], 0.0).astype(BF16)
            bias = bs_ref[:, g:g + 1]
            parts = [jnp.dot(wm, vb[c * CHUNK:(c + 1) * CHUNK, g * GM_GC:(g + 1) * GM_GC],
                             preferred_element_type=F32) + bias for c in range(rows // CHUNK)]
            cols.append(jnp.concatenate(parts, axis=0))
        sv = jnp.concatenate(cols, axis=1)
    else:
        y_ref, vn_ref = out_refs
        vn_ref[...] = vn
        tok = lax.broadcasted_iota(jnp.int32, (rows, 1), 0) % n_new
        cols = []
        for g in range(GM_GROUPS):
            vg = vn[:, g * GM_GC:(g + 1) * GM_GC]
            acc = jnp.zeros((rows, 1), F32)
            for t in range(n_new):
                acc = jnp.where(tok == t, bs_ref[g, t], acc)
            acc = jnp.broadcast_to(acc, (rows, GM_GC))
            for d in range(n_new):
                coef = jnp.zeros((rows, 1), F32)
                for t in range(d, n_new):
                    coef = jnp.where(tok == t, ws_ref[g, t * n_new + (t - d)], coef)
                src = vg if d == 0 else pltpu.roll(vg, d, axis=0)
                acc = acc + coef * src
            cols.append(acc)
        sv = jnp.concatenate(cols, axis=1)
    z = (u * sv * jax.nn.silu(gate)).astype(BF16)
    y_ref[...] = x1 + jnp.dot(z, woo_ref[...], preferred_element_type=F32)


def _tail(a, r, x2d, woe, gno, wio, gv, ws, bs, woo, tm, n_new):
    m = x2d.shape[0]
    row = lambda n: pl.BlockSpec((tm, n), lambda i: (i, 0))
    if n_new is None:
        mix_specs = [_const_spec(ws.shape), _const_spec(bs.shape)]
        out_specs = [row(D_MODEL)]
        out_shape = [jax.ShapeDtypeStruct((m, D_MODEL), F32)]
    else:
        smem = pl.BlockSpec(memory_space=pltpu.SMEM)
        mix_specs = [smem, smem]
        out_specs = [row(D_MODEL), row(GM_WIDTH)]
        out_shape = [jax.ShapeDtypeStruct((m, D_MODEL), F32), jax.ShapeDtypeStruct((m, GM_WIDTH), F32)]
    return pl.pallas_call(
        functools.partial(_tail_kernel, n_new=n_new),
        grid=(m // tm,),
        in_specs=[row(W_FOX), row(W_RET), row(D_MODEL), _const_spec(woe.shape), _const_spec((1, D_MODEL)),
                  _const_spec(wio.shape), _const_spec((1, GM_WIDTH))] + mix_specs + [_const_spec(woo.shape)],
        out_specs=out_specs, out_shape=out_shape,
        compiler_params=_params(("parallel",)),
        name="tail_prompt" if n_new is None else "tail_sample",
    )(a, r, x2d, woe, gno, wio, gv, ws, bs, woo)


def _rope_tables(pos):
    half = RET_DK // 2
    inv = ROPE_BASE ** (-jnp.arange(half, dtype=F32) / half)
    ang = pos.astype(F32)[:, None] * inv[None, :]
    cos, sin = jnp.cos(ang), jnp.sin(ang)
    return jnp.tile(jnp.concatenate([cos, cos], axis=-1), (1, 2)), jnp.tile(jnp.concatenate([-sin, sin], axis=-1), (1, 2))


def _row_tile(m):
    return 256 if m % 256 == 0 else 128


def kernel(x_prompt, x_sample, cache_k, cache_v, cache_logf, state_ret, page_table, norm_even, w_in_even,
           b_forget, qnorm, knorm, ret_norm, w_out_even, norm_odd, w_in_odd, vnorm_odd, w_spatial, b_spatial,
           w_out_odd):
    bp, s, _ = x_prompt.shape
    bd, n_new, _ = x_sample.shape
    n_pages = page_table.shape[1]
    past = n_pages * PAGE
    assert s % 256 == 0 and (bd * n_new) % CHUNK == 0 and w_in_even.shape[0] == 1 and w_in_odd.shape[0] == 1

    wi = w_in_even[0]
    sizes = (W_FOX, W_FOX, W_FOX, H_FOX, W_FOX, W_RQK, W_RQK, W_RET, W_RET)
    offs = np.concatenate([[0], np.cumsum(sizes)])
    fq, fk, fv, ff, fg, rq, rk, rv, rg = [wi[:, offs[i]:offs[i + 1]] for i in range(9)]
    w_even = jnp.concatenate([fg, rq, rk, rv, rg], axis=1).astype(BF16)
    w_even_t = jnp.concatenate([fq.T, fk.T, fv.T, ff.T, jnp.zeros((EVEN_ROWS_T - R_FF - H_FOX, D_MODEL), F32)],
                               axis=0).astype(BF16)
    b_f = b_forget[0][:, None]
    g_q = qnorm[0][:, None]
    g_k = knorm[0][:, None]
    g_ret = ret_norm[0].reshape(1, W_RET)
    woe = w_out_even[0].astype(BF16)
    wio = w_in_odd[0].astype(BF16)
    woo = w_out_odd[0].astype(BF16)
    g_ne, g_no, g_v = norm_even[0][None, :], norm_odd[0][None, :], vnorm_odd[0][None, :]

    cos_p, sin_p = _rope_tables(jnp.arange(s))
    xp2 = x_prompt.reshape(bp * s, D_MODEL)
    tm = _row_tile(s)
    sfg, rqp, rkp, rvp, srg, kn, qt, kt32, _, vt32, vtb, lft = _even_proj(
        xp2, bp, g_ne, w_even, w_even_t, b_f, g_q, g_k, cos_p, sin_p, tm)
    ng = H_FOX // HEADS_PER_GROUP
    c_col4 = jnp.transpose(_cumsum_lanes(lft).reshape(bp, ng, HEADS_PER_GROUP, s), (0, 1, 3, 2))
    r3 = lambda t: t.reshape(bp, s, t.shape[-1])
    a_p = _fox_prompt(qt, r3(kn), vtb, c_col4, r3(sfg), 256)
    r_p, st_p = _ret_prompt(r3(rqp), r3(rkp), r3(rvp), r3(srg), g_ret)
    y_p, = _tail(a_p.reshape(bp * s, W_FOX), r_p.reshape(bp * s, W_RET), xp2, woe, g_no, wio, g_v,
                 w_spatial[0], jnp.transpose(b_spatial[0]), woo, tm, None)

    rows = bd * n_new
    cos_d, sin_d = _rope_tables(past + jnp.arange(n_new))
    xd2 = x_sample.reshape(rows, D_MODEL)
    sfgd, rqd, rkd, rvd, srgd, _, qtd, kt32d, ktbd, vt32d, vtbd, lftd = _even_proj(
        xd2, 1, g_ne, w_even, w_even_t, b_f, g_q, g_k, jnp.tile(cos_d, (bd, 1)), jnp.tile(sin_d, (bd, 1)), rows)
    qd = jnp.transpose(qtd[0], (1, 0))
    d3 = lambda t: t.reshape(bd, n_new, t.shape[-1])
    n_phys = cache_k.shape[1]
    pages_t = lambda cache: jnp.transpose(cache[0], (0, 2, 3, 1)).reshape(n_phys, W_FOX, PAGE)
    a_d = _fox_sample(page_table, d3(qd), ktbd[0], vtbd[0], lftd[0], d3(sfgd), pages_t(cache_k), pages_t(cache_v),
                      jnp.transpose(cache_logf[0], (0, 2, 1)), 8 if n_pages % 8 == 0 else 1)
    r_d, st_d = _ret_sample(rqd, rkd, rvd, srgd, g_ret, state_ret[0], n_new)
    ws_new = w_spatial[0][:, :n_new, :n_new].reshape(GM_GROUPS, n_new * n_new)
    y_d, gv_d = _tail(a_d.reshape(rows, W_FOX), r_d, xd2, woe, g_no, wio, g_v,
                      ws_new, b_spatial[0][:, :n_new], woo, rows, n_new)

    heads_last = lambda t, n, length: jnp.transpose(t.reshape(n, H_FOX, HEAD_DIM, length), (0, 3, 1, 2))[None]
    tokens_first = lambda t, n: jnp.transpose(t[0], (1, 0)).reshape(1, bd, n_new, *n)
    return (y_p.reshape(bp, s, D_MODEL), y_d.reshape(bd, n_new, D_MODEL),
            heads_last(kt32, bp, s), heads_last(vt32, bp, s),
            jnp.transpose(lft, (0, 2, 1))[None], st_p.reshape(1, bp, H_RET, RET_DK, RET_DV),
            tokens_first(kt32d, (H_FOX, HEAD_DIM)), tokens_first(vt32d, (H_FOX, HEAD_DIM)),
            tokens_first(lftd, (H_FOX,)), st_d[None], gv_d.reshape(1, bd, n_new, GM_WIDTH))
```

```python
import functools

import numpy as np
import jax
import jax.numpy as jnp
from jax import lax
from jax.experimental import pallas as pl
from jax.experimental.pallas import tpu as pltpu

D_MODEL = 1024
HEAD_DIM = 64
H_FOX = 8
W_FOX = H_FOX * HEAD_DIM
H_RET = 8
RET_DK = 64
RET_DV = 128
W_RET = H_RET * RET_DV
W_RQK = H_RET * RET_DK
ROPE_BASE = 10000.0
GM_WIDTH = 2 * D_MODEL
GM_GROUPS = 8
GM_GC = GM_WIDTH // GM_GROUPS
CHUNK = 128
PAGE = 128
EPS = 1e-6
F32 = jnp.float32
BF16 = jnp.bfloat16

LANES = 128
VMEM_LIMIT = 56 * 1024 * 1024

C_FG, C_RQ, C_RK, C_RV, C_RG, EVEN_COLS = 0, 512, 1024, 1536, 2560, 3584
R_FQ, R_FK, R_FV, R_FF, EVEN_ROWS_T = 0, 512, 1024, 1536, 1552

LOG_GAMMA = [float(np.log(np.float32(1.0) - np.float32(2.0) ** np.float32(-5.0 - h))) for h in range(H_RET)]

LOG2E = float(np.log2(np.e))
NT_DIMS = (((1,), (1,)), ((), ()))
TN_DIMS = (((0,), (0,)), ((), ()))


def _params(sem):
    return pltpu.CompilerParams(dimension_semantics=sem, vmem_limit_bytes=VMEM_LIMIT)


def _const_spec(shape):
    nd = len(shape)
    return pl.BlockSpec(shape, lambda *_: (0,) * nd, pipeline_mode=pl.Buffered(1))


def _rms(x, g):
    ms = jnp.mean(x * x, axis=-1, keepdims=True)
    return x * lax.rsqrt(ms + EPS) * g


def _layer_norm(x, g):
    xc = x - jnp.mean(x, axis=-1, keepdims=True)
    return xc * lax.rsqrt(jnp.mean(xc * xc, axis=-1, keepdims=True) + EPS) * g


def _rope(x, cos, sin_signed):
    rows = x.shape[0]
    first_half = (lax.broadcasted_iota(jnp.int32, (rows, LANES), 1) % RET_DK) < RET_DK // 2
    outs = []
    for c in range(x.shape[1] // LANES):
        xc = x[:, c * LANES:(c + 1) * LANES]
        nxt = pltpu.roll(xc, LANES - RET_DK // 2, axis=1)
        prv = pltpu.roll(xc, RET_DK // 2, axis=1)
        outs.append(xc * cos + jnp.where(first_half, nxt, prv) * sin_signed)
    return jnp.concatenate(outs, axis=-1)


def _log_sigmoid(x):
    return jnp.minimum(x, 0.0) - jnp.log1p(jnp.exp(-jnp.abs(x)))


def _head_rms_t(x, g):
    tokens = x.shape[1]
    x = x.reshape(H_FOX, HEAD_DIM, tokens)
    x = x * lax.rsqrt(jnp.mean(x * x, axis=1, keepdims=True) + EPS) * g[None]
    return x.reshape(W_FOX, tokens)


def _even_proj_kernel(x_ref, g_ref, w_ref, wt_ref, bf_ref, gq_ref, gk_ref, cos_ref, sin_ref,
                      sfg_ref, rq_ref, rk_ref, rv_ref, srg_ref, kn_ref,
                      qt_ref, kt32_ref, ktb_ref, vt32_ref, vtb_ref, lft_ref):
    h = _rms(x_ref[...], g_ref[...]).astype(BF16)

    def proj(lo, hi):
        return jnp.dot(h, w_ref[:, lo:hi], preferred_element_type=F32)

    sfg_ref[...] = jax.nn.silu(proj(C_FG, C_RQ)).astype(BF16)
    cos = cos_ref[...]
    sin = sin_ref[...]
    rq_ref[...] = _rope(proj(C_RQ, C_RK), cos, sin).astype(BF16)
    rk_ref[...] = (_rope(proj(C_RK, C_RV), cos, sin) * (RET_DK ** -0.5)).astype(BF16)
    rv_ref[...] = proj(C_RV, C_RG).astype(BF16)
    srg_ref[...] = jax.nn.silu(proj(C_RG, EVEN_COLS)).astype(BF16)

    t = lax.dot_general(wt_ref[...], h, NT_DIMS, preferred_element_type=F32)
    qt_ref[0] = (_head_rms_t(t[R_FQ:R_FK], gq_ref[...]) * (HEAD_DIM ** -0.5 * LOG2E)).astype(BF16)
    kt = _head_rms_t(t[R_FK:R_FV], gk_ref[...])
    kt32_ref[0] = kt
    ktb_ref[0] = kt.astype(BF16)
    kn_ref[...] = kt.T.astype(BF16)
    vt = t[R_FV:R_FF]
    vt32_ref[0] = vt
    vtb_ref[0] = vt.astype(BF16)
    lft_ref[0] = _log_sigmoid(t[R_FF:R_FF + H_FOX] + bf_ref[...])


def _even_proj(x2d, n_seq, g_norm, w, wt, b_f, g_q, g_k, cos, sin, tm):
    m = x2d.shape[0]
    s = m // n_seq
    per_seq = s // tm
    row = lambda n: pl.BlockSpec((tm, n), lambda i: (i, 0))
    tab = pl.BlockSpec((tm, LANES), lambda i: (i % per_seq, 0))
    col = lambda n: pl.BlockSpec((1, n, tm), lambda i: (i // per_seq, 0, i % per_seq))
    outs = [(W_FOX, BF16), (W_RQK, BF16), (W_RQK, BF16), (W_RET, BF16), (W_RET, BF16), (W_FOX, BF16)]
    outs_t = [(W_FOX, BF16), (W_FOX, F32), (W_FOX, BF16), (W_FOX, F32), (W_FOX, BF16), (H_FOX, F32)]
    return pl.pallas_call(
        _even_proj_kernel,
        grid=(m // tm,),
        in_specs=[row(D_MODEL), _const_spec((1, D_MODEL)), _const_spec((D_MODEL, EVEN_COLS)),
                  _const_spec((EVEN_ROWS_T, D_MODEL)), _const_spec((H_FOX, 1)), _const_spec((HEAD_DIM, 1)),
                  _const_spec((HEAD_DIM, 1)), tab, tab],
        out_specs=[row(n) for n, _ in outs] + [col(n) for n, _ in outs_t],
        out_shape=([jax.ShapeDtypeStruct((m, n), dt) for n, dt in outs]
                   + [jax.ShapeDtypeStruct((n_seq, n, s), dt) for n, dt in outs_t]),
        compiler_params=_params(("parallel",)),
        name="even_proj",
    )(x2d, g_norm, w, wt, b_f, g_q, g_k, cos, sin)


def _cumsum_kernel(x_ref, o_ref):
    x = x_ref[0]
    n = x.shape[1]
    lane = lax.broadcasted_iota(jnp.int32, x.shape, 1)
    sh = 1
    while sh < n:
        x = x + jnp.where(lane >= sh, pltpu.roll(x, sh, axis=1), 0.0)
        sh *= 2
    o_ref[0] = x


def _cumsum_lanes(x):
    b, h, s = x.shape
    spec = pl.BlockSpec((1, h, s), lambda i: (i, 0, 0))
    return pl.pallas_call(
        _cumsum_kernel, grid=(b,), in_specs=[spec], out_specs=spec,
        out_shape=jax.ShapeDtypeStruct(x.shape, F32),
        compiler_params=_params(("parallel",)), name="logf_cumsum",
    )(x)


HEADS_PER_GROUP = 4
GROUP_W = HEADS_PER_GROUP * HEAD_DIM


def _fox_prompt_kernel(qt_ref, k_ref, v_ref, ccol_ref, sfg_ref, a_ref,
                       qm_ref, vp_ref, cb_ref, m_ref, acc_ref, s_ref, p_ref, *, blk):
    i = pl.program_id(2)
    s_len = k_ref.shape[1]

    @pl.when(i == 0)
    def _():
        ones_row = jnp.where(lax.broadcasted_iota(jnp.int32, (HEAD_DIM, s_len), 0) == 0, 1.0, 0.0).astype(BF16)
        ccol = ccol_ref[0, 0] * (-LOG2E)
        for hh in range(HEADS_PER_GROUP):
            vp_ref[hh, :HEAD_DIM, :] = v_ref[0, hh * HEAD_DIM:(hh + 1) * HEAD_DIM, :]
            vp_ref[hh, HEAD_DIM:, :] = ones_row
            cb_ref[hh] = jnp.broadcast_to(ccol[:, hh:hh + 1], (s_len, LANES))

    qt = qt_ref[0]
    row_head = lax.broadcasted_iota(jnp.int32, (GROUP_W, blk), 0) // HEAD_DIM
    for hh in range(HEADS_PER_GROUP):
        qm_ref[hh] = jnp.where(row_head == hh, qt, jnp.zeros_like(qt))
    m_ref[...] = jnp.full(m_ref.shape, -1e30, F32)
    acc_ref[...] = jnp.zeros(acc_ref.shape, F32)
    strips = [(hh, slice(c * LANES, (c + 1) * LANES), c) for hh in range(HEADS_PER_GROUP) for c in range(blk // LANES)]

    def raw_scores(j, slot):
        off = pl.multiple_of(j * blk, blk)
        kb = k_ref[0, pl.ds(off, blk), :]
        for hh in range(HEADS_PER_GROUP):
            bias = cb_ref[hh, pl.ds(off, blk), :]
            s_ref[slot, hh] = (jnp.dot(kb, qm_ref[hh], preferred_element_type=F32)
                               + jnp.concatenate([bias] * (blk // LANES), axis=1))

    def consume(j, slot, masked):
        off = pl.multiple_of(j * blk, blk)

        def scores(hh, cols, c):
            s = s_ref[slot, hh, :, cols]
            if masked:
                causal = (lax.broadcasted_iota(jnp.int32, (blk, LANES), 0)
                          <= lax.broadcasted_iota(jnp.int32, (blk, LANES), 1) + c * LANES)
                s = jnp.where(causal, s, -jnp.inf)
            return s

        for hh, cols, c in strips:
            m_old = m_ref[hh, :, cols]
            m_new = jnp.maximum(m_old, jnp.max(scores(hh, cols, c), axis=0, keepdims=True))
            acc_ref[hh, :, cols] = jnp.exp2(m_old - m_new) * acc_ref[hh, :, cols]
            m_ref[hh, :, cols] = m_new
        for hh, cols, c in strips:
            p_ref[hh, :, cols] = jnp.exp2(scores(hh, cols, c) - m_ref[hh, :, cols]).astype(BF16)
        for hh in range(HEADS_PER_GROUP):
            acc_ref[hh] += jnp.dot(vp_ref[hh, :, pl.ds(off, blk)], p_ref[hh],
                                   preferred_element_type=F32)

    raw_scores(0, 0)

    def pair(t, carry):
        j = 2 * t
        raw_scores(j + 1, 1)
        consume(j, 0, False)
        raw_scores(j + 2, 0)
        consume(j + 1, 1, False)
        return carry

    lax.fori_loop(0, i // 2, pair, 0)

    @pl.when(i % 2 == 1)
    def _():
        raw_scores(i, 1)
        consume(i - 1, 0, False)
        consume(i, 1, True)

    @pl.when(i % 2 == 0)
    def _():
        consume(i, 0, True)

    ot = jnp.concatenate([acc_ref[hh, :HEAD_DIM, :] / acc_ref[hh, HEAD_DIM:HEAD_DIM + 1, :]
                          for hh in range(HEADS_PER_GROUP)], axis=0)
    a_ref[0] = (ot.T * sfg_ref[0].astype(F32)).astype(BF16)


def _fox_prompt(qt, kn, vtb, ccol, sfg, blk):
    b, s, _ = kn.shape
    ng = H_FOX // HEADS_PER_GROUP
    rows = pl.BlockSpec((1, blk, GROUP_W), lambda bi, g, i: (bi, i, g))
    return pl.pallas_call(
        functools.partial(_fox_prompt_kernel, blk=blk),
        grid=(b, ng, s // blk),
        in_specs=[pl.BlockSpec((1, GROUP_W, blk), lambda bi, g, i: (bi, g, i)),
                  pl.BlockSpec((1, s, GROUP_W), lambda bi, g, i: (bi, 0, g)),
                  pl.BlockSpec((1, GROUP_W, s), lambda bi, g, i: (bi, g, 0)),
                  pl.BlockSpec((1, 1, s, HEADS_PER_GROUP), lambda bi, g, i: (bi, g, 0, 0)),
                  rows],
        out_specs=rows,
        out_shape=jax.ShapeDtypeStruct((b, s, W_FOX), BF16),
        scratch_shapes=[pltpu.VMEM((HEADS_PER_GROUP, GROUP_W, blk), BF16),
                        pltpu.VMEM((HEADS_PER_GROUP, LANES, s), BF16),
                        pltpu.VMEM((HEADS_PER_GROUP, s, LANES), F32),
                        pltpu.VMEM((HEADS_PER_GROUP, 1, blk), F32),
                        pltpu.VMEM((HEADS_PER_GROUP, LANES, blk), F32),
                        pltpu.VMEM((2, HEADS_PER_GROUP, blk, blk), F32),
                        pltpu.VMEM((HEADS_PER_GROUP, blk, blk), BF16)],
        compiler_params=_params(("parallel", "parallel", "arbitrary")),
        name="fox_prompt",
    )(qt, kn, vtb, ccol, sfg)


def _ret_head_out(o, g, srg):
    return (_layer_norm(o, g) * srg.astype(F32)).astype(BF16)


def _ret_prompt_kernel(rq_ref, rk_ref, rv_ref, srg_ref, gret_ref, r_ref, st_ref, dec_ref, kdec_ref, *, chunks):
    L = CHUNK
    c = pl.program_id(1)

    @pl.when(c == 0)
    def _():
        st_ref[...] = jnp.zeros(st_ref.shape, F32)
        diff = (lax.broadcasted_iota(jnp.int32, (L, L), 0)
                - lax.broadcasted_iota(jnp.int32, (L, L), 1)).astype(F32)
        for h in range(H_RET):
            dec_ref[h] = jnp.where(diff >= 0, jnp.exp(LOG_GAMMA[h] * jnp.maximum(diff, 0.0)), 0.0)
        left = (L - 1.0) - lax.broadcasted_iota(jnp.int32, (L, LANES), 0).astype(F32)
        first = lax.broadcasted_iota(jnp.int32, (L, LANES), 1) < RET_DK
        for pp in range(H_RET // 2):
            kdec_ref[pp] = jnp.exp(jnp.where(first, LOG_GAMMA[2 * pp], LOG_GAMMA[2 * pp + 1]) * left)

    pos = lax.broadcasted_iota(jnp.int32, (L, 1), 0).astype(F32)
    lane_half = lax.broadcasted_iota(jnp.int32, (L, LANES), 1) // RET_DK
    for ci in range(chunks):
        rows_c = slice(ci * L, (ci + 1) * L)
        first_stage = []
        for pp in range(H_RET // 2):
            qp = rq_ref[0, rows_c, pp * LANES:(pp + 1) * LANES]
            kp = rk_ref[0, rows_c, pp * LANES:(pp + 1) * LANES]
            s_pair = st_ref[0, pp]
            s_pair_b = s_pair.astype(BF16)
            kd = (kp.astype(F32) * kdec_ref[pp]).astype(BF16)
            for j in range(2):
                h = 2 * pp + j
                qm = jnp.where(lane_half == j, qp, jnp.zeros_like(qp))
                vh = rv_ref[0, rows_c, h * RET_DV:(h + 1) * RET_DV]
                scores = lax.dot_general(qm, kp, NT_DIMS, preferred_element_type=F32)
                cross = jnp.dot(qm, s_pair_b, preferred_element_type=F32)
                upd = lax.dot_general(kd, vh, TN_DIMS, preferred_element_type=F32)
                first_stage.append((scores, cross, upd, vh, s_pair))
        outs = []
        for h, (scores, cross, upd, vh, s_pair) in enumerate(first_stage):
            lg = LOG_GAMMA[h]
            intra = jnp.dot((scores * dec_ref[h]).astype(BF16), vh, preferred_element_type=F32)
            outs.append(intra + cross * jnp.exp(lg * (pos + 1.0)))
            rows = slice((h % 2) * RET_DK, (h % 2 + 1) * RET_DK)
            st_ref[0, h // 2, rows, :] = float(np.exp(lg * L)) * s_pair[rows] + upd[rows]
        for h, o in enumerate(outs):
            cols = slice(h * RET_DV, (h + 1) * RET_DV)
            r_ref[0, rows_c, cols] = _ret_head_out(o, gret_ref[:, cols], srg_ref[0, rows_c, cols])


def _ret_prompt(rq, rk, rv, srg, g_ret):
    b, s, _ = rq.shape
    chunks = 2 if s % (2 * CHUNK) == 0 else 1
    rows = chunks * CHUNK
    qk = pl.BlockSpec((1, rows, W_RQK), lambda bi, c: (bi, c, 0))
    wide = pl.BlockSpec((1, rows, W_RET), lambda bi, c: (bi, c, 0))
    st = pl.BlockSpec((1, H_RET // 2, 2 * RET_DK, RET_DV), lambda bi, c: (bi, 0, 0, 0))
    return pl.pallas_call(
        functools.partial(_ret_prompt_kernel, chunks=chunks),
        grid=(b, s // rows),
        in_specs=[qk, qk, wide, wide, _const_spec((1, W_RET))],
        out_specs=[wide, st],
        out_shape=[jax.ShapeDtypeStruct((b, s, W_RET), BF16),
                   jax.ShapeDtypeStruct((b, H_RET // 2, 2 * RET_DK, RET_DV), F32)],
        scratch_shapes=[pltpu.VMEM((H_RET, CHUNK, CHUNK), F32), pltpu.VMEM((H_RET // 2, CHUNK, LANES), F32)],
        compiler_params=_params(("parallel", "arbitrary")),
        name="ret_prompt",
    )(rq, rk, rv, srg, g_ret)


def _ret_sample_kernel(rq_ref, rk_ref, rv_ref, srg_ref, gret_ref, s0_ref, r_ref, s1_ref, *, n_new):
    rows = rq_ref.shape[0]
    nb = rows // n_new
    two_dk = 2 * RET_DK
    ri = lax.broadcasted_iota(jnp.int32, (rows, rows), 0)
    ci = lax.broadcasted_iota(jnp.int32, (rows, rows), 1)
    same = (ri // n_new) == (ci // n_new)
    diff = ((ri % n_new) - (ci % n_new)).astype(F32)
    pos = (lax.broadcasted_iota(jnp.int32, (rows, 1), 0) % n_new).astype(F32)
    lane_half = lax.broadcasted_iota(jnp.int32, (rows, LANES), 1) // RET_DK
    q_sel = (lax.broadcasted_iota(jnp.int32, (rows, nb * two_dk), 1) // two_dk
             == lax.broadcasted_iota(jnp.int32, (rows, nb * two_dk), 0) // n_new)
    k_sel = (lax.broadcasted_iota(jnp.int32, (nb * two_dk, rows), 0) // two_dk
             == lax.broadcasted_iota(jnp.int32, (nb * two_dk, rows), 1) // n_new)
    row_head = (lax.broadcasted_iota(jnp.int32, (nb * two_dk, 1), 0) % two_dk) // RET_DK

    qp = rq_ref[...]
    kp = rk_ref[...]
    s_old = s0_ref[...].reshape(nb * two_dk, RET_DV)
    s_old_b = s_old.astype(BF16)
    upd = []
    for j in range(2):
        h = 2 * pl.program_id(0) + j
        lg = jnp.where(h == 0, LOG_GAMMA[0], 0.0)
        for hh in range(1, H_RET):
            lg = jnp.where(h == hh, LOG_GAMMA[hh], lg)
        qm = jnp.where(lane_half == j, qp, jnp.zeros_like(qp))
        vh = rv_ref[:, j * RET_DV:(j + 1) * RET_DV]
        decay = jnp.where(same & (diff >= 0), jnp.exp(lg * jnp.maximum(diff, 0.0)), 0.0)
        scores = lax.dot_general(qm, kp, NT_DIMS, preferred_element_type=F32) * decay
        intra = jnp.dot(scores.astype(BF16), vh, preferred_element_type=F32)
        q_exp = jnp.where(q_sel, jnp.tile(qm, (1, nb)), jnp.zeros((), BF16))
        cross = jnp.dot(q_exp, s_old_b, preferred_element_type=F32) * jnp.exp(lg * (pos + 1.0))
        kd = kp.astype(F32) * jnp.exp(lg * (n_new - 1.0 - pos))
        kd_exp = jnp.where(k_sel, jnp.tile(kd.T, (nb, 1)), 0.0).astype(BF16)
        keep = jnp.exp(lg * jnp.full((1, 1), float(n_new), F32))
        upd.append((keep, jnp.dot(kd_exp, vh, preferred_element_type=F32)))
        cols = slice(j * RET_DV, (j + 1) * RET_DV)
        r_ref[:, cols] = _ret_head_out(intra + cross, gret_ref[:, cols], srg_ref[:, cols])
    s_new = jnp.where(row_head == 0, upd[0][0] * s_old + upd[0][1], upd[1][0] * s_old + upd[1][1])
    s1_ref[...] = s_new.reshape(s1_ref.shape)


def _ret_sample(rq, rk, rv, srg, g_ret, state, n_new):
    rows = rq.shape[0]
    nb = rows // n_new
    pair = pl.BlockSpec((rows, LANES), lambda p: (0, p))
    wide = pl.BlockSpec((rows, 2 * RET_DV), lambda p: (0, p))
    st = pl.BlockSpec((nb, 2, RET_DK, RET_DV), lambda p: (0, p, 0, 0))
    return pl.pallas_call(
        functools.partial(_ret_sample_kernel, n_new=n_new),
        grid=(H_RET // 2,),
        in_specs=[pair, pair, wide, wide, pl.BlockSpec((1, 2 * RET_DV), lambda p: (0, p)), st],
        out_specs=[wide, st],
        out_shape=[jax.ShapeDtypeStruct((rows, W_RET), BF16),
                   jax.ShapeDtypeStruct(state.shape, F32)],
        compiler_params=_params(("parallel",)),
        name="ret_sample",
    )(rq, rk, rv, srg, g_ret, state)


def _suffix_sum_lanes(x):
    n = x.shape[1]
    lane = lax.broadcasted_iota(jnp.int32, x.shape, 1)
    incl = x
    sh = 1
    while sh < n:
        incl = incl + jnp.where(lane < n - sh, pltpu.roll(incl, n - sh, axis=1), 0.0)
        sh *= 2
    return incl - x


def _fox_sample_kernel(pt_ref, q_ref, kn_ref, vn_ref, lfn_ref, sfg_ref, lfc_ref, *refs,
                       n_new, pages_per_step, page_groups):
    g = pages_per_step
    k_refs, v_refs = refs[:g], refs[g:2 * g]
    a_ref, qbd_ref, m_ref, l_ref, acc_ref, tail_ref = refs[2 * g:]
    b = pl.program_id(0)
    c = pl.program_id(1)
    first_page = pl.num_programs(1) * g - (c + 1) * g
    nq = n_new * H_FOX
    row_tok = lax.broadcasted_iota(jnp.int32, (nq, 1), 0) // H_FOX
    head_of_row = lax.broadcasted_iota(jnp.int32, (H_FOX, W_FOX), 0)
    head_of_lane = lax.broadcasted_iota(jnp.int32, (H_FOX, W_FOX), 1) // HEAD_DIM

    def partial_softmax(s, vt):
        m = jnp.max(s, axis=-1, keepdims=True)
        p = jnp.exp2(s - m)
        return (m, jnp.sum(p, axis=-1, keepdims=True),
                lax.dot_general(p.astype(BF16), vt, NT_DIMS, preferred_element_type=F32))

    def merge(parts):
        m_new = m_ref[...]
        for m, _, _ in parts:
            m_new = jnp.maximum(m_new, m)
        alpha = jnp.exp2(m_ref[...] - m_new)
        l_new = alpha * l_ref[...]
        acc = alpha * acc_ref[...]
        for m, l, o in parts:
            w = jnp.exp2(m - m_new)
            l_new = l_new + w * l
            acc = acc + w * o
        m_ref[...] = m_new
        l_ref[...] = l_new
        acc_ref[...] = acc

    @pl.when(c == 0)
    def _():
        q = q_ref[0].astype(F32)
        for t in range(n_new):
            qbd_ref[t * H_FOX:(t + 1) * H_FOX, :] = jnp.where(
                head_of_lane == head_of_row, jnp.broadcast_to(q[t:t + 1, :], (H_FOX, W_FOX)), 0.0).astype(BF16)
        tail_ref[...] = jnp.zeros(tail_ref.shape, F32)
        m_ref[...] = jnp.full(m_ref.shape, -1e30, F32)
        l_ref[...] = jnp.zeros(l_ref.shape, F32)
        acc_ref[...] = jnp.zeros(acc_ref.shape, F32)
        lane = lax.broadcasted_iota(jnp.int32, (H_FOX, LANES), 1)
        pfx = lfn_ref[...]
        sh = 1
        while sh < n_new:
            pfx = pfx + jnp.where(lane % n_new >= sh, pltpu.roll(pfx, sh, axis=1), 0.0)
            sh *= 2
        col = lax.broadcasted_iota(jnp.int32, (nq, LANES), 1)
        valid = (col // n_new == b) & (col % n_new <= row_tok)
        s = jnp.dot(qbd_ref[...], kn_ref[...], preferred_element_type=F32)
        merge([partial_softmax(jnp.where(valid, s - jnp.tile(pfx, (n_new, 1)) * LOG2E, -jnp.inf), vn_ref[...])])

    lf = jnp.concatenate([lfc_ref[pt_ref[b, first_page + slot]] for slot in range(g)], axis=1)
    key_bias = jnp.tile((tail_ref[:, 0:1] + _suffix_sum_lanes(lf)) * LOG2E, (n_new, 1))
    tail_ref[...] = tail_ref[...] + jnp.sum(lf, axis=-1, keepdims=True)
    per = g // page_groups
    scores = []
    for grp in range(page_groups):
        pages = slice(grp * per, (grp + 1) * per)
        kc = jnp.concatenate([r[0].astype(BF16) for r in k_refs[pages]], axis=1)
        s = jnp.dot(qbd_ref[...], kc, preferred_element_type=F32)
        scores.append(s + key_bias[:, grp * per * PAGE:(grp + 1) * per * PAGE])
    parts = []
    for grp, s in enumerate(scores):
        vc = jnp.concatenate([r[0].astype(BF16) for r in v_refs[grp * per:(grp + 1) * per]], axis=1)
        parts.append(partial_softmax(s, vc))
    merge(parts)

    @pl.when(c == pl.num_programs(1) - 1)
    def _():
        o = acc_ref[...] / l_ref[...]
        sfg = sfg_ref[0].astype(F32)
        for t in range(n_new):
            ot = jnp.where(head_of_lane == head_of_row, o[t * H_FOX:(t + 1) * H_FOX, :], 0.0)
            a_ref[0, t:t + 1, :] = jnp.sum(ot, axis=0, keepdims=True) * sfg[t:t + 1, :]


def _fox_sample(page_table, q, kt_new, vt_new, lft_new, sfg, cache_kt, cache_vt, cache_lft):
    bd, n_new, _ = q.shape
    n_pages = page_table.shape[1]
    g = next(c for c in (16, 8, 4, 2, 1) if n_pages % c == 0)
    page_groups = 2 if g >= 2 else 1
    n_steps = n_pages // g
    nq = n_new * H_FOX

    def tok(n):
        return pl.BlockSpec((1, n_new, n), lambda b, c, pt: (b, 0, 0))

    def whole(arr):
        nd = arr.ndim
        return pl.BlockSpec(arr.shape, lambda b, c, pt: (0,) * nd, pipeline_mode=pl.Buffered(1))

    def page_spec(slot):
        return pl.BlockSpec((1, W_FOX, PAGE), lambda b, c, pt: (pt[b, n_pages - (c + 1) * g + slot], 0, 0))

    in_specs = [tok(W_FOX), whole(kt_new), whole(vt_new), whole(lft_new), tok(W_FOX), whole(cache_lft)]
    in_specs += [page_spec(s) for s in range(g)] + [page_spec(s) for s in range(g)]
    grid_spec = pltpu.PrefetchScalarGridSpec(
        num_scalar_prefetch=1, grid=(bd, n_steps), in_specs=in_specs, out_specs=tok(W_FOX),
        scratch_shapes=[pltpu.VMEM((nq, W_FOX), BF16), pltpu.VMEM((nq, 1), F32), pltpu.VMEM((nq, 1), F32),
                        pltpu.VMEM((nq, W_FOX), F32), pltpu.VMEM((H_FOX, LANES), F32)])
    return pl.pallas_call(
        functools.partial(_fox_sample_kernel, n_new=n_new, pages_per_step=g, page_groups=page_groups),
        grid_spec=grid_spec,
        out_shape=jax.ShapeDtypeStruct((bd, n_new, W_FOX), F32),
        compiler_params=_params(("parallel", "arbitrary")),
        name="fox_sample",
    )(page_table, q, kt_new, vt_new, lft_new, sfg, cache_lft, *([cache_kt] * g), *([cache_vt] * g))


def _tail_kernel(a_ref, r_ref, x_ref, woe_ref, gno_ref, wio_ref, gv_ref, ws_ref, bs_ref, woo_ref,
                 *out_refs, n_new):
    rows = x_ref.shape[0]
    x1 = (x_ref[...]
          + jnp.dot(a_ref[...].astype(BF16), woe_ref[:W_FOX, :], preferred_element_type=F32)
          + jnp.dot(r_ref[...], woe_ref[W_FOX:, :], preferred_element_type=F32))
    h = _rms(x1, gno_ref[...]).astype(BF16)
    v = jax.nn.gelu(jnp.dot(h, wio_ref[:, GM_WIDTH:2 * GM_WIDTH], preferred_element_type=F32))
    vn = _layer_norm(v, gv_ref[...])
    if n_new is None:
        y_ref, = out_refs
        tril = (lax.broadcasted_iota(jnp.int32, (CHUNK, CHUNK), 0)
                >= lax.broadcasted_iota(jnp.int32, (CHUNK, CHUNK), 1))
        vb = vn.astype(BF16)

        def spatial(g):
            wm = jnp.where(tril, ws_ref[g], 0.0).astype(BF16)
            bias = bs_ref[:, g:g + 1]
            return jnp.concatenate(
                [jnp.dot(wm, vb[c * CHUNK:(c + 1) * CHUNK, g * GM_GC:(g + 1) * GM_GC],
                         preferred_element_type=F32) + bias for c in range(rows // CHUNK)], axis=0)
    else:
        y_ref, vn_ref = out_refs
        vn_ref[...] = vn
        tok = lax.broadcasted_iota(jnp.int32, (rows, 1), 0) % n_new

        def spatial(g):
            vg = vn[:, g * GM_GC:(g + 1) * GM_GC]
            acc = jnp.zeros((rows, 1), F32)
            for t in range(n_new):
                acc = jnp.where(tok == t, bs_ref[g, t], acc)
            acc = jnp.broadcast_to(acc, (rows, GM_GC))
            for d in range(n_new):
                coef = jnp.zeros((rows, 1), F32)
                for t in range(d, n_new):
                    coef = jnp.where(tok == t, ws_ref[g, t * n_new + (t - d)], coef)
                src = vg if d == 0 else pltpu.roll(vg, d, axis=0)
                acc = acc + coef * src
            return acc

    u = jax.nn.gelu(jnp.dot(h, wio_ref[:, :GM_WIDTH], preferred_element_type=F32))
    gate = jnp.dot(h, wio_ref[:, 2 * GM_WIDTH:], preferred_element_type=F32)
    sv = jnp.concatenate([spatial(g) for g in range(GM_GROUPS)], axis=1)
    z = (u * sv * jax.nn.silu(gate)).astype(BF16)
    y_ref[...] = x1 + jnp.dot(z, woo_ref[...], preferred_element_type=F32)


def _tail(a, r, x2d, woe, gno, wio, gv, ws, bs, woo, tm, n_new):
    m = x2d.shape[0]
    row = lambda n: pl.BlockSpec((tm, n), lambda i: (i, 0))
    if n_new is None:
        mix_specs = [_const_spec(ws.shape), _const_spec(bs.shape)]
        out_specs = [row(D_MODEL)]
        out_shape = [jax.ShapeDtypeStruct((m, D_MODEL), F32)]
    else:
        smem = pl.BlockSpec(memory_space=pltpu.SMEM)
        mix_specs = [smem, smem]
        out_specs = [row(D_MODEL), row(GM_WIDTH)]
        out_shape = [jax.ShapeDtypeStruct((m, D_MODEL), F32), jax.ShapeDtypeStruct((m, GM_WIDTH), F32)]
    return pl.pallas_call(
        functools.partial(_tail_kernel, n_new=n_new),
        grid=(m // tm,),
        in_specs=[row(W_FOX), row(W_RET), row(D_MODEL), _const_spec(woe.shape), _const_spec((1, D_MODEL)),
                  _const_spec(wio.shape), _const_spec((1, GM_WIDTH))] + mix_specs + [_const_spec(woo.shape)],
        out_specs=out_specs, out_shape=out_shape,
        compiler_params=_params(("parallel",)),
        name="tail_prompt" if n_new is None else "tail_sample",
    )(a, r, x2d, woe, gno, wio, gv, ws, bs, woo)


def _rope_tables(pos):
    half = RET_DK // 2
    inv = ROPE_BASE ** (-jnp.arange(half, dtype=F32) / half)
    ang = pos.astype(F32)[:, None] * inv[None, :]
    cos, sin = jnp.cos(ang), jnp.sin(ang)
    return jnp.tile(jnp.concatenate([cos, cos], axis=-1), (1, 2)), jnp.tile(jnp.concatenate([-sin, sin], axis=-1), (1, 2))


def _row_tile(m):
    return 256 if m % 256 == 0 else 128


def kernel(x_prompt, x_sample, cache_k, cache_v, cache_logf, state_ret, page_table, norm_even, w_in_even,
           b_forget, qnorm, knorm, ret_norm, w_out_even, norm_odd, w_in_odd, vnorm_odd, w_spatial, b_spatial,
           w_out_odd):
    bp, s, _ = x_prompt.shape
    bd, n_new, _ = x_sample.shape
    n_pages = page_table.shape[1]
    past = n_pages * PAGE
    assert s % 256 == 0 and (bd * n_new) % CHUNK == 0 and w_in_even.shape[0] == 1 and w_in_odd.shape[0] == 1

    wi = w_in_even[0]
    sizes = (W_FOX, W_FOX, W_FOX, H_FOX, W_FOX, W_RQK, W_RQK, W_RET, W_RET)
    offs = np.concatenate([[0], np.cumsum(sizes)])
    fq, fk, fv, ff, fg, rq, rk, rv, rg = [wi[:, offs[i]:offs[i + 1]] for i in range(9)]
    w_even = jnp.concatenate([fg, rq, rk, rv, rg], axis=1).astype(BF16)
    w_even_t = jnp.concatenate([fq.T, fk.T, fv.T, ff.T, jnp.zeros((EVEN_ROWS_T - R_FF - H_FOX, D_MODEL), F32)],
                               axis=0).astype(BF16)
    b_f = b_forget[0][:, None]
    g_q = qnorm[0][:, None]
    g_k = knorm[0][:, None]
    g_ret = ret_norm[0].reshape(1, W_RET)
    woe = w_out_even[0].astype(BF16)
    wio = w_in_odd[0].astype(BF16)
    woo = w_out_odd[0].astype(BF16)
    g_ne, g_no, g_v = norm_even[0][None, :], norm_odd[0][None, :], vnorm_odd[0][None, :]

    cos_p, sin_p = _rope_tables(jnp.arange(s))
    xp2 = x_prompt.reshape(bp * s, D_MODEL)
    tm = _row_tile(s)
    sfg, rqp, rkp, rvp, srg, kn, qt, kt32, _, vt32, vtb, lft = _even_proj(
        xp2, bp, g_ne, w_even, w_even_t, b_f, g_q, g_k, cos_p, sin_p, tm)
    ng = H_FOX // HEADS_PER_GROUP
    c_col4 = jnp.transpose(_cumsum_lanes(lft).reshape(bp, ng, HEADS_PER_GROUP, s), (0, 1, 3, 2))
    r3 = lambda t: t.reshape(bp, s, t.shape[-1])
    a_p = _fox_prompt(qt, r3(kn), vtb, c_col4, r3(sfg), 256)
    r_p, st_p = _ret_prompt(r3(rqp), r3(rkp), r3(rvp), r3(srg), g_ret)
    y_p, = _tail(a_p.reshape(bp * s, W_FOX), r_p.reshape(bp * s, W_RET), xp2, woe, g_no, wio, g_v,
                 w_spatial[0], jnp.transpose(b_spatial[0]), woo, 512 if s % 512 == 0 else tm, None)

    rows = bd * n_new
    cos_d, sin_d = _rope_tables(past + jnp.arange(n_new))
    xd2 = x_sample.reshape(rows, D_MODEL)
    sfgd, rqd, rkd, rvd, srgd, _, qtd, kt32d, ktbd, vt32d, vtbd, lftd = _even_proj(
        xd2, 1, g_ne, w_even, w_even_t, b_f, g_q, g_k, jnp.tile(cos_d, (bd, 1)), jnp.tile(sin_d, (bd, 1)), rows)
    qd = jnp.transpose(qtd[0], (1, 0))
    d3 = lambda t: t.reshape(bd, n_new, t.shape[-1])
    n_phys = cache_k.shape[1]
    pages_t = lambda cache: jnp.transpose(cache[0], (0, 2, 3, 1)).reshape(n_phys, W_FOX, PAGE)
    a_d = _fox_sample(page_table, d3(qd), ktbd[0], vtbd[0], lftd[0], d3(sfgd), pages_t(cache_k), pages_t(cache_v),
                      jnp.transpose(cache_logf[0], (0, 2, 1)))
    r_d, st_d = _ret_sample(rqd, rkd, rvd, srgd, g_ret, state_ret[0], n_new)
    ws_new = w_spatial[0][:, :n_new, :n_new].reshape(GM_GROUPS, n_new * n_new)
    y_d, gv_d = _tail(a_d.reshape(rows, W_FOX), r_d, xd2, woe, g_no, wio, g_v,
                      ws_new, b_spatial[0][:, :n_new], woo, rows, n_new)

    heads_last = lambda t, n, length: jnp.transpose(t.reshape(n, H_FOX, HEAD_DIM, length), (0, 3, 1, 2))[None]
    tokens_first = lambda t, n: jnp.transpose(t[0], (1, 0)).reshape(1, bd, n_new, *n)
    return (y_p.reshape(bp, s, D_MODEL), y_d.reshape(bd, n_new, D_MODEL),
            heads_last(kt32, bp, s), heads_last(vt32, bp, s),
            jnp.transpose(lft, (0, 2, 1))[None], st_p.reshape(1, bp, H_RET, RET_DK, RET_DV),
            tokens_first(kt32d, (H_FOX, HEAD_DIM)), tokens_first(vt32d, (H_FOX, HEAD_DIM)),
            tokens_first(lftd, (H_FOX,)), st_d[None], gv_d.reshape(1, bd, n_new, GM_WIDTH))
```

```python
import functools

import numpy as np
import jax
import jax.numpy as jnp
from jax import lax
from jax.experimental import pallas as pl
from jax.experimental.pallas import tpu as pltpu

D_MODEL = 1024
HEAD_DIM = 64
H_FOX = 8
W_FOX = H_FOX * HEAD_DIM
H_RET = 8
RET_DK = 64
RET_DV = 128
W_RET = H_RET * RET_DV
W_RQK = H_RET * RET_DK
ROPE_BASE = 10000.0
GM_WIDTH = 2 * D_MODEL
GM_GROUPS = 8
GM_GC = GM_WIDTH // GM_GROUPS
CHUNK = 128
PAGE = 128
EPS = 1e-6
F32 = jnp.float32
BF16 = jnp.bfloat16

LANES = 128
VMEM_LIMIT = 56 * 1024 * 1024

C_FG, C_RQ, C_RK, C_RV, C_RG, EVEN_COLS = 0, 512, 1024, 1536, 2560, 3584
R_FQ, R_FK, R_FV, R_FF, EVEN_ROWS_T = 0, 512, 1024, 1536, 1552

LOG_GAMMA = [float(np.log(np.float32(1.0) - np.float32(2.0) ** np.float32(-5.0 - h))) for h in range(H_RET)]

LOG2E = float(np.log2(np.e))
NT_DIMS = (((1,), (1,)), ((), ()))
TN_DIMS = (((0,), (0,)), ((), ()))


def _params(sem):
    return pltpu.CompilerParams(dimension_semantics=sem, vmem_limit_bytes=VMEM_LIMIT)


def _const_spec(shape):
    nd = len(shape)
    return pl.BlockSpec(shape, lambda *_: (0,) * nd, pipeline_mode=pl.Buffered(1))


def _rms(x, g):
    ms = jnp.mean(x * x, axis=-1, keepdims=True)
    return x * lax.rsqrt(ms + EPS) * g


def _layer_norm(x, g):
    xc = x - jnp.mean(x, axis=-1, keepdims=True)
    return xc * lax.rsqrt(jnp.mean(xc * xc, axis=-1, keepdims=True) + EPS) * g


def _rope(x, cos, sin_signed):
    rows = x.shape[0]
    first_half = (lax.broadcasted_iota(jnp.int32, (rows, LANES), 1) % RET_DK) < RET_DK // 2
    outs = []
    for c in range(x.shape[1] // LANES):
        xc = x[:, c * LANES:(c + 1) * LANES]
        nxt = pltpu.roll(xc, LANES - RET_DK // 2, axis=1)
        prv = pltpu.roll(xc, RET_DK // 2, axis=1)
        outs.append(xc * cos + jnp.where(first_half, nxt, prv) * sin_signed)
    return jnp.concatenate(outs, axis=-1)


def _log_sigmoid(x):
    return jnp.minimum(x, 0.0) - jnp.log1p(jnp.exp(-jnp.abs(x)))


def _head_rms_t(x, g):
    tokens = x.shape[1]
    x = x.reshape(H_FOX, HEAD_DIM, tokens)
    x = x * lax.rsqrt(jnp.mean(x * x, axis=1, keepdims=True) + EPS) * g[None]
    return x.reshape(W_FOX, tokens)


def _even_proj_kernel(x_ref, g_ref, w_ref, wt_ref, bf_ref, gq_ref, gk_ref, cos_ref, sin_ref,
                      sfg_ref, rq_ref, rk_ref, rv_ref, srg_ref, kn_ref,
                      qt_ref, kt32_ref, ktb_ref, vt32_ref, vtb_ref, lft_ref):
    h = _rms(x_ref[...], g_ref[...]).astype(BF16)

    def proj(lo, hi):
        return jnp.dot(h, w_ref[:, lo:hi], preferred_element_type=F32)

    sfg_ref[...] = jax.nn.silu(proj(C_FG, C_RQ)).astype(BF16)
    cos = cos_ref[...]
    sin = sin_ref[...]
    rq_ref[...] = _rope(proj(C_RQ, C_RK), cos, sin).astype(BF16)
    rk_ref[...] = (_rope(proj(C_RK, C_RV), cos, sin) * (RET_DK ** -0.5)).astype(BF16)
    rv_ref[...] = proj(C_RV, C_RG).astype(BF16)
    srg_ref[...] = jax.nn.silu(proj(C_RG, EVEN_COLS)).astype(BF16)

    t = lax.dot_general(wt_ref[...], h, NT_DIMS, preferred_element_type=F32)
    qt_ref[0] = (_head_rms_t(t[R_FQ:R_FK], gq_ref[...]) * (HEAD_DIM ** -0.5 * LOG2E)).astype(BF16)
    kt = _head_rms_t(t[R_FK:R_FV], gk_ref[...])
    kt32_ref[0] = kt
    ktb_ref[0] = kt.astype(BF16)
    kn_ref[...] = kt.T.astype(BF16)
    vt = t[R_FV:R_FF]
    vt32_ref[0] = vt
    vtb_ref[0] = vt.astype(BF16)
    lft_ref[0] = _log_sigmoid(t[R_FF:R_FF + H_FOX] + bf_ref[...])


def _even_proj(x2d, n_seq, g_norm, w, wt, b_f, g_q, g_k, cos, sin, tm):
    m = x2d.shape[0]
    s = m // n_seq
    per_seq = s // tm
    row = lambda n: pl.BlockSpec((tm, n), lambda i: (i, 0))
    tab = pl.BlockSpec((tm, LANES), lambda i: (i % per_seq, 0))
    col = lambda n: pl.BlockSpec((1, n, tm), lambda i: (i // per_seq, 0, i % per_seq))
    outs = [(W_FOX, BF16), (W_RQK, BF16), (W_RQK, BF16), (W_RET, BF16), (W_RET, BF16), (W_FOX, BF16)]
    outs_t = [(W_FOX, BF16), (W_FOX, F32), (W_FOX, BF16), (W_FOX, F32), (W_FOX, BF16), (H_FOX, F32)]
    return pl.pallas_call(
        _even_proj_kernel,
        grid=(m // tm,),
        in_specs=[row(D_MODEL), _const_spec((1, D_MODEL)), _const_spec((D_MODEL, EVEN_COLS)),
                  _const_spec((EVEN_ROWS_T, D_MODEL)), _const_spec((H_FOX, 1)), _const_spec((HEAD_DIM, 1)),
                  _const_spec((HEAD_DIM, 1)), tab, tab],
        out_specs=[row(n) for n, _ in outs] + [col(n) for n, _ in outs_t],
        out_shape=([jax.ShapeDtypeStruct((m, n), dt) for n, dt in outs]
                   + [jax.ShapeDtypeStruct((n_seq, n, s), dt) for n, dt in outs_t]),
        compiler_params=_params(("parallel",)),
        name="even_proj",
    )(x2d, g_norm, w, wt, b_f, g_q, g_k, cos, sin)


def _cumsum_kernel(x_ref, o_ref):
    x = x_ref[0]
    n = x.shape[1]
    lane = lax.broadcasted_iota(jnp.int32, x.shape, 1)
    sh = 1
    while sh < n:
        x = x + jnp.where(lane >= sh, pltpu.roll(x, sh, axis=1), 0.0)
        sh *= 2
    o_ref[0] = x


def _cumsum_lanes(x):
    b, h, s = x.shape
    spec = pl.BlockSpec((1, h, s), lambda i: (i, 0, 0))
    return pl.pallas_call(
        _cumsum_kernel, grid=(b,), in_specs=[spec], out_specs=spec,
        out_shape=jax.ShapeDtypeStruct(x.shape, F32),
        compiler_params=_params(("parallel",)), name="logf_cumsum",
    )(x)


HEADS_PER_GROUP = 4
GROUP_W = HEADS_PER_GROUP * HEAD_DIM


def _fox_prompt_kernel(*refs, blk):
    _fox_prompt_body(pl.program_id(2), *refs, blk=blk)


def _fox_prompt_body(i, qt_ref, k_ref, v_ref, ccol_ref, sfg_ref, a_ref,
                     qm_ref, vp_ref, cb_ref, m_ref, acc_ref, s_ref, p_ref, *, blk):
    s_len = k_ref.shape[1]

    @pl.when(i == 0)
    def _():
        ones_row = jnp.where(lax.broadcasted_iota(jnp.int32, (HEAD_DIM, s_len), 0) == 0, 1.0, 0.0).astype(BF16)
        ccol = ccol_ref[0, 0] * (-LOG2E)
        for hh in range(HEADS_PER_GROUP):
            vp_ref[hh, :HEAD_DIM, :] = v_ref[0, hh * HEAD_DIM:(hh + 1) * HEAD_DIM, :]
            vp_ref[hh, HEAD_DIM:, :] = ones_row
            cb_ref[hh] = jnp.broadcast_to(ccol[:, hh:hh + 1], (s_len, LANES))

    qt = qt_ref[0]
    row_head = lax.broadcasted_iota(jnp.int32, (GROUP_W, blk), 0) // HEAD_DIM
    for hh in range(HEADS_PER_GROUP):
        qm_ref[hh] = jnp.where(row_head == hh, qt, jnp.zeros_like(qt))
    m_ref[...] = jnp.full(m_ref.shape, -1e30, F32)
    acc_ref[...] = jnp.zeros(acc_ref.shape, F32)
    strips = [(hh, slice(c * LANES, (c + 1) * LANES), c) for hh in range(HEADS_PER_GROUP) for c in range(blk // LANES)]

    def raw_scores(j, slot):
        off = pl.multiple_of(j * blk, blk)
        kb = k_ref[0, pl.ds(off, blk), :]
        for hh in range(HEADS_PER_GROUP):
            bias = cb_ref[hh, pl.ds(off, blk), :]
            s_ref[slot, hh] = (jnp.dot(kb, qm_ref[hh], preferred_element_type=F32)
                               + jnp.concatenate([bias] * (blk // LANES), axis=1))

    def consume(j, slot, masked):
        off = pl.multiple_of(j * blk, blk)

        def scores(hh, cols, c):
            s = s_ref[slot, hh, :, cols]
            if masked:
                causal = (lax.broadcasted_iota(jnp.int32, (blk, LANES), 0)
                          <= lax.broadcasted_iota(jnp.int32, (blk, LANES), 1) + c * LANES)
                s = jnp.where(causal, s, -jnp.inf)
            return s

        for hh, cols, c in strips:
            s = scores(hh, cols, c)
            m_old = m_ref[hh, :, cols]
            m_new = jnp.maximum(m_old, jnp.max(s, axis=0, keepdims=True))
            p_ref[hh, :, cols] = jnp.exp2(s - m_new).astype(BF16)
            acc_ref[hh, :, cols] = jnp.exp2(m_old - m_new) * acc_ref[hh, :, cols]
            m_ref[hh, :, cols] = m_new
        for hh in range(HEADS_PER_GROUP):
            acc_ref[hh] += jnp.dot(vp_ref[hh, :, pl.ds(off, blk)], p_ref[hh],
                                   preferred_element_type=F32)

    raw_scores(0, 0)

    def pair(t, carry):
        j = 2 * t
        raw_scores(j + 1, 1)
        consume(j, 0, False)
        raw_scores(j + 2, 0)
        consume(j + 1, 1, False)
        return carry

    lax.fori_loop(0, i // 2, pair, 0)

    @pl.when(i % 2 == 1)
    def _():
        raw_scores(i, 1)
        consume(i - 1, 0, False)
        consume(i, 1, True)

    @pl.when(i % 2 == 0)
    def _():
        consume(i, 0, True)

    ot = jnp.concatenate([acc_ref[hh, :HEAD_DIM, :] / acc_ref[hh, HEAD_DIM:HEAD_DIM + 1, :]
                          for hh in range(HEADS_PER_GROUP)], axis=0)
    a_ref[0] = (ot.T * sfg_ref[0].astype(F32)).astype(BF16)


FOX_BLK = 256


def _fox_prompt_specs(s, ids):
    blk = FOX_BLK

    def spec(shape, f):
        return pl.BlockSpec(shape, lambda *a: f(*ids(*a)))

    rows = spec((1, blk, GROUP_W), lambda bi, g, i: (bi, i, g))
    in_specs = [spec((1, GROUP_W, blk), lambda bi, g, i: (bi, g, i)),
                spec((1, s, GROUP_W), lambda bi, g, i: (bi, 0, g)),
                spec((1, GROUP_W, s), lambda bi, g, i: (bi, g, 0)),
                spec((1, 1, s, HEADS_PER_GROUP), lambda bi, g, i: (bi, g, 0, 0)),
                rows]
    scratch = [pltpu.VMEM((HEADS_PER_GROUP, GROUP_W, blk), BF16),
               pltpu.VMEM((HEADS_PER_GROUP, LANES, s), BF16),
               pltpu.VMEM((HEADS_PER_GROUP, s, LANES), F32),
               pltpu.VMEM((HEADS_PER_GROUP, 1, blk), F32),
               pltpu.VMEM((HEADS_PER_GROUP, LANES, blk), F32),
               pltpu.VMEM((2, HEADS_PER_GROUP, blk, blk), F32),
               pltpu.VMEM((HEADS_PER_GROUP, blk, blk), BF16)]
    return in_specs, rows, scratch


def _fox_prompt(qt, kn, vtb, ccol, sfg):
    b, s, _ = kn.shape
    in_specs, out_spec, scratch = _fox_prompt_specs(s, lambda bi, g, i: (bi, g, i))
    return pl.pallas_call(
        functools.partial(_fox_prompt_kernel, blk=FOX_BLK),
        grid=(b, H_FOX // HEADS_PER_GROUP, s // FOX_BLK),
        in_specs=in_specs, out_specs=out_spec,
        out_shape=jax.ShapeDtypeStruct((b, s, W_FOX), BF16),
        scratch_shapes=scratch,
        compiler_params=_params(("parallel", "parallel", "arbitrary")),
        name="fox_prompt",
    )(qt, kn, vtb, ccol, sfg)


def _ret_head_out(o, g, srg):
    return (_layer_norm(o, g) * srg.astype(F32)).astype(BF16)


def _ret_prompt_kernel(rq_ref, rk_ref, rv_ref, srg_ref, gret_ref, r_ref, st_ref, dec_ref, kdec_ref, *, chunks):
    L = CHUNK
    c = pl.program_id(1)

    @pl.when(c == 0)
    def _():
        st_ref[...] = jnp.zeros(st_ref.shape, F32)
        diff = (lax.broadcasted_iota(jnp.int32, (L, L), 0)
                - lax.broadcasted_iota(jnp.int32, (L, L), 1)).astype(F32)
        for h in range(H_RET):
            dec_ref[h] = jnp.where(diff >= 0, jnp.exp(LOG_GAMMA[h] * jnp.maximum(diff, 0.0)), 0.0)
        left = (L - 1.0) - lax.broadcasted_iota(jnp.int32, (L, LANES), 0).astype(F32)
        first = lax.broadcasted_iota(jnp.int32, (L, LANES), 1) < RET_DK
        for pp in range(H_RET // 2):
            kdec_ref[pp] = jnp.exp(jnp.where(first, LOG_GAMMA[2 * pp], LOG_GAMMA[2 * pp + 1]) * left)

    pos = lax.broadcasted_iota(jnp.int32, (L, 1), 0).astype(F32)
    lane_half = lax.broadcasted_iota(jnp.int32, (L, LANES), 1) // RET_DK
    for ci in range(chunks):
        rows_c = slice(ci * L, (ci + 1) * L)
        first_stage = []
        for pp in range(H_RET // 2):
            qp = rq_ref[0, rows_c, pp * LANES:(pp + 1) * LANES]
            kp = rk_ref[0, rows_c, pp * LANES:(pp + 1) * LANES]
            s_pair = st_ref[0, pp]
            s_pair_b = s_pair.astype(BF16)
            kd = (kp.astype(F32) * kdec_ref[pp]).astype(BF16)
            for j in range(2):
                h = 2 * pp + j
                qm = jnp.where(lane_half == j, qp, jnp.zeros_like(qp))
                vh = rv_ref[0, rows_c, h * RET_DV:(h + 1) * RET_DV]
                scores = lax.dot_general(qm, kp, NT_DIMS, preferred_element_type=F32)
                cross = jnp.dot(qm, s_pair_b, preferred_element_type=F32)
                upd = lax.dot_general(kd, vh, TN_DIMS, preferred_element_type=F32)
                first_stage.append((scores, cross, upd, vh, s_pair))
        outs = []
        for h, (scores, cross, upd, vh, s_pair) in enumerate(first_stage):
            lg = LOG_GAMMA[h]
            intra = jnp.dot((scores * dec_ref[h]).astype(BF16), vh, preferred_element_type=F32)
            outs.append(intra + cross * jnp.exp(lg * (pos + 1.0)))
            rows = slice((h % 2) * RET_DK, (h % 2 + 1) * RET_DK)
            st_ref[0, h // 2, rows, :] = float(np.exp(lg * L)) * s_pair[rows] + upd[rows]
        for h, o in enumerate(outs):
            cols = slice(h * RET_DV, (h + 1) * RET_DV)
            r_ref[0, rows_c, cols] = _ret_head_out(o, gret_ref[:, cols], srg_ref[0, rows_c, cols])


def _ret_prompt(rq, rk, rv, srg, g_ret):
    b, s, _ = rq.shape
    chunks = 2 if s % (2 * CHUNK) == 0 else 1
    rows = chunks * CHUNK
    qk = pl.BlockSpec((1, rows, W_RQK), lambda bi, c: (bi, c, 0))
    wide = pl.BlockSpec((1, rows, W_RET), lambda bi, c: (bi, c, 0))
    st = pl.BlockSpec((1, H_RET // 2, 2 * RET_DK, RET_DV), lambda bi, c: (bi, 0, 0, 0))
    return pl.pallas_call(
        functools.partial(_ret_prompt_kernel, chunks=chunks),
        grid=(b, s // rows),
        in_specs=[qk, qk, wide, wide, _const_spec((1, W_RET))],
        out_specs=[wide, st],
        out_shape=[jax.ShapeDtypeStruct((b, s, W_RET), BF16),
                   jax.ShapeDtypeStruct((b, H_RET // 2, 2 * RET_DK, RET_DV), F32)],
        scratch_shapes=[pltpu.VMEM((H_RET, CHUNK, CHUNK), F32), pltpu.VMEM((H_RET // 2, CHUNK, LANES), F32)],
        compiler_params=_params(("parallel", "arbitrary")),
        name="ret_prompt",
    )(rq, rk, rv, srg, g_ret)


def _ret_sample_kernel(rq_ref, rk_ref, rv_ref, srg_ref, gret_ref, s0_ref, r_ref, s1_ref, *, n_new):
    rows = rq_ref.shape[0]
    nb = rows // n_new
    two_dk = 2 * RET_DK
    ri = lax.broadcasted_iota(jnp.int32, (rows, rows), 0)
    ci = lax.broadcasted_iota(jnp.int32, (rows, rows), 1)
    same = (ri // n_new) == (ci // n_new)
    diff = ((ri % n_new) - (ci % n_new)).astype(F32)
    pos = (lax.broadcasted_iota(jnp.int32, (rows, 1), 0) % n_new).astype(F32)
    lane_half = lax.broadcasted_iota(jnp.int32, (rows, LANES), 1) // RET_DK
    q_sel = (lax.broadcasted_iota(jnp.int32, (rows, nb * two_dk), 1) // two_dk
             == lax.broadcasted_iota(jnp.int32, (rows, nb * two_dk), 0) // n_new)
    k_sel = (lax.broadcasted_iota(jnp.int32, (nb * two_dk, rows), 0) // two_dk
             == lax.broadcasted_iota(jnp.int32, (nb * two_dk, rows), 1) // n_new)
    row_head = (lax.broadcasted_iota(jnp.int32, (nb * two_dk, 1), 0) % two_dk) // RET_DK

    qp = rq_ref[...]
    kp = rk_ref[...]
    s_old = s0_ref[...].reshape(nb * two_dk, RET_DV)
    s_old_b = s_old.astype(BF16)
    upd = []
    for j in range(2):
        h = 2 * pl.program_id(0) + j
        lg = jnp.where(h == 0, LOG_GAMMA[0], 0.0)
        for hh in range(1, H_RET):
            lg = jnp.where(h == hh, LOG_GAMMA[hh], lg)
        qm = jnp.where(lane_half == j, qp, jnp.zeros_like(qp))
        vh = rv_ref[:, j * RET_DV:(j + 1) * RET_DV]
        decay = jnp.where(same & (diff >= 0), jnp.exp(lg * jnp.maximum(diff, 0.0)), 0.0)
        scores = lax.dot_general(qm, kp, NT_DIMS, preferred_element_type=F32) * decay
        intra = jnp.dot(scores.astype(BF16), vh, preferred_element_type=F32)
        q_exp = jnp.where(q_sel, jnp.tile(qm, (1, nb)), jnp.zeros((), BF16))
        cross = jnp.dot(q_exp, s_old_b, preferred_element_type=F32) * jnp.exp(lg * (pos + 1.0))
        kd = kp.astype(F32) * jnp.exp(lg * (n_new - 1.0 - pos))
        kd_exp = jnp.where(k_sel, jnp.tile(kd.T, (nb, 1)), 0.0).astype(BF16)
        keep = jnp.exp(lg * jnp.full((1, 1), float(n_new), F32))
        upd.append((keep, jnp.dot(kd_exp, vh, preferred_element_type=F32)))
        cols = slice(j * RET_DV, (j + 1) * RET_DV)
        r_ref[:, cols] = _ret_head_out(intra + cross, gret_ref[:, cols], srg_ref[:, cols])
    s_new = jnp.where(row_head == 0, upd[0][0] * s_old + upd[0][1], upd[1][0] * s_old + upd[1][1])
    s1_ref[...] = s_new.reshape(s1_ref.shape)


def _ret_sample(rq, rk, rv, srg, g_ret, state, n_new):
    rows = rq.shape[0]
    nb = rows // n_new
    pair = pl.BlockSpec((rows, LANES), lambda p: (0, p))
    wide = pl.BlockSpec((rows, 2 * RET_DV), lambda p: (0, p))
    st = pl.BlockSpec((nb, 2, RET_DK, RET_DV), lambda p: (0, p, 0, 0))
    return pl.pallas_call(
        functools.partial(_ret_sample_kernel, n_new=n_new),
        grid=(H_RET // 2,),
        in_specs=[pair, pair, wide, wide, pl.BlockSpec((1, 2 * RET_DV), lambda p: (0, p)), st],
        out_specs=[wide, st],
        out_shape=[jax.ShapeDtypeStruct((rows, W_RET), BF16),
                   jax.ShapeDtypeStruct(state.shape, F32)],
        compiler_params=_params(("parallel",)),
        name="ret_sample",
    )(rq, rk, rv, srg, g_ret, state)


def _suffix_sum_lanes(x):
    n = x.shape[1]
    lane = lax.broadcasted_iota(jnp.int32, x.shape, 1)
    incl = x
    sh = 1
    while sh < n:
        incl = incl + jnp.where(lane < n - sh, pltpu.roll(incl, n - sh, axis=1), 0.0)
        sh *= 2
    return incl - x


def _fox_sample_kernel(pt_ref, *refs, n_steps, **kw):
    _fox_sample_body(pl.program_id(0), pl.program_id(1), n_steps, pt_ref, *refs, **kw)


def _fox_sample_body(b, c, n_steps, pt_ref, q_ref, kn_ref, vn_ref, lfn_ref, sfg_ref, lfc_ref, *refs,
                     n_new, pages_per_step, page_groups):
    g = pages_per_step
    k_refs, v_refs = refs[:g], refs[g:2 * g]
    a_ref, qbd_ref, m_ref, l_ref, acc_ref, tail_ref = refs[2 * g:]
    first_page = (n_steps - 1 - c) * g
    nq = n_new * H_FOX
    row_tok = lax.broadcasted_iota(jnp.int32, (nq, 1), 0) // H_FOX
    head_of_row = lax.broadcasted_iota(jnp.int32, (H_FOX, W_FOX), 0)
    head_of_lane = lax.broadcasted_iota(jnp.int32, (H_FOX, W_FOX), 1) // HEAD_DIM

    def partial_softmax(s, vt):
        m = jnp.max(s, axis=-1, keepdims=True)
        p = jnp.exp2(s - m)
        return (m, jnp.sum(p, axis=-1, keepdims=True),
                lax.dot_general(p.astype(BF16), vt, NT_DIMS, preferred_element_type=F32))

    def merge(parts):
        m_new = m_ref[...]
        for m, _, _ in parts:
            m_new = jnp.maximum(m_new, m)
        alpha = jnp.exp2(m_ref[...] - m_new)
        l_new = alpha * l_ref[...]
        acc = alpha * acc_ref[...]
        for m, l, o in parts:
            w = jnp.exp2(m - m_new)
            l_new = l_new + w * l
            acc = acc + w * o
        m_ref[...] = m_new
        l_ref[...] = l_new
        acc_ref[...] = acc

    @pl.when(c == 0)
    def _():
        q = q_ref[0].astype(F32)
        for t in range(n_new):
            qbd_ref[t * H_FOX:(t + 1) * H_FOX, :] = jnp.where(
                head_of_lane == head_of_row, jnp.broadcast_to(q[t:t + 1, :], (H_FOX, W_FOX)), 0.0).astype(BF16)
        tail_ref[...] = jnp.zeros(tail_ref.shape, F32)
        m_ref[...] = jnp.full(m_ref.shape, -1e30, F32)
        l_ref[...] = jnp.zeros(l_ref.shape, F32)
        acc_ref[...] = jnp.zeros(acc_ref.shape, F32)
        lane = lax.broadcasted_iota(jnp.int32, (H_FOX, LANES), 1)
        pfx = lfn_ref[...]
        sh = 1
        while sh < n_new:
            pfx = pfx + jnp.where(lane % n_new >= sh, pltpu.roll(pfx, sh, axis=1), 0.0)
            sh *= 2
        col = lax.broadcasted_iota(jnp.int32, (nq, LANES), 1)
        valid = (col // n_new == b) & (col % n_new <= row_tok)
        s = jnp.dot(qbd_ref[...], kn_ref[...], preferred_element_type=F32)
        merge([partial_softmax(jnp.where(valid, s - jnp.tile(pfx, (n_new, 1)) * LOG2E, -jnp.inf), vn_ref[...])])

    lf = jnp.concatenate([lfc_ref[pt_ref[b, first_page + slot]] for slot in range(g)], axis=1)
    key_bias = jnp.tile((tail_ref[:, 0:1] + _suffix_sum_lanes(lf)) * LOG2E, (n_new, 1))
    tail_ref[...] = tail_ref[...] + jnp.sum(lf, axis=-1, keepdims=True)
    per = g // page_groups
    scores = []
    for grp in range(page_groups):
        pages = slice(grp * per, (grp + 1) * per)
        kc = jnp.concatenate([r[0].astype(BF16) for r in k_refs[pages]], axis=1)
        s = jnp.dot(qbd_ref[...], kc, preferred_element_type=F32)
        scores.append(s + key_bias[:, grp * per * PAGE:(grp + 1) * per * PAGE])
    parts = []
    for grp, s in enumerate(scores):
        vc = jnp.concatenate([r[0].astype(BF16) for r in v_refs[grp * per:(grp + 1) * per]], axis=1)
        parts.append(partial_softmax(s, vc))
    merge(parts)

    @pl.when(c == n_steps - 1)
    def _():
        o = acc_ref[...] / l_ref[...]
        sfg = sfg_ref[0].astype(F32)
        for t in range(n_new):
            ot = jnp.where(head_of_lane == head_of_row, o[t * H_FOX:(t + 1) * H_FOX, :], 0.0)
            a_ref[0, t:t + 1, :] = jnp.sum(ot, axis=0, keepdims=True) * sfg[t:t + 1, :]


def _fox_sample(page_table, q, kt_new, vt_new, lft_new, sfg, cache_kt, cache_vt, cache_lft):
    bd, n_new, _ = q.shape
    g, n_steps, kw = _fox_sample_plan(page_table, n_new)
    in_specs, out_spec, scratch = _fox_sample_specs(n_new, g, n_steps, (kt_new, vt_new, lft_new, cache_lft),
                                                    lambda b, c, pt: (b, c, pt))
    grid_spec = pltpu.PrefetchScalarGridSpec(
        num_scalar_prefetch=1, grid=(bd, n_steps), in_specs=in_specs, out_specs=out_spec, scratch_shapes=scratch)
    return pl.pallas_call(
        functools.partial(_fox_sample_kernel, n_steps=n_steps, **kw),
        grid_spec=grid_spec,
        out_shape=jax.ShapeDtypeStruct((bd, n_new, W_FOX), F32),
        compiler_params=_params(("parallel", "arbitrary")),
        name="fox_sample",
    )(page_table, q, kt_new, vt_new, lft_new, sfg, cache_lft, *([cache_kt] * g), *([cache_vt] * g))


def _fox_sample_plan(page_table, n_new):
    n_pages = page_table.shape[1]
    g = next(c for c in (16, 8, 4, 2, 1) if n_pages % c == 0)
    page_groups = 2 if g >= 2 else 1
    return g, n_pages // g, dict(n_new=n_new, pages_per_step=g, page_groups=page_groups)


def _fox_sample_specs(n_new, g, n_steps, resident, ids):
    nq = n_new * H_FOX

    def spec(shape, f, **kw):
        return pl.BlockSpec(shape, lambda *a: f(*ids(*a)), **kw)

    def tok(n):
        return spec((1, n_new, n), lambda b, c, pt: (b, 0, 0))

    def whole(arr):
        nd = arr.ndim
        return spec(arr.shape, lambda b, c, pt: (0,) * nd, pipeline_mode=pl.Buffered(1))

    def page_spec(slot):
        return spec((1, W_FOX, PAGE), lambda b, c, pt: (pt[b, (n_steps - 1 - c) * g + slot], 0, 0))

    kt_new, vt_new, lft_new, cache_lft = resident
    in_specs = [tok(W_FOX), whole(kt_new), whole(vt_new), whole(lft_new), tok(W_FOX), whole(cache_lft)]
    in_specs += [page_spec(s) for s in range(g)] + [page_spec(s) for s in range(g)]
    scratch = [pltpu.VMEM((nq, W_FOX), BF16), pltpu.VMEM((nq, 1), F32), pltpu.VMEM((nq, 1), F32),
               pltpu.VMEM((nq, W_FOX), F32), pltpu.VMEM((H_FOX, LANES), F32)]
    return in_specs, tok(W_FOX), scratch


def _fox_fused_kernel(pt_ref, *refs, n_prompt_in, n_sample_in, n_prompt_scratch, n_steps, sample_kw):
    refs = list(refs)
    p_in, refs = refs[:n_prompt_in], refs[n_prompt_in:]
    s_in, refs = refs[:n_sample_in], refs[n_sample_in:]
    (a_p, a_d), refs = refs[:2], refs[2:]
    p_scr, s_scr = refs[:n_prompt_scratch], refs[n_prompt_scratch:]
    step = (pl.program_id(0) * pl.num_programs(1) + pl.program_id(1)) * pl.num_programs(2) + pl.program_id(2)
    _fox_sample_body(step // n_steps, step % n_steps, n_steps, pt_ref, *s_in, a_d, *s_scr, **sample_kw)
    _fox_prompt_body(pl.program_id(2), *p_in, a_p, *p_scr, blk=FOX_BLK)


def _fox_fused(qt, kn, vtb, ccol, sfg, page_table, q, kt_new, vt_new, lft_new, sfg_d, cache_kt, cache_vt, cache_lft):
    b, s, _ = kn.shape
    bd, n_new, _ = q.shape
    ng, nqb = H_FOX // HEADS_PER_GROUP, s // FOX_BLK
    g, n_steps, kw = _fox_sample_plan(page_table, n_new)
    assert b * ng * nqb == bd * n_steps
    p_in, p_out, p_scr = _fox_prompt_specs(s, lambda bi, gi, i, pt: (bi, gi, i))

    def sample_ids(bi, gi, i, pt):
        step = (bi * ng + gi) * nqb + i
        return step // n_steps, step % n_steps, pt

    s_in, s_out, s_scr = _fox_sample_specs(n_new, g, n_steps, (kt_new, vt_new, lft_new, cache_lft), sample_ids)
    grid_spec = pltpu.PrefetchScalarGridSpec(
        num_scalar_prefetch=1, grid=(b, ng, nqb), in_specs=p_in + s_in, out_specs=[p_out, s_out],
        scratch_shapes=p_scr + s_scr)
    return pl.pallas_call(
        functools.partial(_fox_fused_kernel, n_prompt_in=len(p_in), n_sample_in=len(s_in),
                          n_prompt_scratch=len(p_scr), n_steps=n_steps, sample_kw=kw),
        grid_spec=grid_spec,
        out_shape=[jax.ShapeDtypeStruct((b, s, W_FOX), BF16), jax.ShapeDtypeStruct((bd, n_new, W_FOX), F32)],
        compiler_params=_params(("arbitrary", "arbitrary", "arbitrary")),
        name="fox_fused",
    )(page_table, qt, kn, vtb, ccol, sfg, q, kt_new, vt_new, lft_new, sfg_d, cache_lft,
      *([cache_kt] * g), *([cache_vt] * g))


def _tail_kernel(a_ref, r_ref, x_ref, woe_ref, gno_ref, wio_ref, gv_ref, ws_ref, bs_ref, woo_ref,
                 *out_refs, n_new):
    rows = x_ref.shape[0]
    x1 = (x_ref[...]
          + jnp.dot(a_ref[...].astype(BF16), woe_ref[:W_FOX, :], preferred_element_type=F32)
          + jnp.dot(r_ref[...], woe_ref[W_FOX:, :], preferred_element_type=F32))
    h = _rms(x1, gno_ref[...]).astype(BF16)
    v = jax.nn.gelu(jnp.dot(h, wio_ref[:, GM_WIDTH:2 * GM_WIDTH], preferred_element_type=F32))
    vn = _layer_norm(v, gv_ref[...])
    if n_new is None:
        y_ref, = out_refs
        tril = (lax.broadcasted_iota(jnp.int32, (CHUNK, CHUNK), 0)
                >= lax.broadcasted_iota(jnp.int32, (CHUNK, CHUNK), 1))
        vb = vn.astype(BF16)

        def spatial(g):
            wm = jnp.where(tril, ws_ref[g], 0.0).astype(BF16)
            bias = bs_ref[:, g:g + 1]
            return jnp.concatenate(
                [jnp.dot(wm, vb[c * CHUNK:(c + 1) * CHUNK, g * GM_GC:(g + 1) * GM_GC],
                         preferred_element_type=F32) + bias for c in range(rows // CHUNK)], axis=0)
    else:
        y_ref, vn_ref = out_refs
        vn_ref[...] = vn
        tok = lax.broadcasted_iota(jnp.int32, (rows, 1), 0) % n_new

        def spatial(g):
            vg = vn[:, g * GM_GC:(g + 1) * GM_GC]
            acc = jnp.zeros((rows, 1), F32)
            for t in range(n_new):
                acc = jnp.where(tok == t, bs_ref[g, t], acc)
            acc = jnp.broadcast_to(acc, (rows, GM_GC))
            for d in range(n_new):
                coef = jnp.zeros((rows, 1), F32)
                for t in range(d, n_new):
                    coef = jnp.where(tok == t, ws_ref[g, t * n_new + (t - d)], coef)
                src = vg if d == 0 else pltpu.roll(vg, d, axis=0)
                acc = acc + coef * src
            return acc

    u = jax.nn.gelu(jnp.dot(h, wio_ref[:, :GM_WIDTH], preferred_element_type=F32))
    gate = jnp.dot(h, wio_ref[:, 2 * GM_WIDTH:], preferred_element_type=F32)
    sv = jnp.concatenate([spatial(g) for g in range(GM_GROUPS)], axis=1)
    z = (u * sv * jax.nn.silu(gate)).astype(BF16)
    y_ref[...] = x1 + jnp.dot(z, woo_ref[...], preferred_element_type=F32)


def _tail(a, r, x2d, woe, gno, wio, gv, ws, bs, woo, tm, n_new):
    m = x2d.shape[0]
    row = lambda n: pl.BlockSpec((tm, n), lambda i: (i, 0))
    if n_new is None:
        mix_specs = [_const_spec(ws.shape), _const_spec(bs.shape)]
        out_specs = [row(D_MODEL)]
        out_shape = [jax.ShapeDtypeStruct((m, D_MODEL), F32)]
    else:
        smem = pl.BlockSpec(memory_space=pltpu.SMEM)
        mix_specs = [smem, smem]
        out_specs = [row(D_MODEL), row(GM_WIDTH)]
        out_shape = [jax.ShapeDtypeStruct((m, D_MODEL), F32), jax.ShapeDtypeStruct((m, GM_WIDTH), F32)]
    return pl.pallas_call(
        functools.partial(_tail_kernel, n_new=n_new),
        grid=(m // tm,),
        in_specs=[row(W_FOX), row(W_RET), row(D_MODEL), _const_spec(woe.shape), _const_spec((1, D_MODEL)),
                  _const_spec(wio.shape), _const_spec((1, GM_WIDTH))] + mix_specs + [_const_spec(woo.shape)],
        out_specs=out_specs, out_shape=out_shape,
        compiler_params=_params(("parallel",)),
        name="tail_prompt" if n_new is None else "tail_sample",
    )(a, r, x2d, woe, gno, wio, gv, ws, bs, woo)


def _rope_tables(pos):
    half = RET_DK // 2
    inv = ROPE_BASE ** (-jnp.arange(half, dtype=F32) / half)
    ang = pos.astype(F32)[:, None] * inv[None, :]
    cos, sin = jnp.cos(ang), jnp.sin(ang)
    return jnp.tile(jnp.concatenate([cos, cos], axis=-1), (1, 2)), jnp.tile(jnp.concatenate([-sin, sin], axis=-1), (1, 2))


def _row_tile(m):
    return 256 if m % 256 == 0 else 128


def kernel(x_prompt, x_sample, cache_k, cache_v, cache_logf, state_ret, page_table, norm_even, w_in_even,
           b_forget, qnorm, knorm, ret_norm, w_out_even, norm_odd, w_in_odd, vnorm_odd, w_spatial, b_spatial,
           w_out_odd):
    bp, s, _ = x_prompt.shape
    bd, n_new, _ = x_sample.shape
    n_pages = page_table.shape[1]
    past = n_pages * PAGE
    assert s % 256 == 0 and (bd * n_new) % CHUNK == 0 and w_in_even.shape[0] == 1 and w_in_odd.shape[0] == 1

    wi = w_in_even[0]
    sizes = (W_FOX, W_FOX, W_FOX, H_FOX, W_FOX, W_RQK, W_RQK, W_RET, W_RET)
    offs = np.concatenate([[0], np.cumsum(sizes)])
    fq, fk, fv, ff, fg, rq, rk, rv, rg = [wi[:, offs[i]:offs[i + 1]] for i in range(9)]
    w_even = jnp.concatenate([fg, rq, rk, rv, rg], axis=1).astype(BF16)
    w_even_t = jnp.concatenate([fq.T, fk.T, fv.T, ff.T, jnp.zeros((EVEN_ROWS_T - R_FF - H_FOX, D_MODEL), F32)],
                               axis=0).astype(BF16)
    b_f = b_forget[0][:, None]
    g_q = qnorm[0][:, None]
    g_k = knorm[0][:, None]
    g_ret = ret_norm[0].reshape(1, W_RET)
    woe = w_out_even[0].astype(BF16)
    wio = w_in_odd[0].astype(BF16)
    woo = w_out_odd[0].astype(BF16)
    g_ne, g_no, g_v = norm_even[0][None, :], norm_odd[0][None, :], vnorm_odd[0][None, :]

    cos_p, sin_p = _rope_tables(jnp.arange(s))
    xp2 = x_prompt.reshape(bp * s, D_MODEL)
    tm = _row_tile(s)
    sfg, rqp, rkp, rvp, srg, kn, qt, kt32, _, vt32, vtb, lft = _even_proj(
        xp2, bp, g_ne, w_even, w_even_t, b_f, g_q, g_k, cos_p, sin_p, tm)
    rows = bd * n_new
    cos_d, sin_d = _rope_tables(past + jnp.arange(n_new))
    xd2 = x_sample.reshape(rows, D_MODEL)
    sfgd, rqd, rkd, rvd, srgd, _, qtd, kt32d, ktbd, vt32d, vtbd, lftd = _even_proj(
        xd2, 1, g_ne, w_even, w_even_t, b_f, g_q, g_k, jnp.tile(cos_d, (bd, 1)), jnp.tile(sin_d, (bd, 1)), rows)

    ng = H_FOX // HEADS_PER_GROUP
    c_col4 = jnp.transpose(_cumsum_lanes(lft).reshape(bp, ng, HEADS_PER_GROUP, s), (0, 1, 3, 2))
    r3 = lambda t: t.reshape(bp, s, t.shape[-1])
    d3 = lambda t: t.reshape(bd, n_new, t.shape[-1])
    n_phys = cache_k.shape[1]
    pages_t = lambda cache: jnp.transpose(cache[0], (0, 2, 3, 1)).reshape(n_phys, W_FOX, PAGE)
    prompt_args = (qt, r3(kn), vtb, c_col4, r3(sfg))
    sample_args = (page_table, d3(jnp.transpose(qtd[0], (1, 0))), ktbd[0], vtbd[0], lftd[0], d3(sfgd),
                   pages_t(cache_k), pages_t(cache_v), jnp.transpose(cache_logf[0], (0, 2, 1)))
    if bp * ng * (s // FOX_BLK) == bd * _fox_sample_plan(page_table, n_new)[1]:
        a_p, a_d = _fox_fused(*prompt_args, *sample_args)
    else:
        a_p, a_d = _fox_prompt(*prompt_args), _fox_sample(*sample_args)
    r_p, st_p = _ret_prompt(r3(rqp), r3(rkp), r3(rvp), r3(srg), g_ret)
    r_d, st_d = _ret_sample(rqd, rkd, rvd, srgd, g_ret, state_ret[0], n_new)

    y_p, = _tail(a_p.reshape(bp * s, W_FOX), r_p.reshape(bp * s, W_RET), xp2, woe, g_no, wio, g_v,
                 w_spatial[0], jnp.transpose(b_spatial[0]), woo, 512 if s % 512 == 0 else tm, None)
    ws_new = w_spatial[0][:, :n_new, :n_new].reshape(GM_GROUPS, n_new * n_new)
    y_d, gv_d = _tail(a_d.reshape(rows, W_FOX), r_d, xd2, woe, g_no, wio, g_v,
                      ws_new, b_spatial[0][:, :n_new], woo, rows, n_new)

    heads_last = lambda t, n, length: jnp.transpose(t.reshape(n, H_FOX, HEAD_DIM, length), (0, 3, 1, 2))[None]
    tokens_first = lambda t, n: jnp.transpose(t[0], (1, 0)).reshape(1, bd, n_new, *n)
    return (y_p.reshape(bp, s, D_MODEL), y_d.reshape(bd, n_new, D_MODEL),
            heads_last(kt32, bp, s), heads_last(vt32, bp, s),
            jnp.transpose(lft, (0, 2, 1))[None], st_p.reshape(1, bp, H_RET, RET_DK, RET_DV),
            tokens_first(kt32d, (H_FOX, HEAD_DIM)), tokens_first(vt32d, (H_FOX, HEAD_DIM)),
            tokens_first(lftd, (H_FOX,)), st_d[None], gv_d.reshape(1, bd, n_new, GM_WIDTH))
```

```python
import functools

import numpy as np
import jax
import jax.numpy as jnp
from jax import lax
from jax.experimental import pallas as pl
from jax.experimental.pallas import tpu as pltpu

D_MODEL = 1024
HEAD_DIM = 64
H_FOX = 8
W_FOX = H_FOX * HEAD_DIM
H_RET = 8
RET_DK = 64
RET_DV = 128
W_RET = H_RET * RET_DV
W_RQK = H_RET * RET_DK
ROPE_BASE = 10000.0
GM_WIDTH = 2 * D_MODEL
GM_GROUPS = 8
GM_GC = GM_WIDTH // GM_GROUPS
CHUNK = 128
PAGE = 128
EPS = 1e-6
F32 = jnp.float32
BF16 = jnp.bfloat16

LANES = 128
VMEM_LIMIT = 56 * 1024 * 1024

C_FG, C_RQ, C_RK, C_RV, C_RG, EVEN_COLS = 0, 512, 1024, 1536, 2560, 3584
R_FQ, R_FK, R_FV, R_FF, EVEN_ROWS_T = 0, 512, 1024, 1536, 1552

LOG_GAMMA = [float(np.log(np.float32(1.0) - np.float32(2.0) ** np.float32(-5.0 - h))) for h in range(H_RET)]

LOG2E = float(np.log2(np.e))
NT_DIMS = (((1,), (1,)), ((), ()))
TN_DIMS = (((0,), (0,)), ((), ()))


def _params(sem):
    return pltpu.CompilerParams(dimension_semantics=sem, vmem_limit_bytes=VMEM_LIMIT)


def _const_spec(shape):
    nd = len(shape)
    return pl.BlockSpec(shape, lambda *_: (0,) * nd, pipeline_mode=pl.Buffered(1))


def _rms(x, g):
    ms = jnp.mean(x * x, axis=-1, keepdims=True)
    return x * lax.rsqrt(ms + EPS) * g


def _layer_norm(x, g):
    xc = x - jnp.mean(x, axis=-1, keepdims=True)
    return xc * lax.rsqrt(jnp.mean(xc * xc, axis=-1, keepdims=True) + EPS) * g


def _rope(x, cos, sin_signed):
    rows = x.shape[0]
    first_half = (lax.broadcasted_iota(jnp.int32, (rows, LANES), 1) % RET_DK) < RET_DK // 2
    outs = []
    for c in range(x.shape[1] // LANES):
        xc = x[:, c * LANES:(c + 1) * LANES]
        nxt = pltpu.roll(xc, LANES - RET_DK // 2, axis=1)
        prv = pltpu.roll(xc, RET_DK // 2, axis=1)
        outs.append(xc * cos + jnp.where(first_half, nxt, prv) * sin_signed)
    return jnp.concatenate(outs, axis=-1)


def _log_sigmoid(x):
    return jnp.minimum(x, 0.0) - jnp.log1p(jnp.exp(-jnp.abs(x)))


def _head_rms_t(x, g):
    tokens = x.shape[1]
    x = x.reshape(H_FOX, HEAD_DIM, tokens)
    x = x * lax.rsqrt(jnp.mean(x * x, axis=1, keepdims=True) + EPS) * g[None]
    return x.reshape(W_FOX, tokens)


def _even_proj_kernel(x_ref, g_ref, w_ref, wt_ref, bf_ref, gq_ref, gk_ref, cos_ref, sin_ref,
                      sfg_ref, rq_ref, rk_ref, rv_ref, srg_ref, kn_ref,
                      qt_ref, kt32_ref, ktb_ref, vt32_ref, vtb_ref, lft_ref):
    h = _rms(x_ref[...], g_ref[...]).astype(BF16)

    def proj(lo, hi):
        return jnp.dot(h, w_ref[:, lo:hi], preferred_element_type=F32)

    sfg_ref[...] = jax.nn.silu(proj(C_FG, C_RQ)).astype(BF16)
    cos = cos_ref[...]
    sin = sin_ref[...]
    rq_ref[...] = _rope(proj(C_RQ, C_RK), cos, sin).astype(BF16)
    rk_ref[...] = (_rope(proj(C_RK, C_RV), cos, sin) * (RET_DK ** -0.5)).astype(BF16)
    rv_ref[...] = proj(C_RV, C_RG).astype(BF16)
    srg_ref[...] = jax.nn.silu(proj(C_RG, EVEN_COLS)).astype(BF16)

    t = lax.dot_general(wt_ref[...], h, NT_DIMS, preferred_element_type=F32)
    qt_ref[0] = (_head_rms_t(t[R_FQ:R_FK], gq_ref[...]) * (HEAD_DIM ** -0.5 * LOG2E)).astype(BF16)
    kt = _head_rms_t(t[R_FK:R_FV], gk_ref[...])
    kt32_ref[0] = kt
    ktb_ref[0] = kt.astype(BF16)
    kn_ref[...] = kt.T.astype(BF16)
    vt = t[R_FV:R_FF]
    vt32_ref[0] = vt
    vtb_ref[0] = vt.astype(BF16)
    lft_ref[0] = _log_sigmoid(t[R_FF:R_FF + H_FOX] + bf_ref[...])


def _even_proj(x2d, n_seq, g_norm, w, wt, b_f, g_q, g_k, cos, sin, tm):
    m = x2d.shape[0]
    s = m // n_seq
    per_seq = s // tm
    row = lambda n: pl.BlockSpec((tm, n), lambda i: (i, 0))
    tab = pl.BlockSpec((tm, LANES), lambda i: (i % per_seq, 0))
    col = lambda n: pl.BlockSpec((1, n, tm), lambda i: (i // per_seq, 0, i % per_seq))
    outs = [(W_FOX, BF16), (W_RQK, BF16), (W_RQK, BF16), (W_RET, BF16), (W_RET, BF16), (W_FOX, BF16)]
    outs_t = [(W_FOX, BF16), (W_FOX, F32), (W_FOX, BF16), (W_FOX, F32), (W_FOX, BF16), (H_FOX, F32)]
    return pl.pallas_call(
        _even_proj_kernel,
        grid=(m // tm,),
        in_specs=[row(D_MODEL), _const_spec((1, D_MODEL)), _const_spec((D_MODEL, EVEN_COLS)),
                  _const_spec((EVEN_ROWS_T, D_MODEL)), _const_spec((H_FOX, 1)), _const_spec((HEAD_DIM, 1)),
                  _const_spec((HEAD_DIM, 1)), tab, tab],
        out_specs=[row(n) for n, _ in outs] + [col(n) for n, _ in outs_t],
        out_shape=([jax.ShapeDtypeStruct((m, n), dt) for n, dt in outs]
                   + [jax.ShapeDtypeStruct((n_seq, n, s), dt) for n, dt in outs_t]),
        compiler_params=_params(("parallel",)),
        name="even_proj",
    )(x2d, g_norm, w, wt, b_f, g_q, g_k, cos, sin)


def _cumsum_kernel(x_ref, o_ref):
    x = x_ref[0]
    n = x.shape[1]
    lane = lax.broadcasted_iota(jnp.int32, x.shape, 1)
    sh = 1
    while sh < n:
        x = x + jnp.where(lane >= sh, pltpu.roll(x, sh, axis=1), 0.0)
        sh *= 2
    o_ref[0] = x


def _cumsum_lanes(x):
    b, h, s = x.shape
    spec = pl.BlockSpec((1, h, s), lambda i: (i, 0, 0))
    return pl.pallas_call(
        _cumsum_kernel, grid=(b,), in_specs=[spec], out_specs=spec,
        out_shape=jax.ShapeDtypeStruct(x.shape, F32),
        compiler_params=_params(("parallel",)), name="logf_cumsum",
    )(x)


HEADS_PER_GROUP = 4
GROUP_W = HEADS_PER_GROUP * HEAD_DIM


def _fox_prompt_kernel(*refs, blk):
    _fox_prompt_body(pl.program_id(2), *refs, blk=blk)


def _fox_prompt_body(i, qt_ref, k_ref, v_ref, ccol_ref, sfg_ref, a_ref,
                     qm_ref, vp_ref, cb_ref, m_ref, acc_ref, s_ref, p_ref, *, blk, side_work=None):
    s_len = k_ref.shape[1]

    @pl.when(i == 0)
    def _():
        ones_row = jnp.where(lax.broadcasted_iota(jnp.int32, (HEAD_DIM, s_len), 0) == 0, 1.0, 0.0).astype(BF16)
        ccol = ccol_ref[0, 0] * (-LOG2E)
        for hh in range(HEADS_PER_GROUP):
            vp_ref[hh, :HEAD_DIM, :] = v_ref[0, hh * HEAD_DIM:(hh + 1) * HEAD_DIM, :]
            vp_ref[hh, HEAD_DIM:, :] = ones_row
            cb_ref[hh] = jnp.broadcast_to(ccol[:, hh:hh + 1], (s_len, LANES))

    if side_work is not None:
        side_work()
    qt = qt_ref[0]
    row_head = lax.broadcasted_iota(jnp.int32, (GROUP_W, blk), 0) // HEAD_DIM
    for hh in range(HEADS_PER_GROUP):
        qm_ref[hh] = jnp.where(row_head == hh, qt, jnp.zeros_like(qt))
    m_ref[...] = jnp.full(m_ref.shape, -1e30, F32)
    acc_ref[...] = jnp.zeros(acc_ref.shape, F32)
    strips = [(hh, slice(c * LANES, (c + 1) * LANES), c) for hh in range(HEADS_PER_GROUP) for c in range(blk // LANES)]

    def raw_scores(j, slot):
        off = pl.multiple_of(j * blk, blk)
        kb = k_ref[0, pl.ds(off, blk), :]
        for hh in range(HEADS_PER_GROUP):
            bias = cb_ref[hh, pl.ds(off, blk), :]
            s_ref[slot, hh] = (jnp.dot(kb, qm_ref[hh], preferred_element_type=F32)
                               + jnp.concatenate([bias] * (blk // LANES), axis=1))

    def consume(j, slot, masked):
        off = pl.multiple_of(j * blk, blk)

        def scores(hh, cols, c):
            s = s_ref[slot, hh, :, cols]
            if masked:
                causal = (lax.broadcasted_iota(jnp.int32, (blk, LANES), 0)
                          <= lax.broadcasted_iota(jnp.int32, (blk, LANES), 1) + c * LANES)
                s = jnp.where(causal, s, -jnp.inf)
            return s

        for hh, cols, c in strips:
            s = scores(hh, cols, c)
            m_old = m_ref[hh, :, cols]
            m_new = jnp.maximum(m_old, jnp.max(s, axis=0, keepdims=True))
            p_ref[hh, :, cols] = jnp.exp2(s - m_new).astype(BF16)
            acc_ref[hh, :, cols] = jnp.exp2(m_old - m_new) * acc_ref[hh, :, cols]
            m_ref[hh, :, cols] = m_new
        for hh in range(HEADS_PER_GROUP):
            acc_ref[hh] += jnp.dot(vp_ref[hh, :, pl.ds(off, blk)], p_ref[hh],
                                   preferred_element_type=F32)

    raw_scores(0, 0)

    def pair(t, carry):
        j = 2 * t
        raw_scores(j + 1, 1)
        consume(j, 0, False)
        raw_scores(j + 2, 0)
        consume(j + 1, 1, False)
        return carry

    lax.fori_loop(0, i // 2, pair, 0)

    @pl.when(i % 2 == 1)
    def _():
        raw_scores(i, 1)
        consume(i - 1, 0, False)
        consume(i, 1, True)

    @pl.when(i % 2 == 0)
    def _():
        consume(i, 0, True)

    ot = jnp.concatenate([acc_ref[hh, :HEAD_DIM, :] / acc_ref[hh, HEAD_DIM:HEAD_DIM + 1, :]
                          for hh in range(HEADS_PER_GROUP)], axis=0)
    a_ref[0] = (ot.T * sfg_ref[0].astype(F32)).astype(BF16)


FOX_BLK = 256


def _fox_prompt_specs(s, ids):
    blk = FOX_BLK

    def spec(shape, f):
        return pl.BlockSpec(shape, lambda *a: f(*ids(*a)))

    rows = spec((1, blk, GROUP_W), lambda bi, g, i: (bi, i, g))
    in_specs = [spec((1, GROUP_W, blk), lambda bi, g, i: (bi, g, i)),
                spec((1, s, GROUP_W), lambda bi, g, i: (bi, 0, g)),
                spec((1, GROUP_W, s), lambda bi, g, i: (bi, g, 0)),
                spec((1, 1, s, HEADS_PER_GROUP), lambda bi, g, i: (bi, g, 0, 0)),
                rows]
    scratch = [pltpu.VMEM((HEADS_PER_GROUP, GROUP_W, blk), BF16),
               pltpu.VMEM((HEADS_PER_GROUP, LANES, s), BF16),
               pltpu.VMEM((HEADS_PER_GROUP, s, LANES), F32),
               pltpu.VMEM((HEADS_PER_GROUP, 1, blk), F32),
               pltpu.VMEM((HEADS_PER_GROUP, LANES, blk), F32),
               pltpu.VMEM((2, HEADS_PER_GROUP, blk, blk), F32),
               pltpu.VMEM((HEADS_PER_GROUP, blk, blk), BF16)]
    return in_specs, rows, scratch


def _fox_prompt(qt, kn, vtb, ccol, sfg):
    b, s, _ = kn.shape
    in_specs, out_spec, scratch = _fox_prompt_specs(s, lambda bi, g, i: (bi, g, i))
    return pl.pallas_call(
        functools.partial(_fox_prompt_kernel, blk=FOX_BLK),
        grid=(b, H_FOX // HEADS_PER_GROUP, s // FOX_BLK),
        in_specs=in_specs, out_specs=out_spec,
        out_shape=jax.ShapeDtypeStruct((b, s, W_FOX), BF16),
        scratch_shapes=scratch,
        compiler_params=_params(("parallel", "parallel", "arbitrary")),
        name="fox_prompt",
    )(qt, kn, vtb, ccol, sfg)


def _ret_head_out(o, g, srg):
    return (_layer_norm(o, g) * srg.astype(F32)).astype(BF16)


def _ret_prompt_kernel(rq_ref, rk_ref, rv_ref, srg_ref, gret_ref, r_ref, st_ref, dec_ref, kdec_ref, *, chunks):
    L = CHUNK
    c = pl.program_id(1)

    @pl.when(c == 0)
    def _():
        st_ref[...] = jnp.zeros(st_ref.shape, F32)
        diff = (lax.broadcasted_iota(jnp.int32, (L, L), 0)
                - lax.broadcasted_iota(jnp.int32, (L, L), 1)).astype(F32)
        for h in range(H_RET):
            dec_ref[h] = jnp.where(diff >= 0, jnp.exp(LOG_GAMMA[h] * jnp.maximum(diff, 0.0)), 0.0)
        left = (L - 1.0) - lax.broadcasted_iota(jnp.int32, (L, LANES), 0).astype(F32)
        first = lax.broadcasted_iota(jnp.int32, (L, LANES), 1) < RET_DK
        for pp in range(H_RET // 2):
            kdec_ref[pp] = jnp.exp(jnp.where(first, LOG_GAMMA[2 * pp], LOG_GAMMA[2 * pp + 1]) * left)

    pos = lax.broadcasted_iota(jnp.int32, (L, 1), 0).astype(F32)
    lane_half = lax.broadcasted_iota(jnp.int32, (L, LANES), 1) // RET_DK
    for ci in range(chunks):
        rows_c = slice(ci * L, (ci + 1) * L)
        first_stage = []
        for pp in range(H_RET // 2):
            qp = rq_ref[0, rows_c, pp * LANES:(pp + 1) * LANES]
            kp = rk_ref[0, rows_c, pp * LANES:(pp + 1) * LANES]
            s_pair = st_ref[0, pp]
            s_pair_b = s_pair.astype(BF16)
            kd = (kp.astype(F32) * kdec_ref[pp]).astype(BF16)
            for j in range(2):
                h = 2 * pp + j
                qm = jnp.where(lane_half == j, qp, jnp.zeros_like(qp))
                vh = rv_ref[0, rows_c, h * RET_DV:(h + 1) * RET_DV]
                scores = lax.dot_general(qm, kp, NT_DIMS, preferred_element_type=F32)
                cross = jnp.dot(qm, s_pair_b, preferred_element_type=F32)
                upd = lax.dot_general(kd, vh, TN_DIMS, preferred_element_type=F32)
                first_stage.append((scores, cross, upd, vh, s_pair))
        outs = []
        for h, (scores, cross, upd, vh, s_pair) in enumerate(first_stage):
            lg = LOG_GAMMA[h]
            intra = jnp.dot((scores * dec_ref[h]).astype(BF16), vh, preferred_element_type=F32)
            outs.append(intra + cross * jnp.exp(lg * (pos + 1.0)))
            rows = slice((h % 2) * RET_DK, (h % 2 + 1) * RET_DK)
            st_ref[0, h // 2, rows, :] = float(np.exp(lg * L)) * s_pair[rows] + upd[rows]
        for h, o in enumerate(outs):
            cols = slice(h * RET_DV, (h + 1) * RET_DV)
            r_ref[0, rows_c, cols] = _ret_head_out(o, gret_ref[:, cols], srg_ref[0, rows_c, cols])


def _ret_prompt(rq, rk, rv, srg, g_ret):
    b, s, _ = rq.shape
    chunks = next(n for n in (4, 2, 1) if s % (n * CHUNK) == 0)
    rows = chunks * CHUNK
    qk = pl.BlockSpec((1, rows, W_RQK), lambda bi, c: (bi, c, 0))
    wide = pl.BlockSpec((1, rows, W_RET), lambda bi, c: (bi, c, 0))
    st = pl.BlockSpec((1, H_RET // 2, 2 * RET_DK, RET_DV), lambda bi, c: (bi, 0, 0, 0))
    return pl.pallas_call(
        functools.partial(_ret_prompt_kernel, chunks=chunks),
        grid=(b, s // rows),
        in_specs=[qk, qk, wide, wide, _const_spec((1, W_RET))],
        out_specs=[wide, st],
        out_shape=[jax.ShapeDtypeStruct((b, s, W_RET), BF16),
                   jax.ShapeDtypeStruct((b, H_RET // 2, 2 * RET_DK, RET_DV), F32)],
        scratch_shapes=[pltpu.VMEM((H_RET, CHUNK, CHUNK), F32), pltpu.VMEM((H_RET // 2, CHUNK, LANES), F32)],
        compiler_params=_params(("parallel", "arbitrary")),
        name="ret_prompt",
    )(rq, rk, rv, srg, g_ret)


def _ret_sample_kernel(rq_ref, rk_ref, rv_ref, srg_ref, gret_ref, s0_ref, r_ref, s1_ref, *, n_new):
    rows = rq_ref.shape[0]
    nb = rows // n_new
    two_dk = 2 * RET_DK
    ri = lax.broadcasted_iota(jnp.int32, (rows, rows), 0)
    ci = lax.broadcasted_iota(jnp.int32, (rows, rows), 1)
    same = (ri // n_new) == (ci // n_new)
    diff = ((ri % n_new) - (ci % n_new)).astype(F32)
    pos = (lax.broadcasted_iota(jnp.int32, (rows, 1), 0) % n_new).astype(F32)
    lane_half = lax.broadcasted_iota(jnp.int32, (rows, LANES), 1) // RET_DK
    q_sel = (lax.broadcasted_iota(jnp.int32, (rows, nb * two_dk), 1) // two_dk
             == lax.broadcasted_iota(jnp.int32, (rows, nb * two_dk), 0) // n_new)
    k_sel = (lax.broadcasted_iota(jnp.int32, (nb * two_dk, rows), 0) // two_dk
             == lax.broadcasted_iota(jnp.int32, (nb * two_dk, rows), 1) // n_new)
    row_head = (lax.broadcasted_iota(jnp.int32, (nb * two_dk, 1), 0) % two_dk) // RET_DK

    qp = rq_ref[...]
    kp = rk_ref[...]
    s_old = s0_ref[...].reshape(nb * two_dk, RET_DV)
    s_old_b = s_old.astype(BF16)
    upd = []
    for j in range(2):
        h = 2 * pl.program_id(0) + j
        lg = jnp.where(h == 0, LOG_GAMMA[0], 0.0)
        for hh in range(1, H_RET):
            lg = jnp.where(h == hh, LOG_GAMMA[hh], lg)
        qm = jnp.where(lane_half == j, qp, jnp.zeros_like(qp))
        vh = rv_ref[:, j * RET_DV:(j + 1) * RET_DV]
        decay = jnp.where(same & (diff >= 0), jnp.exp(lg * jnp.maximum(diff, 0.0)), 0.0)
        scores = lax.dot_general(qm, kp, NT_DIMS, preferred_element_type=F32) * decay
        intra = jnp.dot(scores.astype(BF16), vh, preferred_element_type=F32)
        q_exp = jnp.where(q_sel, jnp.tile(qm, (1, nb)), jnp.zeros((), BF16))
        cross = jnp.dot(q_exp, s_old_b, preferred_element_type=F32) * jnp.exp(lg * (pos + 1.0))
        kd = kp.astype(F32) * jnp.exp(lg * (n_new - 1.0 - pos))
        kd_exp = jnp.where(k_sel, jnp.tile(kd.T, (nb, 1)), 0.0).astype(BF16)
        keep = jnp.exp(lg * jnp.full((1, 1), float(n_new), F32))
        upd.append((keep, jnp.dot(kd_exp, vh, preferred_element_type=F32)))
        cols = slice(j * RET_DV, (j + 1) * RET_DV)
        r_ref[:, cols] = _ret_head_out(intra + cross, gret_ref[:, cols], srg_ref[:, cols])
    s_new = jnp.where(row_head == 0, upd[0][0] * s_old + upd[0][1], upd[1][0] * s_old + upd[1][1])
    s1_ref[...] = s_new.reshape(s1_ref.shape)


def _ret_sample(rq, rk, rv, srg, g_ret, state, n_new):
    rows = rq.shape[0]
    nb = rows // n_new
    pair = pl.BlockSpec((rows, LANES), lambda p: (0, p))
    wide = pl.BlockSpec((rows, 2 * RET_DV), lambda p: (0, p))
    st = pl.BlockSpec((nb, 2, RET_DK, RET_DV), lambda p: (0, p, 0, 0))
    return pl.pallas_call(
        functools.partial(_ret_sample_kernel, n_new=n_new),
        grid=(H_RET // 2,),
        in_specs=[pair, pair, wide, wide, pl.BlockSpec((1, 2 * RET_DV), lambda p: (0, p)), st],
        out_specs=[wide, st],
        out_shape=[jax.ShapeDtypeStruct((rows, W_RET), BF16),
                   jax.ShapeDtypeStruct(state.shape, F32)],
        compiler_params=_params(("parallel",)),
        name="ret_sample",
    )(rq, rk, rv, srg, g_ret, state)


def _suffix_sum_lanes(x):
    n = x.shape[1]
    lane = lax.broadcasted_iota(jnp.int32, x.shape, 1)
    incl = x
    sh = 1
    while sh < n:
        incl = incl + jnp.where(lane < n - sh, pltpu.roll(incl, n - sh, axis=1), 0.0)
        sh *= 2
    return incl - x


def _fox_sample_kernel(pt_ref, *refs, n_steps, **kw):
    _fox_sample_body(pl.program_id(0), pl.program_id(1), n_steps, pt_ref, *refs, **kw)


def _fox_sample_body(b, c, n_steps, pt_ref, q_ref, kn_ref, vn_ref, lfn_ref, sfg_ref, lfc_ref, *refs,
                     n_new, pages_per_step, page_groups):
    g = pages_per_step
    k_refs, v_refs = refs[:g], refs[g:2 * g]
    a_ref, qbd_ref, m_ref, l_ref, acc_ref, tail_ref = refs[2 * g:]
    first_page = (n_steps - 1 - c) * g
    nq = n_new * H_FOX
    row_tok = lax.broadcasted_iota(jnp.int32, (nq, 1), 0) // H_FOX
    head_of_row = lax.broadcasted_iota(jnp.int32, (H_FOX, W_FOX), 0)
    head_of_lane = lax.broadcasted_iota(jnp.int32, (H_FOX, W_FOX), 1) // HEAD_DIM

    def partial_softmax(s, vt):
        m = jnp.max(s, axis=-1, keepdims=True)
        p = jnp.exp2(s - m)
        return (m, jnp.sum(p, axis=-1, keepdims=True),
                lax.dot_general(p.astype(BF16), vt, NT_DIMS, preferred_element_type=F32))

    def merge(parts):
        m_new = m_ref[...]
        for m, _, _ in parts:
            m_new = jnp.maximum(m_new, m)
        alpha = jnp.exp2(m_ref[...] - m_new)
        l_new = alpha * l_ref[...]
        acc = alpha * acc_ref[...]
        for m, l, o in parts:
            w = jnp.exp2(m - m_new)
            l_new = l_new + w * l
            acc = acc + w * o
        m_ref[...] = m_new
        l_ref[...] = l_new
        acc_ref[...] = acc

    @pl.when(c == 0)
    def _():
        q = q_ref[0].astype(F32)
        for t in range(n_new):
            qbd_ref[t * H_FOX:(t + 1) * H_FOX, :] = jnp.where(
                head_of_lane == head_of_row, jnp.broadcast_to(q[t:t + 1, :], (H_FOX, W_FOX)), 0.0).astype(BF16)
        tail_ref[...] = jnp.zeros(tail_ref.shape, F32)
        m_ref[...] = jnp.full(m_ref.shape, -1e30, F32)
        l_ref[...] = jnp.zeros(l_ref.shape, F32)
        acc_ref[...] = jnp.zeros(acc_ref.shape, F32)
        lane = lax.broadcasted_iota(jnp.int32, (H_FOX, LANES), 1)
        pfx = lfn_ref[...]
        sh = 1
        while sh < n_new:
            pfx = pfx + jnp.where(lane % n_new >= sh, pltpu.roll(pfx, sh, axis=1), 0.0)
            sh *= 2
        col = lax.broadcasted_iota(jnp.int32, (nq, LANES), 1)
        valid = (col // n_new == b) & (col % n_new <= row_tok)
        s = jnp.dot(qbd_ref[...], kn_ref[...], preferred_element_type=F32)
        merge([partial_softmax(jnp.where(valid, s - jnp.tile(pfx, (n_new, 1)) * LOG2E, -jnp.inf), vn_ref[...])])

    lf = jnp.concatenate([lfc_ref[pt_ref[b, first_page + slot]] for slot in range(g)], axis=1)
    key_bias = jnp.tile((tail_ref[:, 0:1] + _suffix_sum_lanes(lf)) * LOG2E, (n_new, 1))
    tail_ref[...] = tail_ref[...] + jnp.sum(lf, axis=-1, keepdims=True)
    per = g // page_groups
    scores = []
    for grp in range(page_groups):
        pages = slice(grp * per, (grp + 1) * per)
        kc = jnp.concatenate([r[0].astype(BF16) for r in k_refs[pages]], axis=1)
        s = jnp.dot(qbd_ref[...], kc, preferred_element_type=F32)
        scores.append(s + key_bias[:, grp * per * PAGE:(grp + 1) * per * PAGE])
    parts = []
    for grp, s in enumerate(scores):
        vc = jnp.concatenate([r[0].astype(BF16) for r in v_refs[grp * per:(grp + 1) * per]], axis=1)
        parts.append(partial_softmax(s, vc))
    merge(parts)

    o = acc_ref[...] / l_ref[...]
    sfg = sfg_ref[0].astype(F32)
    for t in range(n_new):
        ot = jnp.where(head_of_lane == head_of_row, o[t * H_FOX:(t + 1) * H_FOX, :], 0.0)
        a_ref[0, t:t + 1, :] = jnp.sum(ot, axis=0, keepdims=True) * sfg[t:t + 1, :]


def _fox_sample(page_table, q, kt_new, vt_new, lft_new, sfg, cache_kt, cache_vt, cache_lft):
    bd, n_new, _ = q.shape
    g, n_steps, kw = _fox_sample_plan(page_table, n_new)
    in_specs, out_spec, scratch = _fox_sample_specs(n_new, g, n_steps, (kt_new, vt_new, lft_new, cache_lft),
                                                    lambda b, c, pt: (b, c, pt))
    grid_spec = pltpu.PrefetchScalarGridSpec(
        num_scalar_prefetch=1, grid=(bd, n_steps), in_specs=in_specs, out_specs=out_spec, scratch_shapes=scratch)
    return pl.pallas_call(
        functools.partial(_fox_sample_kernel, n_steps=n_steps, **kw),
        grid_spec=grid_spec,
        out_shape=jax.ShapeDtypeStruct((bd, n_new, W_FOX), F32),
        compiler_params=_params(("parallel", "arbitrary")),
        name="fox_sample",
    )(page_table, q, kt_new, vt_new, lft_new, sfg, cache_lft, *([cache_kt] * g), *([cache_vt] * g))


def _fox_sample_plan(page_table, n_new):
    n_pages = page_table.shape[1]
    g = next(c for c in (16, 8, 4, 2, 1) if n_pages % c == 0)
    page_groups = 2 if g >= 2 else 1
    return g, n_pages // g, dict(n_new=n_new, pages_per_step=g, page_groups=page_groups)


def _fox_sample_specs(n_new, g, n_steps, resident, ids):
    nq = n_new * H_FOX

    def spec(shape, f, **kw):
        return pl.BlockSpec(shape, lambda *a: f(*ids(*a)), **kw)

    def tok(n):
        return spec((1, n_new, n), lambda b, c, pt: (b, 0, 0))

    def whole(arr):
        nd = arr.ndim
        return spec(arr.shape, lambda b, c, pt: (0,) * nd, pipeline_mode=pl.Buffered(1))

    def page_spec(slot):
        return spec((1, W_FOX, PAGE), lambda b, c, pt: (pt[b, (n_steps - 1 - c) * g + slot], 0, 0))

    kt_new, vt_new, lft_new, cache_lft = resident
    in_specs = [tok(W_FOX), whole(kt_new), whole(vt_new), whole(lft_new), tok(W_FOX), whole(cache_lft)]
    in_specs += [page_spec(s) for s in range(g)] + [page_spec(s) for s in range(g)]
    scratch = [pltpu.VMEM((nq, W_FOX), BF16), pltpu.VMEM((nq, 1), F32), pltpu.VMEM((nq, 1), F32),
               pltpu.VMEM((nq, W_FOX), F32), pltpu.VMEM((H_FOX, LANES), F32)]
    return in_specs, tok(W_FOX), scratch


def _fox_fused_kernel(pt_ref, *refs, n_prompt_in, n_sample_in, n_prompt_scratch, n_steps, sample_kw):
    refs = list(refs)
    p_in, refs = refs[:n_prompt_in], refs[n_prompt_in:]
    s_in, refs = refs[:n_sample_in], refs[n_sample_in:]
    (a_p, a_d), refs = refs[:2], refs[2:]
    p_scr, s_scr = refs[:n_prompt_scratch], refs[n_prompt_scratch:]
    step = (pl.program_id(0) * pl.num_programs(1) + pl.program_id(1)) * pl.num_programs(2) + pl.program_id(2)
    sample = functools.partial(_fox_sample_body, step // n_steps, step % n_steps, n_steps, pt_ref,
                               *s_in, a_d, *s_scr, **sample_kw)
    _fox_prompt_body(pl.program_id(2), *p_in, a_p, *p_scr, blk=FOX_BLK, side_work=sample)


def _fox_fused(qt, kn, vtb, ccol, sfg, page_table, q, kt_new, vt_new, lft_new, sfg_d, cache_kt, cache_vt, cache_lft):
    b, s, _ = kn.shape
    bd, n_new, _ = q.shape
    ng, nqb = H_FOX // HEADS_PER_GROUP, s // FOX_BLK
    g, n_steps, kw = _fox_sample_plan(page_table, n_new)
    assert b * ng * nqb == bd * n_steps
    p_in, p_out, p_scr = _fox_prompt_specs(s, lambda bi, gi, i, pt: (bi, gi, i))

    def sample_ids(bi, gi, i, pt):
        step = (bi * ng + gi) * nqb + i
        return step // n_steps, step % n_steps, pt

    s_in, s_out, s_scr = _fox_sample_specs(n_new, g, n_steps, (kt_new, vt_new, lft_new, cache_lft), sample_ids)
    grid_spec = pltpu.PrefetchScalarGridSpec(
        num_scalar_prefetch=1, grid=(b, ng, nqb), in_specs=p_in + s_in, out_specs=[p_out, s_out],
        scratch_shapes=p_scr + s_scr)
    return pl.pallas_call(
        functools.partial(_fox_fused_kernel, n_prompt_in=len(p_in), n_sample_in=len(s_in),
                          n_prompt_scratch=len(p_scr), n_steps=n_steps, sample_kw=kw),
        grid_spec=grid_spec,
        out_shape=[jax.ShapeDtypeStruct((b, s, W_FOX), BF16), jax.ShapeDtypeStruct((bd, n_new, W_FOX), F32)],
        compiler_params=_params(("arbitrary", "arbitrary", "arbitrary")),
        name="fox_fused",
    )(page_table, qt, kn, vtb, ccol, sfg, q, kt_new, vt_new, lft_new, sfg_d, cache_lft,
      *([cache_kt] * g), *([cache_vt] * g))


def _tail_kernel(a_ref, r_ref, x_ref, woe_ref, gno_ref, wio_ref, gv_ref, ws_ref, bs_ref, woo_ref,
                 *out_refs, n_new):
    rows = x_ref.shape[0]
    x1 = (x_ref[...]
          + jnp.dot(a_ref[...].astype(BF16), woe_ref[:W_FOX, :], preferred_element_type=F32)
          + jnp.dot(r_ref[...], woe_ref[W_FOX:, :], preferred_element_type=F32))
    h = _rms(x1, gno_ref[...]).astype(BF16)
    v = jax.nn.gelu(jnp.dot(h, wio_ref[:, GM_WIDTH:2 * GM_WIDTH], preferred_element_type=F32))
    vn = _layer_norm(v, gv_ref[...])
    if n_new is None:
        y_ref, = out_refs
        tril = (lax.broadcasted_iota(jnp.int32, (CHUNK, CHUNK), 0)
                >= lax.broadcasted_iota(jnp.int32, (CHUNK, CHUNK), 1))
        vb = vn.astype(BF16)

        def spatial(g):
            wm = jnp.where(tril, ws_ref[g], 0.0).astype(BF16)
            bias = bs_ref[:, g:g + 1]
            return jnp.concatenate(
                [jnp.dot(wm, vb[c * CHUNK:(c + 1) * CHUNK, g * GM_GC:(g + 1) * GM_GC],
                         preferred_element_type=F32) + bias for c in range(rows // CHUNK)], axis=0)
    else:
        y_ref, vn_ref = out_refs
        vn_ref[...] = vn
        tok = lax.broadcasted_iota(jnp.int32, (rows, 1), 0) % n_new

        def spatial(g):
            vg = vn[:, g * GM_GC:(g + 1) * GM_GC]
            acc = jnp.zeros((rows, 1), F32)
            for t in range(n_new):
                acc = jnp.where(tok == t, bs_ref[g, t], acc)
            acc = jnp.broadcast_to(acc, (rows, GM_GC))
            for d in range(n_new):
                coef = jnp.zeros((rows, 1), F32)
                for t in range(d, n_new):
                    coef = jnp.where(tok == t, ws_ref[g, t * n_new + (t - d)], coef)
                src = vg if d == 0 else pltpu.roll(vg, d, axis=0)
                acc = acc + coef * src
            return acc

    u = jax.nn.gelu(jnp.dot(h, wio_ref[:, :GM_WIDTH], preferred_element_type=F32))
    gate = jnp.dot(h, wio_ref[:, 2 * GM_WIDTH:], preferred_element_type=F32)
    sv = jnp.concatenate([spatial(g) for g in range(GM_GROUPS)], axis=1)
    z = (u * sv * jax.nn.silu(gate)).astype(BF16)
    y_ref[...] = x1 + jnp.dot(z, woo_ref[...], preferred_element_type=F32)


def _tail(a, r, x2d, woe, gno, wio, gv, ws, bs, woo, tm, n_new):
    m = x2d.shape[0]
    row = lambda n: pl.BlockSpec((tm, n), lambda i: (i, 0))
    if n_new is None:
        mix_specs = [_const_spec(ws.shape), _const_spec(bs.shape)]
        out_specs = [row(D_MODEL)]
        out_shape = [jax.ShapeDtypeStruct((m, D_MODEL), F32)]
    else:
        smem = pl.BlockSpec(memory_space=pltpu.SMEM)
        mix_specs = [smem, smem]
        out_specs = [row(D_MODEL), row(GM_WIDTH)]
        out_shape = [jax.ShapeDtypeStruct((m, D_MODEL), F32), jax.ShapeDtypeStruct((m, GM_WIDTH), F32)]
    return pl.pallas_call(
        functools.partial(_tail_kernel, n_new=n_new),
        grid=(m // tm,),
        in_specs=[row(W_FOX), row(W_RET), row(D_MODEL), _const_spec(woe.shape), _const_spec((1, D_MODEL)),
                  _const_spec(wio.shape), _const_spec((1, GM_WIDTH))] + mix_specs + [_const_spec(woo.shape)],
        out_specs=out_specs, out_shape=out_shape,
        compiler_params=_params(("parallel",)),
        name="tail_prompt" if n_new is None else "tail_sample",
    )(a, r, x2d, woe, gno, wio, gv, ws, bs, woo)


def _rope_tables(pos):
    half = RET_DK // 2
    inv = ROPE_BASE ** (-jnp.arange(half, dtype=F32) / half)
    ang = pos.astype(F32)[:, None] * inv[None, :]
    cos, sin = jnp.cos(ang), jnp.sin(ang)
    return jnp.tile(jnp.concatenate([cos, cos], axis=-1), (1, 2)), jnp.tile(jnp.concatenate([-sin, sin], axis=-1), (1, 2))


def _row_tile(m):
    return next(t for t in (512, 256, 128) if m % t == 0)


def kernel(x_prompt, x_sample, cache_k, cache_v, cache_logf, state_ret, page_table, norm_even, w_in_even,
           b_forget, qnorm, knorm, ret_norm, w_out_even, norm_odd, w_in_odd, vnorm_odd, w_spatial, b_spatial,
           w_out_odd):
    bp, s, _ = x_prompt.shape
    bd, n_new, _ = x_sample.shape
    n_pages = page_table.shape[1]
    past = n_pages * PAGE
    assert s % 256 == 0 and (bd * n_new) % CHUNK == 0 and w_in_even.shape[0] == 1 and w_in_odd.shape[0] == 1

    wi = w_in_even[0]
    sizes = (W_FOX, W_FOX, W_FOX, H_FOX, W_FOX, W_RQK, W_RQK, W_RET, W_RET)
    offs = np.concatenate([[0], np.cumsum(sizes)])
    fq, fk, fv, ff, fg, rq, rk, rv, rg = [wi[:, offs[i]:offs[i + 1]] for i in range(9)]
    w_even = jnp.concatenate([fg, rq, rk, rv, rg], axis=1).astype(BF16)
    w_even_t = jnp.concatenate([fq.T, fk.T, fv.T, ff.T, jnp.zeros((EVEN_ROWS_T - R_FF - H_FOX, D_MODEL), F32)],
                               axis=0).astype(BF16)
    b_f = b_forget[0][:, None]
    g_q = qnorm[0][:, None]
    g_k = knorm[0][:, None]
    g_ret = ret_norm[0].reshape(1, W_RET)
    woe = w_out_even[0].astype(BF16)
    wio = w_in_odd[0].astype(BF16)
    woo = w_out_odd[0].astype(BF16)
    g_ne, g_no, g_v = norm_even[0][None, :], norm_odd[0][None, :], vnorm_odd[0][None, :]

    cos_p, sin_p = _rope_tables(jnp.arange(s))
    xp2 = x_prompt.reshape(bp * s, D_MODEL)
    tm = _row_tile(s)
    sfg, rqp, rkp, rvp, srg, kn, qt, kt32, _, vt32, vtb, lft = _even_proj(
        xp2, bp, g_ne, w_even, w_even_t, b_f, g_q, g_k, cos_p, sin_p, tm)
    rows = bd * n_new
    cos_d, sin_d = _rope_tables(past + jnp.arange(n_new))
    xd2 = x_sample.reshape(rows, D_MODEL)
    sfgd, rqd, rkd, rvd, srgd, _, qtd, kt32d, ktbd, vt32d, vtbd, lftd = _even_proj(
        xd2, 1, g_ne, w_even, w_even_t, b_f, g_q, g_k, jnp.tile(cos_d, (bd, 1)), jnp.tile(sin_d, (bd, 1)), rows)

    ng = H_FOX // HEADS_PER_GROUP
    c_col4 = jnp.transpose(_cumsum_lanes(lft).reshape(bp, ng, HEADS_PER_GROUP, s), (0, 1, 3, 2))
    r3 = lambda t: t.reshape(bp, s, t.shape[-1])
    d3 = lambda t: t.reshape(bd, n_new, t.shape[-1])
    n_phys = cache_k.shape[1]
    pages_t = lambda cache: jnp.transpose(cache[0], (0, 2, 3, 1)).reshape(n_phys, W_FOX, PAGE)
    prompt_args = (qt, r3(kn), vtb, c_col4, r3(sfg))
    sample_args = (page_table, d3(jnp.transpose(qtd[0], (1, 0))), ktbd[0], vtbd[0], lftd[0], d3(sfgd),
                   pages_t(cache_k), pages_t(cache_v), jnp.transpose(cache_logf[0], (0, 2, 1)))
    if bp * ng * (s // FOX_BLK) == bd * _fox_sample_plan(page_table, n_new)[1]:
        a_p, a_d = _fox_fused(*prompt_args, *sample_args)
    else:
        a_p, a_d = _fox_prompt(*prompt_args), _fox_sample(*sample_args)
    r_p, st_p = _ret_prompt(r3(rqp), r3(rkp), r3(rvp), r3(srg), g_ret)
    r_d, st_d = _ret_sample(rqd, rkd, rvd, srgd, g_ret, state_ret[0], n_new)

    y_p, = _tail(a_p.reshape(bp * s, W_FOX), r_p.reshape(bp * s, W_RET), xp2, woe, g_no, wio, g_v,
                 w_spatial[0], jnp.transpose(b_spatial[0]), woo, 512 if s % 512 == 0 else tm, None)
    ws_new = w_spatial[0][:, :n_new, :n_new].reshape(GM_GROUPS, n_new * n_new)
    y_d, gv_d = _tail(a_d.reshape(rows, W_FOX), r_d, xd2, woe, g_no, wio, g_v,
                      ws_new, b_spatial[0][:, :n_new], woo, rows, n_new)

    heads_last = lambda t, n, length: jnp.transpose(t.reshape(n, H_FOX, HEAD_DIM, length), (0, 3, 1, 2))[None]
    tokens_first = lambda t, n: jnp.transpose(t[0], (1, 0)).reshape(1, bd, n_new, *n)
    return (y_p.reshape(bp, s, D_MODEL), y_d.reshape(bd, n_new, D_MODEL),
            heads_last(kt32, bp, s), heads_last(vt32, bp, s),
            jnp.transpose(lft, (0, 2, 1))[None], st_p.reshape(1, bp, H_RET, RET_DK, RET_DV),
            tokens_first(kt32d, (H_FOX, HEAD_DIM)), tokens_first(vt32d, (H_FOX, HEAD_DIM)),
            tokens_first(lftd, (H_FOX,)), st_d[None], gv_d.reshape(1, bd, n_new, GM_WIDTH))
```

```python
import functools

import numpy as np
import jax
import jax.numpy as jnp
from jax import lax
from jax.experimental import pallas as pl
from jax.experimental.pallas import tpu as pltpu

D_MODEL = 1024
HEAD_DIM = 64
H_FOX = 8
W_FOX = H_FOX * HEAD_DIM
H_RET = 8
RET_DK = 64
RET_DV = 128
W_RET = H_RET * RET_DV
W_RQK = H_RET * RET_DK
ROPE_BASE = 10000.0
GM_WIDTH = 2 * D_MODEL
GM_GROUPS = 8
GM_GC = GM_WIDTH // GM_GROUPS
CHUNK = 128
PAGE = 128
EPS = 1e-6
F32 = jnp.float32
BF16 = jnp.bfloat16

LANES = 128
VMEM_LIMIT = 56 * 1024 * 1024

C_FG, C_RQ, C_RK, C_RV, C_RG, EVEN_COLS = 0, 512, 1024, 1536, 2560, 3584
R_FQ, R_FK, R_FV, R_FF, EVEN_ROWS_T = 0, 512, 1024, 1536, 1552

LOG_GAMMA = [float(np.log(np.float32(1.0) - np.float32(2.0) ** np.float32(-5.0 - h))) for h in range(H_RET)]

LOG2E = float(np.log2(np.e))
NT_DIMS = (((1,), (1,)), ((), ()))
TN_DIMS = (((0,), (0,)), ((), ()))


def _params(sem):
    return pltpu.CompilerParams(dimension_semantics=sem, vmem_limit_bytes=VMEM_LIMIT)


def _const_spec(shape):
    nd = len(shape)
    return pl.BlockSpec(shape, lambda *_: (0,) * nd, pipeline_mode=pl.Buffered(1))


def _rms(x, g):
    ms = jnp.mean(x * x, axis=-1, keepdims=True)
    return x * lax.rsqrt(ms + EPS) * g


def _layer_norm(x, g):
    xc = x - jnp.mean(x, axis=-1, keepdims=True)
    return xc * lax.rsqrt(jnp.mean(xc * xc, axis=-1, keepdims=True) + EPS) * g


def _rope(x, cos, sin_signed):
    rows = x.shape[0]
    first_half = (lax.broadcasted_iota(jnp.int32, (rows, LANES), 1) % RET_DK) < RET_DK // 2
    outs = []
    for c in range(x.shape[1] // LANES):
        xc = x[:, c * LANES:(c + 1) * LANES]
        nxt = pltpu.roll(xc, LANES - RET_DK // 2, axis=1)
        prv = pltpu.roll(xc, RET_DK // 2, axis=1)
        outs.append(xc * cos + jnp.where(first_half, nxt, prv) * sin_signed)
    return jnp.concatenate(outs, axis=-1)


def _log_sigmoid(x):
    return jnp.minimum(x, 0.0) - jnp.log1p(jnp.exp(-jnp.abs(x)))


def _head_rms_t(x, g):
    tokens = x.shape[1]
    x = x.reshape(H_FOX, HEAD_DIM, tokens)
    x = x * lax.rsqrt(jnp.mean(x * x, axis=1, keepdims=True) + EPS) * g[None]
    return x.reshape(W_FOX, tokens)


def _even_proj_kernel(x_ref, g_ref, w_ref, wt_ref, bf_ref, gq_ref, gk_ref, cos_ref, sin_ref,
                      sfg_ref, rq_ref, rk_ref, rv_ref, srg_ref, kn_ref,
                      qt_ref, kt32_ref, ktb_ref, vt32_ref, vtb_ref, lft_ref):
    h = _rms(x_ref[...], g_ref[...]).astype(BF16)

    def proj(lo, hi):
        return jnp.dot(h, w_ref[:, lo:hi], preferred_element_type=F32)

    sfg_ref[...] = jax.nn.silu(proj(C_FG, C_RQ)).astype(BF16)
    cos = cos_ref[...]
    sin = sin_ref[...]
    rq_ref[...] = _rope(proj(C_RQ, C_RK), cos, sin).astype(BF16)
    rk_ref[...] = (_rope(proj(C_RK, C_RV), cos, sin) * (RET_DK ** -0.5)).astype(BF16)
    rv_ref[...] = proj(C_RV, C_RG).astype(BF16)
    srg_ref[...] = jax.nn.silu(proj(C_RG, EVEN_COLS)).astype(BF16)

    t = lax.dot_general(wt_ref[...], h, NT_DIMS, preferred_element_type=F32)
    qt_ref[0] = (_head_rms_t(t[R_FQ:R_FK], gq_ref[...]) * (HEAD_DIM ** -0.5 * LOG2E)).astype(BF16)
    kt = _head_rms_t(t[R_FK:R_FV], gk_ref[...])
    kt32_ref[0] = kt
    ktb_ref[0] = kt.astype(BF16)
    kn_ref[...] = kt.T.astype(BF16)
    vt = t[R_FV:R_FF]
    vt32_ref[0] = vt
    vtb_ref[0] = vt.astype(BF16)
    lft_ref[0] = _log_sigmoid(t[R_FF:R_FF + H_FOX] + bf_ref[...])


def _even_proj(x2d, n_seq, g_norm, w, wt, b_f, g_q, g_k, cos, sin, tm):
    m = x2d.shape[0]
    s = m // n_seq
    per_seq = s // tm
    row = lambda n: pl.BlockSpec((tm, n), lambda i: (i, 0))
    tab = pl.BlockSpec((tm, LANES), lambda i: (i % per_seq, 0))
    col = lambda n: pl.BlockSpec((1, n, tm), lambda i: (i // per_seq, 0, i % per_seq))
    outs = [(W_FOX, BF16), (W_RQK, BF16), (W_RQK, BF16), (W_RET, BF16), (W_RET, BF16), (W_FOX, BF16)]
    outs_t = [(W_FOX, BF16), (W_FOX, F32), (W_FOX, BF16), (W_FOX, F32), (W_FOX, BF16), (H_FOX, F32)]
    return pl.pallas_call(
        _even_proj_kernel,
        grid=(m // tm,),
        in_specs=[row(D_MODEL), _const_spec((1, D_MODEL)), _const_spec((D_MODEL, EVEN_COLS)),
                  _const_spec((EVEN_ROWS_T, D_MODEL)), _const_spec((H_FOX, 1)), _const_spec((HEAD_DIM, 1)),
                  _const_spec((HEAD_DIM, 1)), tab, tab],
        out_specs=[row(n) for n, _ in outs] + [col(n) for n, _ in outs_t],
        out_shape=([jax.ShapeDtypeStruct((m, n), dt) for n, dt in outs]
                   + [jax.ShapeDtypeStruct((n_seq, n, s), dt) for n, dt in outs_t]),
        compiler_params=_params(("parallel",)),
        name="even_proj",
    )(x2d, g_norm, w, wt, b_f, g_q, g_k, cos, sin)


def _cumsum_kernel(x_ref, o_ref):
    x = x_ref[0]
    n = x.shape[1]
    lane = lax.broadcasted_iota(jnp.int32, x.shape, 1)
    sh = 1
    while sh < n:
        x = x + jnp.where(lane >= sh, pltpu.roll(x, sh, axis=1), 0.0)
        sh *= 2
    o_ref[0] = x


def _cumsum_lanes(x):
    b, h, s = x.shape
    spec = pl.BlockSpec((1, h, s), lambda i: (i, 0, 0))
    return pl.pallas_call(
        _cumsum_kernel, grid=(b,), in_specs=[spec], out_specs=spec,
        out_shape=jax.ShapeDtypeStruct(x.shape, F32),
        compiler_params=_params(("parallel",)), name="logf_cumsum",
    )(x)


HEADS_PER_GROUP = 4
GROUP_W = HEADS_PER_GROUP * HEAD_DIM


def _fox_prompt_kernel(*refs, blk):
    _fox_prompt_body(pl.program_id(2), *refs, blk=blk)


def _fox_prompt_body(i, qt_ref, k_ref, v_ref, ccol_ref, sfg_ref, a_ref,
                     qm_ref, vp_ref, cb_ref, m_ref, acc_ref, s_ref, p_ref, *, blk, side_work=None):
    s_len = k_ref.shape[1]

    @pl.when(i == 0)
    def _():
        ones_row = jnp.where(lax.broadcasted_iota(jnp.int32, (HEAD_DIM, s_len), 0) == 0, 1.0, 0.0).astype(BF16)
        ccol = ccol_ref[0, 0] * (-LOG2E)
        for hh in range(HEADS_PER_GROUP):
            vp_ref[hh, :HEAD_DIM, :] = v_ref[0, hh * HEAD_DIM:(hh + 1) * HEAD_DIM, :]
            vp_ref[hh, HEAD_DIM:, :] = ones_row
            cb_ref[hh] = jnp.broadcast_to(ccol[:, hh:hh + 1], (s_len, LANES))

    if side_work is not None:
        side_work()
    qt = qt_ref[0]
    row_head = lax.broadcasted_iota(jnp.int32, (GROUP_W, blk), 0) // HEAD_DIM
    for hh in range(HEADS_PER_GROUP):
        qm_ref[hh] = jnp.where(row_head == hh, qt, jnp.zeros_like(qt))
    m_ref[...] = jnp.full(m_ref.shape, -1e30, F32)
    acc_ref[...] = jnp.zeros(acc_ref.shape, F32)
    strips = [(hh, slice(c * LANES, (c + 1) * LANES), c) for hh in range(HEADS_PER_GROUP) for c in range(blk // LANES)]

    def raw_scores(j, slot):
        off = pl.multiple_of(j * blk, blk)
        kb = k_ref[0, pl.ds(off, blk), :]
        for hh in range(HEADS_PER_GROUP):
            bias = cb_ref[hh, pl.ds(off, blk), :]
            s_ref[slot, hh] = (jnp.dot(kb, qm_ref[hh], preferred_element_type=F32)
                               + jnp.concatenate([bias] * (blk // LANES), axis=1))

    def consume(j, slot, masked):
        off = pl.multiple_of(j * blk, blk)

        def scores(hh, cols, c):
            s = s_ref[slot, hh, :, cols]
            if masked:
                causal = (lax.broadcasted_iota(jnp.int32, (blk, LANES), 0)
                          <= lax.broadcasted_iota(jnp.int32, (blk, LANES), 1) + c * LANES)
                s = jnp.where(causal, s, -jnp.inf)
            return s

        for hh, cols, c in strips:
            s = scores(hh, cols, c)
            m_old = m_ref[hh, :, cols]
            m_new = jnp.maximum(m_old, jnp.max(s, axis=0, keepdims=True))
            p_ref[hh, :, cols] = jnp.exp2(s - m_new).astype(BF16)
            acc_ref[hh, :, cols] = jnp.exp2(m_old - m_new) * acc_ref[hh, :, cols]
            m_ref[hh, :, cols] = m_new
        for hh in range(HEADS_PER_GROUP):
            acc_ref[hh] += jnp.dot(vp_ref[hh, :, pl.ds(off, blk)], p_ref[hh],
                                   preferred_element_type=F32)

    raw_scores(0, 0)

    def pair(t, carry):
        j = 2 * t
        raw_scores(j + 1, 1)
        consume(j, 0, False)
        raw_scores(j + 2, 0)
        consume(j + 1, 1, False)
        return carry

    lax.fori_loop(0, i // 2, pair, 0)

    @pl.when(i % 2 == 1)
    def _():
        raw_scores(i, 1)
        consume(i - 1, 0, False)
        consume(i, 1, True)

    @pl.when(i % 2 == 0)
    def _():
        consume(i, 0, True)

    ot = jnp.concatenate([acc_ref[hh, :HEAD_DIM, :] / acc_ref[hh, HEAD_DIM:HEAD_DIM + 1, :]
                          for hh in range(HEADS_PER_GROUP)], axis=0)
    a_ref[0] = (ot.T * sfg_ref[0].astype(F32)).astype(BF16)


FOX_BLK = 256


def _fox_prompt_specs(s, ids):
    blk = FOX_BLK

    def spec(shape, f):
        return pl.BlockSpec(shape, lambda *a: f(*ids(*a)))

    rows = spec((1, blk, GROUP_W), lambda bi, g, i: (bi, i, g))
    in_specs = [spec((1, GROUP_W, blk), lambda bi, g, i: (bi, g, i)),
                spec((1, s, GROUP_W), lambda bi, g, i: (bi, 0, g)),
                spec((1, GROUP_W, s), lambda bi, g, i: (bi, g, 0)),
                spec((1, 1, s, HEADS_PER_GROUP), lambda bi, g, i: (bi, g, 0, 0)),
                rows]
    scratch = [pltpu.VMEM((HEADS_PER_GROUP, GROUP_W, blk), BF16),
               pltpu.VMEM((HEADS_PER_GROUP, LANES, s), BF16),
               pltpu.VMEM((HEADS_PER_GROUP, s, LANES), F32),
               pltpu.VMEM((HEADS_PER_GROUP, 1, blk), F32),
               pltpu.VMEM((HEADS_PER_GROUP, LANES, blk), F32),
               pltpu.VMEM((2, HEADS_PER_GROUP, blk, blk), F32),
               pltpu.VMEM((HEADS_PER_GROUP, blk, blk), BF16)]
    return in_specs, rows, scratch


def _fox_prompt(qt, kn, vtb, ccol, sfg):
    b, s, _ = kn.shape
    in_specs, out_spec, scratch = _fox_prompt_specs(s, lambda bi, g, i: (bi, g, i))
    return pl.pallas_call(
        functools.partial(_fox_prompt_kernel, blk=FOX_BLK),
        grid=(b, H_FOX // HEADS_PER_GROUP, s // FOX_BLK),
        in_specs=in_specs, out_specs=out_spec,
        out_shape=jax.ShapeDtypeStruct((b, s, W_FOX), BF16),
        scratch_shapes=scratch,
        compiler_params=_params(("parallel", "parallel", "arbitrary")),
        name="fox_prompt",
    )(qt, kn, vtb, ccol, sfg)


def _ret_head_out(o, g, srg):
    return (_layer_norm(o, g) * srg.astype(F32)).astype(BF16)


def _ret_prompt_kernel(rq_ref, rk_ref, rv_ref, srg_ref, gret_ref, r_ref, st_ref, dec_ref, kdec_ref, *, chunks):
    L = CHUNK
    c = pl.program_id(1)

    @pl.when(c == 0)
    def _():
        st_ref[...] = jnp.zeros(st_ref.shape, F32)
        diff = (lax.broadcasted_iota(jnp.int32, (L, L), 0)
                - lax.broadcasted_iota(jnp.int32, (L, L), 1)).astype(F32)
        for h in range(H_RET):
            dec_ref[h] = jnp.where(diff >= 0, jnp.exp(LOG_GAMMA[h] * jnp.maximum(diff, 0.0)), 0.0)
        left = (L - 1.0) - lax.broadcasted_iota(jnp.int32, (L, LANES), 0).astype(F32)
        first = lax.broadcasted_iota(jnp.int32, (L, LANES), 1) < RET_DK
        for pp in range(H_RET // 2):
            kdec_ref[pp] = jnp.exp(jnp.where(first, LOG_GAMMA[2 * pp], LOG_GAMMA[2 * pp + 1]) * left)

    pos = lax.broadcasted_iota(jnp.int32, (L, 1), 0).astype(F32)
    lane_half = lax.broadcasted_iota(jnp.int32, (L, LANES), 1) // RET_DK
    for ci in range(chunks):
        rows_c = slice(ci * L, (ci + 1) * L)
        first_stage = []
        for pp in range(H_RET // 2):
            qp = rq_ref[0, rows_c, pp * LANES:(pp + 1) * LANES]
            kp = rk_ref[0, rows_c, pp * LANES:(pp + 1) * LANES]
            s_pair = st_ref[0, pp]
            s_pair_b = s_pair.astype(BF16)
            kd = (kp.astype(F32) * kdec_ref[pp]).astype(BF16)
            for j in range(2):
                h = 2 * pp + j
                qm = jnp.where(lane_half == j, qp, jnp.zeros_like(qp))
                vh = rv_ref[0, rows_c, h * RET_DV:(h + 1) * RET_DV]
                scores = lax.dot_general(qm, kp, NT_DIMS, preferred_element_type=F32)
                cross = jnp.dot(qm, s_pair_b, preferred_element_type=F32)
                upd = lax.dot_general(kd, vh, TN_DIMS, preferred_element_type=F32)
                first_stage.append((scores, cross, upd, vh, s_pair))
        outs = []
        for h, (scores, cross, upd, vh, s_pair) in enumerate(first_stage):
            lg = LOG_GAMMA[h]
            intra = jnp.dot((scores * dec_ref[h]).astype(BF16), vh, preferred_element_type=F32)
            outs.append(intra + cross * jnp.exp(lg * (pos + 1.0)))
            rows = slice((h % 2) * RET_DK, (h % 2 + 1) * RET_DK)
            st_ref[0, h // 2, rows, :] = float(np.exp(lg * L)) * s_pair[rows] + upd[rows]
        for h, o in enumerate(outs):
            cols = slice(h * RET_DV, (h + 1) * RET_DV)
            r_ref[0, rows_c, cols] = _ret_head_out(o, gret_ref[:, cols], srg_ref[0, rows_c, cols])


def _ret_prompt(rq, rk, rv, srg, g_ret):
    b, s, _ = rq.shape
    chunks = next(n for n in (4, 2, 1) if s % (n * CHUNK) == 0)
    rows = chunks * CHUNK
    qk = pl.BlockSpec((1, rows, W_RQK), lambda bi, c: (bi, c, 0))
    wide = pl.BlockSpec((1, rows, W_RET), lambda bi, c: (bi, c, 0))
    st = pl.BlockSpec((1, H_RET // 2, 2 * RET_DK, RET_DV), lambda bi, c: (bi, 0, 0, 0))
    return pl.pallas_call(
        functools.partial(_ret_prompt_kernel, chunks=chunks),
        grid=(b, s // rows),
        in_specs=[qk, qk, wide, wide, _const_spec((1, W_RET))],
        out_specs=[wide, st],
        out_shape=[jax.ShapeDtypeStruct((b, s, W_RET), BF16),
                   jax.ShapeDtypeStruct((b, H_RET // 2, 2 * RET_DK, RET_DV), F32)],
        scratch_shapes=[pltpu.VMEM((H_RET, CHUNK, CHUNK), F32), pltpu.VMEM((H_RET // 2, CHUNK, LANES), F32)],
        compiler_params=_params(("parallel", "arbitrary")),
        name="ret_prompt",
    )(rq, rk, rv, srg, g_ret)


def _ret_sample_kernel(rq_ref, rk_ref, rv_ref, srg_ref, gret_ref, s0_ref, r_ref, s1_ref, *, n_new):
    rows = rq_ref.shape[0]
    nb = rows // n_new
    two_dk = 2 * RET_DK
    ri = lax.broadcasted_iota(jnp.int32, (rows, rows), 0)
    ci = lax.broadcasted_iota(jnp.int32, (rows, rows), 1)
    same = (ri // n_new) == (ci // n_new)
    diff = ((ri % n_new) - (ci % n_new)).astype(F32)
    pos = (lax.broadcasted_iota(jnp.int32, (rows, 1), 0) % n_new).astype(F32)
    lane_half = lax.broadcasted_iota(jnp.int32, (rows, LANES), 1) // RET_DK
    q_sel = (lax.broadcasted_iota(jnp.int32, (rows, nb * two_dk), 1) // two_dk
             == lax.broadcasted_iota(jnp.int32, (rows, nb * two_dk), 0) // n_new)
    k_sel = (lax.broadcasted_iota(jnp.int32, (nb * two_dk, rows), 0) // two_dk
             == lax.broadcasted_iota(jnp.int32, (nb * two_dk, rows), 1) // n_new)
    row_head = (lax.broadcasted_iota(jnp.int32, (nb * two_dk, 1), 0) % two_dk) // RET_DK

    qp = rq_ref[...]
    kp = rk_ref[...]
    s_old = s0_ref[...].reshape(nb * two_dk, RET_DV)
    s_old_b = s_old.astype(BF16)
    upd = []
    for j in range(2):
        h = 2 * pl.program_id(0) + j
        lg = jnp.where(h == 0, LOG_GAMMA[0], 0.0)
        for hh in range(1, H_RET):
            lg = jnp.where(h == hh, LOG_GAMMA[hh], lg)
        qm = jnp.where(lane_half == j, qp, jnp.zeros_like(qp))
        vh = rv_ref[:, j * RET_DV:(j + 1) * RET_DV]
        decay = jnp.where(same & (diff >= 0), jnp.exp(lg * jnp.maximum(diff, 0.0)), 0.0)
        scores = lax.dot_general(qm, kp, NT_DIMS, preferred_element_type=F32) * decay
        intra = jnp.dot(scores.astype(BF16), vh, preferred_element_type=F32)
        q_exp = jnp.where(q_sel, jnp.tile(qm, (1, nb)), jnp.zeros((), BF16))
        cross = jnp.dot(q_exp, s_old_b, preferred_element_type=F32) * jnp.exp(lg * (pos + 1.0))
        kd = kp.astype(F32) * jnp.exp(lg * (n_new - 1.0 - pos))
        kd_exp = jnp.where(k_sel, jnp.tile(kd.T, (nb, 1)), 0.0).astype(BF16)
        keep = jnp.exp(lg * jnp.full((1, 1), float(n_new), F32))
        upd.append((keep, jnp.dot(kd_exp, vh, preferred_element_type=F32)))
        cols = slice(j * RET_DV, (j + 1) * RET_DV)
        r_ref[:, cols] = _ret_head_out(intra + cross, gret_ref[:, cols], srg_ref[:, cols])
    s_new = jnp.where(row_head == 0, upd[0][0] * s_old + upd[0][1], upd[1][0] * s_old + upd[1][1])
    s1_ref[...] = s_new.reshape(s1_ref.shape)


def _ret_sample(rq, rk, rv, srg, g_ret, state, n_new):
    rows = rq.shape[0]
    nb = rows // n_new
    pair = pl.BlockSpec((rows, LANES), lambda p: (0, p))
    wide = pl.BlockSpec((rows, 2 * RET_DV), lambda p: (0, p))
    st = pl.BlockSpec((nb, 2, RET_DK, RET_DV), lambda p: (0, p, 0, 0))
    return pl.pallas_call(
        functools.partial(_ret_sample_kernel, n_new=n_new),
        grid=(H_RET // 2,),
        in_specs=[pair, pair, wide, wide, pl.BlockSpec((1, 2 * RET_DV), lambda p: (0, p)), st],
        out_specs=[wide, st],
        out_shape=[jax.ShapeDtypeStruct((rows, W_RET), BF16),
                   jax.ShapeDtypeStruct(state.shape, F32)],
        compiler_params=_params(("parallel",)),
        name="ret_sample",
    )(rq, rk, rv, srg, g_ret, state)


def _suffix_sum_lanes(x):
    n = x.shape[1]
    lane = lax.broadcasted_iota(jnp.int32, x.shape, 1)
    incl = x
    sh = 1
    while sh < n:
        incl = incl + jnp.where(lane < n - sh, pltpu.roll(incl, n - sh, axis=1), 0.0)
        sh *= 2
    return incl - x


def _fox_sample_kernel(pt_ref, *refs, n_steps, **kw):
    _fox_sample_body(pl.program_id(0) * n_steps + pl.program_id(1), pl.num_programs(0) * n_steps, n_steps,
                     pt_ref, *refs, **kw)


def _fox_sample_body(step, total_steps, n_steps, pt_ref, q_ref, kn_ref, vn_ref, lfn_ref, sfg_ref, lfc_ref,
                     ck_hbm, cv_hbm, a_ref, qbd_ref, m_ref, l_ref, acc_ref, tail_ref, kbuf, vbuf, sem,
                     *, n_new, pages_per_step, page_groups):
    g = pages_per_step
    b, c = step // n_steps, step % n_steps

    def first_page_of(st):
        return (n_steps - 1 - st % n_steps) * g

    def page_copies(st, slot):
        row, first = st // n_steps, first_page_of(st)
        copies = []
        for p in range(g):
            page = pt_ref[row, first + p]
            copies.append(pltpu.make_async_copy(ck_hbm.at[page], kbuf.at[slot, p], sem.at[slot, 0]))
            copies.append(pltpu.make_async_copy(cv_hbm.at[page], vbuf.at[slot, p], sem.at[slot, 1]))
        return copies

    slot = step % 2
    nxt = jnp.minimum(step + 1, total_steps - 1)

    @pl.when(step == 0)
    def _():
        for cp in page_copies(step, 0):
            cp.start()

    for cp in page_copies(nxt, 1 - slot):
        cp.start()
    for cp in page_copies(step, slot):
        cp.wait()
    first_page = first_page_of(step)
    nq = n_new * H_FOX
    row_tok = lax.broadcasted_iota(jnp.int32, (nq, 1), 0) // H_FOX
    head_of_row = lax.broadcasted_iota(jnp.int32, (H_FOX, W_FOX), 0)
    head_of_lane = lax.broadcasted_iota(jnp.int32, (H_FOX, W_FOX), 1) // HEAD_DIM

    def partial_softmax(s, vt):
        m = jnp.max(s, axis=-1, keepdims=True)
        p = jnp.exp2(s - m)
        return (m, jnp.sum(p, axis=-1, keepdims=True),
                lax.dot_general(p.astype(BF16), vt, NT_DIMS, preferred_element_type=F32))

    def merge(parts):
        m_new = m_ref[...]
        for m, _, _ in parts:
            m_new = jnp.maximum(m_new, m)
        alpha = jnp.exp2(m_ref[...] - m_new)
        l_new = alpha * l_ref[...]
        acc = alpha * acc_ref[...]
        for m, l, o in parts:
            w = jnp.exp2(m - m_new)
            l_new = l_new + w * l
            acc = acc + w * o
        m_ref[...] = m_new
        l_ref[...] = l_new
        acc_ref[...] = acc

    @pl.when(c == 0)
    def _():
        q = q_ref[0].astype(F32)
        for t in range(n_new):
            qbd_ref[t * H_FOX:(t + 1) * H_FOX, :] = jnp.where(
                head_of_lane == head_of_row, jnp.broadcast_to(q[t:t + 1, :], (H_FOX, W_FOX)), 0.0).astype(BF16)
        tail_ref[...] = jnp.zeros(tail_ref.shape, F32)
        m_ref[...] = jnp.full(m_ref.shape, -1e30, F32)
        l_ref[...] = jnp.zeros(l_ref.shape, F32)
        acc_ref[...] = jnp.zeros(acc_ref.shape, F32)
        lane = lax.broadcasted_iota(jnp.int32, (H_FOX, LANES), 1)
        pfx = lfn_ref[...]
        sh = 1
        while sh < n_new:
            pfx = pfx + jnp.where(lane % n_new >= sh, pltpu.roll(pfx, sh, axis=1), 0.0)
            sh *= 2
        col = lax.broadcasted_iota(jnp.int32, (nq, LANES), 1)
        valid = (col // n_new == b) & (col % n_new <= row_tok)
        s = jnp.dot(qbd_ref[...], kn_ref[...], preferred_element_type=F32)
        merge([partial_softmax(jnp.where(valid, s - jnp.tile(pfx, (n_new, 1)) * LOG2E, -jnp.inf), vn_ref[...])])

    lf = jnp.concatenate([lfc_ref[pt_ref[b, first_page + slot]] for slot in range(g)], axis=1)
    key_bias = jnp.tile((tail_ref[:, 0:1] + _suffix_sum_lanes(lf)) * LOG2E, (n_new, 1))
    tail_ref[...] = tail_ref[...] + jnp.sum(lf, axis=-1, keepdims=True)
    per = g // page_groups
    scores = []
    for grp in range(page_groups):
        kc = jnp.concatenate([kbuf[slot, p].astype(BF16) for p in range(grp * per, (grp + 1) * per)],
                             axis=1)
        s = jnp.dot(qbd_ref[...], kc, preferred_element_type=F32)
        scores.append(s + key_bias[:, grp * per * PAGE:(grp + 1) * per * PAGE])
    parts = []
    for grp, s in enumerate(scores):
        vc = jnp.concatenate([vbuf[slot, p].astype(BF16) for p in range(grp * per, (grp + 1) * per)], axis=1)
        parts.append(partial_softmax(s, vc))
    merge(parts)

    o = acc_ref[...] / l_ref[...]
    sfg = sfg_ref[0].astype(F32)
    for t in range(n_new):
        ot = jnp.where(head_of_lane == head_of_row, o[t * H_FOX:(t + 1) * H_FOX, :], 0.0)
        a_ref[0, t:t + 1, :] = jnp.sum(ot, axis=0, keepdims=True) * sfg[t:t + 1, :]

    @pl.when(step == total_steps - 1)
    def _():
        for cp in page_copies(nxt, 1 - slot):
            cp.wait()


def _fox_sample(page_table, q, kt_new, vt_new, lft_new, sfg, cache_kt, cache_vt, cache_lft):
    bd, n_new, _ = q.shape
    g, n_steps, kw = _fox_sample_plan(page_table, n_new)
    in_specs, out_spec, scratch = _fox_sample_specs(n_new, g, n_steps, (kt_new, vt_new, lft_new, cache_lft),
                                                    lambda b, c, pt: (b, c, pt))
    grid_spec = pltpu.PrefetchScalarGridSpec(
        num_scalar_prefetch=1, grid=(bd, n_steps), in_specs=in_specs, out_specs=out_spec, scratch_shapes=scratch)
    return pl.pallas_call(
        functools.partial(_fox_sample_kernel, n_steps=n_steps, **kw),
        grid_spec=grid_spec,
        out_shape=jax.ShapeDtypeStruct((bd, n_new, W_FOX), F32),
        compiler_params=_params(("arbitrary", "arbitrary")),
        name="fox_sample",
    )(page_table, q, kt_new, vt_new, lft_new, sfg, cache_lft, cache_kt, cache_vt)


def _fox_sample_plan(page_table, n_new):
    n_pages = page_table.shape[1]
    g = next(c for c in (16, 8, 4, 2, 1) if n_pages % c == 0)
    page_groups = 2 if g >= 2 else 1
    return g, n_pages // g, dict(n_new=n_new, pages_per_step=g, page_groups=page_groups)


def _fox_sample_specs(n_new, g, n_steps, resident, ids):
    nq = n_new * H_FOX

    def spec(shape, f, **kw):
        return pl.BlockSpec(shape, lambda *a: f(*ids(*a)), **kw)

    def tok(n):
        return spec((1, n_new, n), lambda b, c, pt: (b, 0, 0))

    def whole(arr):
        nd = arr.ndim
        return spec(arr.shape, lambda b, c, pt: (0,) * nd, pipeline_mode=pl.Buffered(1))

    kt_new, vt_new, lft_new, cache_lft = resident
    in_memory = pl.BlockSpec(memory_space=pl.ANY)
    in_specs = [tok(W_FOX), whole(kt_new), whole(vt_new), whole(lft_new), tok(W_FOX), whole(cache_lft),
                in_memory, in_memory]
    scratch = [pltpu.VMEM((nq, W_FOX), BF16), pltpu.VMEM((nq, 1), F32), pltpu.VMEM((nq, 1), F32),
               pltpu.VMEM((nq, W_FOX), F32), pltpu.VMEM((H_FOX, LANES), F32),
               pltpu.VMEM((2, g, W_FOX, PAGE), F32), pltpu.VMEM((2, g, W_FOX, PAGE), F32),
               pltpu.SemaphoreType.DMA((2, 2))]
    return in_specs, tok(W_FOX), scratch


def _fox_fused_kernel(pt_ref, *refs, n_prompt_in, n_sample_in, n_prompt_scratch, n_steps, sample_kw):
    refs = list(refs)
    p_in, refs = refs[:n_prompt_in], refs[n_prompt_in:]
    s_in, refs = refs[:n_sample_in], refs[n_sample_in:]
    (a_p, a_d), refs = refs[:2], refs[2:]
    p_scr, s_scr = refs[:n_prompt_scratch], refs[n_prompt_scratch:]
    step = (pl.program_id(0) * pl.num_programs(1) + pl.program_id(1)) * pl.num_programs(2) + pl.program_id(2)
    total = pl.num_programs(0) * pl.num_programs(1) * pl.num_programs(2)
    sample = functools.partial(_fox_sample_body, step, total, n_steps, pt_ref, *s_in, a_d, *s_scr, **sample_kw)
    _fox_prompt_body(pl.program_id(2), *p_in, a_p, *p_scr, blk=FOX_BLK, side_work=sample)


def _fox_fused(qt, kn, vtb, ccol, sfg, page_table, q, kt_new, vt_new, lft_new, sfg_d, cache_kt, cache_vt, cache_lft):
    b, s, _ = kn.shape
    bd, n_new, _ = q.shape
    ng, nqb = H_FOX // HEADS_PER_GROUP, s // FOX_BLK
    g, n_steps, kw = _fox_sample_plan(page_table, n_new)
    assert b * ng * nqb == bd * n_steps
    p_in, p_out, p_scr = _fox_prompt_specs(s, lambda bi, gi, i, pt: (bi, gi, i))

    def sample_ids(bi, gi, i, pt):
        step = (bi * ng + gi) * nqb + i
        return step // n_steps, step % n_steps, pt

    s_in, s_out, s_scr = _fox_sample_specs(n_new, g, n_steps, (kt_new, vt_new, lft_new, cache_lft), sample_ids)
    grid_spec = pltpu.PrefetchScalarGridSpec(
        num_scalar_prefetch=1, grid=(b, ng, nqb), in_specs=p_in + s_in, out_specs=[p_out, s_out],
        scratch_shapes=p_scr + s_scr)
    return pl.pallas_call(
        functools.partial(_fox_fused_kernel, n_prompt_in=len(p_in), n_sample_in=len(s_in),
                          n_prompt_scratch=len(p_scr), n_steps=n_steps, sample_kw=kw),
        grid_spec=grid_spec,
        out_shape=[jax.ShapeDtypeStruct((b, s, W_FOX), BF16), jax.ShapeDtypeStruct((bd, n_new, W_FOX), F32)],
        compiler_params=_params(("arbitrary", "arbitrary", "arbitrary")),
        name="fox_fused",
    )(page_table, qt, kn, vtb, ccol, sfg, q, kt_new, vt_new, lft_new, sfg_d, cache_lft, cache_kt, cache_vt)


def _tail_kernel(a_ref, r_ref, x_ref, woe_ref, gno_ref, wio_ref, gv_ref, ws_ref, bs_ref, woo_ref,
                 *out_refs, n_new):
    rows = x_ref.shape[0]
    x1 = (x_ref[...]
          + jnp.dot(a_ref[...].astype(BF16), woe_ref[:W_FOX, :], preferred_element_type=F32)
          + jnp.dot(r_ref[...], woe_ref[W_FOX:, :], preferred_element_type=F32))
    h = _rms(x1, gno_ref[...]).astype(BF16)
    v = jax.nn.gelu(jnp.dot(h, wio_ref[:, GM_WIDTH:2 * GM_WIDTH], preferred_element_type=F32))
    vn = _layer_norm(v, gv_ref[...])
    if n_new is None:
        y_ref, = out_refs
        tril = (lax.broadcasted_iota(jnp.int32, (CHUNK, CHUNK), 0)
                >= lax.broadcasted_iota(jnp.int32, (CHUNK, CHUNK), 1))
        vb = vn.astype(BF16)

        def spatial(g):
            wm = jnp.where(tril, ws_ref[g], 0.0).astype(BF16)
            bias = bs_ref[:, g:g + 1]
            return jnp.concatenate(
                [jnp.dot(wm, vb[c * CHUNK:(c + 1) * CHUNK, g * GM_GC:(g + 1) * GM_GC],
                         preferred_element_type=F32) + bias for c in range(rows // CHUNK)], axis=0)
    else:
        y_ref, vn_ref = out_refs
        vn_ref[...] = vn
        tok = lax.broadcasted_iota(jnp.int32, (rows, 1), 0) % n_new

        def spatial(g):
            vg = vn[:, g * GM_GC:(g + 1) * GM_GC]
            acc = jnp.zeros((rows, 1), F32)
            for t in range(n_new):
                acc = jnp.where(tok == t, bs_ref[g, t], acc)
            acc = jnp.broadcast_to(acc, (rows, GM_GC))
            for d in range(n_new):
                coef = jnp.zeros((rows, 1), F32)
                for t in range(d, n_new):
                    coef = jnp.where(tok == t, ws_ref[g, t * n_new + (t - d)], coef)
                src = vg if d == 0 else pltpu.roll(vg, d, axis=0)
                acc = acc + coef * src
            return acc

    u = jax.nn.gelu(jnp.dot(h, wio_ref[:, :GM_WIDTH], preferred_element_type=F32))
    gate = jnp.dot(h, wio_ref[:, 2 * GM_WIDTH:], preferred_element_type=F32)
    sv = jnp.concatenate([spatial(g) for g in range(GM_GROUPS)], axis=1)
    z = (u * sv * jax.nn.silu(gate)).astype(BF16)
    y_ref[...] = x1 + jnp.dot(z, woo_ref[...], preferred_element_type=F32)


def _tail(a, r, x2d, woe, gno, wio, gv, ws, bs, woo, tm, n_new):
    m = x2d.shape[0]
    row = lambda n: pl.BlockSpec((tm, n), lambda i: (i, 0))
    if n_new is None:
        mix_specs = [_const_spec(ws.shape), _const_spec(bs.shape)]
        out_specs = [row(D_MODEL)]
        out_shape = [jax.ShapeDtypeStruct((m, D_MODEL), F32)]
    else:
        smem = pl.BlockSpec(memory_space=pltpu.SMEM)
        mix_specs = [smem, smem]
        out_specs = [row(D_MODEL), row(GM_WIDTH)]
        out_shape = [jax.ShapeDtypeStruct((m, D_MODEL), F32), jax.ShapeDtypeStruct((m, GM_WIDTH), F32)]
    return pl.pallas_call(
        functools.partial(_tail_kernel, n_new=n_new),
        grid=(m // tm,),
        in_specs=[row(W_FOX), row(W_RET), row(D_MODEL), _const_spec(woe.shape), _const_spec((1, D_MODEL)),
                  _const_spec(wio.shape), _const_spec((1, GM_WIDTH))] + mix_specs + [_const_spec(woo.shape)],
        out_specs=out_specs, out_shape=out_shape,
        compiler_params=_params(("parallel",)),
        name="tail_prompt" if n_new is None else "tail_sample",
    )(a, r, x2d, woe, gno, wio, gv, ws, bs, woo)


def _rope_tables(pos):
    half = RET_DK // 2
    inv = ROPE_BASE ** (-jnp.arange(half, dtype=F32) / half)
    ang = pos.astype(F32)[:, None] * inv[None, :]
    cos, sin = jnp.cos(ang), jnp.sin(ang)
    return jnp.tile(jnp.concatenate([cos, cos], axis=-1), (1, 2)), jnp.tile(jnp.concatenate([-sin, sin], axis=-1), (1, 2))


def _row_tile(m):
    return next(t for t in (512, 256, 128) if m % t == 0)


def kernel(x_prompt, x_sample, cache_k, cache_v, cache_logf, state_ret, page_table, norm_even, w_in_even,
           b_forget, qnorm, knorm, ret_norm, w_out_even, norm_odd, w_in_odd, vnorm_odd, w_spatial, b_spatial,
           w_out_odd):
    bp, s, _ = x_prompt.shape
    bd, n_new, _ = x_sample.shape
    n_pages = page_table.shape[1]
    past = n_pages * PAGE
    assert s % 256 == 0 and (bd * n_new) % CHUNK == 0 and w_in_even.shape[0] == 1 and w_in_odd.shape[0] == 1

    wi = w_in_even[0]
    sizes = (W_FOX, W_FOX, W_FOX, H_FOX, W_FOX, W_RQK, W_RQK, W_RET, W_RET)
    offs = np.concatenate([[0], np.cumsum(sizes)])
    fq, fk, fv, ff, fg, rq, rk, rv, rg = [wi[:, offs[i]:offs[i + 1]] for i in range(9)]
    w_even = jnp.concatenate([fg, rq, rk, rv, rg], axis=1).astype(BF16)
    w_even_t = jnp.concatenate([fq.T, fk.T, fv.T, ff.T, jnp.zeros((EVEN_ROWS_T - R_FF - H_FOX, D_MODEL), F32)],
                               axis=0).astype(BF16)
    b_f = b_forget[0][:, None]
    g_q = qnorm[0][:, None]
    g_k = knorm[0][:, None]
    g_ret = ret_norm[0].reshape(1, W_RET)
    woe = w_out_even[0].astype(BF16)
    wio = w_in_odd[0].astype(BF16)
    woo = w_out_odd[0].astype(BF16)
    g_ne, g_no, g_v = norm_even[0][None, :], norm_odd[0][None, :], vnorm_odd[0][None, :]

    cos_p, sin_p = _rope_tables(jnp.arange(s))
    xp2 = x_prompt.reshape(bp * s, D_MODEL)
    tm = _row_tile(s)
    sfg, rqp, rkp, rvp, srg, kn, qt, kt32, _, vt32, vtb, lft = _even_proj(
        xp2, bp, g_ne, w_even, w_even_t, b_f, g_q, g_k, cos_p, sin_p, tm)
    rows = bd * n_new
    cos_d, sin_d = _rope_tables(past + jnp.arange(n_new))
    xd2 = x_sample.reshape(rows, D_MODEL)
    sfgd, rqd, rkd, rvd, srgd, _, qtd, kt32d, ktbd, vt32d, vtbd, lftd = _even_proj(
        xd2, 1, g_ne, w_even, w_even_t, b_f, g_q, g_k, jnp.tile(cos_d, (bd, 1)), jnp.tile(sin_d, (bd, 1)), rows)

    ng = H_FOX // HEADS_PER_GROUP
    c_col4 = jnp.transpose(_cumsum_lanes(lft).reshape(bp, ng, HEADS_PER_GROUP, s), (0, 1, 3, 2))
    r3 = lambda t: t.reshape(bp, s, t.shape[-1])
    d3 = lambda t: t.reshape(bd, n_new, t.shape[-1])
    n_phys = cache_k.shape[1]
    pages_t = lambda cache: jnp.transpose(cache[0], (0, 2, 3, 1)).reshape(n_phys, W_FOX, PAGE)
    prompt_args = (qt, r3(kn), vtb, c_col4, r3(sfg))
    sample_args = (page_table, d3(jnp.transpose(qtd[0], (1, 0))), ktbd[0], vtbd[0], lftd[0], d3(sfgd),
                   pages_t(cache_k), pages_t(cache_v), jnp.transpose(cache_logf[0], (0, 2, 1)))
    if bp * ng * (s // FOX_BLK) == bd * _fox_sample_plan(page_table, n_new)[1]:
        a_p, a_d = _fox_fused(*prompt_args, *sample_args)
    else:
        a_p, a_d = _fox_prompt(*prompt_args), _fox_sample(*sample_args)
    r_p, st_p = _ret_prompt(r3(rqp), r3(rkp), r3(rvp), r3(srg), g_ret)
    r_d, st_d = _ret_sample(rqd, rkd, rvd, srgd, g_ret, state_ret[0], n_new)

    y_p, = _tail(a_p.reshape(bp * s, W_FOX), r_p.reshape(bp * s, W_RET), xp2, woe, g_no, wio, g_v,
                 w_spatial[0], jnp.transpose(b_spatial[0]), woo, 512 if s % 512 == 0 else tm, None)
    ws_new = w_spatial[0][:, :n_new, :n_new].reshape(GM_GROUPS, n_new * n_new)
    y_d, gv_d = _tail(a_d.reshape(rows, W_FOX), r_d, xd2, woe, g_no, wio, g_v,
                      ws_new, b_spatial[0][:, :n_new], woo, rows, n_new)

    heads_last = lambda t, n, length: jnp.transpose(t.reshape(n, H_FOX, HEAD_DIM, length), (0, 3, 1, 2))[None]
    tokens_first = lambda t, n: jnp.transpose(t[0], (1, 0)).reshape(1, bd, n_new, *n)
    return (y_p.reshape(bp, s, D_MODEL), y_d.reshape(bd, n_new, D_MODEL),
            heads_last(kt32, bp, s), heads_last(vt32, bp, s),
            jnp.transpose(lft, (0, 2, 1))[None], st_p.reshape(1, bp, H_RET, RET_DK, RET_DV),
            tokens_first(kt32d, (H_FOX, HEAD_DIM)), tokens_first(vt32d, (H_FOX, HEAD_DIM)),
            tokens_first(lftd, (H_FOX,)), st_d[None], gv_d.reshape(1, bd, n_new, GM_WIDTH))
```

```python
import functools

import numpy as np
import jax
import jax.numpy as jnp
from jax import lax
from jax.experimental import pallas as pl
from jax.experimental.pallas import tpu as pltpu

D_MODEL = 1024
HEAD_DIM = 64
H_FOX = 8
W_FOX = H_FOX * HEAD_DIM
H_RET = 8
RET_DK = 64
RET_DV = 128
W_RET = H_RET * RET_DV
W_RQK = H_RET * RET_DK
ROPE_BASE = 10000.0
GM_WIDTH = 2 * D_MODEL
GM_GROUPS = 8
GM_GC = GM_WIDTH // GM_GROUPS
CHUNK = 128
PAGE = 128
EPS = 1e-6
F32 = jnp.float32
BF16 = jnp.bfloat16

LANES = 128
VMEM_LIMIT = 56 * 1024 * 1024

C_FG, C_RQ, C_RK, C_RV, C_RG, EVEN_COLS = 0, 512, 1024, 1536, 2560, 3584
R_FQ, R_FK, R_FV, R_FF, EVEN_ROWS_T = 0, 512, 1024, 1536, 1552

LOG_GAMMA = [float(np.log(np.float32(1.0) - np.float32(2.0) ** np.float32(-5.0 - h))) for h in range(H_RET)]

LOG2E = float(np.log2(np.e))
NT_DIMS = (((1,), (1,)), ((), ()))
TN_DIMS = (((0,), (0,)), ((), ()))


def _params(sem):
    return pltpu.CompilerParams(dimension_semantics=sem, vmem_limit_bytes=VMEM_LIMIT)


def _const_spec(shape):
    nd = len(shape)
    return pl.BlockSpec(shape, lambda *_: (0,) * nd, pipeline_mode=pl.Buffered(1))


def _rms(x, g):
    ms = jnp.mean(x * x, axis=-1, keepdims=True)
    return x * lax.rsqrt(ms + EPS) * g


def _layer_norm(x, g):
    xc = x - jnp.mean(x, axis=-1, keepdims=True)
    return xc * lax.rsqrt(jnp.mean(xc * xc, axis=-1, keepdims=True) + EPS) * g


def _rope(x, cos, sin_signed):
    rows = x.shape[0]
    first_half = (lax.broadcasted_iota(jnp.int32, (rows, LANES), 1) % RET_DK) < RET_DK // 2
    outs = []
    for c in range(x.shape[1] // LANES):
        xc = x[:, c * LANES:(c + 1) * LANES]
        nxt = pltpu.roll(xc, LANES - RET_DK // 2, axis=1)
        prv = pltpu.roll(xc, RET_DK // 2, axis=1)
        outs.append(xc * cos + jnp.where(first_half, nxt, prv) * sin_signed)
    return jnp.concatenate(outs, axis=-1)


def _log_sigmoid(x):
    return jnp.minimum(x, 0.0) - jnp.log1p(jnp.exp(-jnp.abs(x)))


def _head_rms_t(x, g):
    tokens = x.shape[1]
    x = x.reshape(H_FOX, HEAD_DIM, tokens)
    x = x * lax.rsqrt(jnp.mean(x * x, axis=1, keepdims=True) + EPS) * g[None]
    return x.reshape(W_FOX, tokens)


def _even_proj_kernel(x_ref, g_ref, w_ref, wt_ref, bf_ref, gq_ref, gk_ref, cos_ref, sin_ref,
                      sfg_ref, rq_ref, rk_ref, rv_ref, srg_ref, kn_ref,
                      qt_ref, kt32_ref, ktb_ref, vt32_ref, vtb_ref, lft_ref):
    h = _rms(x_ref[...], g_ref[...]).astype(BF16)

    def proj(lo, hi):
        return jnp.dot(h, w_ref[:, lo:hi], preferred_element_type=F32)

    sfg_ref[...] = jax.nn.silu(proj(C_FG, C_RQ)).astype(BF16)
    cos = cos_ref[...]
    sin = sin_ref[...]
    rq_ref[...] = _rope(proj(C_RQ, C_RK), cos, sin).astype(BF16)
    rk_ref[...] = (_rope(proj(C_RK, C_RV), cos, sin) * (RET_DK ** -0.5)).astype(BF16)
    rv_ref[...] = proj(C_RV, C_RG).astype(BF16)
    srg_ref[...] = jax.nn.silu(proj(C_RG, EVEN_COLS)).astype(BF16)

    t = lax.dot_general(wt_ref[...], h, NT_DIMS, preferred_element_type=F32)
    qt_ref[0] = (_head_rms_t(t[R_FQ:R_FK], gq_ref[...]) * (HEAD_DIM ** -0.5 * LOG2E)).astype(BF16)
    kt = _head_rms_t(t[R_FK:R_FV], gk_ref[...])
    kt32_ref[0] = kt
    ktb_ref[0] = kt.astype(BF16)
    kn_ref[...] = kt.T.astype(BF16)
    vt = t[R_FV:R_FF]
    vt32_ref[0] = vt
    vtb_ref[0] = vt.astype(BF16)
    lft_ref[0] = _log_sigmoid(t[R_FF:R_FF + H_FOX] + bf_ref[...])


def _even_proj(x2d, n_seq, g_norm, w, wt, b_f, g_q, g_k, cos, sin, tm):
    m = x2d.shape[0]
    s = m // n_seq
    per_seq = s // tm
    row = lambda n: pl.BlockSpec((tm, n), lambda i: (i, 0))
    tab = pl.BlockSpec((tm, LANES), lambda i: (i % per_seq, 0))
    col = lambda n: pl.BlockSpec((1, n, tm), lambda i: (i // per_seq, 0, i % per_seq))
    outs = [(W_FOX, BF16), (W_RQK, BF16), (W_RQK, BF16), (W_RET, BF16), (W_RET, BF16), (W_FOX, BF16)]
    outs_t = [(W_FOX, BF16), (W_FOX, F32), (W_FOX, BF16), (W_FOX, F32), (W_FOX, BF16), (H_FOX, F32)]
    return pl.pallas_call(
        _even_proj_kernel,
        grid=(m // tm,),
        in_specs=[row(D_MODEL), _const_spec((1, D_MODEL)), _const_spec((D_MODEL, EVEN_COLS)),
                  _const_spec((EVEN_ROWS_T, D_MODEL)), _const_spec((H_FOX, 1)), _const_spec((HEAD_DIM, 1)),
                  _const_spec((HEAD_DIM, 1)), tab, tab],
        out_specs=[row(n) for n, _ in outs] + [col(n) for n, _ in outs_t],
        out_shape=([jax.ShapeDtypeStruct((m, n), dt) for n, dt in outs]
                   + [jax.ShapeDtypeStruct((n_seq, n, s), dt) for n, dt in outs_t]),
        compiler_params=_params(("parallel",)),
        name="even_proj",
    )(x2d, g_norm, w, wt, b_f, g_q, g_k, cos, sin)


def _cumsum_kernel(x_ref, o_ref):
    x = x_ref[0]
    n = x.shape[1]
    lane = lax.broadcasted_iota(jnp.int32, x.shape, 1)
    sh = 1
    while sh < n:
        x = x + jnp.where(lane >= sh, pltpu.roll(x, sh, axis=1), 0.0)
        sh *= 2
    o_ref[0] = x


def _cumsum_lanes(x):
    b, h, s = x.shape
    spec = pl.BlockSpec((1, h, s), lambda i: (i, 0, 0))
    return pl.pallas_call(
        _cumsum_kernel, grid=(b,), in_specs=[spec], out_specs=spec,
        out_shape=jax.ShapeDtypeStruct(x.shape, F32),
        compiler_params=_params(("parallel",)), name="logf_cumsum",
    )(x)


HEADS_PER_GROUP = 4
GROUP_W = HEADS_PER_GROUP * HEAD_DIM


def _fox_prompt_kernel(*refs, blk):
    _fox_prompt_body(pl.program_id(2), *refs, blk=blk)


def _fox_prompt_body(i, qt_ref, k_ref, v_ref, ccol_ref, sfg_ref, a_ref,
                     qm_ref, vp_ref, cb_ref, m_ref, acc_ref, s_ref, p_ref, *, blk, side_work=None):
    s_len = k_ref.shape[1]

    @pl.when(i == 0)
    def _():
        ones_row = jnp.where(lax.broadcasted_iota(jnp.int32, (HEAD_DIM, s_len), 0) == 0, 1.0, 0.0).astype(BF16)
        ccol = ccol_ref[0, 0] * (-LOG2E)
        for hh in range(HEADS_PER_GROUP):
            vp_ref[hh, :HEAD_DIM, :] = v_ref[0, hh * HEAD_DIM:(hh + 1) * HEAD_DIM, :]
            vp_ref[hh, HEAD_DIM:, :] = ones_row
            cb_ref[hh] = jnp.broadcast_to(ccol[:, hh:hh + 1], (s_len, LANES))

    if side_work is not None:
        side_work()
    qt = qt_ref[0]
    row_head = lax.broadcasted_iota(jnp.int32, (GROUP_W, blk), 0) // HEAD_DIM
    for hh in range(HEADS_PER_GROUP):
        qm_ref[hh] = jnp.where(row_head == hh, qt, jnp.zeros_like(qt))
    m_ref[...] = jnp.full(m_ref.shape, -1e30, F32)
    acc_ref[...] = jnp.zeros(acc_ref.shape, F32)
    strips = [(hh, slice(c * LANES, (c + 1) * LANES), c) for hh in range(HEADS_PER_GROUP) for c in range(blk // LANES)]

    def raw_scores(j, slot):
        off = pl.multiple_of(j * blk, blk)
        kb = k_ref[0, pl.ds(off, blk), :]
        for hh in range(HEADS_PER_GROUP):
            bias = cb_ref[hh, pl.ds(off, blk), :]
            s_ref[slot, hh] = (jnp.dot(kb, qm_ref[hh], preferred_element_type=F32)
                               + jnp.concatenate([bias] * (blk // LANES), axis=1))

    def consume(j, slot, masked):
        off = pl.multiple_of(j * blk, blk)

        def scores(hh, cols, c):
            s = s_ref[slot, hh, :, cols]
            if masked:
                causal = (lax.broadcasted_iota(jnp.int32, (blk, LANES), 0)
                          <= lax.broadcasted_iota(jnp.int32, (blk, LANES), 1) + c * LANES)
                s = jnp.where(causal, s, -jnp.inf)
            return s

        rescale = {}
        for hh, cols, c in strips:
            s = scores(hh, cols, c)
            m_old = m_ref[hh, :, cols]
            m_new = jnp.maximum(m_old, jnp.max(s, axis=0, keepdims=True))
            p_ref[hh, :, cols] = jnp.exp2(s - m_new).astype(BF16)
            rescale[hh, c] = jnp.exp2(m_old - m_new)
            m_ref[hh, :, cols] = m_new
        for hh in range(HEADS_PER_GROUP):
            alpha = jnp.concatenate([rescale[hh, c] for c in range(blk // LANES)], axis=1)
            pv = jnp.dot(vp_ref[hh, :, pl.ds(off, blk)], p_ref[hh], preferred_element_type=F32)
            acc_ref[hh] = alpha * acc_ref[hh] + pv

    raw_scores(0, 0)

    def pair(t, carry):
        j = 2 * t
        raw_scores(j + 1, 1)
        consume(j, 0, False)
        raw_scores(j + 2, 0)
        consume(j + 1, 1, False)
        return carry

    lax.fori_loop(0, i // 2, pair, 0)

    @pl.when(i % 2 == 1)
    def _():
        raw_scores(i, 1)
        consume(i - 1, 0, False)
        consume(i, 1, True)

    @pl.when(i % 2 == 0)
    def _():
        consume(i, 0, True)

    ot = jnp.concatenate([acc_ref[hh, :HEAD_DIM, :] / acc_ref[hh, HEAD_DIM:HEAD_DIM + 1, :]
                          for hh in range(HEADS_PER_GROUP)], axis=0)
    a_ref[0] = (ot.T * sfg_ref[0].astype(F32)).astype(BF16)


FOX_BLK = 256


def _fox_prompt_specs(s, ids):
    blk = FOX_BLK

    def spec(shape, f):
        return pl.BlockSpec(shape, lambda *a: f(*ids(*a)))

    rows = spec((1, blk, GROUP_W), lambda bi, g, i: (bi, i, g))
    in_specs = [spec((1, GROUP_W, blk), lambda bi, g, i: (bi, g, i)),
                spec((1, s, GROUP_W), lambda bi, g, i: (bi, 0, g)),
                spec((1, GROUP_W, s), lambda bi, g, i: (bi, g, 0)),
                spec((1, 1, s, HEADS_PER_GROUP), lambda bi, g, i: (bi, g, 0, 0)),
                rows]
    scratch = [pltpu.VMEM((HEADS_PER_GROUP, GROUP_W, blk), BF16),
               pltpu.VMEM((HEADS_PER_GROUP, LANES, s), BF16),
               pltpu.VMEM((HEADS_PER_GROUP, s, LANES), F32),
               pltpu.VMEM((HEADS_PER_GROUP, 1, blk), F32),
               pltpu.VMEM((HEADS_PER_GROUP, LANES, blk), F32),
               pltpu.VMEM((2, HEADS_PER_GROUP, blk, blk), F32),
               pltpu.VMEM((HEADS_PER_GROUP, blk, blk), BF16)]
    return in_specs, rows, scratch


def _fox_prompt(qt, kn, vtb, ccol, sfg):
    b, s, _ = kn.shape
    in_specs, out_spec, scratch = _fox_prompt_specs(s, lambda bi, g, i: (bi, g, i))
    return pl.pallas_call(
        functools.partial(_fox_prompt_kernel, blk=FOX_BLK),
        grid=(b, H_FOX // HEADS_PER_GROUP, s // FOX_BLK),
        in_specs=in_specs, out_specs=out_spec,
        out_shape=jax.ShapeDtypeStruct((b, s, W_FOX), BF16),
        scratch_shapes=scratch,
        compiler_params=_params(("parallel", "parallel", "arbitrary")),
        name="fox_prompt",
    )(qt, kn, vtb, ccol, sfg)


def _ret_head_out(o, g, srg):
    return (_layer_norm(o, g) * srg.astype(F32)).astype(BF16)


def _ret_prompt_kernel(rq_ref, rk_ref, rv_ref, srg_ref, gret_ref, r_ref, st_ref, dec_ref, kdec_ref, *, chunks):
    L = CHUNK
    c = pl.program_id(1)

    @pl.when(c == 0)
    def _():
        st_ref[...] = jnp.zeros(st_ref.shape, F32)
        diff = (lax.broadcasted_iota(jnp.int32, (L, L), 0)
                - lax.broadcasted_iota(jnp.int32, (L, L), 1)).astype(F32)
        for h in range(H_RET):
            dec_ref[h] = jnp.where(diff >= 0, jnp.exp(LOG_GAMMA[h] * jnp.maximum(diff, 0.0)), 0.0)
        left = (L - 1.0) - lax.broadcasted_iota(jnp.int32, (L, LANES), 0).astype(F32)
        first = lax.broadcasted_iota(jnp.int32, (L, LANES), 1) < RET_DK
        for pp in range(H_RET // 2):
            kdec_ref[pp] = jnp.exp(jnp.where(first, LOG_GAMMA[2 * pp], LOG_GAMMA[2 * pp + 1]) * left)

    pos = lax.broadcasted_iota(jnp.int32, (L, 1), 0).astype(F32)
    lane_half = lax.broadcasted_iota(jnp.int32, (L, LANES), 1) // RET_DK
    for ci in range(chunks):
        rows_c = slice(ci * L, (ci + 1) * L)
        first_stage = []
        for pp in range(H_RET // 2):
            qp = rq_ref[0, rows_c, pp * LANES:(pp + 1) * LANES]
            kp = rk_ref[0, rows_c, pp * LANES:(pp + 1) * LANES]
            s_pair = st_ref[0, pp]
            s_pair_b = s_pair.astype(BF16)
            kd = (kp.astype(F32) * kdec_ref[pp]).astype(BF16)
            for j in range(2):
                h = 2 * pp + j
                qm = jnp.where(lane_half == j, qp, jnp.zeros_like(qp))
                vh = rv_ref[0, rows_c, h * RET_DV:(h + 1) * RET_DV]
                scores = lax.dot_general(qm, kp, NT_DIMS, preferred_element_type=F32)
                cross = jnp.dot(qm, s_pair_b, preferred_element_type=F32)
                upd = lax.dot_general(kd, vh, TN_DIMS, preferred_element_type=F32)
                first_stage.append((scores, cross, upd, vh, s_pair))
        outs = []
        for h, (scores, cross, upd, vh, s_pair) in enumerate(first_stage):
            lg = LOG_GAMMA[h]
            intra = jnp.dot((scores * dec_ref[h]).astype(BF16), vh, preferred_element_type=F32)
            outs.append(intra + cross * jnp.exp(lg * (pos + 1.0)))
            rows = slice((h % 2) * RET_DK, (h % 2 + 1) * RET_DK)
            st_ref[0, h // 2, rows, :] = float(np.exp(lg * L)) * s_pair[rows] + upd[rows]
        for h, o in enumerate(outs):
            cols = slice(h * RET_DV, (h + 1) * RET_DV)
            r_ref[0, rows_c, cols] = _ret_head_out(o, gret_ref[:, cols], srg_ref[0, rows_c, cols])


def _ret_prompt(rq, rk, rv, srg, g_ret):
    b, s, _ = rq.shape
    chunks = next(n for n in (4, 2, 1) if s % (n * CHUNK) == 0)
    rows = chunks * CHUNK
    qk = pl.BlockSpec((1, rows, W_RQK), lambda bi, c: (bi, c, 0))
    wide = pl.BlockSpec((1, rows, W_RET), lambda bi, c: (bi, c, 0))
    st = pl.BlockSpec((1, H_RET // 2, 2 * RET_DK, RET_DV), lambda bi, c: (bi, 0, 0, 0))
    return pl.pallas_call(
        functools.partial(_ret_prompt_kernel, chunks=chunks),
        grid=(b, s // rows),
        in_specs=[qk, qk, wide, wide, _const_spec((1, W_RET))],
        out_specs=[wide, st],
        out_shape=[jax.ShapeDtypeStruct((b, s, W_RET), BF16),
                   jax.ShapeDtypeStruct((b, H_RET // 2, 2 * RET_DK, RET_DV), F32)],
        scratch_shapes=[pltpu.VMEM((H_RET, CHUNK, CHUNK), F32), pltpu.VMEM((H_RET // 2, CHUNK, LANES), F32)],
        compiler_params=_params(("parallel", "arbitrary")),
        name="ret_prompt",
    )(rq, rk, rv, srg, g_ret)


def _ret_sample_kernel(rq_ref, rk_ref, rv_ref, srg_ref, gret_ref, s0_ref, r_ref, s1_ref, *, n_new):
    rows = rq_ref.shape[0]
    nb = rows // n_new
    two_dk = 2 * RET_DK
    ri = lax.broadcasted_iota(jnp.int32, (rows, rows), 0)
    ci = lax.broadcasted_iota(jnp.int32, (rows, rows), 1)
    same = (ri // n_new) == (ci // n_new)
    diff = ((ri % n_new) - (ci % n_new)).astype(F32)
    pos = (lax.broadcasted_iota(jnp.int32, (rows, 1), 0) % n_new).astype(F32)
    lane_half = lax.broadcasted_iota(jnp.int32, (rows, LANES), 1) // RET_DK
    q_sel = (lax.broadcasted_iota(jnp.int32, (rows, nb * two_dk), 1) // two_dk
             == lax.broadcasted_iota(jnp.int32, (rows, nb * two_dk), 0) // n_new)
    k_sel = (lax.broadcasted_iota(jnp.int32, (nb * two_dk, rows), 0) // two_dk
             == lax.broadcasted_iota(jnp.int32, (nb * two_dk, rows), 1) // n_new)
    row_head = (lax.broadcasted_iota(jnp.int32, (nb * two_dk, 1), 0) % two_dk) // RET_DK

    qp = rq_ref[...]
    kp = rk_ref[...]
    s_old = s0_ref[...].reshape(nb * two_dk, RET_DV)
    s_old_b = s_old.astype(BF16)
    upd = []
    for j in range(2):
        h = 2 * pl.program_id(0) + j
        lg = jnp.where(h == 0, LOG_GAMMA[0], 0.0)
        for hh in range(1, H_RET):
            lg = jnp.where(h == hh, LOG_GAMMA[hh], lg)
        qm = jnp.where(lane_half == j, qp, jnp.zeros_like(qp))
        vh = rv_ref[:, j * RET_DV:(j + 1) * RET_DV]
        decay = jnp.where(same & (diff >= 0), jnp.exp(lg * jnp.maximum(diff, 0.0)), 0.0)
        scores = lax.dot_general(qm, kp, NT_DIMS, preferred_element_type=F32) * decay
        intra = jnp.dot(scores.astype(BF16), vh, preferred_element_type=F32)
        q_exp = jnp.where(q_sel, jnp.tile(qm, (1, nb)), jnp.zeros((), BF16))
        cross = jnp.dot(q_exp, s_old_b, preferred_element_type=F32) * jnp.exp(lg * (pos + 1.0))
        kd = kp.astype(F32) * jnp.exp(lg * (n_new - 1.0 - pos))
        kd_exp = jnp.where(k_sel, jnp.tile(kd.T, (nb, 1)), 0.0).astype(BF16)
        keep = jnp.exp(lg * jnp.full((1, 1), float(n_new), F32))
        upd.append((keep, jnp.dot(kd_exp, vh, preferred_element_type=F32)))
        cols = slice(j * RET_DV, (j + 1) * RET_DV)
        r_ref[:, cols] = _ret_head_out(intra + cross, gret_ref[:, cols], srg_ref[:, cols])
    s_new = jnp.where(row_head == 0, upd[0][0] * s_old + upd[0][1], upd[1][0] * s_old + upd[1][1])
    s1_ref[...] = s_new.reshape(s1_ref.shape)


def _ret_sample(rq, rk, rv, srg, g_ret, state, n_new):
    rows = rq.shape[0]
    nb = rows // n_new
    pair = pl.BlockSpec((rows, LANES), lambda p: (0, p))
    wide = pl.BlockSpec((rows, 2 * RET_DV), lambda p: (0, p))
    st = pl.BlockSpec((nb, 2, RET_DK, RET_DV), lambda p: (0, p, 0, 0))
    return pl.pallas_call(
        functools.partial(_ret_sample_kernel, n_new=n_new),
        grid=(H_RET // 2,),
        in_specs=[pair, pair, wide, wide, pl.BlockSpec((1, 2 * RET_DV), lambda p: (0, p)), st],
        out_specs=[wide, st],
        out_shape=[jax.ShapeDtypeStruct((rows, W_RET), BF16),
                   jax.ShapeDtypeStruct(state.shape, F32)],
        compiler_params=_params(("parallel",)),
        name="ret_sample",
    )(rq, rk, rv, srg, g_ret, state)


def _fox_sample_kernel(pt_ref, *refs, n_steps, **kw):
    _fox_sample_body(pl.program_id(0) * n_steps + pl.program_id(1), pl.num_programs(0) * n_steps, n_steps,
                     pt_ref, *refs, **kw)


def _fox_sample_body(step, total_steps, n_steps, pt_ref, q_ref, kn_ref, vn_ref, lfn_ref, sfg_ref, lfc_ref,
                     ck_hbm, cv_hbm, a_ref, qbd_ref, m_ref, l_ref, acc_ref, tail_ref, kbuf, vbuf, sem,
                     *, n_new, pages_per_step, page_groups):
    g = pages_per_step
    b, c = step // n_steps, step % n_steps

    def first_page_of(st):
        return (n_steps - 1 - st % n_steps) * g

    def page_copies(st, slot):
        row, first = st // n_steps, first_page_of(st)
        copies = []
        for p in range(g):
            page = pt_ref[row, first + p]
            copies.append(pltpu.make_async_copy(ck_hbm.at[page], kbuf.at[slot, p], sem.at[slot, 0]))
            copies.append(pltpu.make_async_copy(cv_hbm.at[page], vbuf.at[slot, p], sem.at[slot, 1]))
        return copies

    slot = step % 2
    nxt = jnp.minimum(step + 1, total_steps - 1)

    @pl.when(step == 0)
    def _():
        for cp in page_copies(step, 0):
            cp.start()

    for cp in page_copies(nxt, 1 - slot):
        cp.start()
    for cp in page_copies(step, slot):
        cp.wait()
    first_page = first_page_of(step)
    nq = n_new * H_FOX
    row_tok = lax.broadcasted_iota(jnp.int32, (nq, 1), 0) // H_FOX
    head_of_row = lax.broadcasted_iota(jnp.int32, (H_FOX, W_FOX), 0)
    head_of_lane = lax.broadcasted_iota(jnp.int32, (H_FOX, W_FOX), 1) // HEAD_DIM

    def partial_softmax(s, vt):
        m = jnp.max(s, axis=-1, keepdims=True)
        p = jnp.exp2(s - m)
        return (m, jnp.sum(p, axis=-1, keepdims=True),
                lax.dot_general(p.astype(BF16), vt, NT_DIMS, preferred_element_type=F32))

    def merge(parts):
        m_new = m_ref[...]
        for m, _, _ in parts:
            m_new = jnp.maximum(m_new, m)
        alpha = jnp.exp2(m_ref[...] - m_new)
        l_new = alpha * l_ref[...]
        acc = alpha * acc_ref[...]
        for m, l, o in parts:
            w = jnp.exp2(m - m_new)
            l_new = l_new + w * l
            acc = acc + w * o
        m_ref[...] = m_new
        l_ref[...] = l_new
        acc_ref[...] = acc

    @pl.when(c == 0)
    def _():
        q = q_ref[0].astype(F32)
        for t in range(n_new):
            qbd_ref[t * H_FOX:(t + 1) * H_FOX, :] = jnp.where(
                head_of_lane == head_of_row, jnp.broadcast_to(q[t:t + 1, :], (H_FOX, W_FOX)), 0.0).astype(BF16)
        tail_ref[...] = jnp.zeros(tail_ref.shape, F32)
        m_ref[...] = jnp.full(m_ref.shape, -1e30, F32)
        l_ref[...] = jnp.zeros(l_ref.shape, F32)
        acc_ref[...] = jnp.zeros(acc_ref.shape, F32)
        lane = lax.broadcasted_iota(jnp.int32, (H_FOX, LANES), 1)
        pfx = lfn_ref[...]
        sh = 1
        while sh < n_new:
            pfx = pfx + jnp.where(lane % n_new >= sh, pltpu.roll(pfx, sh, axis=1), 0.0)
            sh *= 2
        col = lax.broadcasted_iota(jnp.int32, (nq, LANES), 1)
        valid = (col // n_new == b) & (col % n_new <= row_tok)
        s = jnp.dot(qbd_ref[...], kn_ref[...], preferred_element_type=F32)
        merge([partial_softmax(jnp.where(valid, s - jnp.tile(pfx, (n_new, 1)) * LOG2E, -jnp.inf), vn_ref[...])])

    lf = jnp.concatenate([lfc_ref[pt_ref[b, first_page + p]] for p in range(g)], axis=0)
    hi = lf.astype(BF16)
    rest = lf - hi.astype(F32)
    mid = rest.astype(BF16)
    low = (rest - mid.astype(F32)).astype(BF16)
    t_idx = lax.broadcasted_iota(jnp.int32, (PAGE, 2 * PAGE), 0)
    j_idx = lax.broadcasted_iota(jnp.int32, (PAGE, 2 * PAGE), 1)
    later_or_all = jnp.where((j_idx >= PAGE) | (t_idx > j_idx), 1.0, 0.0).astype(BF16)
    sums = jnp.dot(jnp.concatenate([hi, mid, low], axis=0), later_or_all, preferred_element_type=F32)
    sums = sums[:g * H_FOX] + sums[g * H_FOX:2 * g * H_FOX] + sums[2 * g * H_FOX:]
    run = tail_ref[...]
    biases = [None] * g
    for p in reversed(range(g)):
        page_rows = slice(p * H_FOX, (p + 1) * H_FOX)
        biases[p] = run + sums[page_rows, :PAGE]
        run = run + sums[page_rows, PAGE:]
    tail_ref[...] = run
    key_bias = jnp.tile(jnp.concatenate(biases, axis=1) * LOG2E, (n_new, 1))
    per = g // page_groups
    scores = []
    for grp in range(page_groups):
        kc = jnp.concatenate([kbuf[slot, p].astype(BF16) for p in range(grp * per, (grp + 1) * per)],
                             axis=1)
        s = jnp.dot(qbd_ref[...], kc, preferred_element_type=F32)
        scores.append(s + key_bias[:, grp * per * PAGE:(grp + 1) * per * PAGE])
    parts = []
    for grp, s in enumerate(scores):
        vc = jnp.concatenate([vbuf[slot, p].astype(BF16) for p in range(grp * per, (grp + 1) * per)], axis=1)
        parts.append(partial_softmax(s, vc))
    merge(parts)

    o = acc_ref[...] / l_ref[...]
    sfg = sfg_ref[0].astype(F32)
    for t in range(n_new):
        ot = jnp.where(head_of_lane == head_of_row, o[t * H_FOX:(t + 1) * H_FOX, :], 0.0)
        a_ref[0, t:t + 1, :] = jnp.sum(ot, axis=0, keepdims=True) * sfg[t:t + 1, :]

    @pl.when(step == total_steps - 1)
    def _():
        for cp in page_copies(nxt, 1 - slot):
            cp.wait()


def _fox_sample(page_table, q, kt_new, vt_new, lft_new, sfg, cache_kt, cache_vt, cache_lft):
    bd, n_new, _ = q.shape
    g, n_steps, kw = _fox_sample_plan(page_table, n_new)
    in_specs, out_spec, scratch = _fox_sample_specs(n_new, g, n_steps, (kt_new, vt_new, lft_new, cache_lft),
                                                    lambda b, c, pt: (b, c, pt))
    grid_spec = pltpu.PrefetchScalarGridSpec(
        num_scalar_prefetch=1, grid=(bd, n_steps), in_specs=in_specs, out_specs=out_spec, scratch_shapes=scratch)
    return pl.pallas_call(
        functools.partial(_fox_sample_kernel, n_steps=n_steps, **kw),
        grid_spec=grid_spec,
        out_shape=jax.ShapeDtypeStruct((bd, n_new, W_FOX), F32),
        compiler_params=_params(("arbitrary", "arbitrary")),
        name="fox_sample",
    )(page_table, q, kt_new, vt_new, lft_new, sfg, cache_lft, cache_kt, cache_vt)


def _fox_sample_plan(page_table, n_new):
    n_pages = page_table.shape[1]
    g = next(c for c in (16, 8, 4, 2, 1) if n_pages % c == 0)
    page_groups = 2 if g >= 2 else 1
    return g, n_pages // g, dict(n_new=n_new, pages_per_step=g, page_groups=page_groups)


def _fox_sample_specs(n_new, g, n_steps, resident, ids):
    nq = n_new * H_FOX

    def spec(shape, f, **kw):
        return pl.BlockSpec(shape, lambda *a: f(*ids(*a)), **kw)

    def tok(n):
        return spec((1, n_new, n), lambda b, c, pt: (b, 0, 0))

    def whole(arr):
        nd = arr.ndim
        return spec(arr.shape, lambda b, c, pt: (0,) * nd, pipeline_mode=pl.Buffered(1))

    kt_new, vt_new, lft_new, cache_lft = resident
    in_memory = pl.BlockSpec(memory_space=pl.ANY)
    in_specs = [tok(W_FOX), whole(kt_new), whole(vt_new), whole(lft_new), tok(W_FOX), whole(cache_lft),
                in_memory, in_memory]
    scratch = [pltpu.VMEM((nq, W_FOX), BF16), pltpu.VMEM((nq, 1), F32), pltpu.VMEM((nq, 1), F32),
               pltpu.VMEM((nq, W_FOX), F32), pltpu.VMEM((H_FOX, LANES), F32),
               pltpu.VMEM((2, g, W_FOX, PAGE), F32), pltpu.VMEM((2, g, W_FOX, PAGE), F32),
               pltpu.SemaphoreType.DMA((2, 2))]
    return in_specs, tok(W_FOX), scratch


def _fox_fused_kernel(pt_ref, *refs, n_prompt_in, n_sample_in, n_prompt_scratch, n_steps, sample_kw):
    refs = list(refs)
    p_in, refs = refs[:n_prompt_in], refs[n_prompt_in:]
    s_in, refs = refs[:n_sample_in], refs[n_sample_in:]
    (a_p, a_d), refs = refs[:2], refs[2:]
    p_scr, s_scr = refs[:n_prompt_scratch], refs[n_prompt_scratch:]
    step = (pl.program_id(0) * pl.num_programs(1) + pl.program_id(1)) * pl.num_programs(2) + pl.program_id(2)
    total = pl.num_programs(0) * pl.num_programs(1) * pl.num_programs(2)
    sample = functools.partial(_fox_sample_body, step, total, n_steps, pt_ref, *s_in, a_d, *s_scr, **sample_kw)
    _fox_prompt_body(pl.program_id(2), *p_in, a_p, *p_scr, blk=FOX_BLK, side_work=sample)


def _fox_fused(qt, kn, vtb, ccol, sfg, page_table, q, kt_new, vt_new, lft_new, sfg_d, cache_kt, cache_vt, cache_lft):
    b, s, _ = kn.shape
    bd, n_new, _ = q.shape
    ng, nqb = H_FOX // HEADS_PER_GROUP, s // FOX_BLK
    g, n_steps, kw = _fox_sample_plan(page_table, n_new)
    assert b * ng * nqb == bd * n_steps
    p_in, p_out, p_scr = _fox_prompt_specs(s, lambda bi, gi, i, pt: (bi, gi, i))

    def sample_ids(bi, gi, i, pt):
        step = (bi * ng + gi) * nqb + i
        return step // n_steps, step % n_steps, pt

    s_in, s_out, s_scr = _fox_sample_specs(n_new, g, n_steps, (kt_new, vt_new, lft_new, cache_lft), sample_ids)
    grid_spec = pltpu.PrefetchScalarGridSpec(
        num_scalar_prefetch=1, grid=(b, ng, nqb), in_specs=p_in + s_in, out_specs=[p_out, s_out],
        scratch_shapes=p_scr + s_scr)
    return pl.pallas_call(
        functools.partial(_fox_fused_kernel, n_prompt_in=len(p_in), n_sample_in=len(s_in),
                          n_prompt_scratch=len(p_scr), n_steps=n_steps, sample_kw=kw),
        grid_spec=grid_spec,
        out_shape=[jax.ShapeDtypeStruct((b, s, W_FOX), BF16), jax.ShapeDtypeStruct((bd, n_new, W_FOX), F32)],
        compiler_params=_params(("arbitrary", "arbitrary", "arbitrary")),
        name="fox_fused",
    )(page_table, qt, kn, vtb, ccol, sfg, q, kt_new, vt_new, lft_new, sfg_d, cache_lft, cache_kt, cache_vt)


def _tail_kernel(a_ref, r_ref, x_ref, woe_ref, gno_ref, wio_ref, gv_ref, ws_ref, bs_ref, woo_ref,
                 *out_refs, n_new):
    rows = x_ref.shape[0]
    x1 = (x_ref[...]
          + jnp.dot(a_ref[...].astype(BF16), woe_ref[:W_FOX, :], preferred_element_type=F32)
          + jnp.dot(r_ref[...], woe_ref[W_FOX:, :], preferred_element_type=F32))
    h = _rms(x1, gno_ref[...]).astype(BF16)
    v = jax.nn.gelu(jnp.dot(h, wio_ref[:, GM_WIDTH:2 * GM_WIDTH], preferred_element_type=F32))
    vn = _layer_norm(v, gv_ref[...])
    if n_new is None:
        y_ref, = out_refs
        tril = (lax.broadcasted_iota(jnp.int32, (CHUNK, CHUNK), 0)
                >= lax.broadcasted_iota(jnp.int32, (CHUNK, CHUNK), 1))
        vb = vn.astype(BF16)

        def spatial(g):
            wm = jnp.where(tril, ws_ref[g], 0.0).astype(BF16)
            bias = bs_ref[:, g:g + 1]
            return jnp.concatenate(
                [jnp.dot(wm, vb[c * CHUNK:(c + 1) * CHUNK, g * GM_GC:(g + 1) * GM_GC],
                         preferred_element_type=F32) + bias for c in range(rows // CHUNK)], axis=0)
    else:
        y_ref, vn_ref = out_refs
        vn_ref[...] = vn
        tok = lax.broadcasted_iota(jnp.int32, (rows, 1), 0) % n_new

        def spatial(g):
            vg = vn[:, g * GM_GC:(g + 1) * GM_GC]
            acc = jnp.zeros((rows, 1), F32)
            for t in range(n_new):
                acc = jnp.where(tok == t, bs_ref[g, t], acc)
            acc = jnp.broadcast_to(acc, (rows, GM_GC))
            for d in range(n_new):
                coef = jnp.zeros((rows, 1), F32)
                for t in range(d, n_new):
                    coef = jnp.where(tok == t, ws_ref[g, t * n_new + (t - d)], coef)
                src = vg if d == 0 else pltpu.roll(vg, d, axis=0)
                acc = acc + coef * src
            return acc

    u = jax.nn.gelu(jnp.dot(h, wio_ref[:, :GM_WIDTH], preferred_element_type=F32))
    gate = jnp.dot(h, wio_ref[:, 2 * GM_WIDTH:], preferred_element_type=F32)
    sv = jnp.concatenate([spatial(g) for g in range(GM_GROUPS)], axis=1)
    z = (u * sv * jax.nn.silu(gate)).astype(BF16)
    y_ref[...] = x1 + jnp.dot(z, woo_ref[...], preferred_element_type=F32)


def _tail(a, r, x2d, woe, gno, wio, gv, ws, bs, woo, tm, n_new):
    m = x2d.shape[0]
    row = lambda n: pl.BlockSpec((tm, n), lambda i: (i, 0))
    if n_new is None:
        mix_specs = [_const_spec(ws.shape), _const_spec(bs.shape)]
        out_specs = [row(D_MODEL)]
        out_shape = [jax.ShapeDtypeStruct((m, D_MODEL), F32)]
    else:
        smem = pl.BlockSpec(memory_space=pltpu.SMEM)
        mix_specs = [smem, smem]
        out_specs = [row(D_MODEL), row(GM_WIDTH)]
        out_shape = [jax.ShapeDtypeStruct((m, D_MODEL), F32), jax.ShapeDtypeStruct((m, GM_WIDTH), F32)]
    return pl.pallas_call(
        functools.partial(_tail_kernel, n_new=n_new),
        grid=(m // tm,),
        in_specs=[row(W_FOX), row(W_RET), row(D_MODEL), _const_spec(woe.shape), _const_spec((1, D_MODEL)),
                  _const_spec(wio.shape), _const_spec((1, GM_WIDTH))] + mix_specs + [_const_spec(woo.shape)],
        out_specs=out_specs, out_shape=out_shape,
        compiler_params=_params(("parallel",)),
        name="tail_prompt" if n_new is None else "tail_sample",
    )(a, r, x2d, woe, gno, wio, gv, ws, bs, woo)


def _rope_tables(pos):
    half = RET_DK // 2
    inv = ROPE_BASE ** (-jnp.arange(half, dtype=F32) / half)
    ang = pos.astype(F32)[:, None] * inv[None, :]
    cos, sin = jnp.cos(ang), jnp.sin(ang)
    return jnp.tile(jnp.concatenate([cos, cos], axis=-1), (1, 2)), jnp.tile(jnp.concatenate([-sin, sin], axis=-1), (1, 2))


def _row_tile(m, largest):
    return next(t for t in (512, 256, 128) if t <= largest and m % t == 0)


def kernel(x_prompt, x_sample, cache_k, cache_v, cache_logf, state_ret, page_table, norm_even, w_in_even,
           b_forget, qnorm, knorm, ret_norm, w_out_even, norm_odd, w_in_odd, vnorm_odd, w_spatial, b_spatial,
           w_out_odd):
    bp, s, _ = x_prompt.shape
    bd, n_new, _ = x_sample.shape
    n_pages = page_table.shape[1]
    past = n_pages * PAGE
    assert s % 256 == 0 and (bd * n_new) % CHUNK == 0 and w_in_even.shape[0] == 1 and w_in_odd.shape[0] == 1

    wi = w_in_even[0]
    sizes = (W_FOX, W_FOX, W_FOX, H_FOX, W_FOX, W_RQK, W_RQK, W_RET, W_RET)
    offs = np.concatenate([[0], np.cumsum(sizes)])
    fq, fk, fv, ff, fg, rq, rk, rv, rg = [wi[:, offs[i]:offs[i + 1]] for i in range(9)]
    w_even = jnp.concatenate([fg, rq, rk, rv, rg], axis=1).astype(BF16)
    w_even_t = jnp.concatenate([fq.T, fk.T, fv.T, ff.T, jnp.zeros((EVEN_ROWS_T - R_FF - H_FOX, D_MODEL), F32)],
                               axis=0).astype(BF16)
    b_f = b_forget[0][:, None]
    g_q = qnorm[0][:, None]
    g_k = knorm[0][:, None]
    g_ret = ret_norm[0].reshape(1, W_RET)
    woe = w_out_even[0].astype(BF16)
    wio = w_in_odd[0].astype(BF16)
    woo = w_out_odd[0].astype(BF16)
    g_ne, g_no, g_v = norm_even[0][None, :], norm_odd[0][None, :], vnorm_odd[0][None, :]

    cos_p, sin_p = _rope_tables(jnp.arange(s))
    xp2 = x_prompt.reshape(bp * s, D_MODEL)
    tm = _row_tile(s, 256)
    sfg, rqp, rkp, rvp, srg, kn, qt, kt32, _, vt32, vtb, lft = _even_proj(
        xp2, bp, g_ne, w_even, w_even_t, b_f, g_q, g_k, cos_p, sin_p, tm)
    rows = bd * n_new
    cos_d, sin_d = _rope_tables(past + jnp.arange(n_new))
    xd2 = x_sample.reshape(rows, D_MODEL)
    sfgd, rqd, rkd, rvd, srgd, _, qtd, kt32d, ktbd, vt32d, vtbd, lftd = _even_proj(
        xd2, 1, g_ne, w_even, w_even_t, b_f, g_q, g_k, jnp.tile(cos_d, (bd, 1)), jnp.tile(sin_d, (bd, 1)), rows)

    ng = H_FOX // HEADS_PER_GROUP
    c_col4 = jnp.transpose(_cumsum_lanes(lft).reshape(bp, ng, HEADS_PER_GROUP, s), (0, 1, 3, 2))
    r3 = lambda t: t.reshape(bp, s, t.shape[-1])
    d3 = lambda t: t.reshape(bd, n_new, t.shape[-1])
    n_phys = cache_k.shape[1]
    pages_t = lambda cache: jnp.transpose(cache[0], (0, 2, 3, 1)).reshape(n_phys, W_FOX, PAGE)
    prompt_args = (qt, r3(kn), vtb, c_col4, r3(sfg))
    sample_args = (page_table, d3(jnp.transpose(qtd[0], (1, 0))), ktbd[0], vtbd[0], lftd[0], d3(sfgd),
                   pages_t(cache_k), pages_t(cache_v), jnp.transpose(cache_logf[0], (0, 2, 1)))
    if bp * ng * (s // FOX_BLK) == bd * _fox_sample_plan(page_table, n_new)[1]:
        a_p, a_d = _fox_fused(*prompt_args, *sample_args)
    else:
        a_p, a_d = _fox_prompt(*prompt_args), _fox_sample(*sample_args)
    r_p, st_p = _ret_prompt(r3(rqp), r3(rkp), r3(rvp), r3(srg), g_ret)
    r_d, st_d = _ret_sample(rqd, rkd, rvd, srgd, g_ret, state_ret[0], n_new)

    y_p, = _tail(a_p.reshape(bp * s, W_FOX), r_p.reshape(bp * s, W_RET), xp2, woe, g_no, wio, g_v,
                 w_spatial[0], jnp.transpose(b_spatial[0]), woo, _row_tile(s, 512), None)
    ws_new = w_spatial[0][:, :n_new, :n_new].reshape(GM_GROUPS, n_new * n_new)
    y_d, gv_d = _tail(a_d.reshape(rows, W_FOX), r_d, xd2, woe, g_no, wio, g_v,
                      ws_new, b_spatial[0][:, :n_new], woo, rows, n_new)

    heads_last = lambda t, n, length: jnp.transpose(t.reshape(n, H_FOX, HEAD_DIM, length), (0, 3, 1, 2))[None]
    tokens_first = lambda t, n: jnp.transpose(t[0], (1, 0)).reshape(1, bd, n_new, *n)
    return (y_p.reshape(bp, s, D_MODEL), y_d.reshape(bd, n_new, D_MODEL),
            heads_last(kt32, bp, s), heads_last(vt32, bp, s),
            jnp.transpose(lft, (0, 2, 1))[None], st_p.reshape(1, bp, H_RET, RET_DK, RET_DV),
            tokens_first(kt32d, (H_FOX, HEAD_DIM)), tokens_first(vt32d, (H_FOX, HEAD_DIM)),
            tokens_first(lftd, (H_FOX,)), st_d[None], gv_d.reshape(1, bd, n_new, GM_WIDTH))
```

```python
import functools

import numpy as np
import jax
import jax.numpy as jnp
from jax import lax
from jax.experimental import pallas as pl
from jax.experimental.pallas import tpu as pltpu

D_MODEL = 1024
HEAD_DIM = 64
H_FOX = 8
W_FOX = H_FOX * HEAD_DIM
H_RET = 8
RET_DK = 64
RET_DV = 128
W_RET = H_RET * RET_DV
W_RQK = H_RET * RET_DK
ROPE_BASE = 10000.0
GM_WIDTH = 2 * D_MODEL
GM_GROUPS = 8
GM_GC = GM_WIDTH // GM_GROUPS
CHUNK = 128
PAGE = 128
EPS = 1e-6
F32 = jnp.float32
BF16 = jnp.bfloat16

LANES = 128
VMEM_LIMIT = 56 * 1024 * 1024

C_FG, C_RQ, C_RK, C_RV, C_RG, EVEN_COLS = 0, 512, 1024, 1536, 2560, 3584
R_FQ, R_FK, R_FV, R_FF, EVEN_ROWS_T = 0, 512, 1024, 1536, 1552

LOG_GAMMA = [float(np.log(np.float32(1.0) - np.float32(2.0) ** np.float32(-5.0 - h))) for h in range(H_RET)]

LOG2E = float(np.log2(np.e))
NT_DIMS = (((1,), (1,)), ((), ()))
TN_DIMS = (((0,), (0,)), ((), ()))


def _params(sem):
    return pltpu.CompilerParams(dimension_semantics=sem, vmem_limit_bytes=VMEM_LIMIT)


def _const_spec(shape):
    nd = len(shape)
    return pl.BlockSpec(shape, lambda *_: (0,) * nd, pipeline_mode=pl.Buffered(1))


def _rms(x, g):
    ms = jnp.mean(x * x, axis=-1, keepdims=True)
    return x * lax.rsqrt(ms + EPS) * g


def _layer_norm(x, g):
    xc = x - jnp.mean(x, axis=-1, keepdims=True)
    return xc * lax.rsqrt(jnp.mean(xc * xc, axis=-1, keepdims=True) + EPS) * g


def _rope(x, cos, sin_signed):
    rows = x.shape[0]
    first_half = (lax.broadcasted_iota(jnp.int32, (rows, LANES), 1) % RET_DK) < RET_DK // 2
    outs = []
    for c in range(x.shape[1] // LANES):
        xc = x[:, c * LANES:(c + 1) * LANES]
        nxt = pltpu.roll(xc, LANES - RET_DK // 2, axis=1)
        prv = pltpu.roll(xc, RET_DK // 2, axis=1)
        outs.append(xc * cos + jnp.where(first_half, nxt, prv) * sin_signed)
    return jnp.concatenate(outs, axis=-1)


def _log_sigmoid(x):
    return jnp.minimum(x, 0.0) - jnp.log1p(jnp.exp(-jnp.abs(x)))


def _head_rms_t(x, g):
    tokens = x.shape[1]
    x = x.reshape(H_FOX, HEAD_DIM, tokens)
    x = x * lax.rsqrt(jnp.mean(x * x, axis=1, keepdims=True) + EPS) * g[None]
    return x.reshape(W_FOX, tokens)


def _even_proj_kernel(x_ref, g_ref, w_ref, wt_ref, bf_ref, gq_ref, gk_ref, cos_ref, sin_ref,
                      sfg_ref, rq_ref, rk_ref, rv_ref, srg_ref, kn_ref,
                      qt_ref, kt32_ref, ktb_ref, vt32_ref, vtb_ref, lft_ref):
    h = _rms(x_ref[...], g_ref[...]).astype(BF16)

    def proj(lo, hi):
        return jnp.dot(h, w_ref[:, lo:hi], preferred_element_type=F32)

    sfg_ref[...] = jax.nn.silu(proj(C_FG, C_RQ)).astype(BF16)
    cos = cos_ref[...]
    sin = sin_ref[...]
    rq_ref[...] = _rope(proj(C_RQ, C_RK), cos, sin).astype(BF16)
    rk_ref[...] = (_rope(proj(C_RK, C_RV), cos, sin) * (RET_DK ** -0.5)).astype(BF16)
    rv_ref[...] = proj(C_RV, C_RG).astype(BF16)
    srg_ref[...] = jax.nn.silu(proj(C_RG, EVEN_COLS)).astype(BF16)

    t = lax.dot_general(wt_ref[...], h, NT_DIMS, preferred_element_type=F32)
    qt_ref[0] = (_head_rms_t(t[R_FQ:R_FK], gq_ref[...]) * (HEAD_DIM ** -0.5 * LOG2E)).astype(BF16)
    kt = _head_rms_t(t[R_FK:R_FV], gk_ref[...])
    kt32_ref[0] = kt
    ktb_ref[0] = kt.astype(BF16)
    kn_ref[...] = kt.T.astype(BF16)
    vt = t[R_FV:R_FF]
    vt32_ref[0] = vt
    vtb_ref[0] = vt.astype(BF16)
    lft_ref[0] = _log_sigmoid(t[R_FF:R_FF + H_FOX] + bf_ref[...])


def _even_proj(x2d, n_seq, g_norm, w, wt, b_f, g_q, g_k, cos, sin, tm):
    m = x2d.shape[0]
    s = m // n_seq
    per_seq = s // tm
    row = lambda n: pl.BlockSpec((tm, n), lambda i: (i, 0))
    tab = pl.BlockSpec((tm, LANES), lambda i: (i % per_seq, 0))
    col = lambda n: pl.BlockSpec((1, n, tm), lambda i: (i // per_seq, 0, i % per_seq))
    outs = [(W_FOX, BF16), (W_RQK, BF16), (W_RQK, BF16), (W_RET, BF16), (W_RET, BF16), (W_FOX, BF16)]
    outs_t = [(W_FOX, BF16), (W_FOX, F32), (W_FOX, BF16), (W_FOX, F32), (W_FOX, BF16), (H_FOX, F32)]
    return pl.pallas_call(
        _even_proj_kernel,
        grid=(m // tm,),
        in_specs=[row(D_MODEL), _const_spec((1, D_MODEL)), _const_spec((D_MODEL, EVEN_COLS)),
                  _const_spec((EVEN_ROWS_T, D_MODEL)), _const_spec((H_FOX, 1)), _const_spec((HEAD_DIM, 1)),
                  _const_spec((HEAD_DIM, 1)), tab, tab],
        out_specs=[row(n) for n, _ in outs] + [col(n) for n, _ in outs_t],
        out_shape=([jax.ShapeDtypeStruct((m, n), dt) for n, dt in outs]
                   + [jax.ShapeDtypeStruct((n_seq, n, s), dt) for n, dt in outs_t]),
        compiler_params=_params(("parallel",)),
        name="even_proj",
    )(x2d, g_norm, w, wt, b_f, g_q, g_k, cos, sin)


def _cumsum_kernel(x_ref, o_ref):
    x = x_ref[0]
    n = x.shape[1]
    lane = lax.broadcasted_iota(jnp.int32, x.shape, 1)
    sh = 1
    while sh < n:
        x = x + jnp.where(lane >= sh, pltpu.roll(x, sh, axis=1), 0.0)
        sh *= 2
    o_ref[0] = x


def _cumsum_lanes(x):
    b, h, s = x.shape
    spec = pl.BlockSpec((1, h, s), lambda i: (i, 0, 0))
    return pl.pallas_call(
        _cumsum_kernel, grid=(b,), in_specs=[spec], out_specs=spec,
        out_shape=jax.ShapeDtypeStruct(x.shape, F32),
        compiler_params=_params(("parallel",)), name="logf_cumsum",
    )(x)


HEADS_PER_GROUP = 4
GROUP_W = HEADS_PER_GROUP * HEAD_DIM


def _fox_prompt_kernel(*refs, blk):
    _fox_prompt_body(pl.program_id(2), *refs, blk=blk)


def _fox_prompt_body(i, qt_ref, k_ref, v_ref, ccol_ref, sfg_ref, a_ref,
                     qm_ref, vp_ref, cb_ref, m_ref, acc_ref, s_ref, p_ref, *, blk, side_work=None):
    s_len = k_ref.shape[1]

    @pl.when(i == 0)
    def _():
        ones_row = jnp.where(lax.broadcasted_iota(jnp.int32, (HEAD_DIM, s_len), 0) == 0, 1.0, 0.0).astype(BF16)
        ccol = ccol_ref[0, 0] * (-LOG2E)
        for hh in range(HEADS_PER_GROUP):
            vp_ref[hh, :HEAD_DIM, :] = v_ref[0, hh * HEAD_DIM:(hh + 1) * HEAD_DIM, :]
            vp_ref[hh, HEAD_DIM:, :] = ones_row
            cb_ref[hh] = jnp.broadcast_to(ccol[:, hh:hh + 1], (s_len, LANES))

    if side_work is not None:
        side_work()
    qt = qt_ref[0]
    row_head = lax.broadcasted_iota(jnp.int32, (GROUP_W, blk), 0) // HEAD_DIM
    for hh in range(HEADS_PER_GROUP):
        qm_ref[hh] = jnp.where(row_head == hh, qt, jnp.zeros_like(qt))
    m_ref[...] = jnp.full(m_ref.shape, -1e30, F32)
    acc_ref[...] = jnp.zeros(acc_ref.shape, F32)
    strips = [(hh, slice(c * LANES, (c + 1) * LANES), c) for hh in range(HEADS_PER_GROUP) for c in range(blk // LANES)]

    def raw_scores(j, slot):
        off = pl.multiple_of(j * blk, blk)
        kb = k_ref[0, pl.ds(off, blk), :]
        for hh in range(HEADS_PER_GROUP):
            bias = cb_ref[hh, pl.ds(off, blk), :]
            s_ref[slot, hh] = (jnp.dot(kb, qm_ref[hh], preferred_element_type=F32)
                               + jnp.concatenate([bias] * (blk // LANES), axis=1))

    def consume(j, slot, masked):
        off = pl.multiple_of(j * blk, blk)

        def scores(hh, cols, c):
            s = s_ref[slot, hh, :, cols]
            if masked:
                causal = (lax.broadcasted_iota(jnp.int32, (blk, LANES), 0)
                          <= lax.broadcasted_iota(jnp.int32, (blk, LANES), 1) + c * LANES)
                s = jnp.where(causal, s, -jnp.inf)
            return s

        rescale = {}
        for hh, cols, c in strips:
            s = scores(hh, cols, c)
            m_old = m_ref[hh, :, cols]
            m_new = jnp.maximum(m_old, jnp.max(s, axis=0, keepdims=True))
            p_ref[hh, :, cols] = jnp.exp2(s - m_new).astype(BF16)
            rescale[hh, c] = jnp.exp2(m_old - m_new)
            m_ref[hh, :, cols] = m_new
        for hh in range(HEADS_PER_GROUP):
            alpha = jnp.concatenate([rescale[hh, c] for c in range(blk // LANES)], axis=1)
            pv = jnp.dot(vp_ref[hh, :, pl.ds(off, blk)], p_ref[hh], preferred_element_type=F32)
            acc_ref[hh] = alpha * acc_ref[hh] + pv

    raw_scores(0, 0)

    def pair(t, carry):
        j = 2 * t
        raw_scores(j + 1, 1)
        consume(j, 0, False)
        raw_scores(j + 2, 0)
        consume(j + 1, 1, False)
        return carry

    lax.fori_loop(0, i // 2, pair, 0)

    @pl.when(i % 2 == 1)
    def _():
        raw_scores(i, 1)
        consume(i - 1, 0, False)
        consume(i, 1, True)

    @pl.when(i % 2 == 0)
    def _():
        consume(i, 0, True)

    ot = jnp.concatenate([acc_ref[hh, :HEAD_DIM, :] / acc_ref[hh, HEAD_DIM:HEAD_DIM + 1, :]
                          for hh in range(HEADS_PER_GROUP)], axis=0)
    a_ref[0] = (ot.T * sfg_ref[0].astype(F32)).astype(BF16)


FOX_BLK = 256


def _fox_prompt_specs(s, ids):
    blk = FOX_BLK

    def spec(shape, f):
        return pl.BlockSpec(shape, lambda *a: f(*ids(*a)))

    rows = spec((1, blk, GROUP_W), lambda bi, g, i: (bi, i, g))
    in_specs = [spec((1, GROUP_W, blk), lambda bi, g, i: (bi, g, i)),
                spec((1, s, GROUP_W), lambda bi, g, i: (bi, 0, g)),
                spec((1, GROUP_W, s), lambda bi, g, i: (bi, g, 0)),
                spec((1, 1, s, HEADS_PER_GROUP), lambda bi, g, i: (bi, g, 0, 0)),
                rows]
    scratch = [pltpu.VMEM((HEADS_PER_GROUP, GROUP_W, blk), BF16),
               pltpu.VMEM((HEADS_PER_GROUP, LANES, s), BF16),
               pltpu.VMEM((HEADS_PER_GROUP, s, LANES), F32),
               pltpu.VMEM((HEADS_PER_GROUP, 1, blk), F32),
               pltpu.VMEM((HEADS_PER_GROUP, LANES, blk), F32),
               pltpu.VMEM((2, HEADS_PER_GROUP, blk, blk), F32),
               pltpu.VMEM((HEADS_PER_GROUP, blk, blk), BF16)]
    return in_specs, rows, scratch


def _fox_prompt(qt, kn, vtb, ccol, sfg):
    b, s, _ = kn.shape
    in_specs, out_spec, scratch = _fox_prompt_specs(s, lambda bi, g, i: (bi, g, i))
    return pl.pallas_call(
        functools.partial(_fox_prompt_kernel, blk=FOX_BLK),
        grid=(b, H_FOX // HEADS_PER_GROUP, s // FOX_BLK),
        in_specs=in_specs, out_specs=out_spec,
        out_shape=jax.ShapeDtypeStruct((b, s, W_FOX), BF16),
        scratch_shapes=scratch,
        compiler_params=_params(("parallel", "parallel", "arbitrary")),
        name="fox_prompt",
    )(qt, kn, vtb, ccol, sfg)


def _ret_head_out(o, g, srg):
    return (_layer_norm(o, g) * srg.astype(F32)).astype(BF16)


def _ret_prompt_kernel(rq_ref, rk_ref, rv_ref, srg_ref, gret_ref, r_ref, st_ref, dec_ref, kdec_ref, *, chunks):
    L = CHUNK
    c = pl.program_id(1)

    @pl.when(c == 0)
    def _():
        st_ref[...] = jnp.zeros(st_ref.shape, F32)
        diff = (lax.broadcasted_iota(jnp.int32, (L, L), 0)
                - lax.broadcasted_iota(jnp.int32, (L, L), 1)).astype(F32)
        for h in range(H_RET):
            dec_ref[h] = jnp.where(diff >= 0, jnp.exp(LOG_GAMMA[h] * jnp.maximum(diff, 0.0)), 0.0)
        left = (L - 1.0) - lax.broadcasted_iota(jnp.int32, (L, LANES), 0).astype(F32)
        first = lax.broadcasted_iota(jnp.int32, (L, LANES), 1) < RET_DK
        for pp in range(H_RET // 2):
            kdec_ref[pp] = jnp.exp(jnp.where(first, LOG_GAMMA[2 * pp], LOG_GAMMA[2 * pp + 1]) * left)

    pos = lax.broadcasted_iota(jnp.int32, (L, 1), 0).astype(F32)
    lane_half = lax.broadcasted_iota(jnp.int32, (L, LANES), 1) // RET_DK
    for ci in range(chunks):
        rows_c = slice(ci * L, (ci + 1) * L)
        first_stage = []
        for pp in range(H_RET // 2):
            qp = rq_ref[0, rows_c, pp * LANES:(pp + 1) * LANES]
            kp = rk_ref[0, rows_c, pp * LANES:(pp + 1) * LANES]
            s_pair = st_ref[0, pp]
            s_pair_b = s_pair.astype(BF16)
            kd = (kp.astype(F32) * kdec_ref[pp]).astype(BF16)
            for j in range(2):
                h = 2 * pp + j
                qm = jnp.where(lane_half == j, qp, jnp.zeros_like(qp))
                vh = rv_ref[0, rows_c, h * RET_DV:(h + 1) * RET_DV]
                scores = lax.dot_general(qm, kp, NT_DIMS, preferred_element_type=F32)
                cross = jnp.dot(qm, s_pair_b, preferred_element_type=F32)
                upd = lax.dot_general(kd, vh, TN_DIMS, preferred_element_type=F32)
                first_stage.append((scores, cross, upd, vh, s_pair))
        outs = []
        for h, (scores, cross, upd, vh, s_pair) in enumerate(first_stage):
            lg = LOG_GAMMA[h]
            intra = jnp.dot((scores * dec_ref[h]).astype(BF16), vh, preferred_element_type=F32)
            outs.append(intra + cross * jnp.exp(lg * (pos + 1.0)))
            rows = slice((h % 2) * RET_DK, (h % 2 + 1) * RET_DK)
            st_ref[0, h // 2, rows, :] = float(np.exp(lg * L)) * s_pair[rows] + upd[rows]
        for h, o in enumerate(outs):
            cols = slice(h * RET_DV, (h + 1) * RET_DV)
            r_ref[0, rows_c, cols] = _ret_head_out(o, gret_ref[:, cols], srg_ref[0, rows_c, cols])


def _ret_prompt(rq, rk, rv, srg, g_ret):
    b, s, _ = rq.shape
    chunks = next(n for n in (4, 2, 1) if s % (n * CHUNK) == 0)
    rows = chunks * CHUNK
    qk = pl.BlockSpec((1, rows, W_RQK), lambda bi, c: (bi, c, 0))
    wide = pl.BlockSpec((1, rows, W_RET), lambda bi, c: (bi, c, 0))
    st = pl.BlockSpec((1, H_RET // 2, 2 * RET_DK, RET_DV), lambda bi, c: (bi, 0, 0, 0))
    return pl.pallas_call(
        functools.partial(_ret_prompt_kernel, chunks=chunks),
        grid=(b, s // rows),
        in_specs=[qk, qk, wide, wide, _const_spec((1, W_RET))],
        out_specs=[wide, st],
        out_shape=[jax.ShapeDtypeStruct((b, s, W_RET), BF16),
                   jax.ShapeDtypeStruct((b, H_RET // 2, 2 * RET_DK, RET_DV), F32)],
        scratch_shapes=[pltpu.VMEM((H_RET, CHUNK, CHUNK), F32), pltpu.VMEM((H_RET // 2, CHUNK, LANES), F32)],
        compiler_params=_params(("parallel", "arbitrary")),
        name="ret_prompt",
    )(rq, rk, rv, srg, g_ret)


def _ret_sample_kernel(rq_ref, rk_ref, rv_ref, srg_ref, gret_ref, s0_ref, r_ref, s1_ref, *, n_new):
    rows = rq_ref.shape[0]
    nb = rows // n_new
    two_dk = 2 * RET_DK
    ri = lax.broadcasted_iota(jnp.int32, (rows, rows), 0)
    ci = lax.broadcasted_iota(jnp.int32, (rows, rows), 1)
    same = (ri // n_new) == (ci // n_new)
    diff = ((ri % n_new) - (ci % n_new)).astype(F32)
    pos = (lax.broadcasted_iota(jnp.int32, (rows, 1), 0) % n_new).astype(F32)
    lane_half = lax.broadcasted_iota(jnp.int32, (rows, LANES), 1) // RET_DK
    q_sel = (lax.broadcasted_iota(jnp.int32, (rows, nb * two_dk), 1) // two_dk
             == lax.broadcasted_iota(jnp.int32, (rows, nb * two_dk), 0) // n_new)
    k_sel = (lax.broadcasted_iota(jnp.int32, (nb * two_dk, rows), 0) // two_dk
             == lax.broadcasted_iota(jnp.int32, (nb * two_dk, rows), 1) // n_new)
    row_head = (lax.broadcasted_iota(jnp.int32, (nb * two_dk, 1), 0) % two_dk) // RET_DK

    qp = rq_ref[...]
    kp = rk_ref[...]
    s_old = s0_ref[...].reshape(nb * two_dk, RET_DV)
    s_old_b = s_old.astype(BF16)
    upd = []
    for j in range(2):
        h = 2 * pl.program_id(0) + j
        lg = jnp.where(h == 0, LOG_GAMMA[0], 0.0)
        for hh in range(1, H_RET):
            lg = jnp.where(h == hh, LOG_GAMMA[hh], lg)
        qm = jnp.where(lane_half == j, qp, jnp.zeros_like(qp))
        vh = rv_ref[:, j * RET_DV:(j + 1) * RET_DV]
        decay = jnp.where(same & (diff >= 0), jnp.exp(lg * jnp.maximum(diff, 0.0)), 0.0)
        scores = lax.dot_general(qm, kp, NT_DIMS, preferred_element_type=F32) * decay
        intra = jnp.dot(scores.astype(BF16), vh, preferred_element_type=F32)
        q_exp = jnp.where(q_sel, jnp.tile(qm, (1, nb)), jnp.zeros((), BF16))
        cross = jnp.dot(q_exp, s_old_b, preferred_element_type=F32) * jnp.exp(lg * (pos + 1.0))
        kd = kp.astype(F32) * jnp.exp(lg * (n_new - 1.0 - pos))
        kd_exp = jnp.where(k_sel, jnp.tile(kd.T, (nb, 1)), 0.0).astype(BF16)
        keep = jnp.exp(lg * jnp.full((1, 1), float(n_new), F32))
        upd.append((keep, jnp.dot(kd_exp, vh, preferred_element_type=F32)))
        cols = slice(j * RET_DV, (j + 1) * RET_DV)
        r_ref[:, cols] = _ret_head_out(intra + cross, gret_ref[:, cols], srg_ref[:, cols])
    s_new = jnp.where(row_head == 0, upd[0][0] * s_old + upd[0][1], upd[1][0] * s_old + upd[1][1])
    s1_ref[...] = s_new.reshape(s1_ref.shape)


def _ret_sample(rq, rk, rv, srg, g_ret, state, n_new):
    rows = rq.shape[0]
    nb = rows // n_new
    pair = pl.BlockSpec((rows, LANES), lambda p: (0, p))
    wide = pl.BlockSpec((rows, 2 * RET_DV), lambda p: (0, p))
    st = pl.BlockSpec((nb, 2, RET_DK, RET_DV), lambda p: (0, p, 0, 0))
    return pl.pallas_call(
        functools.partial(_ret_sample_kernel, n_new=n_new),
        grid=(H_RET // 2,),
        in_specs=[pair, pair, wide, wide, pl.BlockSpec((1, 2 * RET_DV), lambda p: (0, p)), st],
        out_specs=[wide, st],
        out_shape=[jax.ShapeDtypeStruct((rows, W_RET), BF16),
                   jax.ShapeDtypeStruct(state.shape, F32)],
        compiler_params=_params(("parallel",)),
        name="ret_sample",
    )(rq, rk, rv, srg, g_ret, state)


def _fox_sample_kernel(pt_ref, *refs, n_steps, **kw):
    _fox_sample_body(pl.program_id(0) * n_steps + pl.program_id(1), pl.num_programs(0) * n_steps, n_steps,
                     pt_ref, *refs, **kw)


def _fox_sample_body(step, total_steps, n_steps, pt_ref, q_ref, kn_ref, vn_ref, lfn_ref, sfg_ref, lfc_ref,
                     ck_hbm, cv_hbm, a_ref, qbd_ref, m_ref, l_ref, acc_ref, tail_ref, kbuf, vbuf, sem,
                     *, n_new, pages_per_step, page_groups):
    g = pages_per_step
    b, c = step // n_steps, step % n_steps

    def first_page_of(st):
        return (n_steps - 1 - st % n_steps) * g

    def page_copies(st, slot):
        row, first = st // n_steps, first_page_of(st)
        copies = []
        for p in range(g):
            page = pt_ref[row, first + p]
            copies.append(pltpu.make_async_copy(ck_hbm.at[page], kbuf.at[slot, p], sem.at[slot, 0]))
            copies.append(pltpu.make_async_copy(cv_hbm.at[page], vbuf.at[slot, p], sem.at[slot, 1]))
        return copies

    slot = step % 2
    nxt = jnp.minimum(step + 1, total_steps - 1)

    @pl.when(step == 0)
    def _():
        for cp in page_copies(step, 0):
            cp.start()

    for cp in page_copies(nxt, 1 - slot):
        cp.start()
    for cp in page_copies(step, slot):
        cp.wait()
    first_page = first_page_of(step)
    nq = n_new * H_FOX
    row_tok = lax.broadcasted_iota(jnp.int32, (nq, 1), 0) // H_FOX
    head_of_row = lax.broadcasted_iota(jnp.int32, (H_FOX, W_FOX), 0)
    head_of_lane = lax.broadcasted_iota(jnp.int32, (H_FOX, W_FOX), 1) // HEAD_DIM

    def partial_softmax(s, vt):
        m = jnp.max(s, axis=-1, keepdims=True)
        p = jnp.exp2(s - m)
        return (m, jnp.sum(p, axis=-1, keepdims=True),
                lax.dot_general(p.astype(BF16), vt, NT_DIMS, preferred_element_type=F32))

    def merge(parts):
        m_new = m_ref[...]
        for m, _, _ in parts:
            m_new = jnp.maximum(m_new, m)
        alpha = jnp.exp2(m_ref[...] - m_new)
        l_new = alpha * l_ref[...]
        acc = alpha * acc_ref[...]
        for m, l, o in parts:
            w = jnp.exp2(m - m_new)
            l_new = l_new + w * l
            acc = acc + w * o
        m_ref[...] = m_new
        l_ref[...] = l_new
        acc_ref[...] = acc

    @pl.when(c == 0)
    def _():
        q = q_ref[0].astype(F32)
        for t in range(n_new):
            qbd_ref[t * H_FOX:(t + 1) * H_FOX, :] = jnp.where(
                head_of_lane == head_of_row, jnp.broadcast_to(q[t:t + 1, :], (H_FOX, W_FOX)), 0.0).astype(BF16)
        tail_ref[...] = jnp.zeros(tail_ref.shape, F32)
        m_ref[...] = jnp.full(m_ref.shape, -1e30, F32)
        l_ref[...] = jnp.zeros(l_ref.shape, F32)
        acc_ref[...] = jnp.zeros(acc_ref.shape, F32)
        lane = lax.broadcasted_iota(jnp.int32, (H_FOX, LANES), 1)
        pfx = lfn_ref[...]
        sh = 1
        while sh < n_new:
            pfx = pfx + jnp.where(lane % n_new >= sh, pltpu.roll(pfx, sh, axis=1), 0.0)
            sh *= 2
        col = lax.broadcasted_iota(jnp.int32, (nq, LANES), 1)
        valid = (col // n_new == b) & (col % n_new <= row_tok)
        s = jnp.dot(qbd_ref[...], kn_ref[...], preferred_element_type=F32)
        merge([partial_softmax(jnp.where(valid, s - jnp.tile(pfx, (n_new, 1)) * LOG2E, -jnp.inf), vn_ref[...])])

    lf = jnp.concatenate([lfc_ref[pt_ref[b, first_page + p]] for p in range(g)], axis=0)
    hi = lf.astype(BF16)
    rest = lf - hi.astype(F32)
    mid = rest.astype(BF16)
    low = (rest - mid.astype(F32)).astype(BF16)
    t_idx = lax.broadcasted_iota(jnp.int32, (PAGE, 2 * PAGE), 0)
    j_idx = lax.broadcasted_iota(jnp.int32, (PAGE, 2 * PAGE), 1)
    later_or_all = jnp.where((j_idx >= PAGE) | (t_idx > j_idx), 1.0, 0.0).astype(BF16)
    sums = jnp.dot(jnp.concatenate([hi, mid, low], axis=0), later_or_all, preferred_element_type=F32)
    sums = sums[:g * H_FOX] + sums[g * H_FOX:2 * g * H_FOX] + sums[2 * g * H_FOX:]
    run = tail_ref[...]
    biases = [None] * g
    for p in reversed(range(g)):
        page_rows = slice(p * H_FOX, (p + 1) * H_FOX)
        biases[p] = run + sums[page_rows, :PAGE]
        run = run + sums[page_rows, PAGE:]
    tail_ref[...] = run
    key_bias = jnp.tile(jnp.concatenate(biases, axis=1) * LOG2E, (n_new, 1))
    per = g // page_groups
    scores = []
    for grp in range(page_groups):
        kc = jnp.concatenate([kbuf[slot, p].astype(BF16) for p in range(grp * per, (grp + 1) * per)],
                             axis=1)
        s = jnp.dot(qbd_ref[...], kc, preferred_element_type=F32)
        scores.append(s + key_bias[:, grp * per * PAGE:(grp + 1) * per * PAGE])
    parts = []
    for grp, s in enumerate(scores):
        vc = jnp.concatenate([vbuf[slot, p].astype(BF16) for p in range(grp * per, (grp + 1) * per)], axis=1)
        parts.append(partial_softmax(s, vc))
    merge(parts)

    o = acc_ref[...] / l_ref[...]
    sfg = sfg_ref[0].astype(F32)
    for t in range(n_new):
        ot = jnp.where(head_of_lane == head_of_row, o[t * H_FOX:(t + 1) * H_FOX, :], 0.0)
        a_ref[0, t:t + 1, :] = jnp.sum(ot, axis=0, keepdims=True) * sfg[t:t + 1, :]

    @pl.when(step == total_steps - 1)
    def _():
        for cp in page_copies(nxt, 1 - slot):
            cp.wait()


def _fox_sample(page_table, q, kt_new, vt_new, lft_new, sfg, cache_kt, cache_vt, cache_lft):
    bd, n_new, _ = q.shape
    g, n_steps, kw = _fox_sample_plan(page_table, n_new)
    in_specs, out_spec, scratch = _fox_sample_specs(n_new, g, n_steps, (kt_new, vt_new, lft_new, cache_lft),
                                                    lambda b, c, pt: (b, c, pt))
    grid_spec = pltpu.PrefetchScalarGridSpec(
        num_scalar_prefetch=1, grid=(bd, n_steps), in_specs=in_specs, out_specs=out_spec, scratch_shapes=scratch)
    return pl.pallas_call(
        functools.partial(_fox_sample_kernel, n_steps=n_steps, **kw),
        grid_spec=grid_spec,
        out_shape=jax.ShapeDtypeStruct((bd, n_new, W_FOX), F32),
        compiler_params=_params(("arbitrary", "arbitrary")),
        name="fox_sample",
    )(page_table, q, kt_new, vt_new, lft_new, sfg, cache_lft, cache_kt, cache_vt)


def _fox_sample_plan(page_table, n_new):
    n_pages = page_table.shape[1]
    g = next(c for c in (16, 8, 4, 2, 1) if n_pages % c == 0)
    page_groups = 2 if g >= 2 else 1
    return g, n_pages // g, dict(n_new=n_new, pages_per_step=g, page_groups=page_groups)


def _fox_sample_specs(n_new, g, n_steps, resident, ids):
    nq = n_new * H_FOX

    def spec(shape, f, **kw):
        return pl.BlockSpec(shape, lambda *a: f(*ids(*a)), **kw)

    def tok(n):
        return spec((1, n_new, n), lambda b, c, pt: (b, 0, 0))

    def whole(arr):
        nd = arr.ndim
        return spec(arr.shape, lambda b, c, pt: (0,) * nd, pipeline_mode=pl.Buffered(1))

    kt_new, vt_new, lft_new, cache_lft = resident
    in_memory = pl.BlockSpec(memory_space=pl.ANY)
    in_specs = [tok(W_FOX), whole(kt_new), whole(vt_new), whole(lft_new), tok(W_FOX), whole(cache_lft),
                in_memory, in_memory]
    scratch = [pltpu.VMEM((nq, W_FOX), BF16), pltpu.VMEM((nq, 1), F32), pltpu.VMEM((nq, 1), F32),
               pltpu.VMEM((nq, W_FOX), F32), pltpu.VMEM((H_FOX, LANES), F32),
               pltpu.VMEM((2, g, W_FOX, PAGE), F32), pltpu.VMEM((2, g, W_FOX, PAGE), F32),
               pltpu.SemaphoreType.DMA((2, 2))]
    return in_specs, tok(W_FOX), scratch


def _fox_fused_kernel(pt_ref, *refs, n_prompt_in, n_sample_in, n_prompt_scratch, n_steps, sample_kw):
    refs = list(refs)
    p_in, refs = refs[:n_prompt_in], refs[n_prompt_in:]
    s_in, refs = refs[:n_sample_in], refs[n_sample_in:]
    (a_p, a_d), refs = refs[:2], refs[2:]
    p_scr, s_scr = refs[:n_prompt_scratch], refs[n_prompt_scratch:]
    step = (pl.program_id(0) * pl.num_programs(1) + pl.program_id(1)) * pl.num_programs(2) + pl.program_id(2)
    total = pl.num_programs(0) * pl.num_programs(1) * pl.num_programs(2)
    sample = functools.partial(_fox_sample_body, step, total, n_steps, pt_ref, *s_in, a_d, *s_scr, **sample_kw)
    _fox_prompt_body(pl.program_id(2), *p_in, a_p, *p_scr, blk=FOX_BLK, side_work=sample)


def _fox_fused(qt, kn, vtb, ccol, sfg, page_table, q, kt_new, vt_new, lft_new, sfg_d, cache_kt, cache_vt, cache_lft):
    b, s, _ = kn.shape
    bd, n_new, _ = q.shape
    ng, nqb = H_FOX // HEADS_PER_GROUP, s // FOX_BLK
    g, n_steps, kw = _fox_sample_plan(page_table, n_new)
    assert b * ng * nqb == bd * n_steps
    p_in, p_out, p_scr = _fox_prompt_specs(s, lambda bi, gi, i, pt: (bi, gi, i))

    def sample_ids(bi, gi, i, pt):
        step = (bi * ng + gi) * nqb + i
        return step // n_steps, step % n_steps, pt

    s_in, s_out, s_scr = _fox_sample_specs(n_new, g, n_steps, (kt_new, vt_new, lft_new, cache_lft), sample_ids)
    grid_spec = pltpu.PrefetchScalarGridSpec(
        num_scalar_prefetch=1, grid=(b, ng, nqb), in_specs=p_in + s_in, out_specs=[p_out, s_out],
        scratch_shapes=p_scr + s_scr)
    return pl.pallas_call(
        functools.partial(_fox_fused_kernel, n_prompt_in=len(p_in), n_sample_in=len(s_in),
                          n_prompt_scratch=len(p_scr), n_steps=n_steps, sample_kw=kw),
        grid_spec=grid_spec,
        out_shape=[jax.ShapeDtypeStruct((b, s, W_FOX), BF16), jax.ShapeDtypeStruct((bd, n_new, W_FOX), F32)],
        compiler_params=_params(("arbitrary", "arbitrary", "arbitrary")),
        name="fox_fused",
    )(page_table, qt, kn, vtb, ccol, sfg, q, kt_new, vt_new, lft_new, sfg_d, cache_lft, cache_kt, cache_vt)


TAIL_SUB = 256


def _tail_kernel(a_ref, r_ref, x_ref, woe_ref, gno_ref, wio_ref, gv_ref, ws_ref, bs_ref, woo_ref,
                 *out_refs, n_new):
    tile = x_ref.shape[0]
    sub = min(tile, TAIL_SUB)
    y_ref = out_refs[0]

    def front(rows):
        x1 = (x_ref[rows, :]
              + jnp.dot(a_ref[rows, :].astype(BF16), woe_ref[:W_FOX, :], preferred_element_type=F32)
              + jnp.dot(r_ref[rows, :], woe_ref[W_FOX:, :], preferred_element_type=F32))
        h = _rms(x1, gno_ref[...]).astype(BF16)
        v = jax.nn.gelu(jnp.dot(h, wio_ref[:, GM_WIDTH:2 * GM_WIDTH], preferred_element_type=F32))
        vn = _layer_norm(v, gv_ref[...])
        u = jax.nn.gelu(jnp.dot(h, wio_ref[:, :GM_WIDTH], preferred_element_type=F32))
        gate = jnp.dot(h, wio_ref[:, 2 * GM_WIDTH:], preferred_element_type=F32)
        return x1, vn, u, gate

    def back(rows, x1, vn, u, gate):
        if n_new is None:
            tril = (lax.broadcasted_iota(jnp.int32, (CHUNK, CHUNK), 0)
                    >= lax.broadcasted_iota(jnp.int32, (CHUNK, CHUNK), 1))
            vb = vn.astype(BF16)

            def spatial(g):
                wm = jnp.where(tril, ws_ref[g], 0.0).astype(BF16)
                bias = bs_ref[:, g:g + 1]
                return jnp.concatenate(
                    [jnp.dot(wm, vb[c * CHUNK:(c + 1) * CHUNK, g * GM_GC:(g + 1) * GM_GC],
                             preferred_element_type=F32) + bias for c in range(sub // CHUNK)], axis=0)
        else:
            out_refs[1][rows, :] = vn
            tok = lax.broadcasted_iota(jnp.int32, (sub, 1), 0) % n_new

            def spatial(g):
                vg = vn[:, g * GM_GC:(g + 1) * GM_GC]
                acc = jnp.zeros((sub, 1), F32)
                for t in range(n_new):
                    acc = jnp.where(tok == t, bs_ref[g, t], acc)
                acc = jnp.broadcast_to(acc, (sub, GM_GC))
                for d in range(n_new):
                    coef = jnp.zeros((sub, 1), F32)
                    for t in range(d, n_new):
                        coef = jnp.where(tok == t, ws_ref[g, t * n_new + (t - d)], coef)
                    src = vg if d == 0 else pltpu.roll(vg, d, axis=0)
                    acc = acc + coef * src
                return acc

        sv = jnp.concatenate([spatial(g) for g in range(GM_GROUPS)], axis=1)
        z = (u * sv * jax.nn.silu(gate)).astype(BF16)
        y_ref[rows, :] = x1 + jnp.dot(z, woo_ref[...], preferred_element_type=F32)

    subs = [slice(i * sub, (i + 1) * sub) for i in range(tile // sub)]
    fronts = [front(rows) for rows in subs]
    for rows, vals in zip(subs, fronts):
        back(rows, *vals)


def _tail(a, r, x2d, woe, gno, wio, gv, ws, bs, woo, tm, n_new):
    m = x2d.shape[0]
    row = lambda n: pl.BlockSpec((tm, n), lambda i: (i, 0))
    if n_new is None:
        mix_specs = [_const_spec(ws.shape), _const_spec(bs.shape)]
        out_specs = [row(D_MODEL)]
        out_shape = [jax.ShapeDtypeStruct((m, D_MODEL), F32)]
    else:
        smem = pl.BlockSpec(memory_space=pltpu.SMEM)
        mix_specs = [smem, smem]
        out_specs = [row(D_MODEL), row(GM_WIDTH)]
        out_shape = [jax.ShapeDtypeStruct((m, D_MODEL), F32), jax.ShapeDtypeStruct((m, GM_WIDTH), F32)]
    return pl.pallas_call(
        functools.partial(_tail_kernel, n_new=n_new),
        grid=(m // tm,),
        in_specs=[row(W_FOX), row(W_RET), row(D_MODEL), _const_spec(woe.shape), _const_spec((1, D_MODEL)),
                  _const_spec(wio.shape), _const_spec((1, GM_WIDTH))] + mix_specs + [_const_spec(woo.shape)],
        out_specs=out_specs, out_shape=out_shape,
        compiler_params=_params(("parallel",)),
        name="tail_prompt" if n_new is None else "tail_sample",
    )(a, r, x2d, woe, gno, wio, gv, ws, bs, woo)


def _rope_tables(pos):
    half = RET_DK // 2
    inv = ROPE_BASE ** (-jnp.arange(half, dtype=F32) / half)
    ang = pos.astype(F32)[:, None] * inv[None, :]
    cos, sin = jnp.cos(ang), jnp.sin(ang)
    return jnp.tile(jnp.concatenate([cos, cos], axis=-1), (1, 2)), jnp.tile(jnp.concatenate([-sin, sin], axis=-1), (1, 2))


def _row_tile(m, largest):
    return next(t for t in (512, 256, 128) if t <= largest and m % t == 0)


def kernel(x_prompt, x_sample, cache_k, cache_v, cache_logf, state_ret, page_table, norm_even, w_in_even,
           b_forget, qnorm, knorm, ret_norm, w_out_even, norm_odd, w_in_odd, vnorm_odd, w_spatial, b_spatial,
           w_out_odd):
    bp, s, _ = x_prompt.shape
    bd, n_new, _ = x_sample.shape
    n_pages = page_table.shape[1]
    past = n_pages * PAGE
    assert s % 256 == 0 and (bd * n_new) % CHUNK == 0 and w_in_even.shape[0] == 1 and w_in_odd.shape[0] == 1

    wi = w_in_even[0]
    sizes = (W_FOX, W_FOX, W_FOX, H_FOX, W_FOX, W_RQK, W_RQK, W_RET, W_RET)
    offs = np.concatenate([[0], np.cumsum(sizes)])
    fq, fk, fv, ff, fg, rq, rk, rv, rg = [wi[:, offs[i]:offs[i + 1]] for i in range(9)]
    w_even = jnp.concatenate([fg, rq, rk, rv, rg], axis=1).astype(BF16)
    w_even_t = jnp.concatenate([fq.T, fk.T, fv.T, ff.T, jnp.zeros((EVEN_ROWS_T - R_FF - H_FOX, D_MODEL), F32)],
                               axis=0).astype(BF16)
    b_f = b_forget[0][:, None]
    g_q = qnorm[0][:, None]
    g_k = knorm[0][:, None]
    g_ret = ret_norm[0].reshape(1, W_RET)
    woe = w_out_even[0].astype(BF16)
    wio = w_in_odd[0].astype(BF16)
    woo = w_out_odd[0].astype(BF16)
    g_ne, g_no, g_v = norm_even[0][None, :], norm_odd[0][None, :], vnorm_odd[0][None, :]

    cos_p, sin_p = _rope_tables(jnp.arange(s))
    xp2 = x_prompt.reshape(bp * s, D_MODEL)
    tm = _row_tile(s, 256)
    sfg, rqp, rkp, rvp, srg, kn, qt, kt32, _, vt32, vtb, lft = _even_proj(
        xp2, bp, g_ne, w_even, w_even_t, b_f, g_q, g_k, cos_p, sin_p, tm)
    rows = bd * n_new
    cos_d, sin_d = _rope_tables(past + jnp.arange(n_new))
    xd2 = x_sample.reshape(rows, D_MODEL)
    sfgd, rqd, rkd, rvd, srgd, _, qtd, kt32d, ktbd, vt32d, vtbd, lftd = _even_proj(
        xd2, 1, g_ne, w_even, w_even_t, b_f, g_q, g_k, jnp.tile(cos_d, (bd, 1)), jnp.tile(sin_d, (bd, 1)), rows)

    ng = H_FOX // HEADS_PER_GROUP
    c_col4 = jnp.transpose(_cumsum_lanes(lft).reshape(bp, ng, HEADS_PER_GROUP, s), (0, 1, 3, 2))
    r3 = lambda t: t.reshape(bp, s, t.shape[-1])
    d3 = lambda t: t.reshape(bd, n_new, t.shape[-1])
    n_phys = cache_k.shape[1]
    pages_t = lambda cache: jnp.transpose(cache[0], (0, 2, 3, 1)).reshape(n_phys, W_FOX, PAGE)
    prompt_args = (qt, r3(kn), vtb, c_col4, r3(sfg))
    sample_args = (page_table, d3(jnp.transpose(qtd[0], (1, 0))), ktbd[0], vtbd[0], lftd[0], d3(sfgd),
                   pages_t(cache_k), pages_t(cache_v), jnp.transpose(cache_logf[0], (0, 2, 1)))
    if bp * ng * (s // FOX_BLK) == bd * _fox_sample_plan(page_table, n_new)[1]:
        a_p, a_d = _fox_fused(*prompt_args, *sample_args)
    else:
        a_p, a_d = _fox_prompt(*prompt_args), _fox_sample(*sample_args)
    r_p, st_p = _ret_prompt(r3(rqp), r3(rkp), r3(rvp), r3(srg), g_ret)
    r_d, st_d = _ret_sample(rqd, rkd, rvd, srgd, g_ret, state_ret[0], n_new)

    y_p, = _tail(a_p.reshape(bp * s, W_FOX), r_p.reshape(bp * s, W_RET), xp2, woe, g_no, wio, g_v,
                 w_spatial[0], jnp.transpose(b_spatial[0]), woo, _row_tile(s, 512), None)
    ws_new = w_spatial[0][:, :n_new, :n_new].reshape(GM_GROUPS, n_new * n_new)
    y_d, gv_d = _tail(a_d.reshape(rows, W_FOX), r_d, xd2, woe, g_no, wio, g_v,
                      ws_new, b_spatial[0][:, :n_new], woo, rows, n_new)

    heads_last = lambda t, n, length: jnp.transpose(t.reshape(n, H_FOX, HEAD_DIM, length), (0, 3, 1, 2))[None]
    tokens_first = lambda t, n: jnp.transpose(t[0], (1, 0)).reshape(1, bd, n_new, *n)
    return (y_p.reshape(bp, s, D_MODEL), y_d.reshape(bd, n_new, D_MODEL),
            heads_last(kt32, bp, s), heads_last(vt32, bp, s),
            jnp.transpose(lft, (0, 2, 1))[None], st_p.reshape(1, bp, H_RET, RET_DK, RET_DV),
            tokens_first(kt32d, (H_FOX, HEAD_DIM)), tokens_first(vt32d, (H_FOX, HEAD_DIM)),
            tokens_first(lftd, (H_FOX,)), st_d[None], gv_d.reshape(1, bd, n_new, GM_WIDTH))
```

```python
import functools

import numpy as np
import jax
import jax.numpy as jnp
from jax import lax
from jax.experimental import pallas as pl
from jax.experimental.pallas import tpu as pltpu

D_MODEL = 1024
HEAD_DIM = 64
H_FOX = 8
W_FOX = H_FOX * HEAD_DIM
H_RET = 8
RET_DK = 64
RET_DV = 128
W_RET = H_RET * RET_DV
W_RQK = H_RET * RET_DK
ROPE_BASE = 10000.0
GM_WIDTH = 2 * D_MODEL
GM_GROUPS = 8
GM_GC = GM_WIDTH // GM_GROUPS
CHUNK = 128
PAGE = 128
EPS = 1e-6
F32 = jnp.float32
BF16 = jnp.bfloat16

LANES = 128
VMEM_LIMIT = 56 * 1024 * 1024

C_FG, C_RQ, C_RK, C_RV, C_RG, EVEN_COLS = 0, 512, 1024, 1536, 2560, 3584
R_FQ, R_FK, R_FV, R_FF, EVEN_ROWS_T = 0, 512, 1024, 1536, 1552

LOG_GAMMA = [float(np.log(np.float32(1.0) - np.float32(2.0) ** np.float32(-5.0 - h))) for h in range(H_RET)]

LOG2E = float(np.log2(np.e))
NT_DIMS = (((1,), (1,)), ((), ()))
TN_DIMS = (((0,), (0,)), ((), ()))


def _params(sem):
    return pltpu.CompilerParams(dimension_semantics=sem, vmem_limit_bytes=VMEM_LIMIT)


def _const_spec(shape):
    nd = len(shape)
    return pl.BlockSpec(shape, lambda *_: (0,) * nd, pipeline_mode=pl.Buffered(1))


def _rms(x, g):
    ms = jnp.mean(x * x, axis=-1, keepdims=True)
    return x * lax.rsqrt(ms + EPS) * g


def _layer_norm(x, g):
    xc = x - jnp.mean(x, axis=-1, keepdims=True)
    return xc * lax.rsqrt(jnp.mean(xc * xc, axis=-1, keepdims=True) + EPS) * g


def _rope(x, cos, sin_signed):
    rows = x.shape[0]
    first_half = (lax.broadcasted_iota(jnp.int32, (rows, LANES), 1) % RET_DK) < RET_DK // 2
    outs = []
    for c in range(x.shape[1] // LANES):
        xc = x[:, c * LANES:(c + 1) * LANES]
        nxt = pltpu.roll(xc, LANES - RET_DK // 2, axis=1)
        prv = pltpu.roll(xc, RET_DK // 2, axis=1)
        outs.append(xc * cos + jnp.where(first_half, nxt, prv) * sin_signed)
    return jnp.concatenate(outs, axis=-1)


def _log_sigmoid(x):
    return jnp.minimum(x, 0.0) - jnp.log1p(jnp.exp(-jnp.abs(x)))


def _head_rms_t(x, g):
    tokens = x.shape[1]
    x = x.reshape(H_FOX, HEAD_DIM, tokens)
    x = x * lax.rsqrt(jnp.mean(x * x, axis=1, keepdims=True) + EPS) * g[None]
    return x.reshape(W_FOX, tokens)


PROJ_SUB = 256


def _even_proj_kernel(x_ref, g_ref, w_ref, wt_ref, bf_ref, gq_ref, gk_ref, cos_ref, sin_ref,
                      sfg_ref, rq_ref, rk_ref, rv_ref, srg_ref, kn_ref,
                      qt_ref, kt32_ref, ktb_ref, vt32_ref, vtb_ref, lft_ref):
    tile = x_ref.shape[0]
    sub = min(tile, PROJ_SUB)
    for i in range(tile // sub):
        rows = slice(i * sub, (i + 1) * sub)
        h = _rms(x_ref[rows, :], g_ref[...]).astype(BF16)

        def proj(lo, hi, h=h):
            return jnp.dot(h, w_ref[:, lo:hi], preferred_element_type=F32)

        sfg_ref[rows, :] = jax.nn.silu(proj(C_FG, C_RQ)).astype(BF16)
        cos = cos_ref[rows, :]
        sin = sin_ref[rows, :]
        rq_ref[rows, :] = _rope(proj(C_RQ, C_RK), cos, sin).astype(BF16)
        rk_ref[rows, :] = (_rope(proj(C_RK, C_RV), cos, sin) * (RET_DK ** -0.5)).astype(BF16)
        rv_ref[rows, :] = proj(C_RV, C_RG).astype(BF16)
        srg_ref[rows, :] = jax.nn.silu(proj(C_RG, EVEN_COLS)).astype(BF16)

        t = lax.dot_general(wt_ref[...], h, NT_DIMS, preferred_element_type=F32)
        qt_ref[0, :, rows] = (_head_rms_t(t[R_FQ:R_FK], gq_ref[...]) * (HEAD_DIM ** -0.5 * LOG2E)).astype(BF16)
        kt = _head_rms_t(t[R_FK:R_FV], gk_ref[...])
        kt32_ref[0, :, rows] = kt
        ktb_ref[0, :, rows] = kt.astype(BF16)
        kn_ref[rows, :] = kt.T.astype(BF16)
        vt = t[R_FV:R_FF]
        vt32_ref[0, :, rows] = vt
        vtb_ref[0, :, rows] = vt.astype(BF16)
        lft_ref[0, :, rows] = _log_sigmoid(t[R_FF:R_FF + H_FOX] + bf_ref[...])


def _even_proj(x2d, n_seq, g_norm, w, wt, b_f, g_q, g_k, cos, sin, tm):
    m = x2d.shape[0]
    s = m // n_seq
    per_seq = s // tm
    row = lambda n: pl.BlockSpec((tm, n), lambda i: (i, 0))
    tab = pl.BlockSpec((tm, LANES), lambda i: (i % per_seq, 0))
    col = lambda n: pl.BlockSpec((1, n, tm), lambda i: (i // per_seq, 0, i % per_seq))
    outs = [(W_FOX, BF16), (W_RQK, BF16), (W_RQK, BF16), (W_RET, BF16), (W_RET, BF16), (W_FOX, BF16)]
    outs_t = [(W_FOX, BF16), (W_FOX, F32), (W_FOX, BF16), (W_FOX, F32), (W_FOX, BF16), (H_FOX, F32)]
    return pl.pallas_call(
        _even_proj_kernel,
        grid=(m // tm,),
        in_specs=[row(D_MODEL), _const_spec((1, D_MODEL)), _const_spec((D_MODEL, EVEN_COLS)),
                  _const_spec((EVEN_ROWS_T, D_MODEL)), _const_spec((H_FOX, 1)), _const_spec((HEAD_DIM, 1)),
                  _const_spec((HEAD_DIM, 1)), tab, tab],
        out_specs=[row(n) for n, _ in outs] + [col(n) for n, _ in outs_t],
        out_shape=([jax.ShapeDtypeStruct((m, n), dt) for n, dt in outs]
                   + [jax.ShapeDtypeStruct((n_seq, n, s), dt) for n, dt in outs_t]),
        compiler_params=_params(("parallel",)),
        name="even_proj",
    )(x2d, g_norm, w, wt, b_f, g_q, g_k, cos, sin)


def _cumsum_kernel(x_ref, o_ref):
    x = x_ref[0]
    n = x.shape[1]
    lane = lax.broadcasted_iota(jnp.int32, x.shape, 1)
    sh = 1
    while sh < n:
        x = x + jnp.where(lane >= sh, pltpu.roll(x, sh, axis=1), 0.0)
        sh *= 2
    o_ref[0] = x


def _cumsum_lanes(x):
    b, h, s = x.shape
    spec = pl.BlockSpec((1, h, s), lambda i: (i, 0, 0))
    return pl.pallas_call(
        _cumsum_kernel, grid=(b,), in_specs=[spec], out_specs=spec,
        out_shape=jax.ShapeDtypeStruct(x.shape, F32),
        compiler_params=_params(("parallel",)), name="logf_cumsum",
    )(x)


HEADS_PER_GROUP = 4
GROUP_W = HEADS_PER_GROUP * HEAD_DIM


def _fox_prompt_kernel(*refs, blk):
    _fox_prompt_body(pl.program_id(2), *refs, blk=blk)


def _fox_prompt_body(i, qt_ref, k_ref, v_ref, ccol_ref, sfg_ref, a_ref,
                     qm_ref, vp_ref, cb_ref, m_ref, acc_ref, s_ref, p_ref, *, blk, side_work=None):
    s_len = k_ref.shape[1]

    @pl.when(i == 0)
    def _():
        ones_row = jnp.where(lax.broadcasted_iota(jnp.int32, (HEAD_DIM, s_len), 0) == 0, 1.0, 0.0).astype(BF16)
        ccol = ccol_ref[0, 0] * (-LOG2E)
        for hh in range(HEADS_PER_GROUP):
            vp_ref[hh, :HEAD_DIM, :] = v_ref[0, hh * HEAD_DIM:(hh + 1) * HEAD_DIM, :]
            vp_ref[hh, HEAD_DIM:, :] = ones_row
            cb_ref[hh] = jnp.broadcast_to(ccol[:, hh:hh + 1], (s_len, LANES))

    if side_work is not None:
        side_work()
    qt = qt_ref[0]
    row_head = lax.broadcasted_iota(jnp.int32, (GROUP_W, blk), 0) // HEAD_DIM
    for hh in range(HEADS_PER_GROUP):
        qm_ref[hh] = jnp.where(row_head == hh, qt, jnp.zeros_like(qt))
    m_ref[...] = jnp.full(m_ref.shape, -1e30, F32)
    acc_ref[...] = jnp.zeros(acc_ref.shape, F32)
    strips = [(hh, slice(c * LANES, (c + 1) * LANES), c) for hh in range(HEADS_PER_GROUP) for c in range(blk // LANES)]

    def raw_scores(j, slot):
        off = pl.multiple_of(j * blk, blk)
        kb = k_ref[0, pl.ds(off, blk), :]
        for hh in range(HEADS_PER_GROUP):
            bias = cb_ref[hh, pl.ds(off, blk), :]
            s_ref[slot, hh] = (jnp.dot(kb, qm_ref[hh], preferred_element_type=F32)
                               + jnp.concatenate([bias] * (blk // LANES), axis=1))

    def consume(j, slot, masked):
        off = pl.multiple_of(j * blk, blk)

        def scores(hh, cols, c):
            s = s_ref[slot, hh, :, cols]
            if masked:
                causal = (lax.broadcasted_iota(jnp.int32, (blk, LANES), 0)
                          <= lax.broadcasted_iota(jnp.int32, (blk, LANES), 1) + c * LANES)
                s = jnp.where(causal, s, -jnp.inf)
            return s

        rescale = {}
        for hh, cols, c in strips:
            s = scores(hh, cols, c)
            m_old = m_ref[hh, :, cols]
            m_new = jnp.maximum(m_old, jnp.max(s, axis=0, keepdims=True))
            p_ref[hh, :, cols] = jnp.exp2(s - m_new).astype(BF16)
            rescale[hh, c] = jnp.exp2(m_old - m_new)
            m_ref[hh, :, cols] = m_new
        for hh in range(HEADS_PER_GROUP):
            alpha = jnp.concatenate([rescale[hh, c] for c in range(blk // LANES)], axis=1)
            pv = jnp.dot(vp_ref[hh, :, pl.ds(off, blk)], p_ref[hh], preferred_element_type=F32)
            acc_ref[hh] = alpha * acc_ref[hh] + pv

    raw_scores(0, 0)

    def pair(t, carry):
        j = 2 * t
        raw_scores(j + 1, 1)
        consume(j, 0, False)
        raw_scores(j + 2, 0)
        consume(j + 1, 1, False)
        return carry

    lax.fori_loop(0, i // 2, pair, 0)

    @pl.when(i % 2 == 1)
    def _():
        raw_scores(i, 1)
        consume(i - 1, 0, False)
        consume(i, 1, True)

    @pl.when(i % 2 == 0)
    def _():
        consume(i, 0, True)

    ot = jnp.concatenate([acc_ref[hh, :HEAD_DIM, :] / acc_ref[hh, HEAD_DIM:HEAD_DIM + 1, :]
                          for hh in range(HEADS_PER_GROUP)], axis=0)
    a_ref[0] = (ot.T * sfg_ref[0].astype(F32)).astype(BF16)


FOX_BLK = 256


def _fox_prompt_specs(s, ids):
    blk = FOX_BLK

    def spec(shape, f):
        return pl.BlockSpec(shape, lambda *a: f(*ids(*a)))

    rows = spec((1, blk, GROUP_W), lambda bi, g, i: (bi, i, g))
    in_specs = [spec((1, GROUP_W, blk), lambda bi, g, i: (bi, g, i)),
                spec((1, s, GROUP_W), lambda bi, g, i: (bi, 0, g)),
                spec((1, GROUP_W, s), lambda bi, g, i: (bi, g, 0)),
                spec((1, 1, s, HEADS_PER_GROUP), lambda bi, g, i: (bi, g, 0, 0)),
                rows]
    scratch = [pltpu.VMEM((HEADS_PER_GROUP, GROUP_W, blk), BF16),
               pltpu.VMEM((HEADS_PER_GROUP, LANES, s), BF16),
               pltpu.VMEM((HEADS_PER_GROUP, s, LANES), F32),
               pltpu.VMEM((HEADS_PER_GROUP, 1, blk), F32),
               pltpu.VMEM((HEADS_PER_GROUP, LANES, blk), F32),
               pltpu.VMEM((2, HEADS_PER_GROUP, blk, blk), F32),
               pltpu.VMEM((HEADS_PER_GROUP, blk, blk), BF16)]
    return in_specs, rows, scratch


def _fox_prompt(qt, kn, vtb, ccol, sfg):
    b, s, _ = kn.shape
    in_specs, out_spec, scratch = _fox_prompt_specs(s, lambda bi, g, i: (bi, g, i))
    return pl.pallas_call(
        functools.partial(_fox_prompt_kernel, blk=FOX_BLK),
        grid=(b, H_FOX // HEADS_PER_GROUP, s // FOX_BLK),
        in_specs=in_specs, out_specs=out_spec,
        out_shape=jax.ShapeDtypeStruct((b, s, W_FOX), BF16),
        scratch_shapes=scratch,
        compiler_params=_params(("parallel", "parallel", "arbitrary")),
        name="fox_prompt",
    )(qt, kn, vtb, ccol, sfg)


def _ret_head_out(o, g, srg):
    return (_layer_norm(o, g) * srg.astype(F32)).astype(BF16)


def _ret_prompt_kernel(rq_ref, rk_ref, rv_ref, srg_ref, gret_ref, r_ref, st_ref, dec_ref, kdec_ref, *, chunks):
    L = CHUNK
    c = pl.program_id(1)

    @pl.when(c == 0)
    def _():
        st_ref[...] = jnp.zeros(st_ref.shape, F32)
        diff = (lax.broadcasted_iota(jnp.int32, (L, L), 0)
                - lax.broadcasted_iota(jnp.int32, (L, L), 1)).astype(F32)
        for h in range(H_RET):
            dec_ref[h] = jnp.where(diff >= 0, jnp.exp(LOG_GAMMA[h] * jnp.maximum(diff, 0.0)), 0.0)
        left = (L - 1.0) - lax.broadcasted_iota(jnp.int32, (L, LANES), 0).astype(F32)
        first = lax.broadcasted_iota(jnp.int32, (L, LANES), 1) < RET_DK
        for pp in range(H_RET // 2):
            kdec_ref[pp] = jnp.exp(jnp.where(first, LOG_GAMMA[2 * pp], LOG_GAMMA[2 * pp + 1]) * left)

    pos = lax.broadcasted_iota(jnp.int32, (L, 1), 0).astype(F32)
    lane_half = lax.broadcasted_iota(jnp.int32, (L, LANES), 1) // RET_DK
    for ci in range(chunks):
        rows_c = slice(ci * L, (ci + 1) * L)
        first_stage = []
        for pp in range(H_RET // 2):
            qp = rq_ref[0, rows_c, pp * LANES:(pp + 1) * LANES]
            kp = rk_ref[0, rows_c, pp * LANES:(pp + 1) * LANES]
            s_pair = st_ref[0, pp]
            s_pair_b = s_pair.astype(BF16)
            kd = (kp.astype(F32) * kdec_ref[pp]).astype(BF16)
            for j in range(2):
                h = 2 * pp + j
                qm = jnp.where(lane_half == j, qp, jnp.zeros_like(qp))
                vh = rv_ref[0, rows_c, h * RET_DV:(h + 1) * RET_DV]
                scores = lax.dot_general(qm, kp, NT_DIMS, preferred_element_type=F32)
                cross = jnp.dot(qm, s_pair_b, preferred_element_type=F32)
                upd = lax.dot_general(kd, vh, TN_DIMS, preferred_element_type=F32)
                first_stage.append((scores, cross, upd, vh, s_pair))
        outs = []
        for h, (scores, cross, upd, vh, s_pair) in enumerate(first_stage):
            lg = LOG_GAMMA[h]
            intra = jnp.dot((scores * dec_ref[h]).astype(BF16), vh, preferred_element_type=F32)
            outs.append(intra + cross * jnp.exp(lg * (pos + 1.0)))
            rows = slice((h % 2) * RET_DK, (h % 2 + 1) * RET_DK)
            st_ref[0, h // 2, rows, :] = float(np.exp(lg * L)) * s_pair[rows] + upd[rows]
        for h, o in enumerate(outs):
            cols = slice(h * RET_DV, (h + 1) * RET_DV)
            r_ref[0, rows_c, cols] = _ret_head_out(o, gret_ref[:, cols], srg_ref[0, rows_c, cols])


def _ret_prompt(rq, rk, rv, srg, g_ret):
    b, s, _ = rq.shape
    chunks = next(n for n in (4, 2, 1) if s % (n * CHUNK) == 0)
    rows = chunks * CHUNK
    qk = pl.BlockSpec((1, rows, W_RQK), lambda bi, c: (bi, c, 0))
    wide = pl.BlockSpec((1, rows, W_RET), lambda bi, c: (bi, c, 0))
    st = pl.BlockSpec((1, H_RET // 2, 2 * RET_DK, RET_DV), lambda bi, c: (bi, 0, 0, 0))
    return pl.pallas_call(
        functools.partial(_ret_prompt_kernel, chunks=chunks),
        grid=(b, s // rows),
        in_specs=[qk, qk, wide, wide, _const_spec((1, W_RET))],
        out_specs=[wide, st],
        out_shape=[jax.ShapeDtypeStruct((b, s, W_RET), BF16),
                   jax.ShapeDtypeStruct((b, H_RET // 2, 2 * RET_DK, RET_DV), F32)],
        scratch_shapes=[pltpu.VMEM((H_RET, CHUNK, CHUNK), F32), pltpu.VMEM((H_RET // 2, CHUNK, LANES), F32)],
        compiler_params=_params(("parallel", "arbitrary")),
        name="ret_prompt",
    )(rq, rk, rv, srg, g_ret)


def _ret_sample_kernel(rq_ref, rk_ref, rv_ref, srg_ref, gret_ref, s0_ref, r_ref, s1_ref, *, n_new):
    rows = rq_ref.shape[0]
    nb = rows // n_new
    two_dk = 2 * RET_DK
    ri = lax.broadcasted_iota(jnp.int32, (rows, rows), 0)
    ci = lax.broadcasted_iota(jnp.int32, (rows, rows), 1)
    same = (ri // n_new) == (ci // n_new)
    diff = ((ri % n_new) - (ci % n_new)).astype(F32)
    pos = (lax.broadcasted_iota(jnp.int32, (rows, 1), 0) % n_new).astype(F32)
    lane_half = lax.broadcasted_iota(jnp.int32, (rows, LANES), 1) // RET_DK
    q_sel = (lax.broadcasted_iota(jnp.int32, (rows, nb * two_dk), 1) // two_dk
             == lax.broadcasted_iota(jnp.int32, (rows, nb * two_dk), 0) // n_new)
    k_sel = (lax.broadcasted_iota(jnp.int32, (nb * two_dk, rows), 0) // two_dk
             == lax.broadcasted_iota(jnp.int32, (nb * two_dk, rows), 1) // n_new)
    row_head = (lax.broadcasted_iota(jnp.int32, (nb * two_dk, 1), 0) % two_dk) // RET_DK

    qp = rq_ref[...]
    kp = rk_ref[...]
    s_old = s0_ref[...].reshape(nb * two_dk, RET_DV)
    s_old_b = s_old.astype(BF16)
    upd = []
    for j in range(2):
        h = 2 * pl.program_id(0) + j
        lg = jnp.where(h == 0, LOG_GAMMA[0], 0.0)
        for hh in range(1, H_RET):
            lg = jnp.where(h == hh, LOG_GAMMA[hh], lg)
        qm = jnp.where(lane_half == j, qp, jnp.zeros_like(qp))
        vh = rv_ref[:, j * RET_DV:(j + 1) * RET_DV]
        decay = jnp.where(same & (diff >= 0), jnp.exp(lg * jnp.maximum(diff, 0.0)), 0.0)
        scores = lax.dot_general(qm, kp, NT_DIMS, preferred_element_type=F32) * decay
        intra = jnp.dot(scores.astype(BF16), vh, preferred_element_type=F32)
        q_exp = jnp.where(q_sel, jnp.tile(qm, (1, nb)), jnp.zeros((), BF16))
        cross = jnp.dot(q_exp, s_old_b, preferred_element_type=F32) * jnp.exp(lg * (pos + 1.0))
        kd = kp.astype(F32) * jnp.exp(lg * (n_new - 1.0 - pos))
        kd_exp = jnp.where(k_sel, jnp.tile(kd.T, (nb, 1)), 0.0).astype(BF16)
        keep = jnp.exp(lg * jnp.full((1, 1), float(n_new), F32))
        upd.append((keep, jnp.dot(kd_exp, vh, preferred_element_type=F32)))
        cols = slice(j * RET_DV, (j + 1) * RET_DV)
        r_ref[:, cols] = _ret_head_out(intra + cross, gret_ref[:, cols], srg_ref[:, cols])
    s_new = jnp.where(row_head == 0, upd[0][0] * s_old + upd[0][1], upd[1][0] * s_old + upd[1][1])
    s1_ref[...] = s_new.reshape(s1_ref.shape)


def _ret_sample(rq, rk, rv, srg, g_ret, state, n_new):
    rows = rq.shape[0]
    nb = rows // n_new
    pair = pl.BlockSpec((rows, LANES), lambda p: (0, p))
    wide = pl.BlockSpec((rows, 2 * RET_DV), lambda p: (0, p))
    st = pl.BlockSpec((nb, 2, RET_DK, RET_DV), lambda p: (0, p, 0, 0))
    return pl.pallas_call(
        functools.partial(_ret_sample_kernel, n_new=n_new),
        grid=(H_RET // 2,),
        in_specs=[pair, pair, wide, wide, pl.BlockSpec((1, 2 * RET_DV), lambda p: (0, p)), st],
        out_specs=[wide, st],
        out_shape=[jax.ShapeDtypeStruct((rows, W_RET), BF16),
                   jax.ShapeDtypeStruct(state.shape, F32)],
        compiler_params=_params(("parallel",)),
        name="ret_sample",
    )(rq, rk, rv, srg, g_ret, state)


def _fox_sample_kernel(pt_ref, *refs, n_steps, **kw):
    _fox_sample_body(pl.program_id(0) * n_steps + pl.program_id(1), pl.num_programs(0) * n_steps, n_steps,
                     pt_ref, *refs, **kw)


def _fox_sample_body(step, total_steps, n_steps, pt_ref, q_ref, kn_ref, vn_ref, lfn_ref, sfg_ref, lfc_ref,
                     ck_hbm, cv_hbm, a_ref, qbd_ref, m_ref, l_ref, acc_ref, tail_ref, kbuf, vbuf, sem,
                     *, n_new, pages_per_step, page_groups):
    g = pages_per_step
    b, c = step // n_steps, step % n_steps

    def first_page_of(st):
        return (n_steps - 1 - st % n_steps) * g

    def page_copies(st, slot):
        row, first = st // n_steps, first_page_of(st)
        copies = []
        for p in range(g):
            page = pt_ref[row, first + p]
            copies.append(pltpu.make_async_copy(ck_hbm.at[page], kbuf.at[slot, p], sem.at[slot, 0]))
            copies.append(pltpu.make_async_copy(cv_hbm.at[page], vbuf.at[slot, p], sem.at[slot, 1]))
        return copies

    slot = step % 2
    nxt = jnp.minimum(step + 1, total_steps - 1)

    @pl.when(step == 0)
    def _():
        for cp in page_copies(step, 0):
            cp.start()

    for cp in page_copies(nxt, 1 - slot):
        cp.start()
    for cp in page_copies(step, slot):
        cp.wait()
    first_page = first_page_of(step)
    nq = n_new * H_FOX
    row_tok = lax.broadcasted_iota(jnp.int32, (nq, 1), 0) // H_FOX
    head_of_row = lax.broadcasted_iota(jnp.int32, (H_FOX, W_FOX), 0)
    head_of_lane = lax.broadcasted_iota(jnp.int32, (H_FOX, W_FOX), 1) // HEAD_DIM

    def partial_softmax(s, vt):
        m = jnp.max(s, axis=-1, keepdims=True)
        p = jnp.exp2(s - m)
        return (m, jnp.sum(p, axis=-1, keepdims=True),
                lax.dot_general(p.astype(BF16), vt, NT_DIMS, preferred_element_type=F32))

    def merge(parts):
        m_new = m_ref[...]
        for m, _, _ in parts:
            m_new = jnp.maximum(m_new, m)
        alpha = jnp.exp2(m_ref[...] - m_new)
        l_new = alpha * l_ref[...]
        acc = alpha * acc_ref[...]
        for m, l, o in parts:
            w = jnp.exp2(m - m_new)
            l_new = l_new + w * l
            acc = acc + w * o
        m_ref[...] = m_new
        l_ref[...] = l_new
        acc_ref[...] = acc

    @pl.when(c == 0)
    def _():
        q = q_ref[0].astype(F32)
        for t in range(n_new):
            qbd_ref[t * H_FOX:(t + 1) * H_FOX, :] = jnp.where(
                head_of_lane == head_of_row, jnp.broadcast_to(q[t:t + 1, :], (H_FOX, W_FOX)), 0.0).astype(BF16)
        tail_ref[...] = jnp.zeros(tail_ref.shape, F32)
        m_ref[...] = jnp.full(m_ref.shape, -1e30, F32)
        l_ref[...] = jnp.zeros(l_ref.shape, F32)
        acc_ref[...] = jnp.zeros(acc_ref.shape, F32)
        lane = lax.broadcasted_iota(jnp.int32, (H_FOX, LANES), 1)
        pfx = lfn_ref[...]
        sh = 1
        while sh < n_new:
            pfx = pfx + jnp.where(lane % n_new >= sh, pltpu.roll(pfx, sh, axis=1), 0.0)
            sh *= 2
        col = lax.broadcasted_iota(jnp.int32, (nq, LANES), 1)
        valid = (col // n_new == b) & (col % n_new <= row_tok)
        s = jnp.dot(qbd_ref[...], kn_ref[...], preferred_element_type=F32)
        merge([partial_softmax(jnp.where(valid, s - jnp.tile(pfx, (n_new, 1)) * LOG2E, -jnp.inf), vn_ref[...])])

    lf = jnp.concatenate([lfc_ref[pt_ref[b, first_page + p]] for p in range(g)], axis=0)
    hi = lf.astype(BF16)
    rest = lf - hi.astype(F32)
    mid = rest.astype(BF16)
    low = (rest - mid.astype(F32)).astype(BF16)
    t_idx = lax.broadcasted_iota(jnp.int32, (PAGE, 2 * PAGE), 0)
    j_idx = lax.broadcasted_iota(jnp.int32, (PAGE, 2 * PAGE), 1)
    later_or_all = jnp.where((j_idx >= PAGE) | (t_idx > j_idx), 1.0, 0.0).astype(BF16)
    sums = jnp.dot(jnp.concatenate([hi, mid, low], axis=0), later_or_all, preferred_element_type=F32)
    sums = sums[:g * H_FOX] + sums[g * H_FOX:2 * g * H_FOX] + sums[2 * g * H_FOX:]
    run = tail_ref[...]
    biases = [None] * g
    for p in reversed(range(g)):
        page_rows = slice(p * H_FOX, (p + 1) * H_FOX)
        biases[p] = run + sums[page_rows, :PAGE]
        run = run + sums[page_rows, PAGE:]
    tail_ref[...] = run
    key_bias = jnp.tile(jnp.concatenate(biases, axis=1) * LOG2E, (n_new, 1))
    per = g // page_groups
    scores = []
    for grp in range(page_groups):
        kc = jnp.concatenate([kbuf[slot, p].astype(BF16) for p in range(grp * per, (grp + 1) * per)],
                             axis=1)
        s = jnp.dot(qbd_ref[...], kc, preferred_element_type=F32)
        scores.append(s + key_bias[:, grp * per * PAGE:(grp + 1) * per * PAGE])
    parts = []
    for grp, s in enumerate(scores):
        vc = jnp.concatenate([vbuf[slot, p].astype(BF16) for p in range(grp * per, (grp + 1) * per)], axis=1)
        parts.append(partial_softmax(s, vc))
    merge(parts)

    o = acc_ref[...] / l_ref[...]
    sfg = sfg_ref[0].astype(F32)
    for t in range(n_new):
        ot = jnp.where(head_of_lane == head_of_row, o[t * H_FOX:(t + 1) * H_FOX, :], 0.0)
        a_ref[0, t:t + 1, :] = jnp.sum(ot, axis=0, keepdims=True) * sfg[t:t + 1, :]

    @pl.when(step == total_steps - 1)
    def _():
        for cp in page_copies(nxt, 1 - slot):
            cp.wait()


def _fox_sample(page_table, q, kt_new, vt_new, lft_new, sfg, cache_kt, cache_vt, cache_lft):
    bd, n_new, _ = q.shape
    g, n_steps, kw = _fox_sample_plan(page_table, n_new)
    in_specs, out_spec, scratch = _fox_sample_specs(n_new, g, n_steps, (kt_new, vt_new, lft_new, cache_lft),
                                                    lambda b, c, pt: (b, c, pt))
    grid_spec = pltpu.PrefetchScalarGridSpec(
        num_scalar_prefetch=1, grid=(bd, n_steps), in_specs=in_specs, out_specs=out_spec, scratch_shapes=scratch)
    return pl.pallas_call(
        functools.partial(_fox_sample_kernel, n_steps=n_steps, **kw),
        grid_spec=grid_spec,
        out_shape=jax.ShapeDtypeStruct((bd, n_new, W_FOX), F32),
        compiler_params=_params(("arbitrary", "arbitrary")),
        name="fox_sample",
    )(page_table, q, kt_new, vt_new, lft_new, sfg, cache_lft, cache_kt, cache_vt)


def _fox_sample_plan(page_table, n_new):
    n_pages = page_table.shape[1]
    g = next(c for c in (16, 8, 4, 2, 1) if n_pages % c == 0)
    page_groups = 2 if g >= 2 else 1
    return g, n_pages // g, dict(n_new=n_new, pages_per_step=g, page_groups=page_groups)


def _fox_sample_specs(n_new, g, n_steps, resident, ids):
    nq = n_new * H_FOX

    def spec(shape, f, **kw):
        return pl.BlockSpec(shape, lambda *a: f(*ids(*a)), **kw)

    def tok(n):
        return spec((1, n_new, n), lambda b, c, pt: (b, 0, 0))

    def whole(arr):
        nd = arr.ndim
        return spec(arr.shape, lambda b, c, pt: (0,) * nd, pipeline_mode=pl.Buffered(1))

    kt_new, vt_new, lft_new, cache_lft = resident
    in_memory = pl.BlockSpec(memory_space=pl.ANY)
    in_specs = [tok(W_FOX), whole(kt_new), whole(vt_new), whole(lft_new), tok(W_FOX), whole(cache_lft),
                in_memory, in_memory]
    scratch = [pltpu.VMEM((nq, W_FOX), BF16), pltpu.VMEM((nq, 1), F32), pltpu.VMEM((nq, 1), F32),
               pltpu.VMEM((nq, W_FOX), F32), pltpu.VMEM((H_FOX, LANES), F32),
               pltpu.VMEM((2, g, W_FOX, PAGE), F32), pltpu.VMEM((2, g, W_FOX, PAGE), F32),
               pltpu.SemaphoreType.DMA((2, 2))]
    return in_specs, tok(W_FOX), scratch


def _fox_fused_kernel(pt_ref, *refs, n_prompt_in, n_sample_in, n_prompt_scratch, n_steps, sample_kw):
    refs = list(refs)
    p_in, refs = refs[:n_prompt_in], refs[n_prompt_in:]
    s_in, refs = refs[:n_sample_in], refs[n_sample_in:]
    (a_p, a_d), refs = refs[:2], refs[2:]
    p_scr, s_scr = refs[:n_prompt_scratch], refs[n_prompt_scratch:]
    step = (pl.program_id(0) * pl.num_programs(1) + pl.program_id(1)) * pl.num_programs(2) + pl.program_id(2)
    total = pl.num_programs(0) * pl.num_programs(1) * pl.num_programs(2)
    sample = functools.partial(_fox_sample_body, step, total, n_steps, pt_ref, *s_in, a_d, *s_scr, **sample_kw)
    _fox_prompt_body(pl.program_id(2), *p_in, a_p, *p_scr, blk=FOX_BLK, side_work=sample)


def _fox_fused(qt, kn, vtb, ccol, sfg, page_table, q, kt_new, vt_new, lft_new, sfg_d, cache_kt, cache_vt, cache_lft):
    b, s, _ = kn.shape
    bd, n_new, _ = q.shape
    ng, nqb = H_FOX // HEADS_PER_GROUP, s // FOX_BLK
    g, n_steps, kw = _fox_sample_plan(page_table, n_new)
    assert b * ng * nqb == bd * n_steps
    p_in, p_out, p_scr = _fox_prompt_specs(s, lambda bi, gi, i, pt: (bi, gi, i))

    def sample_ids(bi, gi, i, pt):
        step = (bi * ng + gi) * nqb + i
        return step // n_steps, step % n_steps, pt

    s_in, s_out, s_scr = _fox_sample_specs(n_new, g, n_steps, (kt_new, vt_new, lft_new, cache_lft), sample_ids)
    grid_spec = pltpu.PrefetchScalarGridSpec(
        num_scalar_prefetch=1, grid=(b, ng, nqb), in_specs=p_in + s_in, out_specs=[p_out, s_out],
        scratch_shapes=p_scr + s_scr)
    return pl.pallas_call(
        functools.partial(_fox_fused_kernel, n_prompt_in=len(p_in), n_sample_in=len(s_in),
                          n_prompt_scratch=len(p_scr), n_steps=n_steps, sample_kw=kw),
        grid_spec=grid_spec,
        out_shape=[jax.ShapeDtypeStruct((b, s, W_FOX), BF16), jax.ShapeDtypeStruct((bd, n_new, W_FOX), F32)],
        compiler_params=_params(("arbitrary", "arbitrary", "arbitrary")),
        name="fox_fused",
    )(page_table, qt, kn, vtb, ccol, sfg, q, kt_new, vt_new, lft_new, sfg_d, cache_lft, cache_kt, cache_vt)


TAIL_SUB = 256


def _tail_kernel(a_ref, r_ref, x_ref, woe_ref, gno_ref, wio_ref, gv_ref, ws_ref, bs_ref, woo_ref,
                 *out_refs, n_new):
    tile = x_ref.shape[0]
    sub = min(tile, TAIL_SUB)
    y_ref = out_refs[0]

    def front(rows):
        x1 = (x_ref[rows, :]
              + jnp.dot(a_ref[rows, :].astype(BF16), woe_ref[:W_FOX, :], preferred_element_type=F32)
              + jnp.dot(r_ref[rows, :], woe_ref[W_FOX:, :], preferred_element_type=F32))
        h = _rms(x1, gno_ref[...]).astype(BF16)
        v = jax.nn.gelu(jnp.dot(h, wio_ref[:, GM_WIDTH:2 * GM_WIDTH], preferred_element_type=F32))
        vn = _layer_norm(v, gv_ref[...])
        u = jax.nn.gelu(jnp.dot(h, wio_ref[:, :GM_WIDTH], preferred_element_type=F32))
        gate = jnp.dot(h, wio_ref[:, 2 * GM_WIDTH:], preferred_element_type=F32)
        return x1, vn, u, gate

    def back(rows, x1, vn, u, gate):
        if n_new is None:
            tril = (lax.broadcasted_iota(jnp.int32, (CHUNK, CHUNK), 0)
                    >= lax.broadcasted_iota(jnp.int32, (CHUNK, CHUNK), 1))
            vb = vn.astype(BF16)

            def spatial(g):
                wm = jnp.where(tril, ws_ref[g], 0.0).astype(BF16)
                bias = bs_ref[:, g:g + 1]
                return jnp.concatenate(
                    [jnp.dot(wm, vb[c * CHUNK:(c + 1) * CHUNK, g * GM_GC:(g + 1) * GM_GC],
                             preferred_element_type=F32) + bias for c in range(sub // CHUNK)], axis=0)
        else:
            out_refs[1][rows, :] = vn
            tok = lax.broadcasted_iota(jnp.int32, (sub, 1), 0) % n_new

            def spatial(g):
                vg = vn[:, g * GM_GC:(g + 1) * GM_GC]
                acc = jnp.zeros((sub, 1), F32)
                for t in range(n_new):
                    acc = jnp.where(tok == t, bs_ref[g, t], acc)
                acc = jnp.broadcast_to(acc, (sub, GM_GC))
                for d in range(n_new):
                    coef = jnp.zeros((sub, 1), F32)
                    for t in range(d, n_new):
                        coef = jnp.where(tok == t, ws_ref[g, t * n_new + (t - d)], coef)
                    src = vg if d == 0 else pltpu.roll(vg, d, axis=0)
                    acc = acc + coef * src
                return acc

        sv = jnp.concatenate([spatial(g) for g in range(GM_GROUPS)], axis=1)
        z = (u * sv * jax.nn.silu(gate)).astype(BF16)
        y_ref[rows, :] = x1 + jnp.dot(z, woo_ref[...], preferred_element_type=F32)

    subs = [slice(i * sub, (i + 1) * sub) for i in range(tile // sub)]
    fronts = [front(rows) for rows in subs]
    for rows, vals in zip(subs, fronts):
        back(rows, *vals)


def _tail(a, r, x2d, woe, gno, wio, gv, ws, bs, woo, tm, n_new):
    m = x2d.shape[0]
    row = lambda n: pl.BlockSpec((tm, n), lambda i: (i, 0))
    if n_new is None:
        mix_specs = [_const_spec(ws.shape), _const_spec(bs.shape)]
        out_specs = [row(D_MODEL)]
        out_shape = [jax.ShapeDtypeStruct((m, D_MODEL), F32)]
    else:
        smem = pl.BlockSpec(memory_space=pltpu.SMEM)
        mix_specs = [smem, smem]
        out_specs = [row(D_MODEL), row(GM_WIDTH)]
        out_shape = [jax.ShapeDtypeStruct((m, D_MODEL), F32), jax.ShapeDtypeStruct((m, GM_WIDTH), F32)]
    return pl.pallas_call(
        functools.partial(_tail_kernel, n_new=n_new),
        grid=(m // tm,),
        in_specs=[row(W_FOX), row(W_RET), row(D_MODEL), _const_spec(woe.shape), _const_spec((1, D_MODEL)),
                  _const_spec(wio.shape), _const_spec((1, GM_WIDTH))] + mix_specs + [_const_spec(woo.shape)],
        out_specs=out_specs, out_shape=out_shape,
        compiler_params=_params(("parallel",)),
        name="tail_prompt" if n_new is None else "tail_sample",
    )(a, r, x2d, woe, gno, wio, gv, ws, bs, woo)


def _rope_tables(pos):
    half = RET_DK // 2
    inv = ROPE_BASE ** (-jnp.arange(half, dtype=F32) / half)
    ang = pos.astype(F32)[:, None] * inv[None, :]
    cos, sin = jnp.cos(ang), jnp.sin(ang)
    return jnp.tile(jnp.concatenate([cos, cos], axis=-1), (1, 2)), jnp.tile(jnp.concatenate([-sin, sin], axis=-1), (1, 2))


def _row_tile(m, largest):
    return next(t for t in (512, 256, 128) if t <= largest and m % t == 0)


def kernel(x_prompt, x_sample, cache_k, cache_v, cache_logf, state_ret, page_table, norm_even, w_in_even,
           b_forget, qnorm, knorm, ret_norm, w_out_even, norm_odd, w_in_odd, vnorm_odd, w_spatial, b_spatial,
           w_out_odd):
    bp, s, _ = x_prompt.shape
    bd, n_new, _ = x_sample.shape
    n_pages = page_table.shape[1]
    past = n_pages * PAGE
    assert s % 256 == 0 and (bd * n_new) % CHUNK == 0 and w_in_even.shape[0] == 1 and w_in_odd.shape[0] == 1

    wi = w_in_even[0]
    sizes = (W_FOX, W_FOX, W_FOX, H_FOX, W_FOX, W_RQK, W_RQK, W_RET, W_RET)
    offs = np.concatenate([[0], np.cumsum(sizes)])
    fq, fk, fv, ff, fg, rq, rk, rv, rg = [wi[:, offs[i]:offs[i + 1]] for i in range(9)]
    w_even = jnp.concatenate([fg, rq, rk, rv, rg], axis=1).astype(BF16)
    w_even_t = jnp.concatenate([fq.T, fk.T, fv.T, ff.T, jnp.zeros((EVEN_ROWS_T - R_FF - H_FOX, D_MODEL), F32)],
                               axis=0).astype(BF16)
    b_f = b_forget[0][:, None]
    g_q = qnorm[0][:, None]
    g_k = knorm[0][:, None]
    g_ret = ret_norm[0].reshape(1, W_RET)
    woe = w_out_even[0].astype(BF16)
    wio = w_in_odd[0].astype(BF16)
    woo = w_out_odd[0].astype(BF16)
    g_ne, g_no, g_v = norm_even[0][None, :], norm_odd[0][None, :], vnorm_odd[0][None, :]

    cos_p, sin_p = _rope_tables(jnp.arange(s))
    xp2 = x_prompt.reshape(bp * s, D_MODEL)
    tm = _row_tile(s, 512)
    sfg, rqp, rkp, rvp, srg, kn, qt, kt32, _, vt32, vtb, lft = _even_proj(
        xp2, bp, g_ne, w_even, w_even_t, b_f, g_q, g_k, cos_p, sin_p, tm)
    rows = bd * n_new
    cos_d, sin_d = _rope_tables(past + jnp.arange(n_new))
    xd2 = x_sample.reshape(rows, D_MODEL)
    sfgd, rqd, rkd, rvd, srgd, _, qtd, kt32d, ktbd, vt32d, vtbd, lftd = _even_proj(
        xd2, 1, g_ne, w_even, w_even_t, b_f, g_q, g_k, jnp.tile(cos_d, (bd, 1)), jnp.tile(sin_d, (bd, 1)), rows)

    ng = H_FOX // HEADS_PER_GROUP
    c_col4 = jnp.transpose(_cumsum_lanes(lft).reshape(bp, ng, HEADS_PER_GROUP, s), (0, 1, 3, 2))
    r3 = lambda t: t.reshape(bp, s, t.shape[-1])
    d3 = lambda t: t.reshape(bd, n_new, t.shape[-1])
    n_phys = cache_k.shape[1]
    pages_t = lambda cache: jnp.transpose(cache[0], (0, 2, 3, 1)).reshape(n_phys, W_FOX, PAGE)
    prompt_args = (qt, r3(kn), vtb, c_col4, r3(sfg))
    sample_args = (page_table, d3(jnp.transpose(qtd[0], (1, 0))), ktbd[0], vtbd[0], lftd[0], d3(sfgd),
                   pages_t(cache_k), pages_t(cache_v), jnp.transpose(cache_logf[0], (0, 2, 1)))
    if bp * ng * (s // FOX_BLK) == bd * _fox_sample_plan(page_table, n_new)[1]:
        a_p, a_d = _fox_fused(*prompt_args, *sample_args)
    else:
        a_p, a_d = _fox_prompt(*prompt_args), _fox_sample(*sample_args)
    r_p, st_p = _ret_prompt(r3(rqp), r3(rkp), r3(rvp), r3(srg), g_ret)
    r_d, st_d = _ret_sample(rqd, rkd, rvd, srgd, g_ret, state_ret[0], n_new)

    y_p, = _tail(a_p.reshape(bp * s, W_FOX), r_p.reshape(bp * s, W_RET), xp2, woe, g_no, wio, g_v,
                 w_spatial[0], jnp.transpose(b_spatial[0]), woo, _row_tile(s, 512), None)
    ws_new = w_spatial[0][:, :n_new, :n_new].reshape(GM_GROUPS, n_new * n_new)
    y_d, gv_d = _tail(a_d.reshape(rows, W_FOX), r_d, xd2, woe, g_no, wio, g_v,
                      ws_new, b_spatial[0][:, :n_new], woo, rows, n_new)

    heads_last = lambda t, n, length: jnp.transpose(t.reshape(n, H_FOX, HEAD_DIM, length), (0, 3, 1, 2))[None]
    tokens_first = lambda t, n: jnp.transpose(t[0], (1, 0)).reshape(1, bd, n_new, *n)
    return (y_p.reshape(bp, s, D_MODEL), y_d.reshape(bd, n_new, D_MODEL),
            heads_last(kt32, bp, s), heads_last(vt32, bp, s),
            jnp.transpose(lft, (0, 2, 1))[None], st_p.reshape(1, bp, H_RET, RET_DK, RET_DV),
            tokens_first(kt32d, (H_FOX, HEAD_DIM)), tokens_first(vt32d, (H_FOX, HEAD_DIM)),
            tokens_first(lftd, (H_FOX,)), st_d[None], gv_d.reshape(1, bd, n_new, GM_WIDTH))
```

```python
import functools

import numpy as np
import jax
import jax.numpy as jnp
from jax import lax
from jax.experimental import pallas as pl
from jax.experimental.pallas import tpu as pltpu

D_MODEL = 1024
HEAD_DIM = 64
H_FOX = 8
W_FOX = H_FOX * HEAD_DIM
H_RET = 8
RET_DK = 64
RET_DV = 128
W_RET = H_RET * RET_DV
W_RQK = H_RET * RET_DK
ROPE_BASE = 10000.0
GM_WIDTH = 2 * D_MODEL
GM_GROUPS = 8
GM_GC = GM_WIDTH // GM_GROUPS
CHUNK = 128
PAGE = 128
EPS = 1e-6
F32 = jnp.float32
BF16 = jnp.bfloat16

LANES = 128
VMEM_LIMIT = 56 * 1024 * 1024

C_FG, C_RQ, C_RK, C_RV, C_RG, EVEN_COLS = 0, 512, 1024, 1536, 2560, 3584
R_FQ, R_FK, R_FV, R_FF, EVEN_ROWS_T = 0, 512, 1024, 1536, 1552

LOG_GAMMA = [float(np.log(np.float32(1.0) - np.float32(2.0) ** np.float32(-5.0 - h))) for h in range(H_RET)]

LOG2E = float(np.log2(np.e))
NT_DIMS = (((1,), (1,)), ((), ()))
TN_DIMS = (((0,), (0,)), ((), ()))


def _params(sem):
    return pltpu.CompilerParams(dimension_semantics=sem, vmem_limit_bytes=VMEM_LIMIT)


def _const_spec(shape):
    nd = len(shape)
    return pl.BlockSpec(shape, lambda *_: (0,) * nd, pipeline_mode=pl.Buffered(1))


def _rms(x, g):
    ms = jnp.mean(x * x, axis=-1, keepdims=True)
    return x * lax.rsqrt(ms + EPS) * g


def _layer_norm(x, g):
    xc = x - jnp.mean(x, axis=-1, keepdims=True)
    return xc * lax.rsqrt(jnp.mean(xc * xc, axis=-1, keepdims=True) + EPS) * g


def _rope(x, cos, sin_signed):
    rows = x.shape[0]
    first_half = (lax.broadcasted_iota(jnp.int32, (rows, LANES), 1) % RET_DK) < RET_DK // 2
    outs = []
    for c in range(x.shape[1] // LANES):
        xc = x[:, c * LANES:(c + 1) * LANES]
        nxt = pltpu.roll(xc, LANES - RET_DK // 2, axis=1)
        prv = pltpu.roll(xc, RET_DK // 2, axis=1)
        outs.append(xc * cos + jnp.where(first_half, nxt, prv) * sin_signed)
    return jnp.concatenate(outs, axis=-1)


def _log_sigmoid(x):
    return jnp.minimum(x, 0.0) - jnp.log1p(jnp.exp(-jnp.abs(x)))


def _head_rms_t(x, g):
    tokens = x.shape[1]
    x = x.reshape(H_FOX, HEAD_DIM, tokens)
    x = x * lax.rsqrt(jnp.mean(x * x, axis=1, keepdims=True) + EPS) * g[None]
    return x.reshape(W_FOX, tokens)


PROJ_SUB = 256


def _even_proj_kernel(x_ref, g_ref, w_ref, wt_ref, bf_ref, gq_ref, gk_ref, cos_ref, sin_ref,
                      sfg_ref, rq_ref, rk_ref, rv_ref, srg_ref, kn_ref,
                      qt_ref, kt32_ref, ktb_ref, vt32_ref, vtb_ref, lft_ref):
    tile = x_ref.shape[0]
    sub = min(tile, PROJ_SUB)
    for i in range(tile // sub):
        rows = slice(i * sub, (i + 1) * sub)
        h = _rms(x_ref[rows, :], g_ref[...]).astype(BF16)

        def proj(lo, hi, h=h):
            return jnp.dot(h, w_ref[:, lo:hi], preferred_element_type=F32)

        sfg_ref[rows, :] = jax.nn.silu(proj(C_FG, C_RQ)).astype(BF16)
        cos = cos_ref[rows, :]
        sin = sin_ref[rows, :]
        rq_ref[rows, :] = _rope(proj(C_RQ, C_RK), cos, sin).astype(BF16)
        rk_ref[rows, :] = (_rope(proj(C_RK, C_RV), cos, sin) * (RET_DK ** -0.5)).astype(BF16)
        rv_ref[rows, :] = proj(C_RV, C_RG).astype(BF16)
        srg_ref[rows, :] = jax.nn.silu(proj(C_RG, EVEN_COLS)).astype(BF16)

        t = lax.dot_general(wt_ref[...], h, NT_DIMS, preferred_element_type=F32)
        qt_ref[0, :, rows] = (_head_rms_t(t[R_FQ:R_FK], gq_ref[...]) * (HEAD_DIM ** -0.5 * LOG2E)).astype(BF16)
        kt = _head_rms_t(t[R_FK:R_FV], gk_ref[...])
        kt32_ref[0, :, rows] = kt
        ktb_ref[0, :, rows] = kt.astype(BF16)
        kn_ref[rows, :] = kt.T.astype(BF16)
        vt = t[R_FV:R_FF]
        vt32_ref[0, :, rows] = vt
        vtb_ref[0, :, rows] = vt.astype(BF16)
        lft_ref[0, :, rows] = _log_sigmoid(t[R_FF:R_FF + H_FOX] + bf_ref[...])


def _even_proj(x2d, n_seq, g_norm, w, wt, b_f, g_q, g_k, cos, sin, tm):
    m = x2d.shape[0]
    s = m // n_seq
    per_seq = s // tm
    row = lambda n: pl.BlockSpec((tm, n), lambda i: (i, 0))
    tab = pl.BlockSpec((tm, LANES), lambda i: (i % per_seq, 0))
    col = lambda n: pl.BlockSpec((1, n, tm), lambda i: (i // per_seq, 0, i % per_seq))
    outs = [(W_FOX, BF16), (W_RQK, BF16), (W_RQK, BF16), (W_RET, BF16), (W_RET, BF16), (W_FOX, BF16)]
    outs_t = [(W_FOX, BF16), (W_FOX, F32), (W_FOX, BF16), (W_FOX, F32), (W_FOX, BF16), (H_FOX, F32)]
    return pl.pallas_call(
        _even_proj_kernel,
        grid=(m // tm,),
        in_specs=[row(D_MODEL), _const_spec((1, D_MODEL)), _const_spec((D_MODEL, EVEN_COLS)),
                  _const_spec((EVEN_ROWS_T, D_MODEL)), _const_spec((H_FOX, 1)), _const_spec((HEAD_DIM, 1)),
                  _const_spec((HEAD_DIM, 1)), tab, tab],
        out_specs=[row(n) for n, _ in outs] + [col(n) for n, _ in outs_t],
        out_shape=([jax.ShapeDtypeStruct((m, n), dt) for n, dt in outs]
                   + [jax.ShapeDtypeStruct((n_seq, n, s), dt) for n, dt in outs_t]),
        compiler_params=_params(("parallel",)),
        name="even_proj",
    )(x2d, g_norm, w, wt, b_f, g_q, g_k, cos, sin)


def _cumsum_kernel(x_ref, o_ref):
    x = x_ref[0]
    n = x.shape[1]
    lane = lax.broadcasted_iota(jnp.int32, x.shape, 1)
    sh = 1
    while sh < n:
        x = x + jnp.where(lane >= sh, pltpu.roll(x, sh, axis=1), 0.0)
        sh *= 2
    o_ref[0] = x


def _cumsum_lanes(x):
    b, h, s = x.shape
    spec = pl.BlockSpec((1, h, s), lambda i: (i, 0, 0))
    return pl.pallas_call(
        _cumsum_kernel, grid=(b,), in_specs=[spec], out_specs=spec,
        out_shape=jax.ShapeDtypeStruct(x.shape, F32),
        compiler_params=_params(("parallel",)), name="logf_cumsum",
    )(x)


HEADS_PER_GROUP = 4
GROUP_W = HEADS_PER_GROUP * HEAD_DIM


def _fox_prompt_kernel(*refs, blk):
    _fox_prompt_body(pl.program_id(2), *refs, blk=blk)


def _fox_prompt_body(i, qt_ref, k_ref, v_ref, ccol_ref, sfg_ref, a_ref,
                     qm_ref, vp_ref, cb_ref, m_ref, acc_ref, s_ref, p_ref, *, blk, side_work=None):
    s_len = k_ref.shape[1]

    @pl.when(i == 0)
    def _():
        ones_row = jnp.where(lax.broadcasted_iota(jnp.int32, (HEAD_DIM, s_len), 0) == 0, 1.0, 0.0).astype(BF16)
        ccol = ccol_ref[0, 0] * (-LOG2E)
        for hh in range(HEADS_PER_GROUP):
            vp_ref[hh, :HEAD_DIM, :] = v_ref[0, hh * HEAD_DIM:(hh + 1) * HEAD_DIM, :]
            vp_ref[hh, HEAD_DIM:, :] = ones_row
            cb_ref[hh] = jnp.broadcast_to(ccol[:, hh:hh + 1], (s_len, LANES))

    if side_work is not None:
        side_work()
    qt = qt_ref[0]
    row_head = lax.broadcasted_iota(jnp.int32, (GROUP_W, blk), 0) // HEAD_DIM
    for hh in range(HEADS_PER_GROUP):
        qm_ref[hh] = jnp.where(row_head == hh, qt, jnp.zeros_like(qt))
    m_ref[...] = jnp.full(m_ref.shape, -1e30, F32)
    acc_ref[...] = jnp.zeros(acc_ref.shape, F32)
    strips = [(hh, slice(c * LANES, (c + 1) * LANES), c) for hh in range(HEADS_PER_GROUP) for c in range(blk // LANES)]

    def raw_scores(j, slot):
        off = pl.multiple_of(j * blk, blk)
        kb = k_ref[0, pl.ds(off, blk), :]
        for hh in range(HEADS_PER_GROUP):
            bias = cb_ref[hh, pl.ds(off, blk), :]
            s_ref[slot, hh] = (jnp.dot(kb, qm_ref[hh], preferred_element_type=F32)
                               + jnp.concatenate([bias] * (blk // LANES), axis=1))

    def consume(j, slot, masked):
        off = pl.multiple_of(j * blk, blk)

        def scores(hh, cols, c):
            s = s_ref[slot, hh, :, cols]
            if masked:
                causal = (lax.broadcasted_iota(jnp.int32, (blk, LANES), 0)
                          <= lax.broadcasted_iota(jnp.int32, (blk, LANES), 1) + c * LANES)
                s = jnp.where(causal, s, -jnp.inf)
            return s

        rescale = {}
        for hh, cols, c in strips:
            s = scores(hh, cols, c)
            m_old = m_ref[hh, :, cols]
            m_new = jnp.maximum(m_old, jnp.max(s, axis=0, keepdims=True))
            p_ref[hh, :, cols] = jnp.exp2(s - m_new).astype(BF16)
            rescale[hh, c] = jnp.exp2(m_old - m_new)
            m_ref[hh, :, cols] = m_new
        for hh in range(HEADS_PER_GROUP):
            alpha = jnp.concatenate([rescale[hh, c] for c in range(blk // LANES)], axis=1)
            pv = jnp.dot(vp_ref[hh, :, pl.ds(off, blk)], p_ref[hh], preferred_element_type=F32)
            acc_ref[hh] = alpha * acc_ref[hh] + pv

    raw_scores(0, 0)

    def pair(t, carry):
        j = 2 * t
        raw_scores(j + 1, 1)
        consume(j, 0, False)
        raw_scores(j + 2, 0)
        consume(j + 1, 1, False)
        return carry

    lax.fori_loop(0, i // 2, pair, 0)

    @pl.when(i % 2 == 1)
    def _():
        raw_scores(i, 1)
        consume(i - 1, 0, False)
        consume(i, 1, True)

    @pl.when(i % 2 == 0)
    def _():
        consume(i, 0, True)

    ot = jnp.concatenate([acc_ref[hh, :HEAD_DIM, :] / acc_ref[hh, HEAD_DIM:HEAD_DIM + 1, :]
                          for hh in range(HEADS_PER_GROUP)], axis=0)
    a_ref[0] = (ot.T * sfg_ref[0].astype(F32)).astype(BF16)


FOX_BLK = 256


def _fox_prompt_specs(s, ids):
    blk = FOX_BLK

    def spec(shape, f):
        return pl.BlockSpec(shape, lambda *a: f(*ids(*a)))

    rows = spec((1, blk, GROUP_W), lambda bi, g, i: (bi, i, g))
    in_specs = [spec((1, GROUP_W, blk), lambda bi, g, i: (bi, g, i)),
                spec((1, s, GROUP_W), lambda bi, g, i: (bi, 0, g)),
                spec((1, GROUP_W, s), lambda bi, g, i: (bi, g, 0)),
                spec((1, 1, s, HEADS_PER_GROUP), lambda bi, g, i: (bi, g, 0, 0)),
                rows]
    scratch = [pltpu.VMEM((HEADS_PER_GROUP, GROUP_W, blk), BF16),
               pltpu.VMEM((HEADS_PER_GROUP, LANES, s), BF16),
               pltpu.VMEM((HEADS_PER_GROUP, s, LANES), F32),
               pltpu.VMEM((HEADS_PER_GROUP, 1, blk), F32),
               pltpu.VMEM((HEADS_PER_GROUP, LANES, blk), F32),
               pltpu.VMEM((2, HEADS_PER_GROUP, blk, blk), F32),
               pltpu.VMEM((HEADS_PER_GROUP, blk, blk), BF16)]
    return in_specs, rows, scratch


def _fox_prompt(qt, kn, vtb, ccol, sfg):
    b, s, _ = kn.shape
    in_specs, out_spec, scratch = _fox_prompt_specs(s, lambda bi, g, i: (bi, g, i))
    return pl.pallas_call(
        functools.partial(_fox_prompt_kernel, blk=FOX_BLK),
        grid=(b, H_FOX // HEADS_PER_GROUP, s // FOX_BLK),
        in_specs=in_specs, out_specs=out_spec,
        out_shape=jax.ShapeDtypeStruct((b, s, W_FOX), BF16),
        scratch_shapes=scratch,
        compiler_params=_params(("parallel", "parallel", "arbitrary")),
        name="fox_prompt",
    )(qt, kn, vtb, ccol, sfg)


def _ret_head_out(o, g, srg):
    return (_layer_norm(o, g) * srg.astype(F32)).astype(BF16)


def _ret_prompt_kernel(rq_ref, rk_ref, rv_ref, srg_ref, gret_ref, r_ref, st_ref, dec_ref, kdec_ref, *, chunks):
    L = CHUNK
    c = pl.program_id(1)

    @pl.when(c == 0)
    def _():
        st_ref[...] = jnp.zeros(st_ref.shape, F32)
        diff = (lax.broadcasted_iota(jnp.int32, (L, L), 0)
                - lax.broadcasted_iota(jnp.int32, (L, L), 1)).astype(F32)
        for h in range(H_RET):
            dec_ref[h] = jnp.where(diff >= 0, jnp.exp(LOG_GAMMA[h] * jnp.maximum(diff, 0.0)), 0.0)
        left = (L - 1.0) - lax.broadcasted_iota(jnp.int32, (L, LANES), 0).astype(F32)
        first = lax.broadcasted_iota(jnp.int32, (L, LANES), 1) < RET_DK
        for pp in range(H_RET // 2):
            kdec_ref[pp] = jnp.exp(jnp.where(first, LOG_GAMMA[2 * pp], LOG_GAMMA[2 * pp + 1]) * left)

    pos = lax.broadcasted_iota(jnp.int32, (L, 1), 0).astype(F32)
    lane_half = lax.broadcasted_iota(jnp.int32, (L, LANES), 1) // RET_DK
    for ci in range(chunks):
        rows_c = slice(ci * L, (ci + 1) * L)
        first_stage = []
        for pp in range(H_RET // 2):
            qp = rq_ref[0, rows_c, pp * LANES:(pp + 1) * LANES]
            kp = rk_ref[0, rows_c, pp * LANES:(pp + 1) * LANES]
            s_pair = st_ref[0, pp]
            s_pair_b = s_pair.astype(BF16)
            kd = (kp.astype(F32) * kdec_ref[pp]).astype(BF16)
            for j in range(2):
                h = 2 * pp + j
                qm = jnp.where(lane_half == j, qp, jnp.zeros_like(qp))
                vh = rv_ref[0, rows_c, h * RET_DV:(h + 1) * RET_DV]
                scores = lax.dot_general(qm, kp, NT_DIMS, preferred_element_type=F32)
                cross = jnp.dot(qm, s_pair_b, preferred_element_type=F32)
                upd = lax.dot_general(kd, vh, TN_DIMS, preferred_element_type=F32)
                first_stage.append((scores, cross, upd, vh, s_pair))
        outs = []
        for h, (scores, cross, upd, vh, s_pair) in enumerate(first_stage):
            lg = LOG_GAMMA[h]
            intra = jnp.dot((scores * dec_ref[h]).astype(BF16), vh, preferred_element_type=F32)
            outs.append(intra + cross * jnp.exp(lg * (pos + 1.0)))
            rows = slice((h % 2) * RET_DK, (h % 2 + 1) * RET_DK)
            st_ref[0, h // 2, rows, :] = float(np.exp(lg * L)) * s_pair[rows] + upd[rows]
        for h, o in enumerate(outs):
            cols = slice(h * RET_DV, (h + 1) * RET_DV)
            r_ref[0, rows_c, cols] = _ret_head_out(o, gret_ref[:, cols], srg_ref[0, rows_c, cols])


def _ret_prompt(rq, rk, rv, srg, g_ret):
    b, s, _ = rq.shape
    chunks = next(n for n in (8, 4, 2, 1) if s % (n * CHUNK) == 0)
    rows = chunks * CHUNK
    qk = pl.BlockSpec((1, rows, W_RQK), lambda bi, c: (bi, c, 0))
    wide = pl.BlockSpec((1, rows, W_RET), lambda bi, c: (bi, c, 0))
    st = pl.BlockSpec((1, H_RET // 2, 2 * RET_DK, RET_DV), lambda bi, c: (bi, 0, 0, 0))
    return pl.pallas_call(
        functools.partial(_ret_prompt_kernel, chunks=chunks),
        grid=(b, s // rows),
        in_specs=[qk, qk, wide, wide, _const_spec((1, W_RET))],
        out_specs=[wide, st],
        out_shape=[jax.ShapeDtypeStruct((b, s, W_RET), BF16),
                   jax.ShapeDtypeStruct((b, H_RET // 2, 2 * RET_DK, RET_DV), F32)],
        scratch_shapes=[pltpu.VMEM((H_RET, CHUNK, CHUNK), F32), pltpu.VMEM((H_RET // 2, CHUNK, LANES), F32)],
        compiler_params=_params(("parallel", "arbitrary")),
        name="ret_prompt",
    )(rq, rk, rv, srg, g_ret)


def _ret_sample_kernel(rq_ref, rk_ref, rv_ref, srg_ref, gret_ref, s0_ref, r_ref, s1_ref, *, n_new):
    rows = rq_ref.shape[0]
    nb = rows // n_new
    two_dk = 2 * RET_DK
    ri = lax.broadcasted_iota(jnp.int32, (rows, rows), 0)
    ci = lax.broadcasted_iota(jnp.int32, (rows, rows), 1)
    same = (ri // n_new) == (ci // n_new)
    diff = ((ri % n_new) - (ci % n_new)).astype(F32)
    pos = (lax.broadcasted_iota(jnp.int32, (rows, 1), 0) % n_new).astype(F32)
    lane_half = lax.broadcasted_iota(jnp.int32, (rows, LANES), 1) // RET_DK
    q_sel = (lax.broadcasted_iota(jnp.int32, (rows, nb * two_dk), 1) // two_dk
             == lax.broadcasted_iota(jnp.int32, (rows, nb * two_dk), 0) // n_new)
    k_sel = (lax.broadcasted_iota(jnp.int32, (nb * two_dk, rows), 0) // two_dk
             == lax.broadcasted_iota(jnp.int32, (nb * two_dk, rows), 1) // n_new)
    row_head = (lax.broadcasted_iota(jnp.int32, (nb * two_dk, 1), 0) % two_dk) // RET_DK

    qp = rq_ref[...]
    kp = rk_ref[...]
    s_old = s0_ref[...].reshape(nb * two_dk, RET_DV)
    s_old_b = s_old.astype(BF16)
    upd = []
    for j in range(2):
        h = 2 * pl.program_id(0) + j
        lg = jnp.where(h == 0, LOG_GAMMA[0], 0.0)
        for hh in range(1, H_RET):
            lg = jnp.where(h == hh, LOG_GAMMA[hh], lg)
        qm = jnp.where(lane_half == j, qp, jnp.zeros_like(qp))
        vh = rv_ref[:, j * RET_DV:(j + 1) * RET_DV]
        decay = jnp.where(same & (diff >= 0), jnp.exp(lg * jnp.maximum(diff, 0.0)), 0.0)
        scores = lax.dot_general(qm, kp, NT_DIMS, preferred_element_type=F32) * decay
        intra = jnp.dot(scores.astype(BF16), vh, preferred_element_type=F32)
        q_exp = jnp.where(q_sel, jnp.tile(qm, (1, nb)), jnp.zeros((), BF16))
        cross = jnp.dot(q_exp, s_old_b, preferred_element_type=F32) * jnp.exp(lg * (pos + 1.0))
        kd = kp.astype(F32) * jnp.exp(lg * (n_new - 1.0 - pos))
        kd_exp = jnp.where(k_sel, jnp.tile(kd.T, (nb, 1)), 0.0).astype(BF16)
        keep = jnp.exp(lg * jnp.full((1, 1), float(n_new), F32))
        upd.append((keep, jnp.dot(kd_exp, vh, preferred_element_type=F32)))
        cols = slice(j * RET_DV, (j + 1) * RET_DV)
        r_ref[:, cols] = _ret_head_out(intra + cross, gret_ref[:, cols], srg_ref[:, cols])
    s_new = jnp.where(row_head == 0, upd[0][0] * s_old + upd[0][1], upd[1][0] * s_old + upd[1][1])
    s1_ref[...] = s_new.reshape(s1_ref.shape)


def _ret_sample(rq, rk, rv, srg, g_ret, state, n_new):
    rows = rq.shape[0]
    nb = rows // n_new
    pair = pl.BlockSpec((rows, LANES), lambda p: (0, p))
    wide = pl.BlockSpec((rows, 2 * RET_DV), lambda p: (0, p))
    st = pl.BlockSpec((nb, 2, RET_DK, RET_DV), lambda p: (0, p, 0, 0))
    return pl.pallas_call(
        functools.partial(_ret_sample_kernel, n_new=n_new),
        grid=(H_RET // 2,),
        in_specs=[pair, pair, wide, wide, pl.BlockSpec((1, 2 * RET_DV), lambda p: (0, p)), st],
        out_specs=[wide, st],
        out_shape=[jax.ShapeDtypeStruct((rows, W_RET), BF16),
                   jax.ShapeDtypeStruct(state.shape, F32)],
        compiler_params=_params(("parallel",)),
        name="ret_sample",
    )(rq, rk, rv, srg, g_ret, state)


def _fox_sample_kernel(pt_ref, *refs, n_steps, **kw):
    _fox_sample_body(pl.program_id(0) * n_steps + pl.program_id(1), pl.num_programs(0) * n_steps, n_steps,
                     pt_ref, *refs, **kw)


def _fox_sample_body(step, total_steps, n_steps, pt_ref, q_ref, kn_ref, vn_ref, lfn_ref, sfg_ref, lfc_ref,
                     ck_hbm, cv_hbm, a_ref, qbd_ref, m_ref, l_ref, acc_ref, tail_ref, kbuf, vbuf, sem,
                     *, n_new, pages_per_step, page_groups):
    g = pages_per_step
    b, c = step // n_steps, step % n_steps

    def first_page_of(st):
        return (n_steps - 1 - st % n_steps) * g

    def page_copies(st, slot):
        row, first = st // n_steps, first_page_of(st)
        copies = []
        for p in range(g):
            page = pt_ref[row, first + p]
            copies.append(pltpu.make_async_copy(ck_hbm.at[page], kbuf.at[slot, p], sem.at[slot, 0]))
            copies.append(pltpu.make_async_copy(cv_hbm.at[page], vbuf.at[slot, p], sem.at[slot, 1]))
        return copies

    slot = step % 2
    nxt = jnp.minimum(step + 1, total_steps - 1)

    @pl.when(step == 0)
    def _():
        for cp in page_copies(step, 0):
            cp.start()

    for cp in page_copies(nxt, 1 - slot):
        cp.start()
    for cp in page_copies(step, slot):
        cp.wait()
    first_page = first_page_of(step)
    nq = n_new * H_FOX
    row_tok = lax.broadcasted_iota(jnp.int32, (nq, 1), 0) // H_FOX
    head_of_row = lax.broadcasted_iota(jnp.int32, (H_FOX, W_FOX), 0)
    head_of_lane = lax.broadcasted_iota(jnp.int32, (H_FOX, W_FOX), 1) // HEAD_DIM

    def partial_softmax(s, vt):
        m = jnp.max(s, axis=-1, keepdims=True)
        p = jnp.exp2(s - m)
        return (m, jnp.sum(p, axis=-1, keepdims=True),
                lax.dot_general(p.astype(BF16), vt, NT_DIMS, preferred_element_type=F32))

    def merge(parts):
        m_new = m_ref[...]
        for m, _, _ in parts:
            m_new = jnp.maximum(m_new, m)
        alpha = jnp.exp2(m_ref[...] - m_new)
        l_new = alpha * l_ref[...]
        acc = alpha * acc_ref[...]
        for m, l, o in parts:
            w = jnp.exp2(m - m_new)
            l_new = l_new + w * l
            acc = acc + w * o
        m_ref[...] = m_new
        l_ref[...] = l_new
        acc_ref[...] = acc

    @pl.when(c == 0)
    def _():
        q = q_ref[0].astype(F32)
        for t in range(n_new):
            qbd_ref[t * H_FOX:(t + 1) * H_FOX, :] = jnp.where(
                head_of_lane == head_of_row, jnp.broadcast_to(q[t:t + 1, :], (H_FOX, W_FOX)), 0.0).astype(BF16)
        tail_ref[...] = jnp.zeros(tail_ref.shape, F32)
        m_ref[...] = jnp.full(m_ref.shape, -1e30, F32)
        l_ref[...] = jnp.zeros(l_ref.shape, F32)
        acc_ref[...] = jnp.zeros(acc_ref.shape, F32)
        lane = lax.broadcasted_iota(jnp.int32, (H_FOX, LANES), 1)
        pfx = lfn_ref[...]
        sh = 1
        while sh < n_new:
            pfx = pfx + jnp.where(lane % n_new >= sh, pltpu.roll(pfx, sh, axis=1), 0.0)
            sh *= 2
        col = lax.broadcasted_iota(jnp.int32, (nq, LANES), 1)
        valid = (col // n_new == b) & (col % n_new <= row_tok)
        s = jnp.dot(qbd_ref[...], kn_ref[...], preferred_element_type=F32)
        merge([partial_softmax(jnp.where(valid, s - jnp.tile(pfx, (n_new, 1)) * LOG2E, -jnp.inf), vn_ref[...])])

    lf = jnp.concatenate([lfc_ref[pt_ref[b, first_page + p]] for p in range(g)], axis=0)
    hi = lf.astype(BF16)
    rest = lf - hi.astype(F32)
    mid = rest.astype(BF16)
    low = (rest - mid.astype(F32)).astype(BF16)
    t_idx = lax.broadcasted_iota(jnp.int32, (PAGE, 2 * PAGE), 0)
    j_idx = lax.broadcasted_iota(jnp.int32, (PAGE, 2 * PAGE), 1)
    later_or_all = jnp.where((j_idx >= PAGE) | (t_idx > j_idx), 1.0, 0.0).astype(BF16)
    sums = jnp.dot(jnp.concatenate([hi, mid, low], axis=0), later_or_all, preferred_element_type=F32)
    sums = sums[:g * H_FOX] + sums[g * H_FOX:2 * g * H_FOX] + sums[2 * g * H_FOX:]
    run = tail_ref[...]
    biases = [None] * g
    for p in reversed(range(g)):
        page_rows = slice(p * H_FOX, (p + 1) * H_FOX)
        biases[p] = run + sums[page_rows, :PAGE]
        run = run + sums[page_rows, PAGE:]
    tail_ref[...] = run
    key_bias = jnp.tile(jnp.concatenate(biases, axis=1) * LOG2E, (n_new, 1))
    per = g // page_groups
    scores = []
    for grp in range(page_groups):
        kc = jnp.concatenate([kbuf[slot, p].astype(BF16) for p in range(grp * per, (grp + 1) * per)],
                             axis=1)
        s = jnp.dot(qbd_ref[...], kc, preferred_element_type=F32)
        scores.append(s + key_bias[:, grp * per * PAGE:(grp + 1) * per * PAGE])
    parts = []
    for grp, s in enumerate(scores):
        vc = jnp.concatenate([vbuf[slot, p].astype(BF16) for p in range(grp * per, (grp + 1) * per)], axis=1)
        parts.append(partial_softmax(s, vc))
    merge(parts)

    o = acc_ref[...] / l_ref[...]
    sfg = sfg_ref[0].astype(F32)
    for t in range(n_new):
        ot = jnp.where(head_of_lane == head_of_row, o[t * H_FOX:(t + 1) * H_FOX, :], 0.0)
        a_ref[0, t:t + 1, :] = jnp.sum(ot, axis=0, keepdims=True) * sfg[t:t + 1, :]

    @pl.when(step == total_steps - 1)
    def _():
        for cp in page_copies(nxt, 1 - slot):
            cp.wait()


def _fox_sample(page_table, q, kt_new, vt_new, lft_new, sfg, cache_kt, cache_vt, cache_lft):
    bd, n_new, _ = q.shape
    g, n_steps, kw = _fox_sample_plan(page_table, n_new)
    in_specs, out_spec, scratch = _fox_sample_specs(n_new, g, n_steps, (kt_new, vt_new, lft_new, cache_lft),
                                                    lambda b, c, pt: (b, c, pt))
    grid_spec = pltpu.PrefetchScalarGridSpec(
        num_scalar_prefetch=1, grid=(bd, n_steps), in_specs=in_specs, out_specs=out_spec, scratch_shapes=scratch)
    return pl.pallas_call(
        functools.partial(_fox_sample_kernel, n_steps=n_steps, **kw),
        grid_spec=grid_spec,
        out_shape=jax.ShapeDtypeStruct((bd, n_new, W_FOX), F32),
        compiler_params=_params(("arbitrary", "arbitrary")),
        name="fox_sample",
    )(page_table, q, kt_new, vt_new, lft_new, sfg, cache_lft, cache_kt, cache_vt)


def _fox_sample_plan(page_table, n_new):
    n_pages = page_table.shape[1]
    g = next(c for c in (16, 8, 4, 2, 1) if n_pages % c == 0)
    page_groups = 2 if g >= 2 else 1
    return g, n_pages // g, dict(n_new=n_new, pages_per_step=g, page_groups=page_groups)


def _fox_sample_specs(n_new, g, n_steps, resident, ids):
    nq = n_new * H_FOX

    def spec(shape, f, **kw):
        return pl.BlockSpec(shape, lambda *a: f(*ids(*a)), **kw)

    def tok(n):
        return spec((1, n_new, n), lambda b, c, pt: (b, 0, 0))

    def whole(arr):
        nd = arr.ndim
        return spec(arr.shape, lambda b, c, pt: (0,) * nd, pipeline_mode=pl.Buffered(1))

    kt_new, vt_new, lft_new, cache_lft = resident
    in_memory = pl.BlockSpec(memory_space=pl.ANY)
    in_specs = [tok(W_FOX), whole(kt_new), whole(vt_new), whole(lft_new), tok(W_FOX), whole(cache_lft),
                in_memory, in_memory]
    scratch = [pltpu.VMEM((nq, W_FOX), BF16), pltpu.VMEM((nq, 1), F32), pltpu.VMEM((nq, 1), F32),
               pltpu.VMEM((nq, W_FOX), F32), pltpu.VMEM((H_FOX, LANES), F32),
               pltpu.VMEM((2, g, W_FOX, PAGE), F32), pltpu.VMEM((2, g, W_FOX, PAGE), F32),
               pltpu.SemaphoreType.DMA((2, 2))]
    return in_specs, tok(W_FOX), scratch


def _fox_fused_kernel(pt_ref, *refs, n_prompt_in, n_sample_in, n_prompt_scratch, n_steps, sample_kw):
    refs = list(refs)
    p_in, refs = refs[:n_prompt_in], refs[n_prompt_in:]
    s_in, refs = refs[:n_sample_in], refs[n_sample_in:]
    (a_p, a_d), refs = refs[:2], refs[2:]
    p_scr, s_scr = refs[:n_prompt_scratch], refs[n_prompt_scratch:]
    step = (pl.program_id(0) * pl.num_programs(1) + pl.program_id(1)) * pl.num_programs(2) + pl.program_id(2)
    total = pl.num_programs(0) * pl.num_programs(1) * pl.num_programs(2)
    sample = functools.partial(_fox_sample_body, step, total, n_steps, pt_ref, *s_in, a_d, *s_scr, **sample_kw)
    _fox_prompt_body(pl.program_id(2), *p_in, a_p, *p_scr, blk=FOX_BLK, side_work=sample)


def _fox_fused(qt, kn, vtb, ccol, sfg, page_table, q, kt_new, vt_new, lft_new, sfg_d, cache_kt, cache_vt, cache_lft):
    b, s, _ = kn.shape
    bd, n_new, _ = q.shape
    ng, nqb = H_FOX // HEADS_PER_GROUP, s // FOX_BLK
    g, n_steps, kw = _fox_sample_plan(page_table, n_new)
    assert b * ng * nqb == bd * n_steps
    p_in, p_out, p_scr = _fox_prompt_specs(s, lambda bi, gi, i, pt: (bi, gi, i))

    def sample_ids(bi, gi, i, pt):
        step = (bi * ng + gi) * nqb + i
        return step // n_steps, step % n_steps, pt

    s_in, s_out, s_scr = _fox_sample_specs(n_new, g, n_steps, (kt_new, vt_new, lft_new, cache_lft), sample_ids)
    grid_spec = pltpu.PrefetchScalarGridSpec(
        num_scalar_prefetch=1, grid=(b, ng, nqb), in_specs=p_in + s_in, out_specs=[p_out, s_out],
        scratch_shapes=p_scr + s_scr)
    return pl.pallas_call(
        functools.partial(_fox_fused_kernel, n_prompt_in=len(p_in), n_sample_in=len(s_in),
                          n_prompt_scratch=len(p_scr), n_steps=n_steps, sample_kw=kw),
        grid_spec=grid_spec,
        out_shape=[jax.ShapeDtypeStruct((b, s, W_FOX), BF16), jax.ShapeDtypeStruct((bd, n_new, W_FOX), F32)],
        compiler_params=_params(("arbitrary", "arbitrary", "arbitrary")),
        name="fox_fused",
    )(page_table, qt, kn, vtb, ccol, sfg, q, kt_new, vt_new, lft_new, sfg_d, cache_lft, cache_kt, cache_vt)


TAIL_SUB = 256


def _tail_kernel(a_ref, r_ref, x_ref, woe_ref, gno_ref, wio_ref, gv_ref, ws_ref, bs_ref, woo_ref,
                 *out_refs, n_new):
    tile = x_ref.shape[0]
    sub = min(tile, TAIL_SUB)
    y_ref = out_refs[0]

    def front(rows):
        x1 = (x_ref[rows, :]
              + jnp.dot(a_ref[rows, :].astype(BF16), woe_ref[:W_FOX, :], preferred_element_type=F32)
              + jnp.dot(r_ref[rows, :], woe_ref[W_FOX:, :], preferred_element_type=F32))
        h = _rms(x1, gno_ref[...]).astype(BF16)
        v = jax.nn.gelu(jnp.dot(h, wio_ref[:, GM_WIDTH:2 * GM_WIDTH], preferred_element_type=F32))
        vn = _layer_norm(v, gv_ref[...])
        u = jax.nn.gelu(jnp.dot(h, wio_ref[:, :GM_WIDTH], preferred_element_type=F32))
        gate = jnp.dot(h, wio_ref[:, 2 * GM_WIDTH:], preferred_element_type=F32)
        return x1, vn, u, gate

    def back(rows, x1, vn, u, gate):
        if n_new is None:
            tril = (lax.broadcasted_iota(jnp.int32, (CHUNK, CHUNK), 0)
                    >= lax.broadcasted_iota(jnp.int32, (CHUNK, CHUNK), 1))
            vb = vn.astype(BF16)

            def spatial(g):
                wm = jnp.where(tril, ws_ref[g], 0.0).astype(BF16)
                bias = bs_ref[:, g:g + 1]
                return jnp.concatenate(
                    [jnp.dot(wm, vb[c * CHUNK:(c + 1) * CHUNK, g * GM_GC:(g + 1) * GM_GC],
                             preferred_element_type=F32) + bias for c in range(sub // CHUNK)], axis=0)
        else:
            out_refs[1][rows, :] = vn
            tok = lax.broadcasted_iota(jnp.int32, (sub, 1), 0) % n_new

            def spatial(g):
                vg = vn[:, g * GM_GC:(g + 1) * GM_GC]
                acc = jnp.zeros((sub, 1), F32)
                for t in range(n_new):
                    acc = jnp.where(tok == t, bs_ref[g, t], acc)
                acc = jnp.broadcast_to(acc, (sub, GM_GC))
                for d in range(n_new):
                    coef = jnp.zeros((sub, 1), F32)
                    for t in range(d, n_new):
                        coef = jnp.where(tok == t, ws_ref[g, t * n_new + (t - d)], coef)
                    src = vg if d == 0 else pltpu.roll(vg, d, axis=0)
                    acc = acc + coef * src
                return acc

        sv = jnp.concatenate([spatial(g) for g in range(GM_GROUPS)], axis=1)
        z = (u * sv * jax.nn.silu(gate)).astype(BF16)
        y_ref[rows, :] = x1 + jnp.dot(z, woo_ref[...], preferred_element_type=F32)

    subs = [slice(i * sub, (i + 1) * sub) for i in range(tile // sub)]
    fronts = [front(rows) for rows in subs]
    for rows, vals in zip(subs, fronts):
        back(rows, *vals)


def _tail(a, r, x2d, woe, gno, wio, gv, ws, bs, woo, tm, n_new):
    m = x2d.shape[0]
    row = lambda n: pl.BlockSpec((tm, n), lambda i: (i, 0))
    if n_new is None:
        mix_specs = [_const_spec(ws.shape), _const_spec(bs.shape)]
        out_specs = [row(D_MODEL)]
        out_shape = [jax.ShapeDtypeStruct((m, D_MODEL), F32)]
    else:
        smem = pl.BlockSpec(memory_space=pltpu.SMEM)
        mix_specs = [smem, smem]
        out_specs = [row(D_MODEL), row(GM_WIDTH)]
        out_shape = [jax.ShapeDtypeStruct((m, D_MODEL), F32), jax.ShapeDtypeStruct((m, GM_WIDTH), F32)]
    return pl.pallas_call(
        functools.partial(_tail_kernel, n_new=n_new),
        grid=(m // tm,),
        in_specs=[row(W_FOX), row(W_RET), row(D_MODEL), _const_spec(woe.shape), _const_spec((1, D_MODEL)),
                  _const_spec(wio.shape), _const_spec((1, GM_WIDTH))] + mix_specs + [_const_spec(woo.shape)],
        out_specs=out_specs, out_shape=out_shape,
        compiler_params=_params(("parallel",)),
        name="tail_prompt" if n_new is None else "tail_sample",
    )(a, r, x2d, woe, gno, wio, gv, ws, bs, woo)


def _rope_tables(pos):
    half = RET_DK // 2
    inv = ROPE_BASE ** (-jnp.arange(half, dtype=F32) / half)
    ang = pos.astype(F32)[:, None] * inv[None, :]
    cos, sin = jnp.cos(ang), jnp.sin(ang)
    return jnp.tile(jnp.concatenate([cos, cos], axis=-1), (1, 2)), jnp.tile(jnp.concatenate([-sin, sin], axis=-1), (1, 2))


def _row_tile(m, largest):
    return next(t for t in (512, 256, 128) if t <= largest and m % t == 0)


def kernel(x_prompt, x_sample, cache_k, cache_v, cache_logf, state_ret, page_table, norm_even, w_in_even,
           b_forget, qnorm, knorm, ret_norm, w_out_even, norm_odd, w_in_odd, vnorm_odd, w_spatial, b_spatial,
           w_out_odd):
    bp, s, _ = x_prompt.shape
    bd, n_new, _ = x_sample.shape
    n_pages = page_table.shape[1]
    past = n_pages * PAGE
    assert s % 256 == 0 and (bd * n_new) % CHUNK == 0 and w_in_even.shape[0] == 1 and w_in_odd.shape[0] == 1

    wi = w_in_even[0]
    sizes = (W_FOX, W_FOX, W_FOX, H_FOX, W_FOX, W_RQK, W_RQK, W_RET, W_RET)
    offs = np.concatenate([[0], np.cumsum(sizes)])
    fq, fk, fv, ff, fg, rq, rk, rv, rg = [wi[:, offs[i]:offs[i + 1]] for i in range(9)]
    w_even = jnp.concatenate([fg, rq, rk, rv, rg], axis=1).astype(BF16)
    w_even_t = jnp.concatenate([fq.T, fk.T, fv.T, ff.T, jnp.zeros((EVEN_ROWS_T - R_FF - H_FOX, D_MODEL), F32)],
                               axis=0).astype(BF16)
    b_f = b_forget[0][:, None]
    g_q = qnorm[0][:, None]
    g_k = knorm[0][:, None]
    g_ret = ret_norm[0].reshape(1, W_RET)
    woe = w_out_even[0].astype(BF16)
    wio = w_in_odd[0].astype(BF16)
    woo = w_out_odd[0].astype(BF16)
    g_ne, g_no, g_v = norm_even[0][None, :], norm_odd[0][None, :], vnorm_odd[0][None, :]

    cos_p, sin_p = _rope_tables(jnp.arange(s))
    xp2 = x_prompt.reshape(bp * s, D_MODEL)
    tm = _row_tile(s, 512)
    sfg, rqp, rkp, rvp, srg, kn, qt, kt32, _, vt32, vtb, lft = _even_proj(
        xp2, bp, g_ne, w_even, w_even_t, b_f, g_q, g_k, cos_p, sin_p, tm)
    rows = bd * n_new
    cos_d, sin_d = _rope_tables(past + jnp.arange(n_new))
    xd2 = x_sample.reshape(rows, D_MODEL)
    sfgd, rqd, rkd, rvd, srgd, _, qtd, kt32d, ktbd, vt32d, vtbd, lftd = _even_proj(
        xd2, 1, g_ne, w_even, w_even_t, b_f, g_q, g_k, jnp.tile(cos_d, (bd, 1)), jnp.tile(sin_d, (bd, 1)), rows)

    ng = H_FOX // HEADS_PER_GROUP
    c_col4 = jnp.transpose(_cumsum_lanes(lft).reshape(bp, ng, HEADS_PER_GROUP, s), (0, 1, 3, 2))
    r3 = lambda t: t.reshape(bp, s, t.shape[-1])
    d3 = lambda t: t.reshape(bd, n_new, t.shape[-1])
    n_phys = cache_k.shape[1]
    pages_t = lambda cache: jnp.transpose(cache[0], (0, 2, 3, 1)).reshape(n_phys, W_FOX, PAGE)
    prompt_args = (qt, r3(kn), vtb, c_col4, r3(sfg))
    sample_args = (page_table, d3(jnp.transpose(qtd[0], (1, 0))), ktbd[0], vtbd[0], lftd[0], d3(sfgd),
                   pages_t(cache_k), pages_t(cache_v), jnp.transpose(cache_logf[0], (0, 2, 1)))
    if bp * ng * (s // FOX_BLK) == bd * _fox_sample_plan(page_table, n_new)[1]:
        a_p, a_d = _fox_fused(*prompt_args, *sample_args)
    else:
        a_p, a_d = _fox_prompt(*prompt_args), _fox_sample(*sample_args)
    r_p, st_p = _ret_prompt(r3(rqp), r3(rkp), r3(rvp), r3(srg), g_ret)
    r_d, st_d = _ret_sample(rqd, rkd, rvd, srgd, g_ret, state_ret[0], n_new)

    y_p, = _tail(a_p.reshape(bp * s, W_FOX), r_p.reshape(bp * s, W_RET), xp2, woe, g_no, wio, g_v,
                 w_spatial[0], jnp.transpose(b_spatial[0]), woo, _row_tile(s, 512), None)
    ws_new = w_spatial[0][:, :n_new, :n_new].reshape(GM_GROUPS, n_new * n_new)
    y_d, gv_d = _tail(a_d.reshape(rows, W_FOX), r_d, xd2, woe, g_no, wio, g_v,
                      ws_new, b_spatial[0][:, :n_new], woo, rows, n_new)

    heads_last = lambda t, n, length: jnp.transpose(t.reshape(n, H_FOX, HEAD_DIM, length), (0, 3, 1, 2))[None]
    tokens_first = lambda t, n: jnp.transpose(t[0], (1, 0)).reshape(1, bd, n_new, *n)
    return (y_p.reshape(bp, s, D_MODEL), y_d.reshape(bd, n_new, D_MODEL),
            heads_last(kt32, bp, s), heads_last(vt32, bp, s),
            jnp.transpose(lft, (0, 2, 1))[None], st_p.reshape(1, bp, H_RET, RET_DK, RET_DV),
            tokens_first(kt32d, (H_FOX, HEAD_DIM)), tokens_first(vt32d, (H_FOX, HEAD_DIM)),
            tokens_first(lftd, (H_FOX,)), st_d[None], gv_d.reshape(1, bd, n_new, GM_WIDTH))
```

```python
import functools

import numpy as np
import jax
import jax.numpy as jnp
from jax import lax
from jax.experimental import pallas as pl
from jax.experimental.pallas import tpu as pltpu

D_MODEL = 1024
HEAD_DIM = 64
H_FOX = 8
W_FOX = H_FOX * HEAD_DIM
H_RET = 8
RET_DK = 64
RET_DV = 128
W_RET = H_RET * RET_DV
W_RQK = H_RET * RET_DK
ROPE_BASE = 10000.0
GM_WIDTH = 2 * D_MODEL
GM_GROUPS = 8
GM_GC = GM_WIDTH // GM_GROUPS
CHUNK = 128
PAGE = 128
EPS = 1e-6
F32 = jnp.float32
BF16 = jnp.bfloat16

LANES = 128
VMEM_LIMIT = 56 * 1024 * 1024

C_FG, C_RQ, C_RK, C_RV, C_RG, EVEN_COLS = 0, 512, 1024, 1536, 2560, 3584
R_FQ, R_FK, R_FV, R_FF, EVEN_ROWS_T = 0, 512, 1024, 1536, 1552

LOG_GAMMA = [float(np.log(np.float32(1.0) - np.float32(2.0) ** np.float32(-5.0 - h))) for h in range(H_RET)]

LOG2E = float(np.log2(np.e))
NT_DIMS = (((1,), (1,)), ((), ()))
TN_DIMS = (((0,), (0,)), ((), ()))


def _params(sem):
    return pltpu.CompilerParams(dimension_semantics=sem, vmem_limit_bytes=VMEM_LIMIT)


def _const_spec(shape):
    nd = len(shape)
    return pl.BlockSpec(shape, lambda *_: (0,) * nd, pipeline_mode=pl.Buffered(1))


def _rms(x, g):
    ms = jnp.mean(x * x, axis=-1, keepdims=True)
    return x * lax.rsqrt(ms + EPS) * g


def _layer_norm(x, g):
    xc = x - jnp.mean(x, axis=-1, keepdims=True)
    return xc * lax.rsqrt(jnp.mean(xc * xc, axis=-1, keepdims=True) + EPS) * g


def _rope(x, cos, sin_signed):
    rows = x.shape[0]
    first_half = (lax.broadcasted_iota(jnp.int32, (rows, LANES), 1) % RET_DK) < RET_DK // 2
    outs = []
    for c in range(x.shape[1] // LANES):
        xc = x[:, c * LANES:(c + 1) * LANES]
        nxt = pltpu.roll(xc, LANES - RET_DK // 2, axis=1)
        prv = pltpu.roll(xc, RET_DK // 2, axis=1)
        outs.append(xc * cos + jnp.where(first_half, nxt, prv) * sin_signed)
    return jnp.concatenate(outs, axis=-1)


def _log_sigmoid(x):
    return jnp.minimum(x, 0.0) - jnp.log1p(jnp.exp(-jnp.abs(x)))


def _head_rms_t(x, g):
    tokens = x.shape[1]
    x = x.reshape(H_FOX, HEAD_DIM, tokens)
    x = x * lax.rsqrt(jnp.mean(x * x, axis=1, keepdims=True) + EPS) * g[None]
    return x.reshape(W_FOX, tokens)


PROJ_SUB = 256


def _even_proj_kernel(x_ref, g_ref, w_ref, wt_ref, bf_ref, gq_ref, gk_ref, cos_ref, sin_ref,
                      sfg_ref, rq_ref, rk_ref, rv_ref, srg_ref, kn_ref,
                      qt_ref, kt32_ref, ktb_ref, vt32_ref, vtb_ref, lft_ref):
    tile = x_ref.shape[0]
    sub = min(tile, PROJ_SUB)
    for i in range(tile // sub):
        rows = slice(i * sub, (i + 1) * sub)
        h = _rms(x_ref[rows, :], g_ref[...]).astype(BF16)

        def proj(lo, hi, h=h):
            return jnp.dot(h, w_ref[:, lo:hi], preferred_element_type=F32)

        sfg_ref[rows, :] = jax.nn.silu(proj(C_FG, C_RQ)).astype(BF16)
        cos = cos_ref[rows, :]
        sin = sin_ref[rows, :]
        rq_ref[rows, :] = _rope(proj(C_RQ, C_RK), cos, sin).astype(BF16)
        rk_ref[rows, :] = (_rope(proj(C_RK, C_RV), cos, sin) * (RET_DK ** -0.5)).astype(BF16)
        rv_ref[rows, :] = proj(C_RV, C_RG).astype(BF16)
        srg_ref[rows, :] = jax.nn.silu(proj(C_RG, EVEN_COLS)).astype(BF16)

        t = lax.dot_general(wt_ref[...], h, NT_DIMS, preferred_element_type=F32)
        qt_ref[0, :, rows] = (_head_rms_t(t[R_FQ:R_FK], gq_ref[...]) * (HEAD_DIM ** -0.5 * LOG2E)).astype(BF16)
        kt = _head_rms_t(t[R_FK:R_FV], gk_ref[...])
        kt32_ref[0, :, rows] = kt
        ktb_ref[0, :, rows] = kt.astype(BF16)
        kn_ref[rows, :] = kt.T.astype(BF16)
        vt = t[R_FV:R_FF]
        vt32_ref[0, :, rows] = vt
        vtb_ref[0, :, rows] = vt.astype(BF16)
        lft_ref[0, :, rows] = _log_sigmoid(t[R_FF:R_FF + H_FOX] + bf_ref[...])


def _even_proj(x2d, n_seq, g_norm, w, wt, b_f, g_q, g_k, cos, sin, tm):
    m = x2d.shape[0]
    s = m // n_seq
    per_seq = s // tm
    row = lambda n: pl.BlockSpec((tm, n), lambda i: (i, 0))
    tab = pl.BlockSpec((tm, LANES), lambda i: (i % per_seq, 0))
    col = lambda n: pl.BlockSpec((1, n, tm), lambda i: (i // per_seq, 0, i % per_seq))
    outs = [(W_FOX, BF16), (W_RQK, BF16), (W_RQK, BF16), (W_RET, BF16), (W_RET, BF16), (W_FOX, BF16)]
    outs_t = [(W_FOX, BF16), (W_FOX, F32), (W_FOX, BF16), (W_FOX, F32), (W_FOX, BF16), (H_FOX, F32)]
    return pl.pallas_call(
        _even_proj_kernel,
        grid=(m // tm,),
        in_specs=[row(D_MODEL), _const_spec((1, D_MODEL)), _const_spec((D_MODEL, EVEN_COLS)),
                  _const_spec((EVEN_ROWS_T, D_MODEL)), _const_spec((H_FOX, 1)), _const_spec((HEAD_DIM, 1)),
                  _const_spec((HEAD_DIM, 1)), tab, tab],
        out_specs=[row(n) for n, _ in outs] + [col(n) for n, _ in outs_t],
        out_shape=([jax.ShapeDtypeStruct((m, n), dt) for n, dt in outs]
                   + [jax.ShapeDtypeStruct((n_seq, n, s), dt) for n, dt in outs_t]),
        compiler_params=_params(("parallel",)),
        name="even_proj",
    )(x2d, g_norm, w, wt, b_f, g_q, g_k, cos, sin)


def _cumsum_kernel(x_ref, o_ref):
    x = x_ref[0]
    n = x.shape[1]
    lane = lax.broadcasted_iota(jnp.int32, x.shape, 1)
    sh = 1
    while sh < n:
        x = x + jnp.where(lane >= sh, pltpu.roll(x, sh, axis=1), 0.0)
        sh *= 2
    xt = x.T
    for g in range(o_ref.shape[1]):
        o_ref[0, g] = xt[:, g * HEADS_PER_GROUP:(g + 1) * HEADS_PER_GROUP]


def _cumsum_lanes(x):
    b, h, s = x.shape
    ng = h // HEADS_PER_GROUP
    return pl.pallas_call(
        _cumsum_kernel, grid=(b,),
        in_specs=[pl.BlockSpec((1, h, s), lambda i: (i, 0, 0))],
        out_specs=pl.BlockSpec((1, ng, s, HEADS_PER_GROUP), lambda i: (i, 0, 0, 0)),
        out_shape=jax.ShapeDtypeStruct((b, ng, s, HEADS_PER_GROUP), F32),
        compiler_params=_params(("parallel",)), name="logf_cumsum",
    )(x)


HEADS_PER_GROUP = 4
GROUP_W = HEADS_PER_GROUP * HEAD_DIM


def _fox_prompt_kernel(*refs, blk):
    _fox_prompt_body(pl.program_id(2), *refs, blk=blk)


def _fox_prompt_body(i, qt_ref, k_ref, v_ref, ccol_ref, sfg_ref, a_ref,
                     qm_ref, vp_ref, cb_ref, m_ref, acc_ref, s_ref, p_ref, *, blk, side_work=None):
    s_len = k_ref.shape[1]

    @pl.when(i == 0)
    def _():
        ones_row = jnp.where(lax.broadcasted_iota(jnp.int32, (HEAD_DIM, s_len), 0) == 0, 1.0, 0.0).astype(BF16)
        ccol = ccol_ref[0, 0] * (-LOG2E)
        for hh in range(HEADS_PER_GROUP):
            vp_ref[hh, :HEAD_DIM, :] = v_ref[0, hh * HEAD_DIM:(hh + 1) * HEAD_DIM, :]
            vp_ref[hh, HEAD_DIM:, :] = ones_row
            cb_ref[hh] = jnp.broadcast_to(ccol[:, hh:hh + 1], (s_len, LANES))

    if side_work is not None:
        side_work()
    qt = qt_ref[0]
    row_head = lax.broadcasted_iota(jnp.int32, (GROUP_W, blk), 0) // HEAD_DIM
    for hh in range(HEADS_PER_GROUP):
        qm_ref[hh] = jnp.where(row_head == hh, qt, jnp.zeros_like(qt))
    m_ref[...] = jnp.full(m_ref.shape, -1e30, F32)
    acc_ref[...] = jnp.zeros(acc_ref.shape, F32)
    strips = [(hh, slice(c * LANES, (c + 1) * LANES), c) for hh in range(HEADS_PER_GROUP) for c in range(blk // LANES)]

    def raw_scores(j, slot):
        off = pl.multiple_of(j * blk, blk)
        kb = k_ref[0, pl.ds(off, blk), :]
        for hh in range(HEADS_PER_GROUP):
            bias = cb_ref[hh, pl.ds(off, blk), :]
            s_ref[slot, hh] = (jnp.dot(kb, qm_ref[hh], preferred_element_type=F32)
                               + jnp.concatenate([bias] * (blk // LANES), axis=1))

    def consume(j, slot, masked):
        off = pl.multiple_of(j * blk, blk)

        def scores(hh, cols, c):
            s = s_ref[slot, hh, :, cols]
            if masked:
                causal = (lax.broadcasted_iota(jnp.int32, (blk, LANES), 0)
                          <= lax.broadcasted_iota(jnp.int32, (blk, LANES), 1) + c * LANES)
                s = jnp.where(causal, s, -jnp.inf)
            return s

        rescale = {}
        for hh, cols, c in strips:
            s = scores(hh, cols, c)
            m_old = m_ref[hh, :, cols]
            m_new = jnp.maximum(m_old, jnp.max(s, axis=0, keepdims=True))
            p_ref[hh, :, cols] = jnp.exp2(s - m_new).astype(BF16)
            rescale[hh, c] = jnp.exp2(m_old - m_new)
            m_ref[hh, :, cols] = m_new
        for hh in range(HEADS_PER_GROUP):
            alpha = jnp.concatenate([rescale[hh, c] for c in range(blk // LANES)], axis=1)
            pv = jnp.dot(vp_ref[hh, :, pl.ds(off, blk)], p_ref[hh], preferred_element_type=F32)
            acc_ref[hh] = alpha * acc_ref[hh] + pv

    raw_scores(0, 0)

    def pair(t, carry):
        j = 2 * t
        raw_scores(j + 1, 1)
        consume(j, 0, False)
        raw_scores(j + 2, 0)
        consume(j + 1, 1, False)
        return carry

    lax.fori_loop(0, i // 2, pair, 0)

    @pl.when(i % 2 == 1)
    def _():
        raw_scores(i, 1)
        consume(i - 1, 0, False)
        consume(i, 1, True)

    @pl.when(i % 2 == 0)
    def _():
        consume(i, 0, True)

    ot = jnp.concatenate([acc_ref[hh, :HEAD_DIM, :] / acc_ref[hh, HEAD_DIM:HEAD_DIM + 1, :]
                          for hh in range(HEADS_PER_GROUP)], axis=0)
    a_ref[0] = (ot.T * sfg_ref[0].astype(F32)).astype(BF16)


FOX_BLK = 256


def _fox_prompt_specs(s, ids):
    blk = FOX_BLK

    def spec(shape, f):
        return pl.BlockSpec(shape, lambda *a: f(*ids(*a)))

    rows = spec((1, blk, GROUP_W), lambda bi, g, i: (bi, i, g))
    in_specs = [spec((1, GROUP_W, blk), lambda bi, g, i: (bi, g, i)),
                spec((1, s, GROUP_W), lambda bi, g, i: (bi, 0, g)),
                spec((1, GROUP_W, s), lambda bi, g, i: (bi, g, 0)),
                spec((1, 1, s, HEADS_PER_GROUP), lambda bi, g, i: (bi, g, 0, 0)),
                rows]
    scratch = [pltpu.VMEM((HEADS_PER_GROUP, GROUP_W, blk), BF16),
               pltpu.VMEM((HEADS_PER_GROUP, LANES, s), BF16),
               pltpu.VMEM((HEADS_PER_GROUP, s, LANES), F32),
               pltpu.VMEM((HEADS_PER_GROUP, 1, blk), F32),
               pltpu.VMEM((HEADS_PER_GROUP, LANES, blk), F32),
               pltpu.VMEM((2, HEADS_PER_GROUP, blk, blk), F32),
               pltpu.VMEM((HEADS_PER_GROUP, blk, blk), BF16)]
    return in_specs, rows, scratch


def _fox_prompt(qt, kn, vtb, ccol, sfg):
    b, s, _ = kn.shape
    in_specs, out_spec, scratch = _fox_prompt_specs(s, lambda bi, g, i: (bi, g, i))
    return pl.pallas_call(
        functools.partial(_fox_prompt_kernel, blk=FOX_BLK),
        grid=(b, H_FOX // HEADS_PER_GROUP, s // FOX_BLK),
        in_specs=in_specs, out_specs=out_spec,
        out_shape=jax.ShapeDtypeStruct((b, s, W_FOX), BF16),
        scratch_shapes=scratch,
        compiler_params=_params(("parallel", "parallel", "arbitrary")),
        name="fox_prompt",
    )(qt, kn, vtb, ccol, sfg)


def _ret_head_out(o, g, srg):
    return (_layer_norm(o, g) * srg.astype(F32)).astype(BF16)


def _ret_prompt_kernel(rq_ref, rk_ref, rv_ref, srg_ref, gret_ref, r_ref, st_ref, dec_ref, kdec_ref, *, chunks):
    L = CHUNK
    c = pl.program_id(1)

    @pl.when(c == 0)
    def _():
        st_ref[...] = jnp.zeros(st_ref.shape, F32)
        diff = (lax.broadcasted_iota(jnp.int32, (L, L), 0)
                - lax.broadcasted_iota(jnp.int32, (L, L), 1)).astype(F32)
        for h in range(H_RET):
            dec_ref[h] = jnp.where(diff >= 0, jnp.exp(LOG_GAMMA[h] * jnp.maximum(diff, 0.0)), 0.0)
        left = (L - 1.0) - lax.broadcasted_iota(jnp.int32, (L, LANES), 0).astype(F32)
        first = lax.broadcasted_iota(jnp.int32, (L, LANES), 1) < RET_DK
        for pp in range(H_RET // 2):
            kdec_ref[pp] = jnp.exp(jnp.where(first, LOG_GAMMA[2 * pp], LOG_GAMMA[2 * pp + 1]) * left)

    pos = lax.broadcasted_iota(jnp.int32, (L, 1), 0).astype(F32)
    lane_half = lax.broadcasted_iota(jnp.int32, (L, LANES), 1) // RET_DK
    for ci in range(chunks):
        rows_c = slice(ci * L, (ci + 1) * L)
        first_stage = []
        for pp in range(H_RET // 2):
            qp = rq_ref[0, rows_c, pp * LANES:(pp + 1) * LANES]
            kp = rk_ref[0, rows_c, pp * LANES:(pp + 1) * LANES]
            s_pair = st_ref[0, pp]
            s_pair_b = s_pair.astype(BF16)
            kd = (kp.astype(F32) * kdec_ref[pp]).astype(BF16)
            for j in range(2):
                h = 2 * pp + j
                qm = jnp.where(lane_half == j, qp, jnp.zeros_like(qp))
                vh = rv_ref[0, rows_c, h * RET_DV:(h + 1) * RET_DV]
                scores = lax.dot_general(qm, kp, NT_DIMS, preferred_element_type=F32)
                cross = jnp.dot(qm, s_pair_b, preferred_element_type=F32)
                upd = lax.dot_general(kd, vh, TN_DIMS, preferred_element_type=F32)
                first_stage.append((scores, cross, upd, vh, s_pair))
        outs = []
        for h, (scores, cross, upd, vh, s_pair) in enumerate(first_stage):
            lg = LOG_GAMMA[h]
            intra = jnp.dot((scores * dec_ref[h]).astype(BF16), vh, preferred_element_type=F32)
            outs.append(intra + cross * jnp.exp(lg * (pos + 1.0)))
            rows = slice((h % 2) * RET_DK, (h % 2 + 1) * RET_DK)
            st_ref[0, h // 2, rows, :] = float(np.exp(lg * L)) * s_pair[rows] + upd[rows]
        for h, o in enumerate(outs):
            cols = slice(h * RET_DV, (h + 1) * RET_DV)
            r_ref[0, rows_c, cols] = _ret_head_out(o, gret_ref[:, cols], srg_ref[0, rows_c, cols])


def _ret_prompt(rq, rk, rv, srg, g_ret):
    b, s, _ = rq.shape
    chunks = next(n for n in (8, 4, 2, 1) if s % (n * CHUNK) == 0)
    rows = chunks * CHUNK
    qk = pl.BlockSpec((1, rows, W_RQK), lambda bi, c: (bi, c, 0))
    wide = pl.BlockSpec((1, rows, W_RET), lambda bi, c: (bi, c, 0))
    st = pl.BlockSpec((1, H_RET // 2, 2 * RET_DK, RET_DV), lambda bi, c: (bi, 0, 0, 0))
    return pl.pallas_call(
        functools.partial(_ret_prompt_kernel, chunks=chunks),
        grid=(b, s // rows),
        in_specs=[qk, qk, wide, wide, _const_spec((1, W_RET))],
        out_specs=[wide, st],
        out_shape=[jax.ShapeDtypeStruct((b, s, W_RET), BF16),
                   jax.ShapeDtypeStruct((b, H_RET // 2, 2 * RET_DK, RET_DV), F32)],
        scratch_shapes=[pltpu.VMEM((H_RET, CHUNK, CHUNK), F32), pltpu.VMEM((H_RET // 2, CHUNK, LANES), F32)],
        compiler_params=_params(("parallel", "arbitrary")),
        name="ret_prompt",
    )(rq, rk, rv, srg, g_ret)


def _ret_sample_kernel(rq_ref, rk_ref, rv_ref, srg_ref, gret_ref, s0_ref, r_ref, s1_ref, *, n_new):
    rows = rq_ref.shape[0]
    nb = rows // n_new
    two_dk = 2 * RET_DK
    ri = lax.broadcasted_iota(jnp.int32, (rows, rows), 0)
    ci = lax.broadcasted_iota(jnp.int32, (rows, rows), 1)
    same = (ri // n_new) == (ci // n_new)
    diff = ((ri % n_new) - (ci % n_new)).astype(F32)
    pos = (lax.broadcasted_iota(jnp.int32, (rows, 1), 0) % n_new).astype(F32)
    lane_half = lax.broadcasted_iota(jnp.int32, (rows, LANES), 1) // RET_DK
    q_sel = (lax.broadcasted_iota(jnp.int32, (rows, nb * two_dk), 1) // two_dk
             == lax.broadcasted_iota(jnp.int32, (rows, nb * two_dk), 0) // n_new)
    k_sel = (lax.broadcasted_iota(jnp.int32, (nb * two_dk, rows), 0) // two_dk
             == lax.broadcasted_iota(jnp.int32, (nb * two_dk, rows), 1) // n_new)
    row_head = (lax.broadcasted_iota(jnp.int32, (nb * two_dk, 1), 0) % two_dk) // RET_DK

    qp = rq_ref[...]
    kp = rk_ref[...]
    s_old = s0_ref[...].reshape(nb * two_dk, RET_DV)
    s_old_b = s_old.astype(BF16)
    upd = []
    for j in range(2):
        h = 2 * pl.program_id(0) + j
        lg = jnp.where(h == 0, LOG_GAMMA[0], 0.0)
        for hh in range(1, H_RET):
            lg = jnp.where(h == hh, LOG_GAMMA[hh], lg)
        qm = jnp.where(lane_half == j, qp, jnp.zeros_like(qp))
        vh = rv_ref[:, j * RET_DV:(j + 1) * RET_DV]
        decay = jnp.where(same & (diff >= 0), jnp.exp(lg * jnp.maximum(diff, 0.0)), 0.0)
        scores = lax.dot_general(qm, kp, NT_DIMS, preferred_element_type=F32) * decay
        intra = jnp.dot(scores.astype(BF16), vh, preferred_element_type=F32)
        q_exp = jnp.where(q_sel, jnp.tile(qm, (1, nb)), jnp.zeros((), BF16))
        cross = jnp.dot(q_exp, s_old_b, preferred_element_type=F32) * jnp.exp(lg * (pos + 1.0))
        kd = kp.astype(F32) * jnp.exp(lg * (n_new - 1.0 - pos))
        kd_exp = jnp.where(k_sel, jnp.tile(kd.T, (nb, 1)), 0.0).astype(BF16)
        keep = jnp.exp(lg * jnp.full((1, 1), float(n_new), F32))
        upd.append((keep, jnp.dot(kd_exp, vh, preferred_element_type=F32)))
        cols = slice(j * RET_DV, (j + 1) * RET_DV)
        r_ref[:, cols] = _ret_head_out(intra + cross, gret_ref[:, cols], srg_ref[:, cols])
    s_new = jnp.where(row_head == 0, upd[0][0] * s_old + upd[0][1], upd[1][0] * s_old + upd[1][1])
    s1_ref[...] = s_new.reshape(s1_ref.shape)


def _ret_sample(rq, rk, rv, srg, g_ret, state, n_new):
    rows = rq.shape[0]
    nb = rows // n_new
    pair = pl.BlockSpec((rows, LANES), lambda p: (0, p))
    wide = pl.BlockSpec((rows, 2 * RET_DV), lambda p: (0, p))
    st = pl.BlockSpec((nb, 2, RET_DK, RET_DV), lambda p: (0, p, 0, 0))
    return pl.pallas_call(
        functools.partial(_ret_sample_kernel, n_new=n_new),
        grid=(H_RET // 2,),
        in_specs=[pair, pair, wide, wide, pl.BlockSpec((1, 2 * RET_DV), lambda p: (0, p)), st],
        out_specs=[wide, st],
        out_shape=[jax.ShapeDtypeStruct((rows, W_RET), BF16),
                   jax.ShapeDtypeStruct(state.shape, F32)],
        compiler_params=_params(("parallel",)),
        name="ret_sample",
    )(rq, rk, rv, srg, g_ret, state)


def _fox_sample_kernel(pt_ref, *refs, n_steps, **kw):
    _fox_sample_body(pl.program_id(0) * n_steps + pl.program_id(1), pl.num_programs(0) * n_steps, n_steps,
                     pt_ref, *refs, **kw)


def _fox_sample_body(step, total_steps, n_steps, pt_ref, q_ref, kn_ref, vn_ref, lfn_ref, sfg_ref, lfc_ref,
                     ck_hbm, cv_hbm, a_ref, qbd_ref, m_ref, l_ref, acc_ref, tail_ref, kbuf, vbuf, sem,
                     *, n_new, pages_per_step, page_groups):
    g = pages_per_step
    b, c = step // n_steps, step % n_steps

    def first_page_of(st):
        return (n_steps - 1 - st % n_steps) * g

    def page_copies(st, slot):
        row, first = st // n_steps, first_page_of(st)
        copies = []
        for p in range(g):
            page = pt_ref[row, first + p]
            copies.append(pltpu.make_async_copy(ck_hbm.at[page], kbuf.at[slot, p], sem.at[slot, 0]))
            copies.append(pltpu.make_async_copy(cv_hbm.at[page], vbuf.at[slot, p], sem.at[slot, 1]))
        return copies

    slot = step % 2
    nxt = jnp.minimum(step + 1, total_steps - 1)

    @pl.when(step == 0)
    def _():
        for cp in page_copies(step, 0):
            cp.start()

    for cp in page_copies(nxt, 1 - slot):
        cp.start()
    for cp in page_copies(step, slot):
        cp.wait()
    first_page = first_page_of(step)
    nq = n_new * H_FOX
    row_tok = lax.broadcasted_iota(jnp.int32, (nq, 1), 0) // H_FOX
    head_of_row = lax.broadcasted_iota(jnp.int32, (H_FOX, W_FOX), 0)
    head_of_lane = lax.broadcasted_iota(jnp.int32, (H_FOX, W_FOX), 1) // HEAD_DIM

    def partial_softmax(s, vt):
        m = jnp.max(s, axis=-1, keepdims=True)
        p = jnp.exp2(s - m)
        return (m, jnp.sum(p, axis=-1, keepdims=True),
                lax.dot_general(p.astype(BF16), vt, NT_DIMS, preferred_element_type=F32))

    def merge(parts):
        m_new = m_ref[...]
        for m, _, _ in parts:
            m_new = jnp.maximum(m_new, m)
        alpha = jnp.exp2(m_ref[...] - m_new)
        l_new = alpha * l_ref[...]
        acc = alpha * acc_ref[...]
        for m, l, o in parts:
            w = jnp.exp2(m - m_new)
            l_new = l_new + w * l
            acc = acc + w * o
        m_ref[...] = m_new
        l_ref[...] = l_new
        acc_ref[...] = acc

    @pl.when(c == 0)
    def _():
        q = q_ref[0].astype(F32)
        for t in range(n_new):
            qbd_ref[t * H_FOX:(t + 1) * H_FOX, :] = jnp.where(
                head_of_lane == head_of_row, jnp.broadcast_to(q[t:t + 1, :], (H_FOX, W_FOX)), 0.0).astype(BF16)
        tail_ref[...] = jnp.zeros(tail_ref.shape, F32)
        m_ref[...] = jnp.full(m_ref.shape, -1e30, F32)
        l_ref[...] = jnp.zeros(l_ref.shape, F32)
        acc_ref[...] = jnp.zeros(acc_ref.shape, F32)
        lane = lax.broadcasted_iota(jnp.int32, (H_FOX, LANES), 1)
        pfx = lfn_ref[...]
        sh = 1
        while sh < n_new:
            pfx = pfx + jnp.where(lane % n_new >= sh, pltpu.roll(pfx, sh, axis=1), 0.0)
            sh *= 2
        col = lax.broadcasted_iota(jnp.int32, (nq, LANES), 1)
        valid = (col // n_new == b) & (col % n_new <= row_tok)
        s = jnp.dot(qbd_ref[...], kn_ref[...], preferred_element_type=F32)
        merge([partial_softmax(jnp.where(valid, s - jnp.tile(pfx, (n_new, 1)) * LOG2E, -jnp.inf), vn_ref[...])])

    lf = jnp.concatenate([lfc_ref[pt_ref[b, first_page + p]] for p in range(g)], axis=0)
    hi = lf.astype(BF16)
    rest = lf - hi.astype(F32)
    mid = rest.astype(BF16)
    low = (rest - mid.astype(F32)).astype(BF16)
    t_idx = lax.broadcasted_iota(jnp.int32, (PAGE, 2 * PAGE), 0)
    j_idx = lax.broadcasted_iota(jnp.int32, (PAGE, 2 * PAGE), 1)
    later_or_all = jnp.where((j_idx >= PAGE) | (t_idx > j_idx), 1.0, 0.0).astype(BF16)
    sums = jnp.dot(jnp.concatenate([hi, mid, low], axis=0), later_or_all, preferred_element_type=F32)
    sums = sums[:g * H_FOX] + sums[g * H_FOX:2 * g * H_FOX] + sums[2 * g * H_FOX:]
    run = tail_ref[...]
    biases = [None] * g
    for p in reversed(range(g)):
        page_rows = slice(p * H_FOX, (p + 1) * H_FOX)
        biases[p] = run + sums[page_rows, :PAGE]
        run = run + sums[page_rows, PAGE:]
    tail_ref[...] = run
    key_bias = jnp.tile(jnp.concatenate(biases, axis=1) * LOG2E, (n_new, 1))
    per = g // page_groups
    scores = []
    for grp in range(page_groups):
        kc = jnp.concatenate([kbuf[slot, p].astype(BF16) for p in range(grp * per, (grp + 1) * per)],
                             axis=1)
        s = jnp.dot(qbd_ref[...], kc, preferred_element_type=F32)
        scores.append(s + key_bias[:, grp * per * PAGE:(grp + 1) * per * PAGE])
    parts = []
    for grp, s in enumerate(scores):
        vc = jnp.concatenate([vbuf[slot, p].astype(BF16) for p in range(grp * per, (grp + 1) * per)], axis=1)
        parts.append(partial_softmax(s, vc))
    merge(parts)

    o = acc_ref[...] / l_ref[...]
    sfg = sfg_ref[0].astype(F32)
    for t in range(n_new):
        ot = jnp.where(head_of_lane == head_of_row, o[t * H_FOX:(t + 1) * H_FOX, :], 0.0)
        a_ref[0, t:t + 1, :] = jnp.sum(ot, axis=0, keepdims=True) * sfg[t:t + 1, :]

    @pl.when(step == total_steps - 1)
    def _():
        for cp in page_copies(nxt, 1 - slot):
            cp.wait()


def _fox_sample(page_table, q, kt_new, vt_new, lft_new, sfg, cache_kt, cache_vt, cache_lft):
    bd, n_new, _ = q.shape
    g, n_steps, kw = _fox_sample_plan(page_table, n_new)
    in_specs, out_spec, scratch = _fox_sample_specs(n_new, g, n_steps, (kt_new, vt_new, lft_new, cache_lft),
                                                    lambda b, c, pt: (b, c, pt))
    grid_spec = pltpu.PrefetchScalarGridSpec(
        num_scalar_prefetch=1, grid=(bd, n_steps), in_specs=in_specs, out_specs=out_spec, scratch_shapes=scratch)
    return pl.pallas_call(
        functools.partial(_fox_sample_kernel, n_steps=n_steps, **kw),
        grid_spec=grid_spec,
        out_shape=jax.ShapeDtypeStruct((bd, n_new, W_FOX), F32),
        compiler_params=_params(("arbitrary", "arbitrary")),
        name="fox_sample",
    )(page_table, q, kt_new, vt_new, lft_new, sfg, cache_lft, cache_kt, cache_vt)


def _fox_sample_plan(page_table, n_new):
    n_pages = page_table.shape[1]
    g = next(c for c in (16, 8, 4, 2, 1) if n_pages % c == 0)
    page_groups = 2 if g >= 2 else 1
    return g, n_pages // g, dict(n_new=n_new, pages_per_step=g, page_groups=page_groups)


def _fox_sample_specs(n_new, g, n_steps, resident, ids):
    nq = n_new * H_FOX

    def spec(shape, f, **kw):
        return pl.BlockSpec(shape, lambda *a: f(*ids(*a)), **kw)

    def tok(n):
        return spec((1, n_new, n), lambda b, c, pt: (b, 0, 0))

    def whole(arr):
        nd = arr.ndim
        return spec(arr.shape, lambda b, c, pt: (0,) * nd, pipeline_mode=pl.Buffered(1))

    kt_new, vt_new, lft_new, cache_lft = resident
    in_memory = pl.BlockSpec(memory_space=pl.ANY)
    in_specs = [tok(W_FOX), whole(kt_new), whole(vt_new), whole(lft_new), tok(W_FOX), whole(cache_lft),
                in_memory, in_memory]
    scratch = [pltpu.VMEM((nq, W_FOX), BF16), pltpu.VMEM((nq, 1), F32), pltpu.VMEM((nq, 1), F32),
               pltpu.VMEM((nq, W_FOX), F32), pltpu.VMEM((H_FOX, LANES), F32),
               pltpu.VMEM((2, g, W_FOX, PAGE), F32), pltpu.VMEM((2, g, W_FOX, PAGE), F32),
               pltpu.SemaphoreType.DMA((2, 2))]
    return in_specs, tok(W_FOX), scratch


def _fox_fused_kernel(pt_ref, *refs, n_prompt_in, n_sample_in, n_prompt_scratch, n_steps, sample_kw):
    refs = list(refs)
    p_in, refs = refs[:n_prompt_in], refs[n_prompt_in:]
    s_in, refs = refs[:n_sample_in], refs[n_sample_in:]
    (a_p, a_d), refs = refs[:2], refs[2:]
    p_scr, s_scr = refs[:n_prompt_scratch], refs[n_prompt_scratch:]
    step = (pl.program_id(0) * pl.num_programs(1) + pl.program_id(1)) * pl.num_programs(2) + pl.program_id(2)
    total = pl.num_programs(0) * pl.num_programs(1) * pl.num_programs(2)
    sample = functools.partial(_fox_sample_body, step, total, n_steps, pt_ref, *s_in, a_d, *s_scr, **sample_kw)
    _fox_prompt_body(pl.program_id(2), *p_in, a_p, *p_scr, blk=FOX_BLK, side_work=sample)


def _fox_fused(qt, kn, vtb, ccol, sfg, page_table, q, kt_new, vt_new, lft_new, sfg_d, cache_kt, cache_vt, cache_lft):
    b, s, _ = kn.shape
    bd, n_new, _ = q.shape
    ng, nqb = H_FOX // HEADS_PER_GROUP, s // FOX_BLK
    g, n_steps, kw = _fox_sample_plan(page_table, n_new)
    assert b * ng * nqb == bd * n_steps
    p_in, p_out, p_scr = _fox_prompt_specs(s, lambda bi, gi, i, pt: (bi, gi, i))

    def sample_ids(bi, gi, i, pt):
        step = (bi * ng + gi) * nqb + i
        return step // n_steps, step % n_steps, pt

    s_in, s_out, s_scr = _fox_sample_specs(n_new, g, n_steps, (kt_new, vt_new, lft_new, cache_lft), sample_ids)
    grid_spec = pltpu.PrefetchScalarGridSpec(
        num_scalar_prefetch=1, grid=(b, ng, nqb), in_specs=p_in + s_in, out_specs=[p_out, s_out],
        scratch_shapes=p_scr + s_scr)
    return pl.pallas_call(
        functools.partial(_fox_fused_kernel, n_prompt_in=len(p_in), n_sample_in=len(s_in),
                          n_prompt_scratch=len(p_scr), n_steps=n_steps, sample_kw=kw),
        grid_spec=grid_spec,
        out_shape=[jax.ShapeDtypeStruct((b, s, W_FOX), BF16), jax.ShapeDtypeStruct((bd, n_new, W_FOX), F32)],
        compiler_params=_params(("arbitrary", "arbitrary", "arbitrary")),
        name="fox_fused",
    )(page_table, qt, kn, vtb, ccol, sfg, q, kt_new, vt_new, lft_new, sfg_d, cache_lft, cache_kt, cache_vt)


TAIL_SUB = 256


def _tail_kernel(a_ref, r_ref, x_ref, woe_ref, gno_ref, wio_ref, gv_ref, ws_ref, bs_ref, woo_ref,
                 *out_refs, n_new):
    tile = x_ref.shape[0]
    sub = min(tile, TAIL_SUB)
    y_ref = out_refs[0]

    def front(rows):
        x1 = (x_ref[rows, :]
              + jnp.dot(a_ref[rows, :].astype(BF16), woe_ref[:W_FOX, :], preferred_element_type=F32)
              + jnp.dot(r_ref[rows, :], woe_ref[W_FOX:, :], preferred_element_type=F32))
        h = _rms(x1, gno_ref[...]).astype(BF16)
        v = jax.nn.gelu(jnp.dot(h, wio_ref[:, GM_WIDTH:2 * GM_WIDTH], preferred_element_type=F32))
        vn = _layer_norm(v, gv_ref[...])
        u = jax.nn.gelu(jnp.dot(h, wio_ref[:, :GM_WIDTH], preferred_element_type=F32))
        gate = jnp.dot(h, wio_ref[:, 2 * GM_WIDTH:], preferred_element_type=F32)
        return x1, vn, u, gate

    def back(rows, x1, vn, u, gate):
        if n_new is None:
            tril = (lax.broadcasted_iota(jnp.int32, (CHUNK, CHUNK), 0)
                    >= lax.broadcasted_iota(jnp.int32, (CHUNK, CHUNK), 1))
            vb = vn.astype(BF16)

            def spatial(g):
                wm = jnp.where(tril, ws_ref[g], 0.0).astype(BF16)
                bias = bs_ref[:, g:g + 1]
                return jnp.concatenate(
                    [jnp.dot(wm, vb[c * CHUNK:(c + 1) * CHUNK, g * GM_GC:(g + 1) * GM_GC],
                             preferred_element_type=F32) + bias for c in range(sub // CHUNK)], axis=0)
        else:
            out_refs[1][rows, :] = vn
            tok = lax.broadcasted_iota(jnp.int32, (sub, 1), 0) % n_new

            def spatial(g):
                vg = vn[:, g * GM_GC:(g + 1) * GM_GC]
                acc = jnp.zeros((sub, 1), F32)
                for t in range(n_new):
                    acc = jnp.where(tok == t, bs_ref[g, t], acc)
                acc = jnp.broadcast_to(acc, (sub, GM_GC))
                for d in range(n_new):
                    coef = jnp.zeros((sub, 1), F32)
                    for t in range(d, n_new):
                        coef = jnp.where(tok == t, ws_ref[g, t * n_new + (t - d)], coef)
                    src = vg if d == 0 else pltpu.roll(vg, d, axis=0)
                    acc = acc + coef * src
                return acc

        sv = jnp.concatenate([spatial(g) for g in range(GM_GROUPS)], axis=1)
        z = (u * sv * jax.nn.silu(gate)).astype(BF16)
        y_ref[rows, :] = x1 + jnp.dot(z, woo_ref[...], preferred_element_type=F32)

    subs = [slice(i * sub, (i + 1) * sub) for i in range(tile // sub)]
    fronts = [front(rows) for rows in subs]
    for rows, vals in zip(subs, fronts):
        back(rows, *vals)


def _tail(a, r, x2d, woe, gno, wio, gv, ws, bs, woo, tm, n_new):
    m = x2d.shape[0]
    row = lambda n: pl.BlockSpec((tm, n), lambda i: (i, 0))
    if n_new is None:
        mix_specs = [_const_spec(ws.shape), _const_spec(bs.shape)]
        out_specs = [row(D_MODEL)]
        out_shape = [jax.ShapeDtypeStruct((m, D_MODEL), F32)]
    else:
        smem = pl.BlockSpec(memory_space=pltpu.SMEM)
        mix_specs = [smem, smem]
        out_specs = [row(D_MODEL), row(GM_WIDTH)]
        out_shape = [jax.ShapeDtypeStruct((m, D_MODEL), F32), jax.ShapeDtypeStruct((m, GM_WIDTH), F32)]
    return pl.pallas_call(
        functools.partial(_tail_kernel, n_new=n_new),
        grid=(m // tm,),
        in_specs=[row(W_FOX), row(W_RET), row(D_MODEL), _const_spec(woe.shape), _const_spec((1, D_MODEL)),
                  _const_spec(wio.shape), _const_spec((1, GM_WIDTH))] + mix_specs + [_const_spec(woo.shape)],
        out_specs=out_specs, out_shape=out_shape,
        compiler_params=_params(("parallel",)),
        name="tail_prompt" if n_new is None else "tail_sample",
    )(a, r, x2d, woe, gno, wio, gv, ws, bs, woo)


def _rope_tables(pos):
    half = RET_DK // 2
    inv = ROPE_BASE ** (-jnp.arange(half, dtype=F32) / half)
    ang = pos.astype(F32)[:, None] * inv[None, :]
    cos, sin = jnp.cos(ang), jnp.sin(ang)
    return jnp.tile(jnp.concatenate([cos, cos], axis=-1), (1, 2)), jnp.tile(jnp.concatenate([-sin, sin], axis=-1), (1, 2))


def _row_tile(m, largest):
    return next(t for t in (512, 256, 128) if t <= largest and m % t == 0)


def kernel(x_prompt, x_sample, cache_k, cache_v, cache_logf, state_ret, page_table, norm_even, w_in_even,
           b_forget, qnorm, knorm, ret_norm, w_out_even, norm_odd, w_in_odd, vnorm_odd, w_spatial, b_spatial,
           w_out_odd):
    bp, s, _ = x_prompt.shape
    bd, n_new, _ = x_sample.shape
    n_pages = page_table.shape[1]
    past = n_pages * PAGE
    assert s % 256 == 0 and (bd * n_new) % CHUNK == 0 and w_in_even.shape[0] == 1 and w_in_odd.shape[0] == 1

    wi = w_in_even[0]
    sizes = (W_FOX, W_FOX, W_FOX, H_FOX, W_FOX, W_RQK, W_RQK, W_RET, W_RET)
    offs = np.concatenate([[0], np.cumsum(sizes)])
    fq, fk, fv, ff, fg, rq, rk, rv, rg = [wi[:, offs[i]:offs[i + 1]] for i in range(9)]
    w_even = jnp.concatenate([fg, rq, rk, rv, rg], axis=1).astype(BF16)
    w_even_t = jnp.concatenate([fq.T, fk.T, fv.T, ff.T, jnp.zeros((EVEN_ROWS_T - R_FF - H_FOX, D_MODEL), F32)],
                               axis=0).astype(BF16)
    b_f = b_forget[0][:, None]
    g_q = qnorm[0][:, None]
    g_k = knorm[0][:, None]
    g_ret = ret_norm[0].reshape(1, W_RET)
    woe = w_out_even[0].astype(BF16)
    wio = w_in_odd[0].astype(BF16)
    woo = w_out_odd[0].astype(BF16)
    g_ne, g_no, g_v = norm_even[0][None, :], norm_odd[0][None, :], vnorm_odd[0][None, :]

    cos_p, sin_p = _rope_tables(jnp.arange(s))
    xp2 = x_prompt.reshape(bp * s, D_MODEL)
    tm = _row_tile(s, 512)
    sfg, rqp, rkp, rvp, srg, kn, qt, kt32, _, vt32, vtb, lft = _even_proj(
        xp2, bp, g_ne, w_even, w_even_t, b_f, g_q, g_k, cos_p, sin_p, tm)
    rows = bd * n_new
    cos_d, sin_d = _rope_tables(past + jnp.arange(n_new))
    xd2 = x_sample.reshape(rows, D_MODEL)
    sfgd, rqd, rkd, rvd, srgd, _, qtd, kt32d, ktbd, vt32d, vtbd, lftd = _even_proj(
        xd2, 1, g_ne, w_even, w_even_t, b_f, g_q, g_k, jnp.tile(cos_d, (bd, 1)), jnp.tile(sin_d, (bd, 1)), rows)

    ng = H_FOX // HEADS_PER_GROUP
    c_col4 = _cumsum_lanes(lft)
    r3 = lambda t: t.reshape(bp, s, t.shape[-1])
    d3 = lambda t: t.reshape(bd, n_new, t.shape[-1])
    n_phys = cache_k.shape[1]
    pages_t = lambda cache: jnp.transpose(cache[0], (0, 2, 3, 1)).reshape(n_phys, W_FOX, PAGE)
    prompt_args = (qt, r3(kn), vtb, c_col4, r3(sfg))
    sample_args = (page_table, d3(jnp.transpose(qtd[0], (1, 0))), ktbd[0], vtbd[0], lftd[0], d3(sfgd),
                   pages_t(cache_k), pages_t(cache_v), jnp.transpose(cache_logf[0], (0, 2, 1)))
    if bp * ng * (s // FOX_BLK) == bd * _fox_sample_plan(page_table, n_new)[1]:
        a_p, a_d = _fox_fused(*prompt_args, *sample_args)
    else:
        a_p, a_d = _fox_prompt(*prompt_args), _fox_sample(*sample_args)
    r_p, st_p = _ret_prompt(r3(rqp), r3(rkp), r3(rvp), r3(srg), g_ret)
    r_d, st_d = _ret_sample(rqd, rkd, rvd, srgd, g_ret, state_ret[0], n_new)

    y_p, = _tail(a_p.reshape(bp * s, W_FOX), r_p.reshape(bp * s, W_RET), xp2, woe, g_no, wio, g_v,
                 w_spatial[0], jnp.transpose(b_spatial[0]), woo, _row_tile(s, 512), None)
    ws_new = w_spatial[0][:, :n_new, :n_new].reshape(GM_GROUPS, n_new * n_new)
    y_d, gv_d = _tail(a_d.reshape(rows, W_FOX), r_d, xd2, woe, g_no, wio, g_v,
                      ws_new, b_spatial[0][:, :n_new], woo, rows, n_new)

    heads_last = lambda t, n, length: jnp.transpose(t.reshape(n, H_FOX, HEAD_DIM, length), (0, 3, 1, 2))[None]
    tokens_first = lambda t, n: jnp.transpose(t[0], (1, 0)).reshape(1, bd, n_new, *n)
    return (y_p.reshape(bp, s, D_MODEL), y_d.reshape(bd, n_new, D_MODEL),
            heads_last(kt32, bp, s), heads_last(vt32, bp, s),
            jnp.transpose(lft, (0, 2, 1))[None], st_p.reshape(1, bp, H_RET, RET_DK, RET_DV),
            tokens_first(kt32d, (H_FOX, HEAD_DIM)), tokens_first(vt32d, (H_FOX, HEAD_DIM)),
            tokens_first(lftd, (H_FOX,)), st_d[None], gv_d.reshape(1, bd, n_new, GM_WIDTH))
```

```python
import functools

import numpy as np
import jax
import jax.numpy as jnp
from jax import lax
from jax.experimental import pallas as pl
from jax.experimental.pallas import tpu as pltpu

D_MODEL = 1024
HEAD_DIM = 64
H_FOX = 8
W_FOX = H_FOX * HEAD_DIM
H_RET = 8
RET_DK = 64
RET_DV = 128
W_RET = H_RET * RET_DV
W_RQK = H_RET * RET_DK
ROPE_BASE = 10000.0
GM_WIDTH = 2 * D_MODEL
GM_GROUPS = 8
GM_GC = GM_WIDTH // GM_GROUPS
CHUNK = 128
PAGE = 128
EPS = 1e-6
F32 = jnp.float32
BF16 = jnp.bfloat16

LANES = 128
VMEM_LIMIT = 56 * 1024 * 1024

C_FG, C_RQ, C_RK, C_RV, C_RG, EVEN_COLS = 0, 512, 1024, 1536, 2560, 3584
R_FQ, R_FK, R_FV, R_FF, EVEN_ROWS_T = 0, 512, 1024, 1536, 1552

LOG_GAMMA = [float(np.log(np.float32(1.0) - np.float32(2.0) ** np.float32(-5.0 - h))) for h in range(H_RET)]

LOG2E = float(np.log2(np.e))
NT_DIMS = (((1,), (1,)), ((), ()))
TN_DIMS = (((0,), (0,)), ((), ()))


def _params(sem):
    return pltpu.CompilerParams(dimension_semantics=sem, vmem_limit_bytes=VMEM_LIMIT)


def _const_spec(shape):
    nd = len(shape)
    return pl.BlockSpec(shape, lambda *_: (0,) * nd, pipeline_mode=pl.Buffered(1))


def _rms(x, g):
    ms = jnp.mean(x * x, axis=-1, keepdims=True)
    return x * lax.rsqrt(ms + EPS) * g


def _layer_norm(x, g):
    xc = x - jnp.mean(x, axis=-1, keepdims=True)
    return xc * lax.rsqrt(jnp.mean(xc * xc, axis=-1, keepdims=True) + EPS) * g


def _rope(x, cos, sin_signed):
    rows = x.shape[0]
    first_half = (lax.broadcasted_iota(jnp.int32, (rows, LANES), 1) % RET_DK) < RET_DK // 2
    outs = []
    for c in range(x.shape[1] // LANES):
        xc = x[:, c * LANES:(c + 1) * LANES]
        nxt = pltpu.roll(xc, LANES - RET_DK // 2, axis=1)
        prv = pltpu.roll(xc, RET_DK // 2, axis=1)
        outs.append(xc * cos + jnp.where(first_half, nxt, prv) * sin_signed)
    return jnp.concatenate(outs, axis=-1)


def _log_sigmoid(x):
    return jnp.minimum(x, 0.0) - jnp.log1p(jnp.exp(-jnp.abs(x)))


def _head_rms_t(x, g):
    tokens = x.shape[1]
    x = x.reshape(H_FOX, HEAD_DIM, tokens)
    x = x * lax.rsqrt(jnp.mean(x * x, axis=1, keepdims=True) + EPS) * g[None]
    return x.reshape(W_FOX, tokens)


PROJ_SUB = 256


def _even_proj_kernel(x_ref, g_ref, w_ref, wt_ref, bf_ref, gq_ref, gk_ref, cos_ref, sin_ref,
                      sfg_ref, rq_ref, rk_ref, rv_ref, srg_ref, kn_ref,
                      qt_ref, kt32_ref, ktb_ref, vt32_ref, vtb_ref, lft_ref):
    tile = x_ref.shape[0]
    sub = min(tile, PROJ_SUB)
    for i in range(tile // sub):
        rows = slice(i * sub, (i + 1) * sub)
        h = _rms(x_ref[rows, :], g_ref[...]).astype(BF16)

        def proj(lo, hi, h=h):
            return jnp.dot(h, w_ref[:, lo:hi], preferred_element_type=F32)

        sfg_ref[rows, :] = jax.nn.silu(proj(C_FG, C_RQ)).astype(BF16)
        cos = cos_ref[rows, :]
        sin = sin_ref[rows, :]
        rq_ref[rows, :] = _rope(proj(C_RQ, C_RK), cos, sin).astype(BF16)
        rk_ref[rows, :] = (_rope(proj(C_RK, C_RV), cos, sin) * (RET_DK ** -0.5)).astype(BF16)
        rv_ref[rows, :] = proj(C_RV, C_RG).astype(BF16)
        srg_ref[rows, :] = jax.nn.silu(proj(C_RG, EVEN_COLS)).astype(BF16)

        t = lax.dot_general(wt_ref[...], h, NT_DIMS, preferred_element_type=F32)
        qt_ref[0, :, rows] = (_head_rms_t(t[R_FQ:R_FK], gq_ref[...]) * (HEAD_DIM ** -0.5 * LOG2E)).astype(BF16)
        kt = _head_rms_t(t[R_FK:R_FV], gk_ref[...])
        kt32_ref[0, :, rows] = kt
        ktb_ref[0, :, rows] = kt.astype(BF16)
        kn_ref[rows, :] = kt.T.astype(BF16)
        vt = t[R_FV:R_FF]
        vt32_ref[0, :, rows] = vt
        vtb_ref[0, :, rows] = vt.astype(BF16)
        lft_ref[0, :, rows] = _log_sigmoid(t[R_FF:R_FF + H_FOX] + bf_ref[...])


def _even_proj(x2d, n_seq, g_norm, w, wt, b_f, g_q, g_k, cos, sin, tm):
    m = x2d.shape[0]
    s = m // n_seq
    per_seq = s // tm
    row = lambda n: pl.BlockSpec((tm, n), lambda i: (i, 0))
    tab = pl.BlockSpec((tm, LANES), lambda i: (i % per_seq, 0))
    col = lambda n: pl.BlockSpec((1, n, tm), lambda i: (i // per_seq, 0, i % per_seq))
    outs = [(W_FOX, BF16), (W_RQK, BF16), (W_RQK, BF16), (W_RET, BF16), (W_RET, BF16), (W_FOX, BF16)]
    outs_t = [(W_FOX, BF16), (W_FOX, F32), (W_FOX, BF16), (W_FOX, F32), (W_FOX, BF16), (H_FOX, F32)]
    return pl.pallas_call(
        _even_proj_kernel,
        grid=(m // tm,),
        in_specs=[row(D_MODEL), _const_spec((1, D_MODEL)), _const_spec((D_MODEL, EVEN_COLS)),
                  _const_spec((EVEN_ROWS_T, D_MODEL)), _const_spec((H_FOX, 1)), _const_spec((HEAD_DIM, 1)),
                  _const_spec((HEAD_DIM, 1)), tab, tab],
        out_specs=[row(n) for n, _ in outs] + [col(n) for n, _ in outs_t],
        out_shape=([jax.ShapeDtypeStruct((m, n), dt) for n, dt in outs]
                   + [jax.ShapeDtypeStruct((n_seq, n, s), dt) for n, dt in outs_t]),
        compiler_params=_params(("parallel",)),
        name="even_proj",
    )(x2d, g_norm, w, wt, b_f, g_q, g_k, cos, sin)


def _cumsum_kernel(x_ref, o_ref):
    x = x_ref[0]
    n = x.shape[1]
    lane = lax.broadcasted_iota(jnp.int32, x.shape, 1)
    sh = 1
    while sh < n:
        x = x + jnp.where(lane >= sh, pltpu.roll(x, sh, axis=1), 0.0)
        sh *= 2
    xt = x.T
    for g in range(o_ref.shape[1]):
        o_ref[0, g] = xt[:, g * HEADS_PER_GROUP:(g + 1) * HEADS_PER_GROUP]


def _cumsum_lanes(x):
    b, h, s = x.shape
    ng = h // HEADS_PER_GROUP
    return pl.pallas_call(
        _cumsum_kernel, grid=(b,),
        in_specs=[pl.BlockSpec((1, h, s), lambda i: (i, 0, 0))],
        out_specs=pl.BlockSpec((1, ng, s, HEADS_PER_GROUP), lambda i: (i, 0, 0, 0)),
        out_shape=jax.ShapeDtypeStruct((b, ng, s, HEADS_PER_GROUP), F32),
        compiler_params=_params(("parallel",)), name="logf_cumsum",
    )(x)


HEADS_PER_GROUP = 4
GROUP_W = HEADS_PER_GROUP * HEAD_DIM


def _fox_prompt_kernel(*refs, blk):
    _fox_prompt_body(pl.program_id(2), *refs, blk=blk)


def _fox_prompt_body(i, qt_ref, k_ref, v_ref, ccol_ref, sfg_ref, a_ref,
                     qm_ref, vp_ref, cb_ref, m_ref, acc_ref, s_ref, p_ref, *, blk, side_work=None):
    s_len = k_ref.shape[1]

    @pl.when(i == 0)
    def _():
        ones_row = jnp.where(lax.broadcasted_iota(jnp.int32, (HEAD_DIM, s_len), 0) == 0, 1.0, 0.0).astype(BF16)
        ccol = ccol_ref[0, 0] * (-LOG2E)
        for hh in range(HEADS_PER_GROUP):
            vp_ref[hh, :HEAD_DIM, :] = v_ref[0, hh * HEAD_DIM:(hh + 1) * HEAD_DIM, :]
            vp_ref[hh, HEAD_DIM:, :] = ones_row
            cb_ref[hh] = jnp.broadcast_to(ccol[:, hh:hh + 1], (s_len, LANES))

    strips = [(hh, slice(c * LANES, (c + 1) * LANES), c) for hh in range(HEADS_PER_GROUP) for c in range(blk // LANES)]

    def prologue():
        qt = qt_ref[0]
        row_head = lax.broadcasted_iota(jnp.int32, (GROUP_W, blk), 0) // HEAD_DIM
        for hh in range(HEADS_PER_GROUP):
            qm_ref[hh] = jnp.where(row_head == hh, qt, jnp.zeros_like(qt))
        m_ref[...] = jnp.full(m_ref.shape, -1e30, F32)
        acc_ref[...] = jnp.zeros(acc_ref.shape, F32)
        raw_scores(0, 0)

    def raw_scores(j, slot):
        off = pl.multiple_of(j * blk, blk)
        kb = k_ref[0, pl.ds(off, blk), :]
        for hh in range(HEADS_PER_GROUP):
            bias = cb_ref[hh, pl.ds(off, blk), :]
            s_ref[slot, hh] = (jnp.dot(kb, qm_ref[hh], preferred_element_type=F32)
                               + jnp.concatenate([bias] * (blk // LANES), axis=1))

    def consume(j, slot, masked):
        off = pl.multiple_of(j * blk, blk)

        def scores(hh, cols, c):
            s = s_ref[slot, hh, :, cols]
            if masked:
                causal = (lax.broadcasted_iota(jnp.int32, (blk, LANES), 0)
                          <= lax.broadcasted_iota(jnp.int32, (blk, LANES), 1) + c * LANES)
                s = jnp.where(causal, s, -jnp.inf)
            return s

        rescale = {}
        for hh, cols, c in strips:
            s = scores(hh, cols, c)
            m_old = m_ref[hh, :, cols]
            m_new = jnp.maximum(m_old, jnp.max(s, axis=0, keepdims=True))
            p_ref[hh, :, cols] = jnp.exp2(s - m_new).astype(BF16)
            rescale[hh, c] = jnp.exp2(m_old - m_new)
            m_ref[hh, :, cols] = m_new
        for hh in range(HEADS_PER_GROUP):
            alpha = jnp.concatenate([rescale[hh, c] for c in range(blk // LANES)], axis=1)
            pv = jnp.dot(vp_ref[hh, :, pl.ds(off, blk)], p_ref[hh], preferred_element_type=F32)
            acc_ref[hh] = alpha * acc_ref[hh] + pv

    if side_work is not None:
        side_work(prologue)
    else:
        prologue()

    def pair(t, carry):
        j = 2 * t
        raw_scores(j + 1, 1)
        consume(j, 0, False)
        raw_scores(j + 2, 0)
        consume(j + 1, 1, False)
        return carry

    lax.fori_loop(0, i // 2, pair, 0)

    @pl.when(i % 2 == 1)
    def _():
        raw_scores(i, 1)
        consume(i - 1, 0, False)
        consume(i, 1, True)

    @pl.when(i % 2 == 0)
    def _():
        consume(i, 0, True)

    ot = jnp.concatenate([acc_ref[hh, :HEAD_DIM, :] / acc_ref[hh, HEAD_DIM:HEAD_DIM + 1, :]
                          for hh in range(HEADS_PER_GROUP)], axis=0)
    a_ref[0] = (ot.T * sfg_ref[0].astype(F32)).astype(BF16)


FOX_BLK = 256


def _fox_prompt_specs(s, ids):
    blk = FOX_BLK

    def spec(shape, f):
        return pl.BlockSpec(shape, lambda *a: f(*ids(*a)))

    rows = spec((1, blk, GROUP_W), lambda bi, g, i: (bi, i, g))
    in_specs = [spec((1, GROUP_W, blk), lambda bi, g, i: (bi, g, i)),
                spec((1, s, GROUP_W), lambda bi, g, i: (bi, 0, g)),
                spec((1, GROUP_W, s), lambda bi, g, i: (bi, g, 0)),
                spec((1, 1, s, HEADS_PER_GROUP), lambda bi, g, i: (bi, g, 0, 0)),
                rows]
    scratch = [pltpu.VMEM((HEADS_PER_GROUP, GROUP_W, blk), BF16),
               pltpu.VMEM((HEADS_PER_GROUP, LANES, s), BF16),
               pltpu.VMEM((HEADS_PER_GROUP, s, LANES), F32),
               pltpu.VMEM((HEADS_PER_GROUP, 1, blk), F32),
               pltpu.VMEM((HEADS_PER_GROUP, LANES, blk), F32),
               pltpu.VMEM((2, HEADS_PER_GROUP, blk, blk), F32),
               pltpu.VMEM((HEADS_PER_GROUP, blk, blk), BF16)]
    return in_specs, rows, scratch


def _fox_prompt(qt, kn, vtb, ccol, sfg):
    b, s, _ = kn.shape
    in_specs, out_spec, scratch = _fox_prompt_specs(s, lambda bi, g, i: (bi, g, i))
    return pl.pallas_call(
        functools.partial(_fox_prompt_kernel, blk=FOX_BLK),
        grid=(b, H_FOX // HEADS_PER_GROUP, s // FOX_BLK),
        in_specs=in_specs, out_specs=out_spec,
        out_shape=jax.ShapeDtypeStruct((b, s, W_FOX), BF16),
        scratch_shapes=scratch,
        compiler_params=_params(("parallel", "parallel", "arbitrary")),
        name="fox_prompt",
    )(qt, kn, vtb, ccol, sfg)


def _ret_head_out(o, g, srg):
    return (_layer_norm(o, g) * srg.astype(F32)).astype(BF16)


def _ret_prompt_kernel(rq_ref, rk_ref, rv_ref, srg_ref, gret_ref, r_ref, st_ref, dec_ref, kdec_ref, *, chunks):
    L = CHUNK
    c = pl.program_id(1)

    @pl.when(c == 0)
    def _():
        st_ref[...] = jnp.zeros(st_ref.shape, F32)
        diff = (lax.broadcasted_iota(jnp.int32, (L, L), 0)
                - lax.broadcasted_iota(jnp.int32, (L, L), 1)).astype(F32)
        for h in range(H_RET):
            dec_ref[h] = jnp.where(diff >= 0, jnp.exp(LOG_GAMMA[h] * jnp.maximum(diff, 0.0)), 0.0)
        left = (L - 1.0) - lax.broadcasted_iota(jnp.int32, (L, LANES), 0).astype(F32)
        first = lax.broadcasted_iota(jnp.int32, (L, LANES), 1) < RET_DK
        for pp in range(H_RET // 2):
            kdec_ref[pp] = jnp.exp(jnp.where(first, LOG_GAMMA[2 * pp], LOG_GAMMA[2 * pp + 1]) * left)

    pos = lax.broadcasted_iota(jnp.int32, (L, 1), 0).astype(F32)
    lane_half = lax.broadcasted_iota(jnp.int32, (L, LANES), 1) // RET_DK
    for ci in range(chunks):
        rows_c = slice(ci * L, (ci + 1) * L)
        first_stage = []
        for pp in range(H_RET // 2):
            qp = rq_ref[0, rows_c, pp * LANES:(pp + 1) * LANES]
            kp = rk_ref[0, rows_c, pp * LANES:(pp + 1) * LANES]
            s_pair = st_ref[0, pp]
            s_pair_b = s_pair.astype(BF16)
            kd = (kp.astype(F32) * kdec_ref[pp]).astype(BF16)
            for j in range(2):
                h = 2 * pp + j
                qm = jnp.where(lane_half == j, qp, jnp.zeros_like(qp))
                vh = rv_ref[0, rows_c, h * RET_DV:(h + 1) * RET_DV]
                scores = lax.dot_general(qm, kp, NT_DIMS, preferred_element_type=F32)
                cross = jnp.dot(qm, s_pair_b, preferred_element_type=F32)
                upd = lax.dot_general(kd, vh, TN_DIMS, preferred_element_type=F32)
                first_stage.append((scores, cross, upd, vh, s_pair))
        outs = []
        for h, (scores, cross, upd, vh, s_pair) in enumerate(first_stage):
            lg = LOG_GAMMA[h]
            intra = jnp.dot((scores * dec_ref[h]).astype(BF16), vh, preferred_element_type=F32)
            outs.append(intra + cross * jnp.exp(lg * (pos + 1.0)))
            rows = slice((h % 2) * RET_DK, (h % 2 + 1) * RET_DK)
            st_ref[0, h // 2, rows, :] = float(np.exp(lg * L)) * s_pair[rows] + upd[rows]
        for h, o in enumerate(outs):
            cols = slice(h * RET_DV, (h + 1) * RET_DV)
            r_ref[0, rows_c, cols] = _ret_head_out(o, gret_ref[:, cols], srg_ref[0, rows_c, cols])


def _ret_prompt(rq, rk, rv, srg, g_ret):
    b, s, _ = rq.shape
    chunks = next(n for n in (8, 4, 2, 1) if s % (n * CHUNK) == 0)
    rows = chunks * CHUNK
    qk = pl.BlockSpec((1, rows, W_RQK), lambda bi, c: (bi, c, 0))
    wide = pl.BlockSpec((1, rows, W_RET), lambda bi, c: (bi, c, 0))
    st = pl.BlockSpec((1, H_RET // 2, 2 * RET_DK, RET_DV), lambda bi, c: (bi, 0, 0, 0))
    return pl.pallas_call(
        functools.partial(_ret_prompt_kernel, chunks=chunks),
        grid=(b, s // rows),
        in_specs=[qk, qk, wide, wide, _const_spec((1, W_RET))],
        out_specs=[wide, st],
        out_shape=[jax.ShapeDtypeStruct((b, s, W_RET), BF16),
                   jax.ShapeDtypeStruct((b, H_RET // 2, 2 * RET_DK, RET_DV), F32)],
        scratch_shapes=[pltpu.VMEM((H_RET, CHUNK, CHUNK), F32), pltpu.VMEM((H_RET // 2, CHUNK, LANES), F32)],
        compiler_params=_params(("parallel", "arbitrary")),
        name="ret_prompt",
    )(rq, rk, rv, srg, g_ret)


def _ret_sample_kernel(rq_ref, rk_ref, rv_ref, srg_ref, gret_ref, s0_ref, r_ref, s1_ref, *, n_new):
    rows = rq_ref.shape[0]
    nb = rows // n_new
    two_dk = 2 * RET_DK
    ri = lax.broadcasted_iota(jnp.int32, (rows, rows), 0)
    ci = lax.broadcasted_iota(jnp.int32, (rows, rows), 1)
    same = (ri // n_new) == (ci // n_new)
    diff = ((ri % n_new) - (ci % n_new)).astype(F32)
    pos = (lax.broadcasted_iota(jnp.int32, (rows, 1), 0) % n_new).astype(F32)
    lane_half = lax.broadcasted_iota(jnp.int32, (rows, LANES), 1) // RET_DK
    q_sel = (lax.broadcasted_iota(jnp.int32, (rows, nb * two_dk), 1) // two_dk
             == lax.broadcasted_iota(jnp.int32, (rows, nb * two_dk), 0) // n_new)
    k_sel = (lax.broadcasted_iota(jnp.int32, (nb * two_dk, rows), 0) // two_dk
             == lax.broadcasted_iota(jnp.int32, (nb * two_dk, rows), 1) // n_new)
    row_head = (lax.broadcasted_iota(jnp.int32, (nb * two_dk, 1), 0) % two_dk) // RET_DK

    qp = rq_ref[...]
    kp = rk_ref[...]
    s_old = s0_ref[...].reshape(nb * two_dk, RET_DV)
    s_old_b = s_old.astype(BF16)
    upd = []
    for j in range(2):
        h = 2 * pl.program_id(0) + j
        lg = jnp.where(h == 0, LOG_GAMMA[0], 0.0)
        for hh in range(1, H_RET):
            lg = jnp.where(h == hh, LOG_GAMMA[hh], lg)
        qm = jnp.where(lane_half == j, qp, jnp.zeros_like(qp))
        vh = rv_ref[:, j * RET_DV:(j + 1) * RET_DV]
        decay = jnp.where(same & (diff >= 0), jnp.exp(lg * jnp.maximum(diff, 0.0)), 0.0)
        scores = lax.dot_general(qm, kp, NT_DIMS, preferred_element_type=F32) * decay
        intra = jnp.dot(scores.astype(BF16), vh, preferred_element_type=F32)
        q_exp = jnp.where(q_sel, jnp.tile(qm, (1, nb)), jnp.zeros((), BF16))
        cross = jnp.dot(q_exp, s_old_b, preferred_element_type=F32) * jnp.exp(lg * (pos + 1.0))
        kd = kp.astype(F32) * jnp.exp(lg * (n_new - 1.0 - pos))
        kd_exp = jnp.where(k_sel, jnp.tile(kd.T, (nb, 1)), 0.0).astype(BF16)
        keep = jnp.exp(lg * jnp.full((1, 1), float(n_new), F32))
        upd.append((keep, jnp.dot(kd_exp, vh, preferred_element_type=F32)))
        cols = slice(j * RET_DV, (j + 1) * RET_DV)
        r_ref[:, cols] = _ret_head_out(intra + cross, gret_ref[:, cols], srg_ref[:, cols])
    s_new = jnp.where(row_head == 0, upd[0][0] * s_old + upd[0][1], upd[1][0] * s_old + upd[1][1])
    s1_ref[...] = s_new.reshape(s1_ref.shape)


def _ret_sample(rq, rk, rv, srg, g_ret, state, n_new):
    rows = rq.shape[0]
    nb = rows // n_new
    pair = pl.BlockSpec((rows, LANES), lambda p: (0, p))
    wide = pl.BlockSpec((rows, 2 * RET_DV), lambda p: (0, p))
    st = pl.BlockSpec((nb, 2, RET_DK, RET_DV), lambda p: (0, p, 0, 0))
    return pl.pallas_call(
        functools.partial(_ret_sample_kernel, n_new=n_new),
        grid=(H_RET // 2,),
        in_specs=[pair, pair, wide, wide, pl.BlockSpec((1, 2 * RET_DV), lambda p: (0, p)), st],
        out_specs=[wide, st],
        out_shape=[jax.ShapeDtypeStruct((rows, W_RET), BF16),
                   jax.ShapeDtypeStruct(state.shape, F32)],
        compiler_params=_params(("parallel",)),
        name="ret_sample",
    )(rq, rk, rv, srg, g_ret, state)


def _fox_sample_kernel(pt_ref, *refs, n_steps, **kw):
    _fox_sample_body(pl.program_id(0) * n_steps + pl.program_id(1), pl.num_programs(0) * n_steps, n_steps,
                     pt_ref, *refs, **kw)


def _fox_sample_body(step, total_steps, n_steps, pt_ref, q_ref, kn_ref, vn_ref, lfn_ref, sfg_ref, lfc_ref,
                     ck_hbm, cv_hbm, a_ref, qbd_ref, m_ref, l_ref, acc_ref, tail_ref, kbuf, vbuf, sem,
                     *, n_new, pages_per_step, page_groups, before_main=None):
    g = pages_per_step
    b, c = step // n_steps, step % n_steps

    def first_page_of(st):
        return (n_steps - 1 - st % n_steps) * g

    def page_copies(st, slot):
        row, first = st // n_steps, first_page_of(st)
        copies = []
        for p in range(g):
            page = pt_ref[row, first + p]
            copies.append(pltpu.make_async_copy(ck_hbm.at[page], kbuf.at[slot, p], sem.at[slot, 0]))
            copies.append(pltpu.make_async_copy(cv_hbm.at[page], vbuf.at[slot, p], sem.at[slot, 1]))
        return copies

    slot = step % 2
    nxt = jnp.minimum(step + 1, total_steps - 1)

    @pl.when(step == 0)
    def _():
        for cp in page_copies(step, 0):
            cp.start()

    for cp in page_copies(nxt, 1 - slot):
        cp.start()
    for cp in page_copies(step, slot):
        cp.wait()
    first_page = first_page_of(step)
    nq = n_new * H_FOX
    row_tok = lax.broadcasted_iota(jnp.int32, (nq, 1), 0) // H_FOX
    head_of_row = lax.broadcasted_iota(jnp.int32, (H_FOX, W_FOX), 0)
    head_of_lane = lax.broadcasted_iota(jnp.int32, (H_FOX, W_FOX), 1) // HEAD_DIM

    def partial_softmax(s, vt):
        m = jnp.max(s, axis=-1, keepdims=True)
        p = jnp.exp2(s - m)
        return (m, jnp.sum(p, axis=-1, keepdims=True),
                lax.dot_general(p.astype(BF16), vt, NT_DIMS, preferred_element_type=F32))

    def merge(parts):
        m_new = m_ref[...]
        for m, _, _ in parts:
            m_new = jnp.maximum(m_new, m)
        alpha = jnp.exp2(m_ref[...] - m_new)
        l_new = alpha * l_ref[...]
        acc = alpha * acc_ref[...]
        for m, l, o in parts:
            w = jnp.exp2(m - m_new)
            l_new = l_new + w * l
            acc = acc + w * o
        m_ref[...] = m_new
        l_ref[...] = l_new
        acc_ref[...] = acc

    @pl.when(c == 0)
    def _():
        q = q_ref[0].astype(F32)
        for t in range(n_new):
            qbd_ref[t * H_FOX:(t + 1) * H_FOX, :] = jnp.where(
                head_of_lane == head_of_row, jnp.broadcast_to(q[t:t + 1, :], (H_FOX, W_FOX)), 0.0).astype(BF16)
        tail_ref[...] = jnp.zeros(tail_ref.shape, F32)
        m_ref[...] = jnp.full(m_ref.shape, -1e30, F32)
        l_ref[...] = jnp.zeros(l_ref.shape, F32)
        acc_ref[...] = jnp.zeros(acc_ref.shape, F32)
        lane = lax.broadcasted_iota(jnp.int32, (H_FOX, LANES), 1)
        pfx = lfn_ref[...]
        sh = 1
        while sh < n_new:
            pfx = pfx + jnp.where(lane % n_new >= sh, pltpu.roll(pfx, sh, axis=1), 0.0)
            sh *= 2
        col = lax.broadcasted_iota(jnp.int32, (nq, LANES), 1)
        valid = (col // n_new == b) & (col % n_new <= row_tok)
        s = jnp.dot(qbd_ref[...], kn_ref[...], preferred_element_type=F32)
        merge([partial_softmax(jnp.where(valid, s - jnp.tile(pfx, (n_new, 1)) * LOG2E, -jnp.inf), vn_ref[...])])

    if before_main is not None:
        before_main()

    lf = jnp.concatenate([lfc_ref[pt_ref[b, first_page + p]] for p in range(g)], axis=0)
    hi = lf.astype(BF16)
    rest = lf - hi.astype(F32)
    mid = rest.astype(BF16)
    low = (rest - mid.astype(F32)).astype(BF16)
    t_idx = lax.broadcasted_iota(jnp.int32, (PAGE, 2 * PAGE), 0)
    j_idx = lax.broadcasted_iota(jnp.int32, (PAGE, 2 * PAGE), 1)
    later_or_all = jnp.where((j_idx >= PAGE) | (t_idx > j_idx), 1.0, 0.0).astype(BF16)
    sums = jnp.dot(jnp.concatenate([hi, mid, low], axis=0), later_or_all, preferred_element_type=F32)
    sums = sums[:g * H_FOX] + sums[g * H_FOX:2 * g * H_FOX] + sums[2 * g * H_FOX:]
    run = tail_ref[...]
    biases = [None] * g
    for p in reversed(range(g)):
        page_rows = slice(p * H_FOX, (p + 1) * H_FOX)
        biases[p] = run + sums[page_rows, :PAGE]
        run = run + sums[page_rows, PAGE:]
    tail_ref[...] = run
    key_bias = jnp.tile(jnp.concatenate(biases, axis=1) * LOG2E, (n_new, 1))
    per = g // page_groups
    scores = []
    for grp in range(page_groups):
        kc = jnp.concatenate([kbuf[slot, p].astype(BF16) for p in range(grp * per, (grp + 1) * per)],
                             axis=1)
        s = jnp.dot(qbd_ref[...], kc, preferred_element_type=F32)
        scores.append(s + key_bias[:, grp * per * PAGE:(grp + 1) * per * PAGE])
    parts = []
    for grp, s in enumerate(scores):
        vc = jnp.concatenate([vbuf[slot, p].astype(BF16) for p in range(grp * per, (grp + 1) * per)], axis=1)
        parts.append(partial_softmax(s, vc))
    merge(parts)

    o = acc_ref[...] / l_ref[...]
    sfg = sfg_ref[0].astype(F32)
    for t in range(n_new):
        ot = jnp.where(head_of_lane == head_of_row, o[t * H_FOX:(t + 1) * H_FOX, :], 0.0)
        a_ref[0, t:t + 1, :] = jnp.sum(ot, axis=0, keepdims=True) * sfg[t:t + 1, :]

    @pl.when(step == total_steps - 1)
    def _():
        for cp in page_copies(nxt, 1 - slot):
            cp.wait()


def _fox_sample(page_table, q, kt_new, vt_new, lft_new, sfg, cache_kt, cache_vt, cache_lft):
    bd, n_new, _ = q.shape
    g, n_steps, kw = _fox_sample_plan(page_table, n_new)
    in_specs, out_spec, scratch = _fox_sample_specs(n_new, g, n_steps, (kt_new, vt_new, lft_new, cache_lft),
                                                    lambda b, c, pt: (b, c, pt))
    grid_spec = pltpu.PrefetchScalarGridSpec(
        num_scalar_prefetch=1, grid=(bd, n_steps), in_specs=in_specs, out_specs=out_spec, scratch_shapes=scratch)
    return pl.pallas_call(
        functools.partial(_fox_sample_kernel, n_steps=n_steps, **kw),
        grid_spec=grid_spec,
        out_shape=jax.ShapeDtypeStruct((bd, n_new, W_FOX), F32),
        compiler_params=_params(("arbitrary", "arbitrary")),
        name="fox_sample",
    )(page_table, q, kt_new, vt_new, lft_new, sfg, cache_lft, cache_kt, cache_vt)


def _fox_sample_plan(page_table, n_new):
    n_pages = page_table.shape[1]
    g = next(c for c in (16, 8, 4, 2, 1) if n_pages % c == 0)
    page_groups = 2 if g >= 2 else 1
    return g, n_pages // g, dict(n_new=n_new, pages_per_step=g, page_groups=page_groups)


def _fox_sample_specs(n_new, g, n_steps, resident, ids):
    nq = n_new * H_FOX

    def spec(shape, f, **kw):
        return pl.BlockSpec(shape, lambda *a: f(*ids(*a)), **kw)

    def tok(n):
        return spec((1, n_new, n), lambda b, c, pt: (b, 0, 0))

    def whole(arr):
        nd = arr.ndim
        return spec(arr.shape, lambda b, c, pt: (0,) * nd, pipeline_mode=pl.Buffered(1))

    kt_new, vt_new, lft_new, cache_lft = resident
    in_memory = pl.BlockSpec(memory_space=pl.ANY)
    in_specs = [tok(W_FOX), whole(kt_new), whole(vt_new), whole(lft_new), tok(W_FOX), whole(cache_lft),
                in_memory, in_memory]
    scratch = [pltpu.VMEM((nq, W_FOX), BF16), pltpu.VMEM((nq, 1), F32), pltpu.VMEM((nq, 1), F32),
               pltpu.VMEM((nq, W_FOX), F32), pltpu.VMEM((H_FOX, LANES), F32),
               pltpu.VMEM((2, g, W_FOX, PAGE), F32), pltpu.VMEM((2, g, W_FOX, PAGE), F32),
               pltpu.SemaphoreType.DMA((2, 2))]
    return in_specs, tok(W_FOX), scratch


def _fox_fused_kernel(pt_ref, *refs, n_prompt_in, n_sample_in, n_prompt_scratch, n_steps, sample_kw):
    refs = list(refs)
    p_in, refs = refs[:n_prompt_in], refs[n_prompt_in:]
    s_in, refs = refs[:n_sample_in], refs[n_sample_in:]
    (a_p, a_d), refs = refs[:2], refs[2:]
    p_scr, s_scr = refs[:n_prompt_scratch], refs[n_prompt_scratch:]
    step = (pl.program_id(0) * pl.num_programs(1) + pl.program_id(1)) * pl.num_programs(2) + pl.program_id(2)
    total = pl.num_programs(0) * pl.num_programs(1) * pl.num_programs(2)
    def sample(prompt_prologue):
        _fox_sample_body(step, total, n_steps, pt_ref, *s_in, a_d, *s_scr, before_main=prompt_prologue, **sample_kw)

    _fox_prompt_body(pl.program_id(2), *p_in, a_p, *p_scr, blk=FOX_BLK, side_work=sample)


def _fox_fused(qt, kn, vtb, ccol, sfg, page_table, q, kt_new, vt_new, lft_new, sfg_d, cache_kt, cache_vt, cache_lft):
    b, s, _ = kn.shape
    bd, n_new, _ = q.shape
    ng, nqb = H_FOX // HEADS_PER_GROUP, s // FOX_BLK
    g, n_steps, kw = _fox_sample_plan(page_table, n_new)
    assert b * ng * nqb == bd * n_steps
    p_in, p_out, p_scr = _fox_prompt_specs(s, lambda bi, gi, i, pt: (bi, gi, i))

    def sample_ids(bi, gi, i, pt):
        step = (bi * ng + gi) * nqb + i
        return step // n_steps, step % n_steps, pt

    s_in, s_out, s_scr = _fox_sample_specs(n_new, g, n_steps, (kt_new, vt_new, lft_new, cache_lft), sample_ids)
    grid_spec = pltpu.PrefetchScalarGridSpec(
        num_scalar_prefetch=1, grid=(b, ng, nqb), in_specs=p_in + s_in, out_specs=[p_out, s_out],
        scratch_shapes=p_scr + s_scr)
    return pl.pallas_call(
        functools.partial(_fox_fused_kernel, n_prompt_in=len(p_in), n_sample_in=len(s_in),
                          n_prompt_scratch=len(p_scr), n_steps=n_steps, sample_kw=kw),
        grid_spec=grid_spec,
        out_shape=[jax.ShapeDtypeStruct((b, s, W_FOX), BF16), jax.ShapeDtypeStruct((bd, n_new, W_FOX), F32)],
        compiler_params=_params(("arbitrary", "arbitrary", "arbitrary")),
        name="fox_fused",
    )(page_table, qt, kn, vtb, ccol, sfg, q, kt_new, vt_new, lft_new, sfg_d, cache_lft, cache_kt, cache_vt)


TAIL_SUB = 256


def _tail_kernel(a_ref, r_ref, x_ref, woe_ref, gno_ref, wio_ref, gv_ref, ws_ref, bs_ref, woo_ref,
                 *out_refs, n_new):
    tile = x_ref.shape[0]
    sub = min(tile, TAIL_SUB)
    y_ref = out_refs[0]

    def front(rows):
        x1 = (x_ref[rows, :]
              + jnp.dot(a_ref[rows, :].astype(BF16), woe_ref[:W_FOX, :], preferred_element_type=F32)
              + jnp.dot(r_ref[rows, :], woe_ref[W_FOX:, :], preferred_element_type=F32))
        h = _rms(x1, gno_ref[...]).astype(BF16)
        v = jax.nn.gelu(jnp.dot(h, wio_ref[:, GM_WIDTH:2 * GM_WIDTH], preferred_element_type=F32))
        vn = _layer_norm(v, gv_ref[...])
        u = jax.nn.gelu(jnp.dot(h, wio_ref[:, :GM_WIDTH], preferred_element_type=F32))
        gate = jnp.dot(h, wio_ref[:, 2 * GM_WIDTH:], preferred_element_type=F32)
        return x1, vn, u, gate

    def back(rows, x1, vn, u, gate):
        if n_new is None:
            tril = (lax.broadcasted_iota(jnp.int32, (CHUNK, CHUNK), 0)
                    >= lax.broadcasted_iota(jnp.int32, (CHUNK, CHUNK), 1))
            vb = vn.astype(BF16)

            def spatial(g):
                wm = jnp.where(tril, ws_ref[g], 0.0).astype(BF16)
                bias = bs_ref[:, g:g + 1]
                return jnp.concatenate(
                    [jnp.dot(wm, vb[c * CHUNK:(c + 1) * CHUNK, g * GM_GC:(g + 1) * GM_GC],
                             preferred_element_type=F32) + bias for c in range(sub // CHUNK)], axis=0)
        else:
            out_refs[1][rows, :] = vn
            tok = lax.broadcasted_iota(jnp.int32, (sub, 1), 0) % n_new

            def spatial(g):
                vg = vn[:, g * GM_GC:(g + 1) * GM_GC]
                acc = jnp.zeros((sub, 1), F32)
                for t in range(n_new):
                    acc = jnp.where(tok == t, bs_ref[g, t], acc)
                acc = jnp.broadcast_to(acc, (sub, GM_GC))
                for d in range(n_new):
                    coef = jnp.zeros((sub, 1), F32)
                    for t in range(d, n_new):
                        coef = jnp.where(tok == t, ws_ref[g, t * n_new + (t - d)], coef)
                    src = vg if d == 0 else pltpu.roll(vg, d, axis=0)
                    acc = acc + coef * src
                return acc

        sv = jnp.concatenate([spatial(g) for g in range(GM_GROUPS)], axis=1)
        z = (u * sv * jax.nn.silu(gate)).astype(BF16)
        y_ref[rows, :] = x1 + jnp.dot(z, woo_ref[...], preferred_element_type=F32)

    subs = [slice(i * sub, (i + 1) * sub) for i in range(tile // sub)]
    fronts = [front(rows) for rows in subs]
    for rows, vals in zip(subs, fronts):
        back(rows, *vals)


def _tail(a, r, x2d, woe, gno, wio, gv, ws, bs, woo, tm, n_new):
    m = x2d.shape[0]
    row = lambda n: pl.BlockSpec((tm, n), lambda i: (i, 0))
    if n_new is None:
        mix_specs = [_const_spec(ws.shape), _const_spec(bs.shape)]
        out_specs = [row(D_MODEL)]
        out_shape = [jax.ShapeDtypeStruct((m, D_MODEL), F32)]
    else:
        smem = pl.BlockSpec(memory_space=pltpu.SMEM)
        mix_specs = [smem, smem]
        out_specs = [row(D_MODEL), row(GM_WIDTH)]
        out_shape = [jax.ShapeDtypeStruct((m, D_MODEL), F32), jax.ShapeDtypeStruct((m, GM_WIDTH), F32)]
    return pl.pallas_call(
        functools.partial(_tail_kernel, n_new=n_new),
        grid=(m // tm,),
        in_specs=[row(W_FOX), row(W_RET), row(D_MODEL), _const_spec(woe.shape), _const_spec((1, D_MODEL)),
                  _const_spec(wio.shape), _const_spec((1, GM_WIDTH))] + mix_specs + [_const_spec(woo.shape)],
        out_specs=out_specs, out_shape=out_shape,
        compiler_params=_params(("parallel",)),
        name="tail_prompt" if n_new is None else "tail_sample",
    )(a, r, x2d, woe, gno, wio, gv, ws, bs, woo)


def _rope_tables(pos):
    half = RET_DK // 2
    inv = ROPE_BASE ** (-jnp.arange(half, dtype=F32) / half)
    ang = pos.astype(F32)[:, None] * inv[None, :]
    cos, sin = jnp.cos(ang), jnp.sin(ang)
    return jnp.tile(jnp.concatenate([cos, cos], axis=-1), (1, 2)), jnp.tile(jnp.concatenate([-sin, sin], axis=-1), (1, 2))


def _row_tile(m, largest):
    return next(t for t in (512, 256, 128) if t <= largest and m % t == 0)


def kernel(x_prompt, x_sample, cache_k, cache_v, cache_logf, state_ret, page_table, norm_even, w_in_even,
           b_forget, qnorm, knorm, ret_norm, w_out_even, norm_odd, w_in_odd, vnorm_odd, w_spatial, b_spatial,
           w_out_odd):
    bp, s, _ = x_prompt.shape
    bd, n_new, _ = x_sample.shape
    n_pages = page_table.shape[1]
    past = n_pages * PAGE
    assert s % 256 == 0 and (bd * n_new) % CHUNK == 0 and w_in_even.shape[0] == 1 and w_in_odd.shape[0] == 1

    wi = w_in_even[0]
    sizes = (W_FOX, W_FOX, W_FOX, H_FOX, W_FOX, W_RQK, W_RQK, W_RET, W_RET)
    offs = np.concatenate([[0], np.cumsum(sizes)])
    fq, fk, fv, ff, fg, rq, rk, rv, rg = [wi[:, offs[i]:offs[i + 1]] for i in range(9)]
    w_even = jnp.concatenate([fg, rq, rk, rv, rg], axis=1).astype(BF16)
    w_even_t = jnp.concatenate([fq.T, fk.T, fv.T, ff.T, jnp.zeros((EVEN_ROWS_T - R_FF - H_FOX, D_MODEL), F32)],
                               axis=0).astype(BF16)
    b_f = b_forget[0][:, None]
    g_q = qnorm[0][:, None]
    g_k = knorm[0][:, None]
    g_ret = ret_norm[0].reshape(1, W_RET)
    woe = w_out_even[0].astype(BF16)
    wio = w_in_odd[0].astype(BF16)
    woo = w_out_odd[0].astype(BF16)
    g_ne, g_no, g_v = norm_even[0][None, :], norm_odd[0][None, :], vnorm_odd[0][None, :]

    cos_p, sin_p = _rope_tables(jnp.arange(s))
    xp2 = x_prompt.reshape(bp * s, D_MODEL)
    tm = _row_tile(s, 512)
    sfg, rqp, rkp, rvp, srg, kn, qt, kt32, _, vt32, vtb, lft = _even_proj(
        xp2, bp, g_ne, w_even, w_even_t, b_f, g_q, g_k, cos_p, sin_p, tm)
    rows = bd * n_new
    cos_d, sin_d = _rope_tables(past + jnp.arange(n_new))
    xd2 = x_sample.reshape(rows, D_MODEL)
    sfgd, rqd, rkd, rvd, srgd, _, qtd, kt32d, ktbd, vt32d, vtbd, lftd = _even_proj(
        xd2, 1, g_ne, w_even, w_even_t, b_f, g_q, g_k, jnp.tile(cos_d, (bd, 1)), jnp.tile(sin_d, (bd, 1)), rows)

    ng = H_FOX // HEADS_PER_GROUP
    c_col4 = _cumsum_lanes(lft)
    r3 = lambda t: t.reshape(bp, s, t.shape[-1])
    d3 = lambda t: t.reshape(bd, n_new, t.shape[-1])
    n_phys = cache_k.shape[1]
    pages_t = lambda cache: jnp.transpose(cache[0], (0, 2, 3, 1)).reshape(n_phys, W_FOX, PAGE)
    prompt_args = (qt, r3(kn), vtb, c_col4, r3(sfg))
    sample_args = (page_table, d3(jnp.transpose(qtd[0], (1, 0))), ktbd[0], vtbd[0], lftd[0], d3(sfgd),
                   pages_t(cache_k), pages_t(cache_v), jnp.transpose(cache_logf[0], (0, 2, 1)))
    if bp * ng * (s // FOX_BLK) == bd * _fox_sample_plan(page_table, n_new)[1]:
        a_p, a_d = _fox_fused(*prompt_args, *sample_args)
    else:
        a_p, a_d = _fox_prompt(*prompt_args), _fox_sample(*sample_args)
    r_p, st_p = _ret_prompt(r3(rqp), r3(rkp), r3(rvp), r3(srg), g_ret)
    r_d, st_d = _ret_sample(rqd, rkd, rvd, srgd, g_ret, state_ret[0], n_new)

    y_p, = _tail(a_p.reshape(bp * s, W_FOX), r_p.reshape(bp * s, W_RET), xp2, woe, g_no, wio, g_v,
                 w_spatial[0], jnp.transpose(b_spatial[0]), woo, _row_tile(s, 512), None)
    ws_new = w_spatial[0][:, :n_new, :n_new].reshape(GM_GROUPS, n_new * n_new)
    y_d, gv_d = _tail(a_d.reshape(rows, W_FOX), r_d, xd2, woe, g_no, wio, g_v,
                      ws_new, b_spatial[0][:, :n_new], woo, rows, n_new)

    heads_last = lambda t, n, length: jnp.transpose(t.reshape(n, H_FOX, HEAD_DIM, length), (0, 3, 1, 2))[None]
    tokens_first = lambda t, n: jnp.transpose(t[0], (1, 0)).reshape(1, bd, n_new, *n)
    return (y_p.reshape(bp, s, D_MODEL), y_d.reshape(bd, n_new, D_MODEL),
            heads_last(kt32, bp, s), heads_last(vt32, bp, s),
            jnp.transpose(lft, (0, 2, 1))[None], st_p.reshape(1, bp, H_RET, RET_DK, RET_DV),
            tokens_first(kt32d, (H_FOX, HEAD_DIM)), tokens_first(vt32d, (H_FOX, HEAD_DIM)),
            tokens_first(lftd, (H_FOX,)), st_d[None], gv_d.reshape(1, bd, n_new, GM_WIDTH))
```

```python
import functools

import numpy as np
import jax
import jax.numpy as jnp
from jax import lax
from jax.experimental import pallas as pl
from jax.experimental.pallas import tpu as pltpu

D_MODEL = 1024
HEAD_DIM = 64
H_FOX = 8
W_FOX = H_FOX * HEAD_DIM
H_RET = 8
RET_DK = 64
RET_DV = 128
W_RET = H_RET * RET_DV
W_RQK = H_RET * RET_DK
ROPE_BASE = 10000.0
GM_WIDTH = 2 * D_MODEL
GM_GROUPS = 8
GM_GC = GM_WIDTH // GM_GROUPS
CHUNK = 128
PAGE = 128
EPS = 1e-6
F32 = jnp.float32
BF16 = jnp.bfloat16

LANES = 128
VMEM_LIMIT = 56 * 1024 * 1024

C_FG, C_RQ, C_RK, C_RV, C_RG, EVEN_COLS = 0, 512, 1024, 1536, 2560, 3584
R_FQ, R_FK, R_FV, R_FF, EVEN_ROWS_T = 0, 512, 1024, 1536, 1552

LOG_GAMMA = [float(np.log(np.float32(1.0) - np.float32(2.0) ** np.float32(-5.0 - h))) for h in range(H_RET)]

LOG2E = float(np.log2(np.e))
NT_DIMS = (((1,), (1,)), ((), ()))
TN_DIMS = (((0,), (0,)), ((), ()))


def _params(sem):
    return pltpu.CompilerParams(dimension_semantics=sem, vmem_limit_bytes=VMEM_LIMIT)


def _const_spec(shape):
    nd = len(shape)
    return pl.BlockSpec(shape, lambda *_: (0,) * nd, pipeline_mode=pl.Buffered(1))


def _rms(x, g):
    ms = jnp.mean(x * x, axis=-1, keepdims=True)
    return x * lax.rsqrt(ms + EPS) * g


def _layer_norm(x, g):
    xc = x - jnp.mean(x, axis=-1, keepdims=True)
    return xc * lax.rsqrt(jnp.mean(xc * xc, axis=-1, keepdims=True) + EPS) * g


def _rope(x, cos, sin_signed):
    rows = x.shape[0]
    first_half = (lax.broadcasted_iota(jnp.int32, (rows, LANES), 1) % RET_DK) < RET_DK // 2
    outs = []
    for c in range(x.shape[1] // LANES):
        xc = x[:, c * LANES:(c + 1) * LANES]
        nxt = pltpu.roll(xc, LANES - RET_DK // 2, axis=1)
        prv = pltpu.roll(xc, RET_DK // 2, axis=1)
        outs.append(xc * cos + jnp.where(first_half, nxt, prv) * sin_signed)
    return jnp.concatenate(outs, axis=-1)


def _log_sigmoid(x):
    return jnp.minimum(x, 0.0) - jnp.log1p(jnp.exp(-jnp.abs(x)))


def _head_rms_t(x, g):
    tokens = x.shape[1]
    x = x.reshape(H_FOX, HEAD_DIM, tokens)
    x = x * lax.rsqrt(jnp.mean(x * x, axis=1, keepdims=True) + EPS) * g[None]
    return x.reshape(W_FOX, tokens)


PROJ_SUB = 256


def _even_proj_kernel(x_ref, g_ref, w_ref, wt_ref, bf_ref, gq_ref, gk_ref, cos_ref, sin_ref,
                      sfg_ref, rq_ref, rk_ref, rv_ref, srg_ref, kn_ref,
                      qt_ref, kt32_ref, ktb_ref, vt32_ref, vtb_ref, lft_ref):
    tile = x_ref.shape[0]
    sub = min(tile, PROJ_SUB)
    for i in range(tile // sub):
        rows = slice(i * sub, (i + 1) * sub)
        h = _rms(x_ref[rows, :], g_ref[...]).astype(BF16)

        def proj(lo, hi, h=h):
            return jnp.dot(h, w_ref[:, lo:hi], preferred_element_type=F32)

        sfg_ref[rows, :] = jax.nn.silu(proj(C_FG, C_RQ)).astype(BF16)
        cos = cos_ref[rows, :]
        sin = sin_ref[rows, :]
        rq_ref[rows, :] = _rope(proj(C_RQ, C_RK), cos, sin).astype(BF16)
        rk_ref[rows, :] = (_rope(proj(C_RK, C_RV), cos, sin) * (RET_DK ** -0.5)).astype(BF16)
        rv_ref[rows, :] = proj(C_RV, C_RG).astype(BF16)
        srg_ref[rows, :] = jax.nn.silu(proj(C_RG, EVEN_COLS)).astype(BF16)

        t = lax.dot_general(wt_ref[...], h, NT_DIMS, preferred_element_type=F32)
        qt_ref[0, :, rows] = (_head_rms_t(t[R_FQ:R_FK], gq_ref[...]) * (HEAD_DIM ** -0.5 * LOG2E)).astype(BF16)
        kt = _head_rms_t(t[R_FK:R_FV], gk_ref[...])
        kt32_ref[0, :, rows] = kt
        ktb_ref[0, :, rows] = kt.astype(BF16)
        kn_ref[rows, :] = kt.T.astype(BF16)
        vt = t[R_FV:R_FF]
        vt32_ref[0, :, rows] = vt
        vtb_ref[0, :, rows] = vt.astype(BF16)
        lft_ref[0, :, rows] = _log_sigmoid(t[R_FF:R_FF + H_FOX] + bf_ref[...])


def _even_proj(x2d, n_seq, g_norm, w, wt, b_f, g_q, g_k, cos, sin, tm):
    m = x2d.shape[0]
    s = m // n_seq
    per_seq = s // tm
    row = lambda n: pl.BlockSpec((tm, n), lambda i: (i, 0))
    tab = pl.BlockSpec((tm, LANES), lambda i: (i % per_seq, 0))
    col = lambda n: pl.BlockSpec((1, n, tm), lambda i: (i // per_seq, 0, i % per_seq))
    outs = [(W_FOX, BF16), (W_RQK, BF16), (W_RQK, BF16), (W_RET, BF16), (W_RET, BF16), (W_FOX, BF16)]
    outs_t = [(W_FOX, BF16), (W_FOX, F32), (W_FOX, BF16), (W_FOX, F32), (W_FOX, BF16), (H_FOX, F32)]
    return pl.pallas_call(
        _even_proj_kernel,
        grid=(m // tm,),
        in_specs=[row(D_MODEL), _const_spec((1, D_MODEL)), _const_spec((D_MODEL, EVEN_COLS)),
                  _const_spec((EVEN_ROWS_T, D_MODEL)), _const_spec((H_FOX, 1)), _const_spec((HEAD_DIM, 1)),
                  _const_spec((HEAD_DIM, 1)), tab, tab],
        out_specs=[row(n) for n, _ in outs] + [col(n) for n, _ in outs_t],
        out_shape=([jax.ShapeDtypeStruct((m, n), dt) for n, dt in outs]
                   + [jax.ShapeDtypeStruct((n_seq, n, s), dt) for n, dt in outs_t]),
        compiler_params=_params(("parallel",)),
        name="even_proj",
    )(x2d, g_norm, w, wt, b_f, g_q, g_k, cos, sin)


def _cumsum_kernel(x_ref, o_ref):
    x = x_ref[0]
    n = x.shape[1]
    lane = lax.broadcasted_iota(jnp.int32, x.shape, 1)
    sh = 1
    while sh < n:
        x = x + jnp.where(lane >= sh, pltpu.roll(x, sh, axis=1), 0.0)
        sh *= 2
    xt = x.T
    for g in range(o_ref.shape[1]):
        o_ref[0, g] = xt[:, g * HEADS_PER_GROUP:(g + 1) * HEADS_PER_GROUP]


def _cumsum_lanes(x):
    b, h, s = x.shape
    ng = h // HEADS_PER_GROUP
    return pl.pallas_call(
        _cumsum_kernel, grid=(b,),
        in_specs=[pl.BlockSpec((1, h, s), lambda i: (i, 0, 0))],
        out_specs=pl.BlockSpec((1, ng, s, HEADS_PER_GROUP), lambda i: (i, 0, 0, 0)),
        out_shape=jax.ShapeDtypeStruct((b, ng, s, HEADS_PER_GROUP), F32),
        compiler_params=_params(("parallel",)), name="logf_cumsum",
    )(x)


HEADS_PER_GROUP = 4
GROUP_W = HEADS_PER_GROUP * HEAD_DIM


def _fox_prompt_kernel(*refs, blk):
    _fox_prompt_body(pl.program_id(2), *refs, blk=blk)


def _fox_prompt_body(i, qt_ref, k_ref, v_ref, ccol_ref, sfg_ref, a_ref,
                     qm_ref, vp_ref, cb_ref, m_ref, acc_ref, s_ref, p_ref, *, blk, side_work=None):
    s_len = k_ref.shape[1]

    @pl.when(i == 0)
    def _():
        ones_row = jnp.where(lax.broadcasted_iota(jnp.int32, (HEAD_DIM, s_len), 0) == 0, 1.0, 0.0).astype(BF16)
        ccol = ccol_ref[0, 0] * (-LOG2E)
        for hh in range(HEADS_PER_GROUP):
            vp_ref[hh, :HEAD_DIM, :] = v_ref[0, hh * HEAD_DIM:(hh + 1) * HEAD_DIM, :]
            vp_ref[hh, HEAD_DIM:, :] = ones_row
            cb_ref[hh] = jnp.broadcast_to(ccol[:, hh:hh + 1], (s_len, LANES))

    strips = [(hh, slice(c * LANES, (c + 1) * LANES), c) for hh in range(HEADS_PER_GROUP) for c in range(blk // LANES)]

    def prologue():
        qt = qt_ref[0]
        row_head = lax.broadcasted_iota(jnp.int32, (GROUP_W, blk), 0) // HEAD_DIM
        for hh in range(HEADS_PER_GROUP):
            qm_ref[hh] = jnp.where(row_head == hh, qt, jnp.zeros_like(qt))
        m_ref[...] = jnp.full(m_ref.shape, -1e30, F32)
        acc_ref[...] = jnp.zeros(acc_ref.shape, F32)
        raw_scores(0, 0)

    def raw_scores(j, slot):
        off = pl.multiple_of(j * blk, blk)
        kb = k_ref[0, pl.ds(off, blk), :]
        for hh in range(HEADS_PER_GROUP):
            bias = cb_ref[hh, pl.ds(off, blk), :]
            s_ref[slot, hh] = (jnp.dot(kb, qm_ref[hh], preferred_element_type=F32)
                               + jnp.concatenate([bias] * (blk // LANES), axis=1))

    def consume(j, slot, masked):
        off = pl.multiple_of(j * blk, blk)

        def scores(hh, cols, c):
            s = s_ref[slot, hh, :, cols]
            if masked:
                causal = (lax.broadcasted_iota(jnp.int32, (blk, LANES), 0)
                          <= lax.broadcasted_iota(jnp.int32, (blk, LANES), 1) + c * LANES)
                s = jnp.where(causal, s, -jnp.inf)
            return s

        rescale = {}
        for hh, cols, c in strips:
            s = scores(hh, cols, c)
            m_old = m_ref[hh, :, cols]
            m_new = jnp.maximum(m_old, jnp.max(s, axis=0, keepdims=True))
            p_ref[hh, :, cols] = jnp.exp2(s - m_new).astype(BF16)
            rescale[hh, c] = jnp.exp2(m_old - m_new)
            m_ref[hh, :, cols] = m_new
        for hh in range(HEADS_PER_GROUP):
            alpha = jnp.concatenate([rescale[hh, c] for c in range(blk // LANES)], axis=1)
            pv = jnp.dot(vp_ref[hh, :, pl.ds(off, blk)], p_ref[hh], preferred_element_type=F32)
            acc_ref[hh] = alpha * acc_ref[hh] + pv

    if side_work is not None:
        side_work(prologue)
    else:
        prologue()

    def pair(t, carry):
        j = 2 * t
        raw_scores(j + 1, 1)
        consume(j, 0, False)
        raw_scores(j + 2, 0)
        consume(j + 1, 1, False)
        return carry

    lax.fori_loop(0, i // 2, pair, 0)

    @pl.when(i % 2 == 1)
    def _():
        raw_scores(i, 1)
        consume(i - 1, 0, False)
        consume(i, 1, True)

    @pl.when(i % 2 == 0)
    def _():
        consume(i, 0, True)

    ot = jnp.concatenate([acc_ref[hh, :HEAD_DIM, :] / acc_ref[hh, HEAD_DIM:HEAD_DIM + 1, :]
                          for hh in range(HEADS_PER_GROUP)], axis=0)
    a_ref[0] = (ot.T * sfg_ref[0].astype(F32)).astype(BF16)


FOX_BLK = 256


def _fox_prompt_specs(s, ids):
    blk = FOX_BLK

    def spec(shape, f):
        return pl.BlockSpec(shape, lambda *a: f(*ids(*a)))

    rows = spec((1, blk, GROUP_W), lambda bi, g, i: (bi, i, g))
    in_specs = [spec((1, GROUP_W, blk), lambda bi, g, i: (bi, g, i)),
                spec((1, s, GROUP_W), lambda bi, g, i: (bi, 0, g)),
                spec((1, GROUP_W, s), lambda bi, g, i: (bi, g, 0)),
                spec((1, 1, s, HEADS_PER_GROUP), lambda bi, g, i: (bi, g, 0, 0)),
                rows]
    scratch = [pltpu.VMEM((HEADS_PER_GROUP, GROUP_W, blk), BF16),
               pltpu.VMEM((HEADS_PER_GROUP, LANES, s), BF16),
               pltpu.VMEM((HEADS_PER_GROUP, s, LANES), F32),
               pltpu.VMEM((HEADS_PER_GROUP, 1, blk), F32),
               pltpu.VMEM((HEADS_PER_GROUP, LANES, blk), F32),
               pltpu.VMEM((2, HEADS_PER_GROUP, blk, blk), F32),
               pltpu.VMEM((HEADS_PER_GROUP, blk, blk), BF16)]
    return in_specs, rows, scratch


def _fox_prompt(qt, kn, vtb, ccol, sfg):
    b, s, _ = kn.shape
    in_specs, out_spec, scratch = _fox_prompt_specs(s, lambda bi, g, i: (bi, g, i))
    return pl.pallas_call(
        functools.partial(_fox_prompt_kernel, blk=FOX_BLK),
        grid=(b, H_FOX // HEADS_PER_GROUP, s // FOX_BLK),
        in_specs=in_specs, out_specs=out_spec,
        out_shape=jax.ShapeDtypeStruct((b, s, W_FOX), BF16),
        scratch_shapes=scratch,
        compiler_params=_params(("parallel", "parallel", "arbitrary")),
        name="fox_prompt",
    )(qt, kn, vtb, ccol, sfg)


def _ret_head_out(o, g, srg):
    return (_layer_norm(o, g) * srg.astype(F32)).astype(BF16)


def _ret_prompt_kernel(rq_ref, rk_ref, rv_ref, srg_ref, gret_ref, r_ref, st_ref, dec_ref, kdec_ref, *, chunks):
    L = CHUNK
    c = pl.program_id(1)

    @pl.when(c == 0)
    def _():
        st_ref[...] = jnp.zeros(st_ref.shape, F32)
        diff = (lax.broadcasted_iota(jnp.int32, (L, L), 0)
                - lax.broadcasted_iota(jnp.int32, (L, L), 1)).astype(F32)
        for h in range(H_RET):
            dec_ref[h] = jnp.where(diff >= 0, jnp.exp(LOG_GAMMA[h] * jnp.maximum(diff, 0.0)), 0.0)
        left = (L - 1.0) - lax.broadcasted_iota(jnp.int32, (L, LANES), 0).astype(F32)
        first = lax.broadcasted_iota(jnp.int32, (L, LANES), 1) < RET_DK
        for pp in range(H_RET // 2):
            kdec_ref[pp] = jnp.exp(jnp.where(first, LOG_GAMMA[2 * pp], LOG_GAMMA[2 * pp + 1]) * left)

    pos = lax.broadcasted_iota(jnp.int32, (L, 1), 0).astype(F32)
    lane_half = lax.broadcasted_iota(jnp.int32, (L, LANES), 1) // RET_DK
    for ci in range(chunks):
        rows_c = slice(ci * L, (ci + 1) * L)
        first_stage = []
        for pp in range(H_RET // 2):
            qp = rq_ref[0, rows_c, pp * LANES:(pp + 1) * LANES]
            kp = rk_ref[0, rows_c, pp * LANES:(pp + 1) * LANES]
            s_pair = st_ref[0, pp]
            s_pair_b = s_pair.astype(BF16)
            kd = (kp.astype(F32) * kdec_ref[pp]).astype(BF16)
            for j in range(2):
                h = 2 * pp + j
                qm = jnp.where(lane_half == j, qp, jnp.zeros_like(qp))
                vh = rv_ref[0, rows_c, h * RET_DV:(h + 1) * RET_DV]
                scores = lax.dot_general(qm, kp, NT_DIMS, preferred_element_type=F32)
                cross = jnp.dot(qm, s_pair_b, preferred_element_type=F32)
                upd = lax.dot_general(kd, vh, TN_DIMS, preferred_element_type=F32)
                first_stage.append((scores, cross, upd, vh, s_pair))
        outs = []
        for h, (scores, cross, upd, vh, s_pair) in enumerate(first_stage):
            lg = LOG_GAMMA[h]
            intra = jnp.dot((scores * dec_ref[h]).astype(BF16), vh, preferred_element_type=F32)
            outs.append(intra + cross * jnp.exp(lg * (pos + 1.0)))
            rows = slice((h % 2) * RET_DK, (h % 2 + 1) * RET_DK)
            st_ref[0, h // 2, rows, :] = float(np.exp(lg * L)) * s_pair[rows] + upd[rows]
        for h, o in enumerate(outs):
            cols = slice(h * RET_DV, (h + 1) * RET_DV)
            r_ref[0, rows_c, cols] = _ret_head_out(o, gret_ref[:, cols], srg_ref[0, rows_c, cols])


def _ret_prompt(rq, rk, rv, srg, g_ret):
    b, s, _ = rq.shape
    chunks = next(n for n in (8, 4, 2, 1) if s % (n * CHUNK) == 0)
    rows = chunks * CHUNK
    qk = pl.BlockSpec((1, rows, W_RQK), lambda bi, c: (bi, c, 0))
    wide = pl.BlockSpec((1, rows, W_RET), lambda bi, c: (bi, c, 0))
    st = pl.BlockSpec((1, H_RET // 2, 2 * RET_DK, RET_DV), lambda bi, c: (bi, 0, 0, 0))
    return pl.pallas_call(
        functools.partial(_ret_prompt_kernel, chunks=chunks),
        grid=(b, s // rows),
        in_specs=[qk, qk, wide, wide, _const_spec((1, W_RET))],
        out_specs=[wide, st],
        out_shape=[jax.ShapeDtypeStruct((b, s, W_RET), BF16),
                   jax.ShapeDtypeStruct((b, H_RET // 2, 2 * RET_DK, RET_DV), F32)],
        scratch_shapes=[pltpu.VMEM((H_RET, CHUNK, CHUNK), F32), pltpu.VMEM((H_RET // 2, CHUNK, LANES), F32)],
        compiler_params=_params(("parallel", "arbitrary")),
        name="ret_prompt",
    )(rq, rk, rv, srg, g_ret)


def _ret_sample_kernel(rq_ref, rk_ref, rv_ref, srg_ref, gret_ref, s0_ref, r_ref, s1_ref, *, n_new):
    rows = rq_ref.shape[0]
    nb = rows // n_new
    two_dk = 2 * RET_DK
    ri = lax.broadcasted_iota(jnp.int32, (rows, rows), 0)
    ci = lax.broadcasted_iota(jnp.int32, (rows, rows), 1)
    same = (ri // n_new) == (ci // n_new)
    diff = ((ri % n_new) - (ci % n_new)).astype(F32)
    pos = (lax.broadcasted_iota(jnp.int32, (rows, 1), 0) % n_new).astype(F32)
    lane_half = lax.broadcasted_iota(jnp.int32, (rows, LANES), 1) // RET_DK
    q_sel = (lax.broadcasted_iota(jnp.int32, (rows, nb * two_dk), 1) // two_dk
             == lax.broadcasted_iota(jnp.int32, (rows, nb * two_dk), 0) // n_new)
    k_sel = (lax.broadcasted_iota(jnp.int32, (nb * two_dk, rows), 0) // two_dk
             == lax.broadcasted_iota(jnp.int32, (nb * two_dk, rows), 1) // n_new)
    row_head = (lax.broadcasted_iota(jnp.int32, (nb * two_dk, 1), 0) % two_dk) // RET_DK

    qp = rq_ref[...]
    kp = rk_ref[...]
    s_old = s0_ref[...].reshape(nb * two_dk, RET_DV)
    s_old_b = s_old.astype(BF16)
    upd = []
    for j in range(2):
        h = 2 * pl.program_id(0) + j
        lg = jnp.where(h == 0, LOG_GAMMA[0], 0.0)
        for hh in range(1, H_RET):
            lg = jnp.where(h == hh, LOG_GAMMA[hh], lg)
        qm = jnp.where(lane_half == j, qp, jnp.zeros_like(qp))
        vh = rv_ref[:, j * RET_DV:(j + 1) * RET_DV]
        decay = jnp.where(same & (diff >= 0), jnp.exp(lg * jnp.maximum(diff, 0.0)), 0.0)
        scores = lax.dot_general(qm, kp, NT_DIMS, preferred_element_type=F32) * decay
        intra = jnp.dot(scores.astype(BF16), vh, preferred_element_type=F32)
        q_exp = jnp.where(q_sel, jnp.tile(qm, (1, nb)), jnp.zeros((), BF16))
        cross = jnp.dot(q_exp, s_old_b, preferred_element_type=F32) * jnp.exp(lg * (pos + 1.0))
        kd = kp.astype(F32) * jnp.exp(lg * (n_new - 1.0 - pos))
        kd_exp = jnp.where(k_sel, jnp.tile(kd.T, (nb, 1)), 0.0).astype(BF16)
        keep = jnp.exp(lg * jnp.full((1, 1), float(n_new), F32))
        upd.append((keep, jnp.dot(kd_exp, vh, preferred_element_type=F32)))
        cols = slice(j * RET_DV, (j + 1) * RET_DV)
        r_ref[:, cols] = _ret_head_out(intra + cross, gret_ref[:, cols], srg_ref[:, cols])
    s_new = jnp.where(row_head == 0, upd[0][0] * s_old + upd[0][1], upd[1][0] * s_old + upd[1][1])
    s1_ref[...] = s_new.reshape(s1_ref.shape)


def _ret_sample(rq, rk, rv, srg, g_ret, state, n_new):
    rows = rq.shape[0]
    nb = rows // n_new
    pair = pl.BlockSpec((rows, LANES), lambda p: (0, p))
    wide = pl.BlockSpec((rows, 2 * RET_DV), lambda p: (0, p))
    st = pl.BlockSpec((nb, 2, RET_DK, RET_DV), lambda p: (0, p, 0, 0))
    return pl.pallas_call(
        functools.partial(_ret_sample_kernel, n_new=n_new),
        grid=(H_RET // 2,),
        in_specs=[pair, pair, wide, wide, pl.BlockSpec((1, 2 * RET_DV), lambda p: (0, p)), st],
        out_specs=[wide, st],
        out_shape=[jax.ShapeDtypeStruct((rows, W_RET), BF16),
                   jax.ShapeDtypeStruct(state.shape, F32)],
        compiler_params=_params(("parallel",)),
        name="ret_sample",
    )(rq, rk, rv, srg, g_ret, state)


def _fox_sample_kernel(pt_ref, *refs, n_steps, **kw):
    _fox_sample_body(pl.program_id(0) * n_steps + pl.program_id(1), pl.num_programs(0) * n_steps, n_steps,
                     pt_ref, *refs, **kw)


def _fox_sample_body(step, total_steps, n_steps, pt_ref, q_ref, kn_ref, vn_ref, lfn_ref, sfg_ref, lfc_ref,
                     ck_hbm, cv_hbm, a_ref, qbd_ref, m_ref, l_ref, acc_ref, tail_ref, kbuf, vbuf, sem,
                     *, n_new, pages_per_step, page_groups, before_main=None, copies=True, compute=True):
    g = pages_per_step
    b, c = step // n_steps, step % n_steps

    def first_page_of(st):
        return (n_steps - 1 - st % n_steps) * g

    def page_copies(st, slot):
        row, first = st // n_steps, first_page_of(st)
        copies = []
        for p in range(g):
            page = pt_ref[row, first + p]
            copies.append(pltpu.make_async_copy(ck_hbm.at[page], kbuf.at[slot, p], sem.at[slot, 0]))
            copies.append(pltpu.make_async_copy(cv_hbm.at[page], vbuf.at[slot, p], sem.at[slot, 1]))
        return copies

    slot = step % 2
    nxt = jnp.minimum(step + 1, total_steps - 1)

    if copies:
        @pl.when(step == 0)
        def _():
            for cp in page_copies(step, 0):
                cp.start()

        for cp in page_copies(nxt, 1 - slot):
            cp.start()
        for cp in page_copies(step, slot):
            cp.wait()
    if not compute:
        return
    first_page = first_page_of(step)
    nq = n_new * H_FOX
    row_tok = lax.broadcasted_iota(jnp.int32, (nq, 1), 0) // H_FOX
    head_of_row = lax.broadcasted_iota(jnp.int32, (H_FOX, W_FOX), 0)
    head_of_lane = lax.broadcasted_iota(jnp.int32, (H_FOX, W_FOX), 1) // HEAD_DIM

    def partial_softmax(s, vt):
        m = jnp.max(s, axis=-1, keepdims=True)
        p = jnp.exp2(s - m)
        return (m, jnp.sum(p, axis=-1, keepdims=True),
                lax.dot_general(p.astype(BF16), vt, NT_DIMS, preferred_element_type=F32))

    def merge(parts):
        m_new = m_ref[...]
        for m, _, _ in parts:
            m_new = jnp.maximum(m_new, m)
        alpha = jnp.exp2(m_ref[...] - m_new)
        l_new = alpha * l_ref[...]
        acc = alpha * acc_ref[...]
        for m, l, o in parts:
            w = jnp.exp2(m - m_new)
            l_new = l_new + w * l
            acc = acc + w * o
        m_ref[...] = m_new
        l_ref[...] = l_new
        acc_ref[...] = acc

    @pl.when(c == 0)
    def _():
        q = q_ref[0].astype(F32)
        for t in range(n_new):
            qbd_ref[t * H_FOX:(t + 1) * H_FOX, :] = jnp.where(
                head_of_lane == head_of_row, jnp.broadcast_to(q[t:t + 1, :], (H_FOX, W_FOX)), 0.0).astype(BF16)
        tail_ref[...] = jnp.zeros(tail_ref.shape, F32)
        m_ref[...] = jnp.full(m_ref.shape, -1e30, F32)
        l_ref[...] = jnp.zeros(l_ref.shape, F32)
        acc_ref[...] = jnp.zeros(acc_ref.shape, F32)
        lane = lax.broadcasted_iota(jnp.int32, (H_FOX, LANES), 1)
        pfx = lfn_ref[...]
        sh = 1
        while sh < n_new:
            pfx = pfx + jnp.where(lane % n_new >= sh, pltpu.roll(pfx, sh, axis=1), 0.0)
            sh *= 2
        col = lax.broadcasted_iota(jnp.int32, (nq, LANES), 1)
        valid = (col // n_new == b) & (col % n_new <= row_tok)
        s = jnp.dot(qbd_ref[...], kn_ref[...], preferred_element_type=F32)
        merge([partial_softmax(jnp.where(valid, s - jnp.tile(pfx, (n_new, 1)) * LOG2E, -jnp.inf), vn_ref[...])])

    if before_main is not None:
        before_main()

    lf = jnp.concatenate([lfc_ref[pt_ref[b, first_page + p]] for p in range(g)], axis=0)
    hi = lf.astype(BF16)
    rest = lf - hi.astype(F32)
    mid = rest.astype(BF16)
    low = (rest - mid.astype(F32)).astype(BF16)
    t_idx = lax.broadcasted_iota(jnp.int32, (PAGE, 2 * PAGE), 0)
    j_idx = lax.broadcasted_iota(jnp.int32, (PAGE, 2 * PAGE), 1)
    later_or_all = jnp.where((j_idx >= PAGE) | (t_idx > j_idx), 1.0, 0.0).astype(BF16)
    sums = jnp.dot(jnp.concatenate([hi, mid, low], axis=0), later_or_all, preferred_element_type=F32)
    sums = sums[:g * H_FOX] + sums[g * H_FOX:2 * g * H_FOX] + sums[2 * g * H_FOX:]
    run = tail_ref[...]
    biases = [None] * g
    for p in reversed(range(g)):
        page_rows = slice(p * H_FOX, (p + 1) * H_FOX)
        biases[p] = run + sums[page_rows, :PAGE]
        run = run + sums[page_rows, PAGE:]
    tail_ref[...] = run
    key_bias = jnp.tile(jnp.concatenate(biases, axis=1) * LOG2E, (n_new, 1))
    per = g // page_groups
    scores = []
    for grp in range(page_groups):
        kc = jnp.concatenate([kbuf[slot, p].astype(BF16) for p in range(grp * per, (grp + 1) * per)],
                             axis=1)
        s = jnp.dot(qbd_ref[...], kc, preferred_element_type=F32)
        scores.append(s + key_bias[:, grp * per * PAGE:(grp + 1) * per * PAGE])
    parts = []
    for grp, s in enumerate(scores):
        vc = jnp.concatenate([vbuf[slot, p].astype(BF16) for p in range(grp * per, (grp + 1) * per)], axis=1)
        parts.append(partial_softmax(s, vc))
    merge(parts)

    o = acc_ref[...] / l_ref[...]
    sfg = sfg_ref[0].astype(F32)
    for t in range(n_new):
        ot = jnp.where(head_of_lane == head_of_row, o[t * H_FOX:(t + 1) * H_FOX, :], 0.0)
        a_ref[0, t:t + 1, :] = jnp.sum(ot, axis=0, keepdims=True) * sfg[t:t + 1, :]

    @pl.when(step == total_steps - 1)
    def _():
        for cp in page_copies(nxt, 1 - slot):
            cp.wait()


def _fox_sample(page_table, q, kt_new, vt_new, lft_new, sfg, cache_kt, cache_vt, cache_lft):
    bd, n_new, _ = q.shape
    g, n_steps, kw = _fox_sample_plan(page_table, n_new)
    in_specs, out_spec, scratch = _fox_sample_specs(n_new, g, n_steps, (kt_new, vt_new, lft_new, cache_lft),
                                                    lambda b, c, pt: (b, c, pt))
    grid_spec = pltpu.PrefetchScalarGridSpec(
        num_scalar_prefetch=1, grid=(bd, n_steps), in_specs=in_specs, out_specs=out_spec, scratch_shapes=scratch)
    return pl.pallas_call(
        functools.partial(_fox_sample_kernel, n_steps=n_steps, **kw),
        grid_spec=grid_spec,
        out_shape=jax.ShapeDtypeStruct((bd, n_new, W_FOX), F32),
        compiler_params=_params(("arbitrary", "arbitrary")),
        name="fox_sample",
    )(page_table, q, kt_new, vt_new, lft_new, sfg, cache_lft, cache_kt, cache_vt)


def _fox_sample_plan(page_table, n_new):
    n_pages = page_table.shape[1]
    g = next(c for c in (16, 8, 4, 2, 1) if n_pages % c == 0)
    page_groups = 2 if g >= 2 else 1
    return g, n_pages // g, dict(n_new=n_new, pages_per_step=g, page_groups=page_groups)


def _fox_sample_specs(n_new, g, n_steps, resident, ids):
    nq = n_new * H_FOX

    def spec(shape, f, **kw):
        return pl.BlockSpec(shape, lambda *a: f(*ids(*a)), **kw)

    def tok(n):
        return spec((1, n_new, n), lambda b, c, pt: (b, 0, 0))

    def whole(arr):
        nd = arr.ndim
        return spec(arr.shape, lambda b, c, pt: (0,) * nd, pipeline_mode=pl.Buffered(1))

    kt_new, vt_new, lft_new, cache_lft = resident
    in_memory = pl.BlockSpec(memory_space=pl.ANY)
    in_specs = [tok(W_FOX), whole(kt_new), whole(vt_new), whole(lft_new), tok(W_FOX), whole(cache_lft),
                in_memory, in_memory]
    scratch = [pltpu.VMEM((nq, W_FOX), BF16), pltpu.VMEM((nq, 1), F32), pltpu.VMEM((nq, 1), F32),
               pltpu.VMEM((nq, W_FOX), F32), pltpu.VMEM((H_FOX, LANES), F32),
               pltpu.VMEM((2, g, W_FOX, PAGE), F32), pltpu.VMEM((2, g, W_FOX, PAGE), F32),
               pltpu.SemaphoreType.DMA((2, 2))]
    return in_specs, tok(W_FOX), scratch


def _fox_fused_kernel(pt_ref, *refs, n_prompt_in, n_sample_in, n_prompt_scratch, n_steps, sample_kw):
    refs = list(refs)
    p_in, refs = refs[:n_prompt_in], refs[n_prompt_in:]
    s_in, refs = refs[:n_sample_in], refs[n_sample_in:]
    (a_p, a_d), refs = refs[:2], refs[2:]
    p_scr, s_scr = refs[:n_prompt_scratch], refs[n_prompt_scratch:]
    step = (pl.program_id(0) * pl.num_programs(1) + pl.program_id(1)) * pl.num_programs(2) + pl.program_id(2)
    total = pl.num_programs(0) * pl.num_programs(1) * pl.num_programs(2)
    sample = functools.partial(_fox_sample_body, step, total, n_steps, pt_ref, *s_in, a_d, *s_scr, **sample_kw)
    sample(compute=False)
    _fox_prompt_body(pl.program_id(2), *p_in, a_p, *p_scr, blk=FOX_BLK,
                     side_work=lambda prologue: sample(copies=False, before_main=prologue))


def _fox_fused(qt, kn, vtb, ccol, sfg, page_table, q, kt_new, vt_new, lft_new, sfg_d, cache_kt, cache_vt, cache_lft):
    b, s, _ = kn.shape
    bd, n_new, _ = q.shape
    ng, nqb = H_FOX // HEADS_PER_GROUP, s // FOX_BLK
    g, n_steps, kw = _fox_sample_plan(page_table, n_new)
    assert b * ng * nqb == bd * n_steps
    p_in, p_out, p_scr = _fox_prompt_specs(s, lambda bi, gi, i, pt: (bi, gi, i))

    def sample_ids(bi, gi, i, pt):
        step = (bi * ng + gi) * nqb + i
        return step // n_steps, step % n_steps, pt

    s_in, s_out, s_scr = _fox_sample_specs(n_new, g, n_steps, (kt_new, vt_new, lft_new, cache_lft), sample_ids)
    grid_spec = pltpu.PrefetchScalarGridSpec(
        num_scalar_prefetch=1, grid=(b, ng, nqb), in_specs=p_in + s_in, out_specs=[p_out, s_out],
        scratch_shapes=p_scr + s_scr)
    return pl.pallas_call(
        functools.partial(_fox_fused_kernel, n_prompt_in=len(p_in), n_sample_in=len(s_in),
                          n_prompt_scratch=len(p_scr), n_steps=n_steps, sample_kw=kw),
        grid_spec=grid_spec,
        out_shape=[jax.ShapeDtypeStruct((b, s, W_FOX), BF16), jax.ShapeDtypeStruct((bd, n_new, W_FOX), F32)],
        compiler_params=_params(("arbitrary", "arbitrary", "arbitrary")),
        name="fox_fused",
    )(page_table, qt, kn, vtb, ccol, sfg, q, kt_new, vt_new, lft_new, sfg_d, cache_lft, cache_kt, cache_vt)


TAIL_SUB = 256


def _tail_kernel(a_ref, r_ref, x_ref, woe_ref, gno_ref, wio_ref, gv_ref, ws_ref, bs_ref, woo_ref,
                 *out_refs, n_new):
    tile = x_ref.shape[0]
    sub = min(tile, TAIL_SUB)
    y_ref = out_refs[0]

    def front(rows):
        x1 = (x_ref[rows, :]
              + jnp.dot(a_ref[rows, :].astype(BF16), woe_ref[:W_FOX, :], preferred_element_type=F32)
              + jnp.dot(r_ref[rows, :], woe_ref[W_FOX:, :], preferred_element_type=F32))
        h = _rms(x1, gno_ref[...]).astype(BF16)
        v = jax.nn.gelu(jnp.dot(h, wio_ref[:, GM_WIDTH:2 * GM_WIDTH], preferred_element_type=F32))
        vn = _layer_norm(v, gv_ref[...])
        u = jax.nn.gelu(jnp.dot(h, wio_ref[:, :GM_WIDTH], preferred_element_type=F32))
        gate = jnp.dot(h, wio_ref[:, 2 * GM_WIDTH:], preferred_element_type=F32)
        return x1, vn, u, gate

    def back(rows, x1, vn, u, gate):
        if n_new is None:
            tril = (lax.broadcasted_iota(jnp.int32, (CHUNK, CHUNK), 0)
                    >= lax.broadcasted_iota(jnp.int32, (CHUNK, CHUNK), 1))
            vb = vn.astype(BF16)

            def spatial(g):
                wm = jnp.where(tril, ws_ref[g], 0.0).astype(BF16)
                bias = bs_ref[:, g:g + 1]
                return jnp.concatenate(
                    [jnp.dot(wm, vb[c * CHUNK:(c + 1) * CHUNK, g * GM_GC:(g + 1) * GM_GC],
                             preferred_element_type=F32) + bias for c in range(sub // CHUNK)], axis=0)
        else:
            out_refs[1][rows, :] = vn
            tok = lax.broadcasted_iota(jnp.int32, (sub, 1), 0) % n_new

            def spatial(g):
                vg = vn[:, g * GM_GC:(g + 1) * GM_GC]
                acc = jnp.zeros((sub, 1), F32)
                for t in range(n_new):
                    acc = jnp.where(tok == t, bs_ref[g, t], acc)
                acc = jnp.broadcast_to(acc, (sub, GM_GC))
                for d in range(n_new):
                    coef = jnp.zeros((sub, 1), F32)
                    for t in range(d, n_new):
                        coef = jnp.where(tok == t, ws_ref[g, t * n_new + (t - d)], coef)
                    src = vg if d == 0 else pltpu.roll(vg, d, axis=0)
                    acc = acc + coef * src
                return acc

        sv = jnp.concatenate([spatial(g) for g in range(GM_GROUPS)], axis=1)
        z = (u * sv * jax.nn.silu(gate)).astype(BF16)
        y_ref[rows, :] = x1 + jnp.dot(z, woo_ref[...], preferred_element_type=F32)

    subs = [slice(i * sub, (i + 1) * sub) for i in range(tile // sub)]
    fronts = [front(rows) for rows in subs]
    for rows, vals in zip(subs, fronts):
        back(rows, *vals)


def _tail(a, r, x2d, woe, gno, wio, gv, ws, bs, woo, tm, n_new):
    m = x2d.shape[0]
    row = lambda n: pl.BlockSpec((tm, n), lambda i: (i, 0))
    if n_new is None:
        mix_specs = [_const_spec(ws.shape), _const_spec(bs.shape)]
        out_specs = [row(D_MODEL)]
        out_shape = [jax.ShapeDtypeStruct((m, D_MODEL), F32)]
    else:
        smem = pl.BlockSpec(memory_space=pltpu.SMEM)
        mix_specs = [smem, smem]
        out_specs = [row(D_MODEL), row(GM_WIDTH)]
        out_shape = [jax.ShapeDtypeStruct((m, D_MODEL), F32), jax.ShapeDtypeStruct((m, GM_WIDTH), F32)]
    return pl.pallas_call(
        functools.partial(_tail_kernel, n_new=n_new),
        grid=(m // tm,),
        in_specs=[row(W_FOX), row(W_RET), row(D_MODEL), _const_spec(woe.shape), _const_spec((1, D_MODEL)),
                  _const_spec(wio.shape), _const_spec((1, GM_WIDTH))] + mix_specs + [_const_spec(woo.shape)],
        out_specs=out_specs, out_shape=out_shape,
        compiler_params=_params(("parallel",)),
        name="tail_prompt" if n_new is None else "tail_sample",
    )(a, r, x2d, woe, gno, wio, gv, ws, bs, woo)


def _rope_tables(pos):
    half = RET_DK // 2
    inv = ROPE_BASE ** (-jnp.arange(half, dtype=F32) / half)
    ang = pos.astype(F32)[:, None] * inv[None, :]
    cos, sin = jnp.cos(ang), jnp.sin(ang)
    return jnp.tile(jnp.concatenate([cos, cos], axis=-1), (1, 2)), jnp.tile(jnp.concatenate([-sin, sin], axis=-1), (1, 2))


def _row_tile(m, largest):
    return next(t for t in (512, 256, 128) if t <= largest and m % t == 0)


def kernel(x_prompt, x_sample, cache_k, cache_v, cache_logf, state_ret, page_table, norm_even, w_in_even,
           b_forget, qnorm, knorm, ret_norm, w_out_even, norm_odd, w_in_odd, vnorm_odd, w_spatial, b_spatial,
           w_out_odd):
    bp, s, _ = x_prompt.shape
    bd, n_new, _ = x_sample.shape
    n_pages = page_table.shape[1]
    past = n_pages * PAGE
    assert s % 256 == 0 and (bd * n_new) % CHUNK == 0 and w_in_even.shape[0] == 1 and w_in_odd.shape[0] == 1

    wi = w_in_even[0]
    sizes = (W_FOX, W_FOX, W_FOX, H_FOX, W_FOX, W_RQK, W_RQK, W_RET, W_RET)
    offs = np.concatenate([[0], np.cumsum(sizes)])
    fq, fk, fv, ff, fg, rq, rk, rv, rg = [wi[:, offs[i]:offs[i + 1]] for i in range(9)]
    w_even = jnp.concatenate([fg, rq, rk, rv, rg], axis=1).astype(BF16)
    w_even_t = jnp.concatenate([fq.T, fk.T, fv.T, ff.T, jnp.zeros((EVEN_ROWS_T - R_FF - H_FOX, D_MODEL), F32)],
                               axis=0).astype(BF16)
    b_f = b_forget[0][:, None]
    g_q = qnorm[0][:, None]
    g_k = knorm[0][:, None]
    g_ret = ret_norm[0].reshape(1, W_RET)
    woe = w_out_even[0].astype(BF16)
    wio = w_in_odd[0].astype(BF16)
    woo = w_out_odd[0].astype(BF16)
    g_ne, g_no, g_v = norm_even[0][None, :], norm_odd[0][None, :], vnorm_odd[0][None, :]

    cos_p, sin_p = _rope_tables(jnp.arange(s))
    xp2 = x_prompt.reshape(bp * s, D_MODEL)
    tm = _row_tile(s, 512)
    sfg, rqp, rkp, rvp, srg, kn, qt, kt32, _, vt32, vtb, lft = _even_proj(
        xp2, bp, g_ne, w_even, w_even_t, b_f, g_q, g_k, cos_p, sin_p, tm)
    rows = bd * n_new
    cos_d, sin_d = _rope_tables(past + jnp.arange(n_new))
    xd2 = x_sample.reshape(rows, D_MODEL)
    sfgd, rqd, rkd, rvd, srgd, _, qtd, kt32d, ktbd, vt32d, vtbd, lftd = _even_proj(
        xd2, 1, g_ne, w_even, w_even_t, b_f, g_q, g_k, jnp.tile(cos_d, (bd, 1)), jnp.tile(sin_d, (bd, 1)), rows)

    ng = H_FOX // HEADS_PER_GROUP
    c_col4 = _cumsum_lanes(lft)
    r3 = lambda t: t.reshape(bp, s, t.shape[-1])
    d3 = lambda t: t.reshape(bd, n_new, t.shape[-1])
    n_phys = cache_k.shape[1]
    pages_t = lambda cache: jnp.transpose(cache[0], (0, 2, 3, 1)).reshape(n_phys, W_FOX, PAGE)
    prompt_args = (qt, r3(kn), vtb, c_col4, r3(sfg))
    sample_args = (page_table, d3(jnp.transpose(qtd[0], (1, 0))), ktbd[0], vtbd[0], lftd[0], d3(sfgd),
                   pages_t(cache_k), pages_t(cache_v), jnp.transpose(cache_logf[0], (0, 2, 1)))
    if bp * ng * (s // FOX_BLK) == bd * _fox_sample_plan(page_table, n_new)[1]:
        a_p, a_d = _fox_fused(*prompt_args, *sample_args)
    else:
        a_p, a_d = _fox_prompt(*prompt_args), _fox_sample(*sample_args)
    r_p, st_p = _ret_prompt(r3(rqp), r3(rkp), r3(rvp), r3(srg), g_ret)
    r_d, st_d = _ret_sample(rqd, rkd, rvd, srgd, g_ret, state_ret[0], n_new)

    y_p, = _tail(a_p.reshape(bp * s, W_FOX), r_p.reshape(bp * s, W_RET), xp2, woe, g_no, wio, g_v,
                 w_spatial[0], jnp.transpose(b_spatial[0]), woo, _row_tile(s, 512), None)
    ws_new = w_spatial[0][:, :n_new, :n_new].reshape(GM_GROUPS, n_new * n_new)
    y_d, gv_d = _tail(a_d.reshape(rows, W_FOX), r_d, xd2, woe, g_no, wio, g_v,
                      ws_new, b_spatial[0][:, :n_new], woo, rows, n_new)

    heads_last = lambda t, n, length: jnp.transpose(t.reshape(n, H_FOX, HEAD_DIM, length), (0, 3, 1, 2))[None]
    tokens_first = lambda t, n: jnp.transpose(t[0], (1, 0)).reshape(1, bd, n_new, *n)
    return (y_p.reshape(bp, s, D_MODEL), y_d.reshape(bd, n_new, D_MODEL),
            heads_last(kt32, bp, s), heads_last(vt32, bp, s),
            jnp.transpose(lft, (0, 2, 1))[None], st_p.reshape(1, bp, H_RET, RET_DK, RET_DV),
            tokens_first(kt32d, (H_FOX, HEAD_DIM)), tokens_first(vt32d, (H_FOX, HEAD_DIM)),
            tokens_first(lftd, (H_FOX,)), st_d[None], gv_d.reshape(1, bd, n_new, GM_WIDTH))
```

```python
import functools

import numpy as np
import jax
import jax.numpy as jnp
from jax import lax
from jax.experimental import pallas as pl
from jax.experimental.pallas import tpu as pltpu

D_MODEL = 1024
HEAD_DIM = 64
H_FOX = 8
W_FOX = H_FOX * HEAD_DIM
H_RET = 8
RET_DK = 64
RET_DV = 128
W_RET = H_RET * RET_DV
W_RQK = H_RET * RET_DK
ROPE_BASE = 10000.0
GM_WIDTH = 2 * D_MODEL
GM_GROUPS = 8
GM_GC = GM_WIDTH // GM_GROUPS
CHUNK = 128
PAGE = 128
EPS = 1e-6
F32 = jnp.float32
BF16 = jnp.bfloat16

LANES = 128
VMEM_LIMIT = 56 * 1024 * 1024

C_FG, C_RQ, C_RK, C_RV, C_RG, EVEN_COLS = 0, 512, 1024, 1536, 2560, 3584
R_FQ, R_FK, R_FV, R_FF, EVEN_ROWS_T = 0, 512, 1024, 1536, 1552

LOG_GAMMA = [float(np.log(np.float32(1.0) - np.float32(2.0) ** np.float32(-5.0 - h))) for h in range(H_RET)]

LOG2E = float(np.log2(np.e))
NT_DIMS = (((1,), (1,)), ((), ()))
TN_DIMS = (((0,), (0,)), ((), ()))


def _params(sem):
    return pltpu.CompilerParams(dimension_semantics=sem, vmem_limit_bytes=VMEM_LIMIT)


def _const_spec(shape):
    nd = len(shape)
    return pl.BlockSpec(shape, lambda *_: (0,) * nd, pipeline_mode=pl.Buffered(1))


def _rms(x, g):
    ms = jnp.mean(x * x, axis=-1, keepdims=True)
    return x * lax.rsqrt(ms + EPS) * g


def _layer_norm(x, g):
    xc = x - jnp.mean(x, axis=-1, keepdims=True)
    return xc * lax.rsqrt(jnp.mean(xc * xc, axis=-1, keepdims=True) + EPS) * g


def _rope(x, cos, sin_signed):
    rows = x.shape[0]
    first_half = (lax.broadcasted_iota(jnp.int32, (rows, LANES), 1) % RET_DK) < RET_DK // 2
    outs = []
    for c in range(x.shape[1] // LANES):
        xc = x[:, c * LANES:(c + 1) * LANES]
        nxt = pltpu.roll(xc, LANES - RET_DK // 2, axis=1)
        prv = pltpu.roll(xc, RET_DK // 2, axis=1)
        outs.append(xc * cos + jnp.where(first_half, nxt, prv) * sin_signed)
    return jnp.concatenate(outs, axis=-1)


def _log_sigmoid(x):
    return jnp.minimum(x, 0.0) - jnp.log1p(jnp.exp(-jnp.abs(x)))


def _head_rms_t(x, g):
    tokens = x.shape[1]
    x = x.reshape(H_FOX, HEAD_DIM, tokens)
    x = x * lax.rsqrt(jnp.mean(x * x, axis=1, keepdims=True) + EPS) * g[None]
    return x.reshape(W_FOX, tokens)


PROJ_SUB = 256


def _even_proj_kernel(x_ref, g_ref, w_ref, wt_ref, bf_ref, gq_ref, gk_ref, cos_ref, sin_ref,
                      sfg_ref, rq_ref, rk_ref, rv_ref, srg_ref, kn_ref,
                      qt_ref, kt32_ref, ktb_ref, vt32_ref, vtb_ref, lft_ref):
    tile = x_ref.shape[0]
    sub = min(tile, PROJ_SUB)
    for i in range(tile // sub):
        rows = slice(i * sub, (i + 1) * sub)
        h = _rms(x_ref[rows, :], g_ref[...]).astype(BF16)

        def proj(lo, hi, h=h):
            return jnp.dot(h, w_ref[:, lo:hi], preferred_element_type=F32)

        sfg_ref[rows, :] = jax.nn.silu(proj(C_FG, C_RQ)).astype(BF16)
        cos = cos_ref[rows, :]
        sin = sin_ref[rows, :]
        rq_ref[rows, :] = _rope(proj(C_RQ, C_RK), cos, sin).astype(BF16)
        rk_ref[rows, :] = (_rope(proj(C_RK, C_RV), cos, sin) * (RET_DK ** -0.5)).astype(BF16)
        rv_ref[rows, :] = proj(C_RV, C_RG).astype(BF16)
        srg_ref[rows, :] = jax.nn.silu(proj(C_RG, EVEN_COLS)).astype(BF16)

        t = lax.dot_general(wt_ref[...], h, NT_DIMS, preferred_element_type=F32)
        qt_ref[0, :, rows] = (_head_rms_t(t[R_FQ:R_FK], gq_ref[...]) * (HEAD_DIM ** -0.5 * LOG2E)).astype(BF16)
        kt = _head_rms_t(t[R_FK:R_FV], gk_ref[...])
        kt32_ref[0, :, rows] = kt
        ktb_ref[0, :, rows] = kt.astype(BF16)
        kn_ref[rows, :] = kt.T.astype(BF16)
        vt = t[R_FV:R_FF]
        vt32_ref[0, :, rows] = vt
        vtb_ref[0, :, rows] = vt.astype(BF16)
        lft_ref[0, :, rows] = _log_sigmoid(t[R_FF:R_FF + H_FOX] + bf_ref[...])


def _even_proj(x2d, n_seq, g_norm, w, wt, b_f, g_q, g_k, cos, sin, tm):
    m = x2d.shape[0]
    s = m // n_seq
    per_seq = s // tm
    row = lambda n: pl.BlockSpec((tm, n), lambda i: (i, 0))
    tab = pl.BlockSpec((tm, LANES), lambda i: (i % per_seq, 0))
    col = lambda n: pl.BlockSpec((1, n, tm), lambda i: (i // per_seq, 0, i % per_seq))
    outs = [(W_FOX, BF16), (W_RQK, BF16), (W_RQK, BF16), (W_RET, BF16), (W_RET, BF16), (W_FOX, BF16)]
    outs_t = [(W_FOX, BF16), (W_FOX, F32), (W_FOX, BF16), (W_FOX, F32), (W_FOX, BF16), (H_FOX, F32)]
    return pl.pallas_call(
        _even_proj_kernel,
        grid=(m // tm,),
        in_specs=[row(D_MODEL), _const_spec((1, D_MODEL)), _const_spec((D_MODEL, EVEN_COLS)),
                  _const_spec((EVEN_ROWS_T, D_MODEL)), _const_spec((H_FOX, 1)), _const_spec((HEAD_DIM, 1)),
                  _const_spec((HEAD_DIM, 1)), tab, tab],
        out_specs=[row(n) for n, _ in outs] + [col(n) for n, _ in outs_t],
        out_shape=([jax.ShapeDtypeStruct((m, n), dt) for n, dt in outs]
                   + [jax.ShapeDtypeStruct((n_seq, n, s), dt) for n, dt in outs_t]),
        compiler_params=_params(("parallel",)),
        name="even_proj",
    )(x2d, g_norm, w, wt, b_f, g_q, g_k, cos, sin)


def _cumsum_kernel(x_ref, o_ref):
    x = x_ref[0]
    n = x.shape[1]
    lane = lax.broadcasted_iota(jnp.int32, x.shape, 1)
    sh = 1
    while sh < n:
        x = x + jnp.where(lane >= sh, pltpu.roll(x, sh, axis=1), 0.0)
        sh *= 2
    xt = x.T
    for g in range(o_ref.shape[1]):
        o_ref[0, g] = xt[:, g * HEADS_PER_GROUP:(g + 1) * HEADS_PER_GROUP]


def _cumsum_lanes(x):
    b, h, s = x.shape
    ng = h // HEADS_PER_GROUP
    return pl.pallas_call(
        _cumsum_kernel, grid=(b,),
        in_specs=[pl.BlockSpec((1, h, s), lambda i: (i, 0, 0))],
        out_specs=pl.BlockSpec((1, ng, s, HEADS_PER_GROUP), lambda i: (i, 0, 0, 0)),
        out_shape=jax.ShapeDtypeStruct((b, ng, s, HEADS_PER_GROUP), F32),
        compiler_params=_params(("parallel",)), name="logf_cumsum",
    )(x)


HEADS_PER_GROUP = 4
GROUP_W = HEADS_PER_GROUP * HEAD_DIM


def _fox_prompt_kernel(*refs, blk):
    _fox_prompt_body(pl.program_id(2), *refs, blk=blk)


def _fox_prompt_body(i, qt_ref, k_ref, v_ref, ccol_ref, sfg_ref, a_ref,
                     qm_ref, vp_ref, cb_ref, m_ref, acc_ref, s_ref, p_ref, *, blk, side_work=None):
    s_len = k_ref.shape[1]

    @pl.when(i == 0)
    def _():
        ones_row = jnp.where(lax.broadcasted_iota(jnp.int32, (HEAD_DIM, s_len), 0) == 0, 1.0, 0.0).astype(BF16)
        ccol = ccol_ref[0, 0] * (-LOG2E)
        for hh in range(HEADS_PER_GROUP):
            vp_ref[hh, :HEAD_DIM, :] = v_ref[0, hh * HEAD_DIM:(hh + 1) * HEAD_DIM, :]
            vp_ref[hh, HEAD_DIM:, :] = ones_row
            cb_ref[hh] = jnp.broadcast_to(ccol[:, hh:hh + 1], (s_len, LANES))

    strips = [(hh, slice(c * LANES, (c + 1) * LANES), c) for hh in range(HEADS_PER_GROUP) for c in range(blk // LANES)]

    def prologue():
        qt = qt_ref[0]
        row_head = lax.broadcasted_iota(jnp.int32, (GROUP_W, blk), 0) // HEAD_DIM
        for hh in range(HEADS_PER_GROUP):
            qm_ref[hh] = jnp.where(row_head == hh, qt, jnp.zeros_like(qt))
        m_ref[...] = jnp.full(m_ref.shape, -1e30, F32)
        acc_ref[...] = jnp.zeros(acc_ref.shape, F32)
        raw_scores(0, 0)

    def raw_scores(j, slot):
        off = pl.multiple_of(j * blk, blk)
        kb = k_ref[0, pl.ds(off, blk), :]
        for hh in range(HEADS_PER_GROUP):
            bias = cb_ref[hh, pl.ds(off, blk), :]
            s_ref[slot, hh] = (jnp.dot(kb, qm_ref[hh], preferred_element_type=F32)
                               + jnp.concatenate([bias] * (blk // LANES), axis=1))

    def consume(j, slot, masked):
        off = pl.multiple_of(j * blk, blk)

        def scores(hh, cols, c):
            s = s_ref[slot, hh, :, cols]
            if masked:
                causal = (lax.broadcasted_iota(jnp.int32, (blk, LANES), 0)
                          <= lax.broadcasted_iota(jnp.int32, (blk, LANES), 1) + c * LANES)
                s = jnp.where(causal, s, -jnp.inf)
            return s

        rescale = {}
        for hh, cols, c in strips:
            s = scores(hh, cols, c)
            m_old = m_ref[hh, :, cols]
            m_new = jnp.maximum(m_old, jnp.max(s, axis=0, keepdims=True))
            p_ref[hh, :, cols] = jnp.exp2(s - m_new).astype(BF16)
            rescale[hh, c] = jnp.exp2(m_old - m_new)
            m_ref[hh, :, cols] = m_new
        for hh in range(HEADS_PER_GROUP):
            alpha = jnp.concatenate([rescale[hh, c] for c in range(blk // LANES)], axis=1)
            pv = jnp.dot(vp_ref[hh, :, pl.ds(off, blk)], p_ref[hh], preferred_element_type=F32)
            acc_ref[hh] = alpha * acc_ref[hh] + pv

    if side_work is not None:
        side_work(prologue)
    else:
        prologue()

    def pair(t, carry):
        j = 2 * t
        raw_scores(j + 1, 1)
        consume(j, 0, False)
        raw_scores(j + 2, 0)
        consume(j + 1, 1, False)
        return carry

    lax.fori_loop(0, i // 2, pair, 0)

    @pl.when(i % 2 == 1)
    def _():
        raw_scores(i, 1)
        consume(i - 1, 0, False)
        consume(i, 1, True)

    @pl.when(i % 2 == 0)
    def _():
        consume(i, 0, True)

    ot = jnp.concatenate([acc_ref[hh, :HEAD_DIM, :] / acc_ref[hh, HEAD_DIM:HEAD_DIM + 1, :]
                          for hh in range(HEADS_PER_GROUP)], axis=0)
    a_ref[0] = (ot.T * sfg_ref[0].astype(F32)).astype(BF16)


FOX_BLK = 256


def _fox_prompt_specs(s, ids):
    blk = FOX_BLK

    def spec(shape, f):
        return pl.BlockSpec(shape, lambda *a: f(*ids(*a)))

    rows = spec((1, blk, GROUP_W), lambda bi, g, i: (bi, i, g))
    in_specs = [spec((1, GROUP_W, blk), lambda bi, g, i: (bi, g, i)),
                spec((1, s, GROUP_W), lambda bi, g, i: (bi, 0, g)),
                spec((1, GROUP_W, s), lambda bi, g, i: (bi, g, 0)),
                spec((1, 1, s, HEADS_PER_GROUP), lambda bi, g, i: (bi, g, 0, 0)),
                rows]
    scratch = [pltpu.VMEM((HEADS_PER_GROUP, GROUP_W, blk), BF16),
               pltpu.VMEM((HEADS_PER_GROUP, LANES, s), BF16),
               pltpu.VMEM((HEADS_PER_GROUP, s, LANES), F32),
               pltpu.VMEM((HEADS_PER_GROUP, 1, blk), F32),
               pltpu.VMEM((HEADS_PER_GROUP, LANES, blk), F32),
               pltpu.VMEM((2, HEADS_PER_GROUP, blk, blk), F32),
               pltpu.VMEM((HEADS_PER_GROUP, blk, blk), BF16)]
    return in_specs, rows, scratch


def _fox_prompt(qt, kn, vtb, ccol, sfg):
    b, s, _ = kn.shape
    in_specs, out_spec, scratch = _fox_prompt_specs(s, lambda bi, g, i: (bi, g, i))
    return pl.pallas_call(
        functools.partial(_fox_prompt_kernel, blk=FOX_BLK),
        grid=(b, H_FOX // HEADS_PER_GROUP, s // FOX_BLK),
        in_specs=in_specs, out_specs=out_spec,
        out_shape=jax.ShapeDtypeStruct((b, s, W_FOX), BF16),
        scratch_shapes=scratch,
        compiler_params=_params(("parallel", "parallel", "arbitrary")),
        name="fox_prompt",
    )(qt, kn, vtb, ccol, sfg)


def _ret_head_out(o, g, srg):
    return (_layer_norm(o, g) * srg.astype(F32)).astype(BF16)


def _ret_prompt_kernel(rq_ref, rk_ref, rv_ref, srg_ref, gret_ref, r_ref, st_ref, dec_ref, kdec_ref, *, chunks):
    L = CHUNK
    c = pl.program_id(1)

    @pl.when(c == 0)
    def _():
        st_ref[...] = jnp.zeros(st_ref.shape, F32)
        diff = (lax.broadcasted_iota(jnp.int32, (L, L), 0)
                - lax.broadcasted_iota(jnp.int32, (L, L), 1)).astype(F32)
        for h in range(H_RET):
            dec_ref[h] = jnp.where(diff >= 0, jnp.exp(LOG_GAMMA[h] * jnp.maximum(diff, 0.0)), 0.0)
        left = (L - 1.0) - lax.broadcasted_iota(jnp.int32, (L, LANES), 0).astype(F32)
        first = lax.broadcasted_iota(jnp.int32, (L, LANES), 1) < RET_DK
        for pp in range(H_RET // 2):
            kdec_ref[pp] = jnp.exp(jnp.where(first, LOG_GAMMA[2 * pp], LOG_GAMMA[2 * pp + 1]) * left)

    pos = lax.broadcasted_iota(jnp.int32, (L, 1), 0).astype(F32)
    lane_half = lax.broadcasted_iota(jnp.int32, (L, LANES), 1) // RET_DK
    for ci in range(chunks):
        rows_c = slice(ci * L, (ci + 1) * L)
        first_stage = []
        for pp in range(H_RET // 2):
            qp = rq_ref[0, rows_c, pp * LANES:(pp + 1) * LANES]
            kp = rk_ref[0, rows_c, pp * LANES:(pp + 1) * LANES]
            s_pair = st_ref[0, pp]
            s_pair_b = s_pair.astype(BF16)
            kd = (kp.astype(F32) * kdec_ref[pp]).astype(BF16)
            for j in range(2):
                h = 2 * pp + j
                qm = jnp.where(lane_half == j, qp, jnp.zeros_like(qp))
                vh = rv_ref[0, rows_c, h * RET_DV:(h + 1) * RET_DV]
                scores = lax.dot_general(qm, kp, NT_DIMS, preferred_element_type=F32)
                cross = jnp.dot(qm, s_pair_b, preferred_element_type=F32)
                upd = lax.dot_general(kd, vh, TN_DIMS, preferred_element_type=F32)
                first_stage.append((scores, cross, upd, vh, s_pair))
        outs = []
        for h, (scores, cross, upd, vh, s_pair) in enumerate(first_stage):
            lg = LOG_GAMMA[h]
            intra = jnp.dot((scores * dec_ref[h]).astype(BF16), vh, preferred_element_type=F32)
            outs.append(intra + cross * jnp.exp(lg * (pos + 1.0)))
            rows = slice((h % 2) * RET_DK, (h % 2 + 1) * RET_DK)
            st_ref[0, h // 2, rows, :] = float(np.exp(lg * L)) * s_pair[rows] + upd[rows]
        for h, o in enumerate(outs):
            cols = slice(h * RET_DV, (h + 1) * RET_DV)
            r_ref[0, rows_c, cols] = _ret_head_out(o, gret_ref[:, cols], srg_ref[0, rows_c, cols])


def _ret_prompt(rq, rk, rv, srg, g_ret):
    b, s, _ = rq.shape
    chunks = next(n for n in (8, 4, 2, 1) if s % (n * CHUNK) == 0)
    rows = chunks * CHUNK
    qk = pl.BlockSpec((1, rows, W_RQK), lambda bi, c: (bi, c, 0))
    wide = pl.BlockSpec((1, rows, W_RET), lambda bi, c: (bi, c, 0))
    st = pl.BlockSpec((1, H_RET // 2, 2 * RET_DK, RET_DV), lambda bi, c: (bi, 0, 0, 0))
    return pl.pallas_call(
        functools.partial(_ret_prompt_kernel, chunks=chunks),
        grid=(b, s // rows),
        in_specs=[qk, qk, wide, wide, _const_spec((1, W_RET))],
        out_specs=[wide, st],
        out_shape=[jax.ShapeDtypeStruct((b, s, W_RET), BF16),
                   jax.ShapeDtypeStruct((b, H_RET // 2, 2 * RET_DK, RET_DV), F32)],
        scratch_shapes=[pltpu.VMEM((H_RET, CHUNK, CHUNK), F32), pltpu.VMEM((H_RET // 2, CHUNK, LANES), F32)],
        compiler_params=_params(("parallel", "arbitrary")),
        name="ret_prompt",
    )(rq, rk, rv, srg, g_ret)


def _ret_sample_kernel(rq_ref, rk_ref, rv_ref, srg_ref, gret_ref, s0_ref, r_ref, s1_ref, *, n_new):
    rows = rq_ref.shape[0]
    nb = rows // n_new
    two_dk = 2 * RET_DK
    ri = lax.broadcasted_iota(jnp.int32, (rows, rows), 0)
    ci = lax.broadcasted_iota(jnp.int32, (rows, rows), 1)
    same = (ri // n_new) == (ci // n_new)
    diff = ((ri % n_new) - (ci % n_new)).astype(F32)
    pos = (lax.broadcasted_iota(jnp.int32, (rows, 1), 0) % n_new).astype(F32)
    lane_half = lax.broadcasted_iota(jnp.int32, (rows, LANES), 1) // RET_DK
    q_sel = (lax.broadcasted_iota(jnp.int32, (rows, nb * two_dk), 1) // two_dk
             == lax.broadcasted_iota(jnp.int32, (rows, nb * two_dk), 0) // n_new)
    k_sel = (lax.broadcasted_iota(jnp.int32, (nb * two_dk, rows), 0) // two_dk
             == lax.broadcasted_iota(jnp.int32, (nb * two_dk, rows), 1) // n_new)
    row_head = (lax.broadcasted_iota(jnp.int32, (nb * two_dk, 1), 0) % two_dk) // RET_DK

    qp = rq_ref[...]
    kp = rk_ref[...]
    s_old = s0_ref[...].reshape(nb * two_dk, RET_DV)
    s_old_b = s_old.astype(BF16)
    upd = []
    for j in range(2):
        h = 2 * pl.program_id(0) + j
        lg = jnp.where(h == 0, LOG_GAMMA[0], 0.0)
        for hh in range(1, H_RET):
            lg = jnp.where(h == hh, LOG_GAMMA[hh], lg)
        qm = jnp.where(lane_half == j, qp, jnp.zeros_like(qp))
        vh = rv_ref[:, j * RET_DV:(j + 1) * RET_DV]
        decay = jnp.where(same & (diff >= 0), jnp.exp(lg * jnp.maximum(diff, 0.0)), 0.0)
        scores = lax.dot_general(qm, kp, NT_DIMS, preferred_element_type=F32) * decay
        intra = jnp.dot(scores.astype(BF16), vh, preferred_element_type=F32)
        q_exp = jnp.where(q_sel, jnp.tile(qm, (1, nb)), jnp.zeros((), BF16))
        cross = jnp.dot(q_exp, s_old_b, preferred_element_type=F32) * jnp.exp(lg * (pos + 1.0))
        kd = kp.astype(F32) * jnp.exp(lg * (n_new - 1.0 - pos))
        kd_exp = jnp.where(k_sel, jnp.tile(kd.T, (nb, 1)), 0.0).astype(BF16)
        keep = jnp.exp(lg * jnp.full((1, 1), float(n_new), F32))
        upd.append((keep, jnp.dot(kd_exp, vh, preferred_element_type=F32)))
        cols = slice(j * RET_DV, (j + 1) * RET_DV)
        r_ref[:, cols] = _ret_head_out(intra + cross, gret_ref[:, cols], srg_ref[:, cols])
    s_new = jnp.where(row_head == 0, upd[0][0] * s_old + upd[0][1], upd[1][0] * s_old + upd[1][1])
    s1_ref[...] = s_new.reshape(s1_ref.shape)


def _ret_sample(rq, rk, rv, srg, g_ret, state, n_new):
    rows = rq.shape[0]
    nb = rows // n_new
    pair = pl.BlockSpec((rows, LANES), lambda p: (0, p))
    wide = pl.BlockSpec((rows, 2 * RET_DV), lambda p: (0, p))
    st = pl.BlockSpec((nb, 2, RET_DK, RET_DV), lambda p: (0, p, 0, 0))
    return pl.pallas_call(
        functools.partial(_ret_sample_kernel, n_new=n_new),
        grid=(H_RET // 2,),
        in_specs=[pair, pair, wide, wide, pl.BlockSpec((1, 2 * RET_DV), lambda p: (0, p)), st],
        out_specs=[wide, st],
        out_shape=[jax.ShapeDtypeStruct((rows, W_RET), BF16),
                   jax.ShapeDtypeStruct(state.shape, F32)],
        compiler_params=_params(("parallel",)),
        name="ret_sample",
    )(rq, rk, rv, srg, g_ret, state)


def _fox_sample_kernel(pt_ref, *refs, n_steps, **kw):
    _fox_sample_body(pl.program_id(0) * n_steps + pl.program_id(1), pl.num_programs(0) * n_steps, n_steps,
                     pt_ref, *refs, **kw)


def _fox_sample_body(step, total_steps, n_steps, pt_ref, q_ref, kn_ref, vn_ref, lfn_ref, sfg_ref, lfc_ref,
                     ck_hbm, cv_hbm, a_ref, qbd_ref, m_ref, l_ref, acc_ref, tail_ref, kbuf, vbuf, sem,
                     *, n_new, pages_per_step, page_groups, before_main=None, copies=True, compute=True):
    g = pages_per_step
    b, c = step // n_steps, step % n_steps

    def first_page_of(st):
        return (n_steps - 1 - st % n_steps) * g

    def page_copies(st, slot):
        row, first = st // n_steps, first_page_of(st)
        copies = []
        for p in range(g):
            page = pt_ref[row, first + p]
            copies.append(pltpu.make_async_copy(ck_hbm.at[page], kbuf.at[slot, p], sem.at[slot, 0]))
            copies.append(pltpu.make_async_copy(cv_hbm.at[page], vbuf.at[slot, p], sem.at[slot, 1]))
        return copies

    slot = step % 2
    nxt = jnp.minimum(step + 1, total_steps - 1)

    if copies:
        @pl.when(step == 0)
        def _():
            for cp in page_copies(step, 0):
                cp.start()

        for cp in page_copies(nxt, 1 - slot):
            cp.start()
        for cp in page_copies(step, slot):
            cp.wait()
    if not compute:
        return
    first_page = first_page_of(step)
    nq = n_new * H_FOX
    row_tok = lax.broadcasted_iota(jnp.int32, (nq, 1), 0) // H_FOX
    head_of_row = lax.broadcasted_iota(jnp.int32, (H_FOX, W_FOX), 0)
    head_of_lane = lax.broadcasted_iota(jnp.int32, (H_FOX, W_FOX), 1) // HEAD_DIM

    def partial_softmax(s, vt):
        m = jnp.max(s, axis=-1, keepdims=True)
        p = jnp.exp2(s - m)
        return (m, jnp.sum(p, axis=-1, keepdims=True),
                lax.dot_general(p.astype(BF16), vt, NT_DIMS, preferred_element_type=F32))

    def merge(parts):
        m_new = m_ref[...]
        for m, _, _ in parts:
            m_new = jnp.maximum(m_new, m)
        alpha = jnp.exp2(m_ref[...] - m_new)
        l_new = alpha * l_ref[...]
        acc = alpha * acc_ref[...]
        for m, l, o in parts:
            w = jnp.exp2(m - m_new)
            l_new = l_new + w * l
            acc = acc + w * o
        m_ref[...] = m_new
        l_ref[...] = l_new
        acc_ref[...] = acc

    @pl.when(c == 0)
    def _():
        q = q_ref[0].astype(F32)
        for t in range(n_new):
            qbd_ref[t * H_FOX:(t + 1) * H_FOX, :] = jnp.where(
                head_of_lane == head_of_row, jnp.broadcast_to(q[t:t + 1, :], (H_FOX, W_FOX)), 0.0).astype(BF16)
        tail_ref[...] = jnp.zeros(tail_ref.shape, F32)
        m_ref[...] = jnp.full(m_ref.shape, -1e30, F32)
        l_ref[...] = jnp.zeros(l_ref.shape, F32)
        acc_ref[...] = jnp.zeros(acc_ref.shape, F32)
        lane = lax.broadcasted_iota(jnp.int32, (H_FOX, LANES), 1)
        pfx = lfn_ref[...]
        sh = 1
        while sh < n_new:
            pfx = pfx + jnp.where(lane % n_new >= sh, pltpu.roll(pfx, sh, axis=1), 0.0)
            sh *= 2
        col = lax.broadcasted_iota(jnp.int32, (nq, LANES), 1)
        valid = (col // n_new == b) & (col % n_new <= row_tok)
        s = jnp.dot(qbd_ref[...], kn_ref[...], preferred_element_type=F32)
        merge([partial_softmax(jnp.where(valid, s - jnp.tile(pfx, (n_new, 1)) * LOG2E, -jnp.inf), vn_ref[...])])

    if before_main is not None:
        before_main()

    lf = jnp.concatenate([lfc_ref[pt_ref[b, first_page + p]] for p in range(g)], axis=0)
    hi = lf.astype(BF16)
    rest = lf - hi.astype(F32)
    mid = rest.astype(BF16)
    low = (rest - mid.astype(F32)).astype(BF16)
    t_idx = lax.broadcasted_iota(jnp.int32, (PAGE, 2 * PAGE), 0)
    j_idx = lax.broadcasted_iota(jnp.int32, (PAGE, 2 * PAGE), 1)
    later_or_all = jnp.where((j_idx >= PAGE) | (t_idx > j_idx), 1.0, 0.0).astype(BF16)
    sums = jnp.dot(jnp.concatenate([hi, mid, low], axis=0), later_or_all, preferred_element_type=F32)
    sums = sums[:g * H_FOX] + sums[g * H_FOX:2 * g * H_FOX] + sums[2 * g * H_FOX:]
    run = tail_ref[...]
    biases = [None] * g
    for p in reversed(range(g)):
        page_rows = slice(p * H_FOX, (p + 1) * H_FOX)
        biases[p] = run + sums[page_rows, :PAGE]
        run = run + sums[page_rows, PAGE:]
    tail_ref[...] = run
    key_bias = jnp.tile(jnp.concatenate(biases, axis=1) * LOG2E, (n_new, 1))
    per = g // page_groups
    scores = []
    for grp in range(page_groups):
        kc = jnp.concatenate([kbuf[slot, p].astype(BF16) for p in range(grp * per, (grp + 1) * per)],
                             axis=1)
        s = jnp.dot(qbd_ref[...], kc, preferred_element_type=F32)
        scores.append(s + key_bias[:, grp * per * PAGE:(grp + 1) * per * PAGE])
    parts = []
    for grp, s in enumerate(scores):
        vc = jnp.concatenate([vbuf[slot, p].astype(BF16) for p in range(grp * per, (grp + 1) * per)], axis=1)
        parts.append(partial_softmax(s, vc))
    merge(parts)

    o = acc_ref[...] / l_ref[...]
    sfg = sfg_ref[0].astype(F32)
    for t in range(n_new):
        ot = jnp.where(head_of_lane == head_of_row, o[t * H_FOX:(t + 1) * H_FOX, :], 0.0)
        a_ref[0, t:t + 1, :] = jnp.sum(ot, axis=0, keepdims=True) * sfg[t:t + 1, :]

    @pl.when(step == total_steps - 1)
    def _():
        for cp in page_copies(nxt, 1 - slot):
            cp.wait()


def _fox_sample(page_table, q, kt_new, vt_new, lft_new, sfg, cache_kt, cache_vt, cache_lft):
    bd, n_new, _ = q.shape
    g, n_steps, kw = _fox_sample_plan(page_table, n_new)
    in_specs, out_spec, scratch = _fox_sample_specs(n_new, g, n_steps, (kt_new, vt_new, lft_new, cache_lft),
                                                    lambda b, c, pt: (b, c, pt))
    grid_spec = pltpu.PrefetchScalarGridSpec(
        num_scalar_prefetch=1, grid=(bd, n_steps), in_specs=in_specs, out_specs=out_spec, scratch_shapes=scratch)
    return pl.pallas_call(
        functools.partial(_fox_sample_kernel, n_steps=n_steps, **kw),
        grid_spec=grid_spec,
        out_shape=jax.ShapeDtypeStruct((bd, n_new, W_FOX), F32),
        compiler_params=_params(("arbitrary", "arbitrary")),
        name="fox_sample",
    )(page_table, q, kt_new, vt_new, lft_new, sfg, cache_lft, cache_kt, cache_vt)


def _fox_sample_plan(page_table, n_new):
    n_pages = page_table.shape[1]
    g = next(c for c in (16, 8, 4, 2, 1) if n_pages % c == 0)
    page_groups = 2 if g >= 2 else 1
    return g, n_pages // g, dict(n_new=n_new, pages_per_step=g, page_groups=page_groups)


def _fox_sample_specs(n_new, g, n_steps, resident, ids):
    nq = n_new * H_FOX

    def spec(shape, f, **kw):
        return pl.BlockSpec(shape, lambda *a: f(*ids(*a)), **kw)

    def tok(n):
        return spec((1, n_new, n), lambda b, c, pt: (b, 0, 0))

    def whole(arr):
        nd = arr.ndim
        return spec(arr.shape, lambda b, c, pt: (0,) * nd, pipeline_mode=pl.Buffered(1))

    kt_new, vt_new, lft_new, cache_lft = resident
    in_memory = pl.BlockSpec(memory_space=pl.ANY)
    in_specs = [tok(W_FOX), whole(kt_new), whole(vt_new), whole(lft_new), tok(W_FOX), whole(cache_lft),
                in_memory, in_memory]
    scratch = [pltpu.VMEM((nq, W_FOX), BF16), pltpu.VMEM((nq, 1), F32), pltpu.VMEM((nq, 1), F32),
               pltpu.VMEM((nq, W_FOX), F32), pltpu.VMEM((H_FOX, LANES), F32),
               pltpu.VMEM((2, g, W_FOX, PAGE), F32), pltpu.VMEM((2, g, W_FOX, PAGE), F32),
               pltpu.SemaphoreType.DMA((2, 2))]
    return in_specs, tok(W_FOX), scratch


def _fox_fused_kernel(pt_ref, *refs, n_prompt_in, n_sample_in, n_prompt_scratch, n_steps, sample_kw):
    refs = list(refs)
    p_in, refs = refs[:n_prompt_in], refs[n_prompt_in:]
    s_in, refs = refs[:n_sample_in], refs[n_sample_in:]
    (a_p, a_d), refs = refs[:2], refs[2:]
    p_scr, s_scr = refs[:n_prompt_scratch], refs[n_prompt_scratch:]
    step = (pl.program_id(0) * pl.num_programs(1) + pl.program_id(1)) * pl.num_programs(2) + pl.program_id(2)
    total = pl.num_programs(0) * pl.num_programs(1) * pl.num_programs(2)
    sample = functools.partial(_fox_sample_body, step, total, n_steps, pt_ref, *s_in, a_d, *s_scr, **sample_kw)
    sample(compute=False)
    _fox_prompt_body(pl.program_id(2), *p_in, a_p, *p_scr, blk=FOX_BLK,
                     side_work=lambda prologue: sample(copies=False, before_main=prologue))


def _fox_fused(qt, kn, vtb, ccol, sfg, page_table, q, kt_new, vt_new, lft_new, sfg_d, cache_kt, cache_vt, cache_lft):
    b, s, _ = kn.shape
    bd, n_new, _ = q.shape
    ng, nqb = H_FOX // HEADS_PER_GROUP, s // FOX_BLK
    g, n_steps, kw = _fox_sample_plan(page_table, n_new)
    assert b * ng * nqb == bd * n_steps
    p_in, p_out, p_scr = _fox_prompt_specs(s, lambda bi, gi, i, pt: (bi, gi, i))

    def sample_ids(bi, gi, i, pt):
        step = (bi * ng + gi) * nqb + i
        return step // n_steps, step % n_steps, pt

    s_in, s_out, s_scr = _fox_sample_specs(n_new, g, n_steps, (kt_new, vt_new, lft_new, cache_lft), sample_ids)
    grid_spec = pltpu.PrefetchScalarGridSpec(
        num_scalar_prefetch=1, grid=(b, ng, nqb), in_specs=p_in + s_in, out_specs=[p_out, s_out],
        scratch_shapes=p_scr + s_scr)
    return pl.pallas_call(
        functools.partial(_fox_fused_kernel, n_prompt_in=len(p_in), n_sample_in=len(s_in),
                          n_prompt_scratch=len(p_scr), n_steps=n_steps, sample_kw=kw),
        grid_spec=grid_spec,
        out_shape=[jax.ShapeDtypeStruct((b, s, W_FOX), BF16), jax.ShapeDtypeStruct((bd, n_new, W_FOX), F32)],
        compiler_params=_params(("arbitrary", "arbitrary", "arbitrary")),
        name="fox_fused",
    )(page_table, qt, kn, vtb, ccol, sfg, q, kt_new, vt_new, lft_new, sfg_d, cache_lft, cache_kt, cache_vt)


TAIL_SUB = 256


def _tail_kernel(a_ref, r_ref, x_ref, woe_ref, gno_ref, wio_ref, gv_ref, ws_ref, bs_ref, woo_ref,
                 *out_refs, n_new):
    tile = x_ref.shape[0]
    sub = min(tile, TAIL_SUB)
    y_ref = out_refs[0]

    def residual(rows):
        return (x_ref[rows, :]
                + jnp.dot(a_ref[rows, :].astype(BF16), woe_ref[:W_FOX, :], preferred_element_type=F32)
                + jnp.dot(r_ref[rows, :], woe_ref[W_FOX:, :], preferred_element_type=F32))

    def front(rows, x1):
        h = _rms(x1, gno_ref[...]).astype(BF16)
        v = jax.nn.gelu(jnp.dot(h, wio_ref[:, GM_WIDTH:2 * GM_WIDTH], preferred_element_type=F32))
        vn = _layer_norm(v, gv_ref[...])
        u = jax.nn.gelu(jnp.dot(h, wio_ref[:, :GM_WIDTH], preferred_element_type=F32))
        gate = jnp.dot(h, wio_ref[:, 2 * GM_WIDTH:], preferred_element_type=F32)
        return x1, vn, u, gate

    def back(rows, x1, vn, u, gate):
        if n_new is None:
            tril = (lax.broadcasted_iota(jnp.int32, (CHUNK, CHUNK), 0)
                    >= lax.broadcasted_iota(jnp.int32, (CHUNK, CHUNK), 1))
            vb = vn.astype(BF16)

            def spatial(g):
                wm = jnp.where(tril, ws_ref[g], 0.0).astype(BF16)
                bias = bs_ref[:, g:g + 1]
                return jnp.concatenate(
                    [jnp.dot(wm, vb[c * CHUNK:(c + 1) * CHUNK, g * GM_GC:(g + 1) * GM_GC],
                             preferred_element_type=F32) + bias for c in range(sub // CHUNK)], axis=0)
        else:
            out_refs[1][rows, :] = vn
            tok = lax.broadcasted_iota(jnp.int32, (sub, 1), 0) % n_new

            def spatial(g):
                vg = vn[:, g * GM_GC:(g + 1) * GM_GC]
                acc = jnp.zeros((sub, 1), F32)
                for t in range(n_new):
                    acc = jnp.where(tok == t, bs_ref[g, t], acc)
                acc = jnp.broadcast_to(acc, (sub, GM_GC))
                for d in range(n_new):
                    coef = jnp.zeros((sub, 1), F32)
                    for t in range(d, n_new):
                        coef = jnp.where(tok == t, ws_ref[g, t * n_new + (t - d)], coef)
                    src = vg if d == 0 else pltpu.roll(vg, d, axis=0)
                    acc = acc + coef * src
                return acc

        sv = jnp.concatenate([spatial(g) for g in range(GM_GROUPS)], axis=1)
        z = (u * sv * jax.nn.silu(gate)).astype(BF16)
        y_ref[rows, :] = x1 + jnp.dot(z, woo_ref[...], preferred_element_type=F32)

    subs = [slice(i * sub, (i + 1) * sub) for i in range(tile // sub)]
    residuals = [residual(rows) for rows in subs]
    fronts = [front(rows, x1) for rows, x1 in zip(subs, residuals)]
    for rows, vals in zip(subs, fronts):
        back(rows, *vals)


def _tail(a, r, x2d, woe, gno, wio, gv, ws, bs, woo, tm, n_new):
    m = x2d.shape[0]
    row = lambda n: pl.BlockSpec((tm, n), lambda i: (i, 0))
    if n_new is None:
        mix_specs = [_const_spec(ws.shape), _const_spec(bs.shape)]
        out_specs = [row(D_MODEL)]
        out_shape = [jax.ShapeDtypeStruct((m, D_MODEL), F32)]
    else:
        smem = pl.BlockSpec(memory_space=pltpu.SMEM)
        mix_specs = [smem, smem]
        out_specs = [row(D_MODEL), row(GM_WIDTH)]
        out_shape = [jax.ShapeDtypeStruct((m, D_MODEL), F32), jax.ShapeDtypeStruct((m, GM_WIDTH), F32)]
    return pl.pallas_call(
        functools.partial(_tail_kernel, n_new=n_new),
        grid=(m // tm,),
        in_specs=[row(W_FOX), row(W_RET), row(D_MODEL), _const_spec(woe.shape), _const_spec((1, D_MODEL)),
                  _const_spec(wio.shape), _const_spec((1, GM_WIDTH))] + mix_specs + [_const_spec(woo.shape)],
        out_specs=out_specs, out_shape=out_shape,
        compiler_params=_params(("parallel",)),
        name="tail_prompt" if n_new is None else "tail_sample",
    )(a, r, x2d, woe, gno, wio, gv, ws, bs, woo)


def _rope_tables(pos):
    half = RET_DK // 2
    inv = ROPE_BASE ** (-jnp.arange(half, dtype=F32) / half)
    ang = pos.astype(F32)[:, None] * inv[None, :]
    cos, sin = jnp.cos(ang), jnp.sin(ang)
    return jnp.tile(jnp.concatenate([cos, cos], axis=-1), (1, 2)), jnp.tile(jnp.concatenate([-sin, sin], axis=-1), (1, 2))


def _row_tile(m, largest):
    return next(t for t in (512, 256, 128) if t <= largest and m % t == 0)


def kernel(x_prompt, x_sample, cache_k, cache_v, cache_logf, state_ret, page_table, norm_even, w_in_even,
           b_forget, qnorm, knorm, ret_norm, w_out_even, norm_odd, w_in_odd, vnorm_odd, w_spatial, b_spatial,
           w_out_odd):
    bp, s, _ = x_prompt.shape
    bd, n_new, _ = x_sample.shape
    n_pages = page_table.shape[1]
    past = n_pages * PAGE
    assert s % 256 == 0 and (bd * n_new) % CHUNK == 0 and w_in_even.shape[0] == 1 and w_in_odd.shape[0] == 1

    wi = w_in_even[0]
    sizes = (W_FOX, W_FOX, W_FOX, H_FOX, W_FOX, W_RQK, W_RQK, W_RET, W_RET)
    offs = np.concatenate([[0], np.cumsum(sizes)])
    fq, fk, fv, ff, fg, rq, rk, rv, rg = [wi[:, offs[i]:offs[i + 1]] for i in range(9)]
    w_even = jnp.concatenate([fg, rq, rk, rv, rg], axis=1).astype(BF16)
    w_even_t = jnp.concatenate([fq.T, fk.T, fv.T, ff.T, jnp.zeros((EVEN_ROWS_T - R_FF - H_FOX, D_MODEL), F32)],
                               axis=0).astype(BF16)
    b_f = b_forget[0][:, None]
    g_q = qnorm[0][:, None]
    g_k = knorm[0][:, None]
    g_ret = ret_norm[0].reshape(1, W_RET)
    woe = w_out_even[0].astype(BF16)
    wio = w_in_odd[0].astype(BF16)
    woo = w_out_odd[0].astype(BF16)
    g_ne, g_no, g_v = norm_even[0][None, :], norm_odd[0][None, :], vnorm_odd[0][None, :]

    cos_p, sin_p = _rope_tables(jnp.arange(s))
    xp2 = x_prompt.reshape(bp * s, D_MODEL)
    tm = _row_tile(s, 512)
    sfg, rqp, rkp, rvp, srg, kn, qt, kt32, _, vt32, vtb, lft = _even_proj(
        xp2, bp, g_ne, w_even, w_even_t, b_f, g_q, g_k, cos_p, sin_p, tm)
    rows = bd * n_new
    cos_d, sin_d = _rope_tables(past + jnp.arange(n_new))
    xd2 = x_sample.reshape(rows, D_MODEL)
    sfgd, rqd, rkd, rvd, srgd, _, qtd, kt32d, ktbd, vt32d, vtbd, lftd = _even_proj(
        xd2, 1, g_ne, w_even, w_even_t, b_f, g_q, g_k, jnp.tile(cos_d, (bd, 1)), jnp.tile(sin_d, (bd, 1)), rows)

    ng = H_FOX // HEADS_PER_GROUP
    c_col4 = _cumsum_lanes(lft)
    r3 = lambda t: t.reshape(bp, s, t.shape[-1])
    d3 = lambda t: t.reshape(bd, n_new, t.shape[-1])
    n_phys = cache_k.shape[1]
    pages_t = lambda cache: jnp.transpose(cache[0], (0, 2, 3, 1)).reshape(n_phys, W_FOX, PAGE)
    prompt_args = (qt, r3(kn), vtb, c_col4, r3(sfg))
    sample_args = (page_table, d3(jnp.transpose(qtd[0], (1, 0))), ktbd[0], vtbd[0], lftd[0], d3(sfgd),
                   pages_t(cache_k), pages_t(cache_v), jnp.transpose(cache_logf[0], (0, 2, 1)))
    if bp * ng * (s // FOX_BLK) == bd * _fox_sample_plan(page_table, n_new)[1]:
        a_p, a_d = _fox_fused(*prompt_args, *sample_args)
    else:
        a_p, a_d = _fox_prompt(*prompt_args), _fox_sample(*sample_args)
    r_p, st_p = _ret_prompt(r3(rqp), r3(rkp), r3(rvp), r3(srg), g_ret)
    r_d, st_d = _ret_sample(rqd, rkd, rvd, srgd, g_ret, state_ret[0], n_new)

    y_p, = _tail(a_p.reshape(bp * s, W_FOX), r_p.reshape(bp * s, W_RET), xp2, woe, g_no, wio, g_v,
                 w_spatial[0], jnp.transpose(b_spatial[0]), woo, _row_tile(s, 512), None)
    ws_new = w_spatial[0][:, :n_new, :n_new].reshape(GM_GROUPS, n_new * n_new)
    y_d, gv_d = _tail(a_d.reshape(rows, W_FOX), r_d, xd2, woe, g_no, wio, g_v,
                      ws_new, b_spatial[0][:, :n_new], woo, rows, n_new)

    heads_last = lambda t, n, length: jnp.transpose(t.reshape(n, H_FOX, HEAD_DIM, length), (0, 3, 1, 2))[None]
    tokens_first = lambda t, n: jnp.transpose(t[0], (1, 0)).reshape(1, bd, n_new, *n)
    return (y_p.reshape(bp, s, D_MODEL), y_d.reshape(bd, n_new, D_MODEL),
            heads_last(kt32, bp, s), heads_last(vt32, bp, s),
            jnp.transpose(lft, (0, 2, 1))[None], st_p.reshape(1, bp, H_RET, RET_DK, RET_DV),
            tokens_first(kt32d, (H_FOX, HEAD_DIM)), tokens_first(vt32d, (H_FOX, HEAD_DIM)),
            tokens_first(lftd, (H_FOX,)), st_d[None], gv_d.reshape(1, bd, n_new, GM_WIDTH))
```
